```python
import jax, jax.numpy as jnp
from jax import lax
import numpy as np

D_MODEL = 1024
BATCH = 8
SEQ = 2048
DEPTH = 2

MIX_WIDTH = D_MODEL
POOL_WIDTH = D_MODEL // 4
POOL_WINDOWS = (2, 4, 8, 16)
POOL_GROUPS = len(POOL_WINDOWS)
POOL_GROUP_DIM = POOL_WIDTH // POOL_GROUPS
RET_WIDTH = (MIX_WIDTH - POOL_WIDTH) // 2
RET_HEADS = 4
RET_DV = RET_WIDTH // RET_HEADS
RET_DK = RET_DV // 2
RET_CHUNK = 128
ROPE_BASE = 10000.0
GLA_WIDTH = MIX_WIDTH - POOL_WIDTH - RET_WIDTH
GLA_HEADS = 4
GLA_DV = GLA_WIDTH // GLA_HEADS
GLA_DK = GLA_DV // 2
GLA_CHUNK = 64
GLA_GATE_RANK = 16
GLA_GATE_TAU = 16.0
D_FF = 2816
CONV_WIDTH = 3
EPS = 1e-6

IN_SPLITS = (POOL_WIDTH,
             RET_HEADS * RET_DK, RET_HEADS * RET_DK, RET_WIDTH, RET_WIDTH,
             GLA_HEADS * GLA_DK, GLA_HEADS * GLA_DK, GLA_WIDTH, GLA_GATE_RANK, GLA_WIDTH)
D_IN = (POOL_WIDTH + 2 * RET_HEADS * RET_DK + 2 * RET_WIDTH
        + 2 * GLA_HEADS * GLA_DK + 2 * GLA_WIDTH + GLA_GATE_RANK)

kernel_name = "hybrid_pool_retention_gla_convffn_adaln"


def rms_norm(x, g):
    xf = x.astype(jnp.float32)
    y = xf * lax.rsqrt(jnp.mean(xf * xf, axis=-1, keepdims=True) + EPS)
    return (y * g.astype(jnp.float32)).astype(x.dtype)


def head_rms(y):
    return y * lax.rsqrt(jnp.mean(y * y, axis=-1, keepdims=True) + EPS)


def chunk_view(t, C):
    B, T, H, d = t.shape
    return t.reshape(B, T // C, C, H, d).transpose(1, 0, 3, 2, 4)


def unchunk(y):
    N, B, H, C, d = y.shape
    return y.transpose(1, 0, 3, 2, 4).reshape(B, N * C, H * d)


def rotary(t, cos, sin):
    t1, t2 = jnp.split(t, 2, axis=-1)
    return jnp.concatenate([t1 * cos - t2 * sin, t2 * cos + t1 * sin], axis=-1)


def causal_multiscale_pool(u, pool_w, pool_scale):
    B, T, _ = u.shape
    uf = u.astype(jnp.float32).reshape(B, T, POOL_GROUPS, POOL_GROUP_DIM)
    cs = jnp.concatenate([jnp.zeros((B, 1, POOL_GROUPS, POOL_GROUP_DIM), jnp.float32),
                          jnp.cumsum(uf, axis=1)], axis=1)
    t = jnp.arange(T)
    outs = []
    for g, w in enumerate(POOL_WINDOWS):
        lo = jnp.maximum(t + 1 - w, 0)
        window_sum = cs[:, 1:, g] - cs[:, lo, g]
        count = jnp.minimum(t + 1, w).astype(jnp.float32)[None, :, None]
        outs.append(window_sum / count - uf[:, :, g])
    pooled = jnp.stack(outs, axis=2)
    mixed = jnp.einsum('btgi,gio->btgo', pooled, pool_w.astype(jnp.float32))
    return (mixed.reshape(B, T, POOL_WIDTH) * pool_scale.astype(jnp.float32)).astype(u.dtype)


def chunkwise_retention(q, k, v):
    B, T, H, _ = q.shape
    C = RET_CHUNK
    log_gamma = jnp.log(1.0 - 2.0 ** (-5.0 - jnp.arange(H, dtype=jnp.float32)))
    idx = jnp.arange(C, dtype=jnp.float32)
    rel = idx[:, None] - idx[None, :]
    decay = jnp.where(rel >= 0, jnp.exp(jnp.maximum(rel, 0.0)[None] * log_gamma[:, None, None]), 0.0)
    xi = jnp.exp((idx + 1.0)[None, :] * log_gamma[:, None])[None, :, :, None]
    zeta = jnp.exp((C - 1.0 - idx)[None, :] * log_gamma[:, None])[None, :, :, None]
    gamma_c = jnp.exp(C * log_gamma)[None, :, None, None]

    def step(S, inp):
        qi, ki, vi = inp
        scores = jnp.einsum('bhid,bhjd->bhij', qi, ki) * decay
        o = (jnp.einsum('bhij,bhjv->bhiv', scores, vi)
             + jnp.einsum('bhid,bhdv->bhiv', qi, S) * xi)
        S = S * gamma_c + jnp.einsum('bhjd,bhjv->bhdv', ki * zeta, vi)
        return S, o

    S0 = jnp.zeros((B, H, q.shape[-1], v.shape[-1]), jnp.float32)
    _, o = lax.scan(step, S0, (chunk_view(q, C), chunk_view(k, C), chunk_view(v, C)))
    return o


def chunked_gla(q, k, v, log_alpha):
    B, T, H, _ = q.shape
    C = GLA_CHUNK
    causal = jnp.tril(jnp.ones((C, C), dtype=bool))[:, :, None]

    def step(S, inp):
        qi, ki, vi, lai = inp
        b = jnp.cumsum(lai, axis=2)
        diff = b[:, :, :, None, :] - b[:, :, None, :, :]
        ratio = jnp.exp(jnp.where(causal, diff, -jnp.inf))
        scores = jnp.einsum('bhid,bhijd,bhjd->bhij', qi, ratio, ki)
        b_last = b[:, :, -1:, :]
        o = (jnp.einsum('bhij,bhjv->bhiv', scores, vi)
             + jnp.einsum('bhid,bhdv->bhiv', qi * jnp.exp(b), S))
        S = (jnp.exp(b_last[:, :, 0, :])[..., None] * S
             + jnp.einsum('bhjd,bhjv->bhdv', ki * jnp.exp(b_last - b), vi))
        return S, o

    S0 = jnp.zeros((B, H, q.shape[-1], v.shape[-1]), jnp.float32)
    _, o = lax.scan(step, S0, (chunk_view(q, C), chunk_view(k, C),
                               chunk_view(v, C), chunk_view(log_alpha, C)))
    return o


def causal_depthwise_conv(u, w, b):
    T = u.shape[1]
    up = jnp.pad(u, ((0, 0), (CONV_WIDTH - 1, 0), (0, 0)))
    y = up[:, 0:T] * w[0]
    for kk in range(1, CONV_WIDTH):
        y = y + up[:, kk:kk + T] * w[kk]
    return y + b


def hybrid_layer(x, cos, sin, mod, norm1_g, w_in, pool_w, pool_scale, gla_wa2, gla_ba,
                 gla_norm_g, w_out, norm2_g, w_up, conv_w, conv_b, w_down):
    B, T, _ = x.shape
    shift1, scale1, gate1, shift2, scale2, gate2 = jnp.split(mod[:, None, :], 6, axis=-1)

    h = rms_norm(x, norm1_g) * (1 + scale1) + shift1
    proj = h @ w_in
    offsets = [int(o) for o in np.cumsum(IN_SPLITS)[:-1]]
    u_pool, r_q, r_k, r_v, r_g, g_q, g_k, g_v, g_a, g_g = jnp.split(proj, offsets, axis=-1)

    y_pool = causal_multiscale_pool(u_pool, pool_w, pool_scale)

    rq = rotary(r_q.astype(jnp.float32).reshape(B, T, RET_HEADS, RET_DK), cos, sin)
    rk = rotary(r_k.astype(jnp.float32).reshape(B, T, RET_HEADS, RET_DK), cos, sin) * (RET_DK ** -0.5)
    rv = r_v.astype(jnp.float32).reshape(B, T, RET_HEADS, RET_DV)
    o_ret = unchunk(head_rms(chunkwise_retention(rq, rk, rv)))
    y_ret = (jax.nn.silu(r_g.astype(jnp.float32)) * o_ret).astype(x.dtype)

    gq = g_q.astype(jnp.float32).reshape(B, T, GLA_HEADS, GLA_DK) * (GLA_DK ** -0.5)
    gk = g_k.astype(jnp.float32).reshape(B, T, GLA_HEADS, GLA_DK)
    gv = g_v.astype(jnp.float32).reshape(B, T, GLA_HEADS, GLA_DV)
    gate_logits = (g_a @ gla_wa2 + gla_ba).astype(jnp.float32)
    log_alpha = (jax.nn.log_sigmoid(gate_logits) / GLA_GATE_TAU).reshape(B, T, GLA_HEADS, GLA_DK)
    o_gla = unchunk(head_rms(chunked_gla(gq, gk, gv, log_alpha))) * gla_norm_g.astype(jnp.float32)
    y_gla = (jax.nn.silu(g_g.astype(jnp.float32)) * o_gla).astype(x.dtype)

    mixed = jnp.concatenate([y_pool, y_ret, y_gla], axis=-1) @ w_out
    x = x + gate1 * mixed

    h = rms_norm(x, norm2_g) * (1 + scale2) + shift2
    u = causal_depthwise_conv(h @ w_up, conv_w, conv_b)
    a, g = jnp.split(u, 2, axis=-1)
    y = (jax.nn.silu(g) * a) @ w_down
    return x + gate2 * y


def setup_inputs(seed: int = 0) -> dict:
    key = jax.random.key(seed)
    ks = jax.random.split(key, 20)
    L, D, F = DEPTH, D_MODEL, D_FF
    nrm = jax.random.normal
    x = nrm(ks[0], (BATCH, SEQ, D), jnp.float32)
    c = nrm(ks[1], (BATCH, D), jnp.float32)
    offsets = jax.random.randint(ks[2], (BATCH, 1), 0, 4096, dtype=jnp.int32)
    positions = (jnp.arange(SEQ, dtype=jnp.int32)[None, :] + offsets).astype(jnp.int32)
    ada_w = nrm(ks[3], (L, D, 6 * D), jnp.float32) * 0.02
    ada_b = nrm(ks[4], (L, 6 * D), jnp.float32) * 0.02
    norm1_g = 1.0 + 0.1 * nrm(ks[5], (L, D), jnp.float32)
    w_in = nrm(ks[6], (L, D, D_IN), jnp.float32) * D ** -0.5
    pool_w = nrm(ks[7], (L, POOL_GROUPS, POOL_GROUP_DIM, POOL_GROUP_DIM), jnp.float32) * POOL_GROUP_DIM ** -0.5
    pool_scale = 1.0 + 0.1 * nrm(ks[8], (L, POOL_WIDTH), jnp.float32)
    gla_wa2 = nrm(ks[9], (L, GLA_GATE_RANK, GLA_HEADS * GLA_DK), jnp.float32) * GLA_GATE_RANK ** -0.5
    gla_ba = 0.1 * nrm(ks[10], (L, GLA_HEADS * GLA_DK), jnp.float32)
    gla_norm_g = 1.0 + 0.1 * nrm(ks[11], (L, GLA_WIDTH), jnp.float32)
    w_out = nrm(ks[12], (L, MIX_WIDTH, D), jnp.float32) * MIX_WIDTH ** -0.5
    norm2_g = 1.0 + 0.1 * nrm(ks[13], (L, D), jnp.float32)
    w_up = nrm(ks[14], (L, D, 2 * F), jnp.float32) * D ** -0.5
    conv_w = nrm(ks[15], (L, CONV_WIDTH, 2 * F), jnp.float32) * CONV_WIDTH ** -0.5
    conv_b = 0.02 * nrm(ks[16], (L, 2 * F), jnp.float32)
    w_down = nrm(ks[17], (L, F, D), jnp.float32) * F ** -0.5
    final_g = 1.0 + 0.1 * nrm(ks[18], (D,), jnp.float32)
    return {"x": x, "c": c, "positions": positions, "ada_w": ada_w, "ada_b": ada_b,
            "norm1_g": norm1_g, "w_in": w_in, "pool_w": pool_w, "pool_scale": pool_scale,
            "gla_wa2": gla_wa2, "gla_ba": gla_ba, "gla_norm_g": gla_norm_g, "w_out": w_out,
            "norm2_g": norm2_g, "w_up": w_up, "conv_w": conv_w, "conv_b": conv_b,
            "w_down": w_down, "final_g": final_g}


def reference(x, c, positions, ada_w, ada_b, norm1_g, w_in, pool_w, pool_scale, gla_wa2, gla_ba,
              gla_norm_g, w_out, norm2_g, w_up, conv_w, conv_b, w_down, final_g):
    inv_freq = ROPE_BASE ** (-jnp.arange(0, RET_DK, 2, dtype=jnp.float32) / RET_DK)
    ang = positions.astype(jnp.float32)[..., None] * inv_freq
    cos = jnp.cos(ang)[:, :, None, :]
    sin = jnp.sin(ang)[:, :, None, :]
    c_act = jax.nn.silu(c)
    for l in range(DEPTH):
        mod = c_act @ ada_w[l] + ada_b[l]
        x = hybrid_layer(x, cos, sin, mod, norm1_g[l], w_in[l], pool_w[l], pool_scale[l],
                         gla_wa2[l], gla_ba[l], gla_norm_g[l], w_out[l], norm2_g[l],
                         w_up[l], conv_w[l], conv_b[l], w_down[l])
    return rms_norm(x, final_g)
```

```python
import functools

import numpy as np
import jax
import jax.numpy as jnp
from jax import lax
from jax.experimental import pallas as pl
from jax.experimental.pallas import tpu as pltpu

POOL_WIDTH = 256
POOL_WINDOWS = (2, 4, 8, 16)
POOL_GROUP_DIM = 64
N_HEADS = 4
DK = 48
DV = 96
KW = N_HEADS * DK
VW = N_HEADS * DV
RET_CHUNK = 128
GLA_CHUNK = 64
GLA_GATE_RANK = 16
GLA_GATE_TAU = 16.0
ROPE_BASE = 10000.0
CONV_WIDTH = 3
EPS = 1e-6

LANES = 128
SUBLANES = 8
KP = 256
GAP = LANES
VMEM_LIMIT_BYTES = 56 * 1024 * 1024

OFF_POOL = 0
OFF_RQ = OFF_POOL + POOL_WIDTH
OFF_RQS = OFF_RQ + KP
OFF_RK = OFF_RQS + KP
OFF_RKS = OFF_RK + KP
OFF_RV = OFF_RKS + KP
OFF_RG = OFF_RV + VW
OFF_GQ = OFF_RG + VW
OFF_GK = OFF_GQ + KP
OFF_GV = OFF_GK + KP
OFF_GA = OFF_GV + VW
OFF_GG = OFF_GA + GAP
N_IN = OFF_GG + VW

MIX_TILE = 256
FFN_TILE = 512
FFN_BLOCK = 256
POOL_HIST = 32

_BF = jnp.bfloat16
_F32 = jnp.float32


def _dot(a, b):
    return jnp.dot(a, b, preferred_element_type=_F32)


def _dot_nt(a, b):
    return lax.dot_general(a, b, (((1,), (1,)), ((), ())), preferred_element_type=_F32)


def _dot_tn(a, b):
    return lax.dot_general(a, b, (((0,), (0,)), ((), ())), preferred_element_type=_F32)


def _sigmoid(x):
    return 1.0 / (1.0 + jnp.exp(-x))


def _silu(x):
    return x * _sigmoid(x)


@functools.lru_cache(maxsize=None)
def _win_layout():
    src = np.zeros((N_IN,), np.int32)
    sign = np.zeros((N_IN,), np.float32)

    def put(dst, lo, n):
        src[dst:dst + n] = np.arange(lo, lo + n)
        sign[dst:dst + n] = 1.0

    def put_swapped(dst, lo):
        for h in range(N_HEADS):
            for i in range(DK):
                col = dst + h * DK + i
                if i < DK // 2:
                    src[col] = lo + h * DK + i + DK // 2
                    sign[col] = -1.0
                else:
                    src[col] = lo + h * DK + i - DK // 2
                    sign[col] = 1.0

    o = 0
    put(OFF_POOL, o, POOL_WIDTH); o += POOL_WIDTH
    put(OFF_RQ, o, KW); put_swapped(OFF_RQS, o); o += KW
    put(OFF_RK, o, KW); put_swapped(OFF_RKS, o); o += KW
    put(OFF_RV, o, VW); o += VW
    put(OFF_RG, o, VW); o += VW
    put(OFF_GQ, o, KW); o += KW
    put(OFF_GK, o, KW); o += KW
    put(OFF_GV, o, VW); o += VW
    put(OFF_GA, o, GLA_GATE_RANK); o += GLA_GATE_RANK
    put(OFF_GG, o, VW); o += VW
    return src, sign


def _key_head(d):
    return np.where(d < KW, d // DK, -1)


@functools.lru_cache(maxsize=None)
def _tables():
    t = {}
    kd = _key_head(np.arange(KP))
    vd = np.arange(VW) // DV

    def head_masks(chunk):
        rows = np.repeat(np.arange(N_HEADS), chunk)
        return ((rows[:, None] == kd[None, :]).astype(np.float32),
                (rows[:, None] == vd[None, :]).astype(np.float32))

    C = RET_CHUNK
    lg = np.log(1.0 - 2.0 ** (-5.0 - np.arange(N_HEADS, dtype=np.float64)))
    i = np.arange(C)
    rel = i[:, None] - i[None, :]
    dec = np.where(rel[None] >= 0, np.exp(np.maximum(rel, 0)[None] * lg[:, None, None]), 0.0)
    t["ret_decay"] = np.transpose(dec, (1, 0, 2)).reshape(C, N_HEADS * C).astype(np.float32)
    t["ret_xi"] = np.exp((i[:, None] + 1.0) * lg[vd][None, :]).astype(np.float32)
    zeta = np.exp((C - 1.0 - i)[:, None] * lg[np.maximum(kd, 0)][None, :]) * (kd >= 0)[None, :]
    t["ret_zeta"] = zeta.astype(np.float32)
    t["ret_gc"] = (np.exp(C * lg[np.maximum(kd, 0)]) * (kd >= 0))[None, :].astype(np.float32)
    t["ret_hmk"], t["ret_hmv"] = head_masks(C)
    t["state_mask"] = (vd[:, None] == kd[None, :]).astype(np.float32)
    t["head_ind"] = (vd[:, None] == vd[None, :]).astype(np.float32)

    C = GLA_CHUNK
    i = np.arange(C)
    blocks, masks = [], []
    s = C // 2
    while s >= 1:
        parent = (i // (2 * s)) * (2 * s)
        upper = (i - parent) >= s
        ref = parent + s - 1
        tt = i[None, :]
        m_up = (tt > ref[:, None]) & (tt <= i[:, None])
        m_lo = (tt > i[:, None]) & (tt <= ref[:, None])
        blocks.append(np.where(upper[:, None], m_up, m_lo))
        masks.append(upper[:, None] & (~upper)[None, :] & (parent[:, None] == parent[None, :]))
        s //= 2
    blocks.append(i[None, :] <= i[:, None])
    blocks.append(i[None, :] > i[:, None])
    masks.append(i[:, None] == i[None, :])
    t["gla_sel"] = np.concatenate(blocks, axis=0).astype(np.float32)
    t["gla_mask"] = np.stack([np.tile(m, (1, N_HEADS)) for m in masks]).astype(np.float32)
    t["gla_hmk"], t["gla_hmv"] = head_masks(C)

    t["pool_win"] = np.repeat(np.asarray(POOL_WINDOWS, np.float32), POOL_GROUP_DIM)[None, :]
    return t


N_GLA_LEVELS = 6


def _mod_kernel(c_ref, w_ref, b_ref, o_ref):
    ca = _silu(c_ref[...]).astype(_BF)
    o_ref[...] = _dot(ca, w_ref[...].astype(_BF)) + b_ref[...]


def _modulation(c, ada_w, ada_b):
    L, D, N = ada_w.shape
    B = c.shape[0]
    tn = D
    return pl.pallas_call(
        _mod_kernel,
        grid=(L, N // tn),
        in_specs=[
            pl.BlockSpec((B, D), lambda l, n: (0, 0)),
            pl.BlockSpec((None, D, tn), lambda l, n: (l, 0, n)),
            pl.BlockSpec((None, 1, tn), lambda l, n: (l, 0, n)),
        ],
        out_specs=pl.BlockSpec((None, B, tn), lambda l, n: (l, 0, n)),
        out_shape=jax.ShapeDtypeStruct((L, B, N), _F32),
        compiler_params=pltpu.CompilerParams(
            dimension_semantics=("arbitrary", "arbitrary"), vmem_limit_bytes=VMEM_LIMIT_BYTES),
        name="adaln_modulation",
    )(c, ada_w, ada_b.reshape(L, 1, N))


def _trig_kernel(a_ref, c_ref, s_ref):
    a = a_ref[...]
    c_ref[...] = jnp.cos(a)
    s_ref[...] = jnp.sin(a)


def _rotary_tables(positions):
    B, T = positions.shape
    half = DK // 2
    inv_freq = ROPE_BASE ** (-jnp.arange(0, DK, 2, dtype=_F32) / DK)
    ang = positions.astype(_F32)[..., None] * inv_freq
    rows = (B * T * half) // LANES
    ang2 = ang.reshape(rows, LANES)
    tr = 512 if rows % 512 == 0 else rows
    cos, sin = pl.pallas_call(
        _trig_kernel,
        grid=(rows // tr,),
        in_specs=[pl.BlockSpec((tr, LANES), lambda i: (i, 0))],
        out_specs=[pl.BlockSpec((tr, LANES), lambda i: (i, 0))] * 2,
        out_shape=[jax.ShapeDtypeStruct((rows, LANES), _F32)] * 2,
        compiler_params=pltpu.CompilerParams(dimension_semantics=("arbitrary",)),
        name="rotary_table",
    )(ang2)

    def widen(x):
        x = x.reshape(B, T, half)
        x = jnp.tile(x, (1, 1, 2 * N_HEADS))
        return jnp.pad(x, ((0, 0), (0, 0), (0, KP - KW)))
    return widen(cos), widen(sin)


def _mixer_kernel(x_ref, mod_ref, ng_ref, win_ref, cos_ref, sin_ref,
                  poolw_ref, pools_ref, poolwin_ref,
                  wa2_ref, ba_ref, gng_ref, wout_ref,
                  rdecay_ref, rxi_ref, rzeta_ref, rgc_ref, rhmk_ref, rhmv_ref,
                  smask_ref, hind_ref,
                  gsel_ref, gmask_ref, ghmk_ref, ghmv_ref,
                  o_ref,
                  proj_ref, ycat_ref, la_ref, ext_ref, s2_ref, s4_ref, s8_ref,
                  hist_ref, sret_ref, sgla_ref, *, tile):
    t_idx = pl.program_id(1)

    @pl.when(t_idx == 0)
    def _():
        hist_ref[...] = jnp.zeros_like(hist_ref)
        sret_ref[...] = jnp.zeros_like(sret_ref)
        sgla_ref[...] = jnp.zeros_like(sgla_ref)

    x = x_ref[0]
    shift1 = mod_ref[0, 0:1, :]
    scale1 = mod_ref[0, 1:2, :]
    gate1 = mod_ref[0, 2:3, :]
    ms = jnp.mean(x * x, axis=-1, keepdims=True)
    h = (x * lax.rsqrt(ms + EPS) * ng_ref[...]) * (1.0 + scale1) + shift1
    hb = h.astype(_BF)
    for off, width in ((OFF_POOL, POOL_WIDTH), (OFF_RQ, 2 * KP), (OFF_RK, 2 * KP),
                       (OFF_RV, VW), (OFF_RG, VW), (OFF_GQ, 2 * KP), (OFF_GV, VW + GAP),
                       (OFF_GG, VW)):
        proj_ref[:, off:off + width] = _dot(hb, win_ref[:, off:off + width])

    n = tile + POOL_HIST
    u = proj_ref[:, OFF_POOL:OFF_POOL + POOL_WIDTH]
    ext_ref[0:POOL_HIST, :] = hist_ref[...]
    ext_ref[POOL_HIST:n, :] = u
    hist_ref[...] = u[tile - POOL_HIST:tile, :]
    s2_ref[8:n, :] = ext_ref[8:n, :] + ext_ref[7:n - 1, :]
    s4_ref[16:n, :] = s2_ref[16:n, :] + s2_ref[14:n - 2, :]
    s8_ref[24:n, :] = s4_ref[24:n, :] + s4_ref[20:n - 4, :]
    s16 = s8_ref[32:n, :] + s8_ref[24:n - 8, :]
    lane = lax.broadcasted_iota(jnp.int32, (tile, POOL_WIDTH), 1)
    wsum = jnp.where(lane < POOL_GROUP_DIM, s2_ref[32:n, :],
                     jnp.where(lane < 2 * POOL_GROUP_DIM, s4_ref[32:n, :],
                               jnp.where(lane < 3 * POOL_GROUP_DIM, s8_ref[32:n, :], s16)))
    t_abs = (t_idx * tile + lax.broadcasted_iota(jnp.int32, (tile, POOL_WIDTH), 0)).astype(_F32)
    cnt = jnp.minimum(t_abs + 1.0, poolwin_ref[...])
    pooled = wsum / cnt - u
    ycat_ref[:, 0:POOL_WIDTH] = (_dot(pooled.astype(_BF), poolw_ref[...]) * pools_ref[...]).astype(_BF)

    inv_dv = 1.0 / DV
    k_scale = DK ** -0.5
    smask = smask_ref[...]
    hind = hind_ref[...]

    def head_rms(o):
        ms_h = _dot((o * o).astype(_BF), hind) * inv_dv
        return o * lax.rsqrt(ms_h + EPS)

    def ret_chunk(c, carry):
        r0 = pl.multiple_of(c * RET_CHUNK, RET_CHUNK)
        rows = pl.ds(r0, RET_CHUNK)
        cosv = cos_ref[0, rows, :]
        sinv = sin_ref[0, rows, :]
        q = proj_ref[rows, OFF_RQ:OFF_RQ + KP] * cosv + proj_ref[rows, OFF_RQS:OFF_RQS + KP] * sinv
        k = (proj_ref[rows, OFF_RK:OFF_RK + KP] * cosv
             + proj_ref[rows, OFF_RKS:OFF_RKS + KP] * sinv) * k_scale
        v = proj_ref[rows, OFF_RV:OFF_RV + VW]
        qb = q.astype(_BF)
        kbd = (jnp.concatenate([k] * N_HEADS, axis=0) * rhmk_ref[...]).astype(_BF)
        scores = _dot_nt(qb, kbd) * rdecay_ref[...]
        vbd = (jnp.concatenate([v] * N_HEADS, axis=0) * rhmv_ref[...]).astype(_BF)
        state = sret_ref[...]
        o = _dot(scores.astype(_BF), vbd) + _dot_nt(qb, state.astype(_BF)) * rxi_ref[...]
        kz = (k * rzeta_ref[...]).astype(_BF)
        sret_ref[...] = state * rgc_ref[...] + _dot_tn(v.astype(_BF), kz) * smask
        g = proj_ref[rows, OFF_RG:OFF_RG + VW]
        ycat_ref[rows, POOL_WIDTH:POOL_WIDTH + VW] = (_silu(g) * head_rms(o)).astype(_BF)
        return carry

    lax.fori_loop(0, tile // RET_CHUNK, ret_chunk, 0)

    ga = proj_ref[:, OFF_GA:OFF_GA + GAP].astype(_BF)
    logits = _dot(ga, wa2_ref[...]) + ba_ref[...]
    log_sig = jnp.minimum(logits, 0.0) - jnp.log1p(jnp.exp(-jnp.abs(logits)))
    la_ref[...] = log_sig * (1.0 / GLA_GATE_TAU)
    q_scale = DK ** -0.5
    C = GLA_CHUNK

    def gla_chunk(c, carry):
        r0 = pl.multiple_of(c * C, C)
        rows = pl.ds(r0, C)
        q = proj_ref[rows, OFF_GQ:OFF_GQ + KP] * q_scale
        k = proj_ref[rows, OFF_GK:OFF_GK + KP]
        v = proj_ref[rows, OFF_GV:OFF_GV + VW]
        la = la_ref[rows, :]
        p1 = la.astype(_BF)
        r1 = la - p1.astype(_F32)
        p2 = r1.astype(_BF)
        p3 = (r1 - p2.astype(_F32)).astype(_BF)
        sel = gsel_ref[...]
        decay = jnp.exp(_dot(sel, p1) + _dot(sel, p2) + _dot(sel, p3))
        hmk = ghmk_ref[...]
        scores = jnp.zeros((C, N_HEADS * C), _F32)
        for lvl in range(N_GLA_LEVELS + 1):
            if lvl < N_GLA_LEVELS:
                e = decay[lvl * C:(lvl + 1) * C, :]
                ql, kl = q * e, k * e
            else:
                ql, kl = q, k
            kbd = (jnp.concatenate([kl] * N_HEADS, axis=0) * hmk).astype(_BF)
            scores = scores + _dot_nt(ql.astype(_BF), kbd) * gmask_ref[lvl]
        vbd = (jnp.concatenate([v] * N_HEADS, axis=0) * ghmv_ref[...]).astype(_BF)
        e_cum = decay[N_GLA_LEVELS * C:(N_GLA_LEVELS + 1) * C, :]
        e_rev = decay[(N_GLA_LEVELS + 1) * C:(N_GLA_LEVELS + 2) * C, :]
        state = sgla_ref[...]
        o = _dot(scores.astype(_BF), vbd) + _dot_nt((q * e_cum).astype(_BF), state.astype(_BF))
        krev = (k * e_rev).astype(_BF)
        sgla_ref[...] = state * e_cum[C - 1:C, :] + _dot_tn(v.astype(_BF), krev) * smask
        g = proj_ref[rows, OFF_GG:OFF_GG + VW]
        y = _silu(g) * (head_rms(o) * gng_ref[...])
        ycat_ref[rows, POOL_WIDTH + VW:POOL_WIDTH + 2 * VW] = y.astype(_BF)
        return carry

    lax.fori_loop(0, tile // C, gla_chunk, 0)

    o_ref[0] = x_ref[0] + gate1 * _dot(ycat_ref[...], wout_ref[...])


def _const_spec(shape):
    nd = len(shape)
    return pl.BlockSpec(shape, lambda b, t: (0,) * nd)


def _mixer(x, mod, norm_g, win, cos, sin, poolw, pools, wa2, ba, gng, wout, tb):
    B, T, D = x.shape
    tile = MIX_TILE if T % MIX_TILE == 0 else T
    consts = [tb["pool_win"]]
    tail = [tb["ret_decay"], tb["ret_xi"], tb["ret_zeta"], tb["ret_gc"], tb["ret_hmk"], tb["ret_hmv"],
            tb["state_mask"], tb["head_ind"], tb["gla_sel"], tb["gla_mask"], tb["gla_hmk"], tb["gla_hmv"]]
    args = [x, mod, norm_g, win, cos, sin, poolw, pools] + consts + [wa2, ba, gng, wout] + tail
    in_specs = [
        pl.BlockSpec((1, tile, D), lambda b, t: (b, t, 0)),
        pl.BlockSpec((1, 6, D), lambda b, t: (b, 0, 0)),
        _const_spec(norm_g.shape),
        _const_spec(win.shape),
        pl.BlockSpec((1, tile, KP), lambda b, t: (b, t, 0)),
        pl.BlockSpec((1, tile, KP), lambda b, t: (b, t, 0)),
    ] + [_const_spec(a.shape) for a in args[6:]]
    n = tile + POOL_HIST
    return pl.pallas_call(
        functools.partial(_mixer_kernel, tile=tile),
        grid=(B, T // tile),
        in_specs=in_specs,
        out_specs=pl.BlockSpec((1, tile, D), lambda b, t: (b, t, 0)),
        out_shape=jax.ShapeDtypeStruct((B, T, D), _F32),
        scratch_shapes=[
            pltpu.VMEM((tile, N_IN), _F32),
            pltpu.VMEM((tile, POOL_WIDTH + 2 * VW), _BF),
            pltpu.VMEM((tile, KP), _F32),
            pltpu.VMEM((n, POOL_WIDTH), _F32),
            pltpu.VMEM((n, POOL_WIDTH), _F32),
            pltpu.VMEM((n, POOL_WIDTH), _F32),
            pltpu.VMEM((n, POOL_WIDTH), _F32),
            pltpu.VMEM((POOL_HIST, POOL_WIDTH), _F32),
            pltpu.VMEM((VW, KP), _F32),
            pltpu.VMEM((VW, KP), _F32),
        ],
        compiler_params=pltpu.CompilerParams(
            dimension_semantics=("arbitrary", "arbitrary"), vmem_limit_bytes=VMEM_LIMIT_BYTES),
        name="token_mixer",
    )(*args)


def _ffn_kernel(x_ref, mod_ref, ng_ref, wa_ref, wg_ref, ca_ref, cg_ref, wd_ref, fg_ref,
                o_ref, h_ref, ua_ref, ug_ref, carry_a_ref, carry_g_ref, acc_ref,
                *, tile, n_blocks, final_norm):
    t_idx = pl.program_id(1)

    @pl.when(t_idx == 0)
    def _():
        carry_a_ref[...] = jnp.zeros_like(carry_a_ref)
        carry_g_ref[...] = jnp.zeros_like(carry_g_ref)

    x = x_ref[0]
    shift2 = mod_ref[0, 3:4, :]
    scale2 = mod_ref[0, 4:5, :]
    gate2 = mod_ref[0, 5:6, :]
    ms = jnp.mean(x * x, axis=-1, keepdims=True)
    h = (x * lax.rsqrt(ms + EPS) * ng_ref[...]) * (1.0 + scale2) + shift2
    h_ref[...] = h.astype(_BF)
    acc_ref[...] = jnp.zeros_like(acc_ref)
    S = SUBLANES

    def conv(u, u_ref, carry_ref, cw_ref, j):
        u_ref[0:S, :] = carry_ref[j]
        u_ref[S:S + tile, :] = u
        carry_ref[j] = u[tile - S:tile, :]
        cw = cw_ref[j]
        return (u_ref[S - 2:S - 2 + tile, :] * cw[0:1, :] + u_ref[S - 1:S - 1 + tile, :] * cw[1:2, :]
                + u * cw[2:3, :] + cw[3:4, :])

    def block(j, carry):
        hb = h_ref[...]
        ya = conv(_dot(hb, wa_ref[j]), ua_ref, carry_a_ref, ca_ref, j)
        yg = conv(_dot(hb, wg_ref[j]), ug_ref, carry_g_ref, cg_ref, j)
        act = (_silu(yg) * ya).astype(_BF)
        acc_ref[...] += _dot(act, wd_ref[j])
        return carry

    lax.fori_loop(0, n_blocks, block, 0)
    out = x_ref[0] + gate2 * acc_ref[...]
    if final_norm:
        ms_o = jnp.mean(out * out, axis=-1, keepdims=True)
        out = out * lax.rsqrt(ms_o + EPS) * fg_ref[...]
    o_ref[0] = out


def _ffn(x, mod, norm_g, wa, wg, ca, cg, wd, final_g, final_norm):
    B, T, D = x.shape
    tile = FFN_TILE if T % FFN_TILE == 0 else T
    nb, _, fb = wa.shape
    args = [x, mod, norm_g, wa, wg, ca, cg, wd, final_g]
    in_specs = [
        pl.BlockSpec((1, tile, D), lambda b, t: (b, t, 0)),
        pl.BlockSpec((1, 6, D), lambda b, t: (b, 0, 0)),
    ] + [_const_spec(a.shape) for a in args[2:]]
    return pl.pallas_call(
        functools.partial(_ffn_kernel, tile=tile, n_blocks=nb, final_norm=final_norm),
        grid=(B, T // tile),
        in_specs=in_specs,
        out_specs=pl.BlockSpec((1, tile, D), lambda b, t: (b, t, 0)),
        out_shape=jax.ShapeDtypeStruct((B, T, D), _F32),
        scratch_shapes=[
            pltpu.VMEM((tile, D), _BF),
            pltpu.VMEM((tile + SUBLANES, fb), _F32),
            pltpu.VMEM((tile + SUBLANES, fb), _F32),
            pltpu.VMEM((nb, SUBLANES, fb), _F32),
            pltpu.VMEM((nb, SUBLANES, fb), _F32),
            pltpu.VMEM((tile, D), _F32),
        ],
        compiler_params=pltpu.CompilerParams(
            dimension_semantics=("arbitrary", "arbitrary"), vmem_limit_bytes=VMEM_LIMIT_BYTES),
        name="conv_ffn",
    )(*args)


def _prep_mixer_params(w_in, pool_w, pool_scale, gla_wa2, gla_ba, gla_norm_g, w_out):
    src, sign = _win_layout()
    win = (jnp.take(w_in, jnp.asarray(src), axis=1) * jnp.asarray(sign)).astype(_BF)
    G = pool_w.shape[0]
    eye = jnp.eye(G, dtype=pool_w.dtype)
    poolw = (pool_w[:, :, None, :] * eye[:, None, :, None]).reshape(POOL_WIDTH, POOL_WIDTH).astype(_BF)
    wa2 = jnp.pad(gla_wa2, ((0, GAP - GLA_GATE_RANK), (0, KP - KW))).astype(_BF)
    ba = jnp.pad(gla_ba, (0, KP - KW)).reshape(1, KP)
    return (win, poolw, pool_scale.reshape(1, POOL_WIDTH), wa2, ba,
            gla_norm_g.reshape(1, VW), w_out.astype(_BF))


def _prep_ffn_params(w_up, conv_w, conv_b, w_down):
    D, F2 = w_up.shape
    F = F2 // 2
    fb = FFN_BLOCK
    nb = F // fb

    def blocks(w):
        return w.reshape(D, nb, fb).transpose(1, 0, 2).astype(_BF)

    def conv_rows(lo):
        rows = jnp.concatenate([conv_w[:, lo:lo + F], conv_b[None, lo:lo + F]], axis=0)
        rows = jnp.pad(rows, ((0, SUBLANES - CONV_WIDTH - 1), (0, 0)))
        return rows.reshape(SUBLANES, nb, fb).transpose(1, 0, 2)

    return (blocks(w_up[:, :F]), blocks(w_up[:, F:]), conv_rows(0), conv_rows(F),
            w_down.reshape(nb, fb, D).astype(_BF))


def kernel(x, c, positions, ada_w, ada_b, norm1_g, w_in, pool_w, pool_scale, gla_wa2, gla_ba,
           gla_norm_g, w_out, norm2_g, w_up, conv_w, conv_b, w_down, final_g):
    L = ada_w.shape[0]
    B, T, D = x.shape
    tb = {k: jnp.asarray(v) for k, v in _tables().items()}
    tb["head_ind"] = tb["head_ind"].astype(_BF)
    tb["gla_sel"] = tb["gla_sel"].astype(_BF)
    mod = _modulation(c, ada_w, ada_b).reshape(L, B, 6, D)
    cos, sin = _rotary_tables(positions)
    fg = final_g.reshape(1, D)
    for l in range(L):
        mp = _prep_mixer_params(w_in[l], pool_w[l], pool_scale[l], gla_wa2[l], gla_ba[l],
                                gla_norm_g[l], w_out[l])
        x = _mixer(x, mod[l], norm1_g[l].reshape(1, D), mp[0], cos, sin, mp[1], mp[2],
                   mp[3], mp[4], mp[5], mp[6], tb)
        fp = _prep_ffn_params(w_up[l], conv_w[l], conv_b[l], w_down[l])
        x = _ffn(x, mod[l], norm2_g[l].reshape(1, D), *fp, fg, final_norm=(l == L - 1))
    return x
```

```python
import functools

import numpy as np
import jax
import jax.numpy as jnp
from jax import lax
from jax.experimental import pallas as pl
from jax.experimental.pallas import tpu as pltpu

POOL_WIDTH = 256
POOL_WINDOWS = (2, 4, 8, 16)
POOL_GROUP_DIM = 64
N_HEADS = 4
DK = 48
DV = 96
KW = N_HEADS * DK
VW = N_HEADS * DV
RET_CHUNK = 128
GLA_CHUNK = 64
GLA_GATE_RANK = 16
GLA_GATE_TAU = 16.0
ROPE_BASE = 10000.0
CONV_WIDTH = 3
EPS = 1e-6

LANES = 128
SUBLANES = 8
KP = 256
GAP = LANES
VMEM_LIMIT_BYTES = 56 * 1024 * 1024

OFF_POOL = 0
OFF_RQ = OFF_POOL + POOL_WIDTH
OFF_RQS = OFF_RQ + KP
OFF_RK = OFF_RQS + KP
OFF_RKS = OFF_RK + KP
OFF_RV = OFF_RKS + KP
OFF_RG = OFF_RV + VW
OFF_GQ = OFF_RG + VW
OFF_GK = OFF_GQ + KP
OFF_GV = OFF_GK + KP
OFF_GA = OFF_GV + VW
OFF_GG = OFF_GA + GAP
N_IN = OFF_GG + VW

MIX_TILE = 256
FFN_TILE = 512
FFN_BLOCK = 256
POOL_HIST = 32

_BF = jnp.bfloat16
_F32 = jnp.float32


def _dot(a, b):
    return jnp.dot(a, b, preferred_element_type=_F32)


def _dot_nt(a, b):
    return lax.dot_general(a, b, (((1,), (1,)), ((), ())), preferred_element_type=_F32)


def _dot_tn(a, b):
    return lax.dot_general(a, b, (((0,), (0,)), ((), ())), preferred_element_type=_F32)


def _sigmoid(x):
    return 1.0 / (1.0 + jnp.exp(-x))


def _silu(x):
    return x * _sigmoid(x)


def _key_head(d):
    return np.where(d < KW, d // DK, -1)


@functools.lru_cache(maxsize=None)
def _tables():
    t = {}
    kd = _key_head(np.arange(KP))
    vd = np.arange(VW) // DV

    def head_masks(chunk):
        rows = np.repeat(np.arange(N_HEADS), chunk)
        return ((rows[:, None] == kd[None, :]).astype(np.float32),
                (rows[:, None] == vd[None, :]).astype(np.float32))

    C = RET_CHUNK
    lg = np.log(1.0 - 2.0 ** (-5.0 - np.arange(N_HEADS, dtype=np.float64)))
    i = np.arange(C)
    rel = i[:, None] - i[None, :]
    dec = np.where(rel[None] >= 0, np.exp(np.maximum(rel, 0)[None] * lg[:, None, None]), 0.0)
    t["ret_decay"] = np.transpose(dec, (1, 0, 2)).reshape(C, N_HEADS * C).astype(np.float32)
    t["ret_xi"] = np.exp((i[:, None] + 1.0) * lg[vd][None, :]).astype(np.float32)
    zeta = np.exp((C - 1.0 - i)[:, None] * lg[np.maximum(kd, 0)][None, :]) * (kd >= 0)[None, :]
    t["ret_zeta"] = zeta.astype(np.float32)
    t["ret_gc"] = (np.exp(C * lg[np.maximum(kd, 0)]) * (kd >= 0))[None, :].astype(np.float32)
    t["ret_hmk"], t["ret_hmv"] = head_masks(C)
    t["state_mask"] = (vd[:, None] == kd[None, :]).astype(np.float32)
    t["head_ind"] = (vd[:, None] == vd[None, :]).astype(np.float32)

    C = GLA_CHUNK
    i = np.arange(C)
    blocks, masks = [], []
    s = C // 2
    while s >= 1:
        parent = (i // (2 * s)) * (2 * s)
        upper = (i - parent) >= s
        ref = parent + s - 1
        tt = i[None, :]
        m_up = (tt > ref[:, None]) & (tt <= i[:, None])
        m_lo = (tt > i[:, None]) & (tt <= ref[:, None])
        blocks.append(np.where(upper[:, None], m_up, m_lo))
        masks.append(upper[:, None] & (~upper)[None, :] & (parent[:, None] == parent[None, :]))
        s //= 2
    blocks.append(i[None, :] <= i[:, None])
    blocks.append(i[None, :] > i[:, None])
    masks.append(i[:, None] == i[None, :])
    t["gla_sel"] = np.concatenate(blocks, axis=0).astype(np.float32)
    t["gla_mask"] = np.stack([np.tile(m, (1, N_HEADS)) for m in masks]).astype(np.float32)
    t["gla_hmk"], t["gla_hmv"] = head_masks(C)

    t["pool_win"] = np.repeat(np.asarray(POOL_WINDOWS, np.float32), POOL_GROUP_DIM)[None, :]
    return t


N_GLA_LEVELS = 6


def _mod_kernel(c_ref, w_ref, b_ref, o_ref):
    ca = _silu(c_ref[...]).astype(_BF)
    o_ref[...] = _dot(ca, w_ref[...].astype(_BF)) + b_ref[...]


def _modulation(c, ada_w, ada_b):
    L, D, N = ada_w.shape
    B = c.shape[0]
    tn = D
    return pl.pallas_call(
        _mod_kernel,
        grid=(L, N // tn),
        in_specs=[
            pl.BlockSpec((B, D), lambda l, n: (0, 0)),
            pl.BlockSpec((None, D, tn), lambda l, n: (l, 0, n)),
            pl.BlockSpec((None, 1, tn), lambda l, n: (l, 0, n)),
        ],
        out_specs=pl.BlockSpec((None, B, tn), lambda l, n: (l, 0, n)),
        out_shape=jax.ShapeDtypeStruct((L, B, N), _F32),
        compiler_params=pltpu.CompilerParams(
            dimension_semantics=("arbitrary", "arbitrary"), vmem_limit_bytes=VMEM_LIMIT_BYTES),
        name="adaln_modulation",
    )(c, ada_w, ada_b.reshape(L, 1, N))


def _trig_kernel(a_ref, c_ref, s_ref):
    a = a_ref[...]
    c_ref[...] = jnp.cos(a)
    s_ref[...] = jnp.sin(a)


def _rotary_tables(positions):
    B, T = positions.shape
    half = DK // 2
    inv_freq = ROPE_BASE ** (-jnp.arange(0, DK, 2, dtype=_F32) / DK)
    ang = positions.astype(_F32)[..., None] * inv_freq
    rows = (B * T * half) // LANES
    ang2 = ang.reshape(rows, LANES)
    tr = 512 if rows % 512 == 0 else rows
    cos, sin = pl.pallas_call(
        _trig_kernel,
        grid=(rows // tr,),
        in_specs=[pl.BlockSpec((tr, LANES), lambda i: (i, 0))],
        out_specs=[pl.BlockSpec((tr, LANES), lambda i: (i, 0))] * 2,
        out_shape=[jax.ShapeDtypeStruct((rows, LANES), _F32)] * 2,
        compiler_params=pltpu.CompilerParams(dimension_semantics=("arbitrary",)),
        name="rotary_table",
    )(ang2)

    def widen(x):
        x = x.reshape(B, T, half)
        x = jnp.tile(x, (1, 1, 2 * N_HEADS))
        return jnp.pad(x, ((0, 0), (0, 0), (0, KP - KW)))
    return widen(cos), widen(sin)


def _mixer_kernel(x_ref, mod_ref, ng_ref, win_ref, cos_ref, sin_ref,
                  poolw_ref, pools_ref, poolwin_ref,
                  wa2_ref, ba_ref, gng_ref, wout_ref,
                  rdecay_ref, rxi_ref, rzeta_ref, rgc_ref, rhmk_ref, rhmv_ref,
                  smask_ref, hind_ref,
                  gsel_ref, gmask_ref, ghmk_ref, ghmv_ref,
                  o_ref,
                  proj_ref, ycat_ref, dec_ref, oscr_ref, ext_ref, s2_ref, s4_ref, s8_ref,
                  hist_ref, sret_ref, sgla_ref, *, tile):
    t_idx = pl.program_id(1)

    @pl.when(t_idx == 0)
    def _():
        hist_ref[...] = jnp.zeros_like(hist_ref)
        sret_ref[...] = jnp.zeros_like(sret_ref)
        sgla_ref[...] = jnp.zeros_like(sgla_ref)

    x = x_ref[0]
    shift1 = mod_ref[0, 0:1, :]
    scale1 = mod_ref[0, 1:2, :]
    gate1 = mod_ref[0, 2:3, :]
    ms = jnp.mean(x * x, axis=-1, keepdims=True)
    h = (x * lax.rsqrt(ms + EPS)) * (ng_ref[...] * (1.0 + scale1)) + shift1
    hb = h.astype(_BF)
    for off, width in ((OFF_POOL, POOL_WIDTH), (OFF_RQ, 2 * KP), (OFF_RK, 2 * KP),
                       (OFF_RV, VW), (OFF_RG, VW), (OFF_GQ, 2 * KP), (OFF_GV, VW + GAP),
                       (OFF_GG, VW)):
        proj_ref[:, off:off + width] = _dot(hb, win_ref[:, off:off + width])

    n = tile + POOL_HIST
    u = proj_ref[:, OFF_POOL:OFF_POOL + POOL_WIDTH]
    ext_ref[0:POOL_HIST, :] = hist_ref[...]
    ext_ref[POOL_HIST:n, :] = u
    hist_ref[...] = u[tile - POOL_HIST:tile, :]
    s2_ref[8:n, :] = ext_ref[8:n, :] + ext_ref[7:n - 1, :]
    s4_ref[16:n, :] = s2_ref[16:n, :] + s2_ref[14:n - 2, :]
    s8_ref[24:n, :] = s4_ref[24:n, :] + s4_ref[20:n - 4, :]
    s16 = s8_ref[32:n, :] + s8_ref[24:n - 8, :]
    lane = lax.broadcasted_iota(jnp.int32, (tile, POOL_WIDTH), 1)
    wsum = jnp.where(lane < POOL_GROUP_DIM, s2_ref[32:n, :],
                     jnp.where(lane < 2 * POOL_GROUP_DIM, s4_ref[32:n, :],
                               jnp.where(lane < 3 * POOL_GROUP_DIM, s8_ref[32:n, :], s16)))
    t_abs = (t_idx * tile + lax.broadcasted_iota(jnp.int32, (tile, POOL_WIDTH), 0)).astype(_F32)
    cnt = jnp.minimum(t_abs + 1.0, poolwin_ref[...])
    pooled = wsum / cnt - u
    ycat_ref[:, 0:POOL_WIDTH] = (_dot(pooled.astype(_BF), poolw_ref[...]) * pools_ref[...]).astype(_BF)

    inv_dv = 1.0 / DV
    k_scale = DK ** -0.5
    smask = smask_ref[...]
    hind = hind_ref[...]

    def head_rms(o):
        ms_h = _dot((o * o).astype(_BF), hind) * inv_dv
        return o * lax.rsqrt(ms_h + EPS)

    def stack_heads(a):
        return jnp.concatenate([a] * N_HEADS, axis=0)

    C = RET_CHUNK
    ret_parts = []
    for c in range(tile // C):
        rows = slice(c * C, (c + 1) * C)
        cosv = cos_ref[0, rows, :]
        sinv = sin_ref[0, rows, :]
        q = proj_ref[rows, OFF_RQ:OFF_RQ + KP] * cosv + proj_ref[rows, OFF_RQS:OFF_RQS + KP] * sinv
        k = (proj_ref[rows, OFF_RK:OFF_RK + KP] * cosv
             + proj_ref[rows, OFF_RKS:OFF_RKS + KP] * sinv) * k_scale
        vb = proj_ref[rows, OFF_RV:OFF_RV + VW].astype(_BF)
        qb = q.astype(_BF)
        kbd = stack_heads(k.astype(_BF)) * rhmk_ref[...]
        scores = _dot_nt(qb, kbd) * rdecay_ref[...]
        vbd = stack_heads(vb) * rhmv_ref[...]
        o_intra = _dot(scores.astype(_BF), vbd)
        upd = _dot_tn(vb, (k * rzeta_ref[...]).astype(_BF)) * smask
        ret_parts.append((rows, qb, o_intra, upd))
    state = sret_ref[...]
    for rows, qb, o_intra, upd in ret_parts:
        oscr_ref[rows, 0:VW] = o_intra + _dot_nt(qb, state.astype(_BF)) * rxi_ref[...]
        state = state * rgc_ref[...] + upd
    sret_ref[...] = state

    ga = proj_ref[:, OFF_GA:OFF_GA + GAP].astype(_BF)
    logits = _dot(ga, wa2_ref[...]) + ba_ref[...]
    log_sig = jnp.minimum(logits, 0.0) - jnp.log1p(jnp.exp(-jnp.abs(logits)))
    la = log_sig * (1.0 / GLA_GATE_TAU)
    q_scale = DK ** -0.5
    C = GLA_CHUNK
    n_chunks = tile // C
    p1 = la.astype(_BF)
    r1 = la - p1.astype(_F32)
    p2 = r1.astype(_BF)
    p3 = (r1 - p2.astype(_F32)).astype(_BF)

    def chunks_on_lanes(p):
        return jnp.concatenate([p[c * C:(c + 1) * C, :] for c in range(n_chunks)], axis=1)

    sel = gsel_ref[...]
    dec_ref[...] = jnp.exp(_dot(sel, chunks_on_lanes(p1)) + _dot(sel, chunks_on_lanes(p2))
                           + _dot(sel, chunks_on_lanes(p3)))
    hmk = ghmk_ref[...]
    gla_parts = []
    for c in range(n_chunks):
        rows = slice(c * C, (c + 1) * C)
        lanes = slice(c * KP, (c + 1) * KP)
        q = proj_ref[rows, OFF_GQ:OFF_GQ + KP] * q_scale
        k = proj_ref[rows, OFF_GK:OFF_GK + KP]
        vb = proj_ref[rows, OFF_GV:OFF_GV + VW].astype(_BF)
        scores = jnp.zeros((C, N_HEADS * C), _F32)
        for lvl in range(N_GLA_LEVELS + 1):
            if lvl < N_GLA_LEVELS:
                e = dec_ref[lvl * C:(lvl + 1) * C, lanes]
                ql, kl = q * e, k * e
            else:
                ql, kl = q, k
            kbd = stack_heads(kl.astype(_BF)) * hmk
            scores = scores + _dot_nt(ql.astype(_BF), kbd) * gmask_ref[lvl]
        vbd = stack_heads(vb) * ghmv_ref[...]
        e_cum = dec_ref[N_GLA_LEVELS * C:(N_GLA_LEVELS + 1) * C, lanes]
        e_rev = dec_ref[(N_GLA_LEVELS + 1) * C:(N_GLA_LEVELS + 2) * C, lanes]
        o_intra = _dot(scores.astype(_BF), vbd)
        upd = _dot_tn(vb, (k * e_rev).astype(_BF)) * smask
        gla_parts.append((rows, (q * e_cum).astype(_BF), o_intra, upd, e_cum[C - 1:C, :]))
    state = sgla_ref[...]
    for rows, qe, o_intra, upd, e_last in gla_parts:
        oscr_ref[rows, VW:2 * VW] = o_intra + _dot_nt(qe, state.astype(_BF))
        state = state * e_last + upd
    sgla_ref[...] = state

    y_ret = _silu(proj_ref[:, OFF_RG:OFF_RG + VW]) * head_rms(oscr_ref[:, 0:VW])
    ycat_ref[:, POOL_WIDTH:POOL_WIDTH + VW] = y_ret.astype(_BF)
    y_gla = _silu(proj_ref[:, OFF_GG:OFF_GG + VW]) * (head_rms(oscr_ref[:, VW:2 * VW]) * gng_ref[...])
    ycat_ref[:, POOL_WIDTH + VW:POOL_WIDTH + 2 * VW] = y_gla.astype(_BF)

    o_ref[0] = x_ref[0] + gate1 * _dot(ycat_ref[...], wout_ref[...])


def _const_spec(shape):
    nd = len(shape)
    return pl.BlockSpec(shape, lambda b, t: (0,) * nd, pipeline_mode=pl.Buffered(1))


def _mixer(x, mod, norm_g, win, cos, sin, poolw, pools, wa2, ba, gng, wout, tb):
    B, T, D = x.shape
    tile = MIX_TILE if T % MIX_TILE == 0 else T
    consts = [tb["pool_win"]]
    tail = [tb["ret_decay"], tb["ret_xi"], tb["ret_zeta"], tb["ret_gc"], tb["ret_hmk"], tb["ret_hmv"],
            tb["state_mask"], tb["head_ind"], tb["gla_sel"], tb["gla_mask"], tb["gla_hmk"], tb["gla_hmv"]]
    args = [x, mod, norm_g, win, cos, sin, poolw, pools] + consts + [wa2, ba, gng, wout] + tail
    in_specs = [
        pl.BlockSpec((1, tile, D), lambda b, t: (b, t, 0)),
        pl.BlockSpec((1, 6, D), lambda b, t: (b, 0, 0)),
        _const_spec(norm_g.shape),
        _const_spec(win.shape),
        pl.BlockSpec((1, tile, KP), lambda b, t: (b, t, 0)),
        pl.BlockSpec((1, tile, KP), lambda b, t: (b, t, 0)),
    ] + [_const_spec(a.shape) for a in args[6:]]
    n = tile + POOL_HIST
    return pl.pallas_call(
        functools.partial(_mixer_kernel, tile=tile),
        grid=(B, T // tile),
        in_specs=in_specs,
        out_specs=pl.BlockSpec((1, tile, D), lambda b, t: (b, t, 0)),
        out_shape=jax.ShapeDtypeStruct((B, T, D), _F32),
        scratch_shapes=[
            pltpu.VMEM((tile, N_IN), _F32),
            pltpu.VMEM((tile, POOL_WIDTH + 2 * VW), _BF),
            pltpu.VMEM(((N_GLA_LEVELS + 2) * GLA_CHUNK, (tile // GLA_CHUNK) * KP), _F32),
            pltpu.VMEM((tile, 2 * VW), _F32),
            pltpu.VMEM((n, POOL_WIDTH), _F32),
            pltpu.VMEM((n, POOL_WIDTH), _F32),
            pltpu.VMEM((n, POOL_WIDTH), _F32),
            pltpu.VMEM((n, POOL_WIDTH), _F32),
            pltpu.VMEM((POOL_HIST, POOL_WIDTH), _F32),
            pltpu.VMEM((VW, KP), _F32),
            pltpu.VMEM((VW, KP), _F32),
        ],
        compiler_params=pltpu.CompilerParams(
            dimension_semantics=("arbitrary", "arbitrary"), vmem_limit_bytes=VMEM_LIMIT_BYTES),
        name="token_mixer",
    )(*args)


def _ffn_kernel(x_ref, mod_ref, ng_ref, wup_ref, cw_ref, wd_ref, fg_ref,
                o_ref, h_ref, u_ref, carry_ref, act_ref, *, tile, d_ff, fb, final_norm):
    t_idx = pl.program_id(1)

    @pl.when(t_idx == 0)
    def _():
        carry_ref[...] = jnp.zeros_like(carry_ref)

    x = x_ref[0]
    shift2 = mod_ref[0, 3:4, :]
    scale2 = mod_ref[0, 4:5, :]
    gate2 = mod_ref[0, 5:6, :]
    ms = jnp.mean(x * x, axis=-1, keepdims=True)
    h = (x * lax.rsqrt(ms + EPS)) * (ng_ref[...] * (1.0 + scale2)) + shift2
    h_ref[...] = h.astype(_BF)
    S = SUBLANES

    def up_conv(slot, col):
        cols = slice(col, col + fb)
        u = _dot(h_ref[...], wup_ref[:, cols])
        u_ref[slot, 0:S, :] = carry_ref[:, cols]
        u_ref[slot, S:S + tile, :] = u
        carry_ref[:, cols] = u[tile - S:tile, :]
        cw = cw_ref[:, cols]
        return (u_ref[slot, S - 2:S - 2 + tile, :] * cw[0:1, :]
                + u_ref[slot, S - 1:S - 1 + tile, :] * cw[1:2, :] + u * cw[2:3, :] + cw[3:4, :])

    for j in range(d_ff // fb):
        ya = up_conv(2 * (j % 2), j * fb)
        yg = up_conv(2 * (j % 2) + 1, d_ff + j * fb)
        act_ref[:, j * fb:(j + 1) * fb] = (_silu(yg) * ya).astype(_BF)

    out = x_ref[0] + gate2 * _dot(act_ref[...], wd_ref[...])
    if final_norm:
        ms_o = jnp.mean(out * out, axis=-1, keepdims=True)
        out = out * lax.rsqrt(ms_o + EPS) * fg_ref[...]
    o_ref[0] = out


def _ffn(x, mod, norm_g, wup, cw, wd, final_g, final_norm):
    B, T, D = x.shape
    tile = FFN_TILE if T % FFN_TILE == 0 else T
    d_ff = wd.shape[0]
    fb = FFN_BLOCK
    args = [x, mod, norm_g, wup, cw, wd, final_g]
    in_specs = [
        pl.BlockSpec((1, tile, D), lambda b, t: (b, t, 0)),
        pl.BlockSpec((1, 6, D), lambda b, t: (b, 0, 0)),
    ] + [_const_spec(a.shape) for a in args[2:]]
    return pl.pallas_call(
        functools.partial(_ffn_kernel, tile=tile, d_ff=d_ff, fb=fb, final_norm=final_norm),
        grid=(B, T // tile),
        in_specs=in_specs,
        out_specs=pl.BlockSpec((1, tile, D), lambda b, t: (b, t, 0)),
        out_shape=jax.ShapeDtypeStruct((B, T, D), _F32),
        scratch_shapes=[
            pltpu.VMEM((tile, D), _BF),
            pltpu.VMEM((4, tile + SUBLANES, fb), _F32),
            pltpu.VMEM((SUBLANES, 2 * d_ff), _F32),
            pltpu.VMEM((tile, d_ff), _BF),
        ],
        compiler_params=pltpu.CompilerParams(
            dimension_semantics=("arbitrary", "arbitrary"), vmem_limit_bytes=VMEM_LIMIT_BYTES),
        name="conv_ffn",
    )(*args)


def _prep_mixer_params(w_in, pool_w, pool_scale, gla_wa2, gla_ba, gla_norm_g, w_out):
    D = w_in.shape[0]
    wb = w_in.astype(_BF)
    o_rq = POOL_WIDTH
    o_rk = o_rq + KW
    o_rv = o_rk + KW
    o_rg = o_rv + VW
    o_gq = o_rg + VW
    o_gk = o_gq + KW
    o_gv = o_gk + KW
    o_ga = o_gv + VW
    o_gg = o_ga + GLA_GATE_RANK

    def pad_to(w, width):
        return jnp.pad(w, ((0, 0), (0, width - w.shape[1])))

    def swapped(w):
        w4 = w.reshape(D, N_HEADS, 2, DK // 2)
        return jnp.stack([-w4[:, :, 1, :], w4[:, :, 0, :]], axis=2).reshape(D, KW)

    rq, rk = wb[:, o_rq:o_rk], wb[:, o_rk:o_rv]
    win = jnp.concatenate([
        wb[:, :POOL_WIDTH],
        pad_to(rq, KP), pad_to(swapped(rq), KP), pad_to(rk, KP), pad_to(swapped(rk), KP),
        wb[:, o_rv:o_gq],
        pad_to(wb[:, o_gq:o_gk], KP), pad_to(wb[:, o_gk:o_gv], KP),
        wb[:, o_gv:o_ga], pad_to(wb[:, o_ga:o_gg], GAP), wb[:, o_gg:],
    ], axis=1)
    G = pool_w.shape[0]
    eye = jnp.eye(G, dtype=pool_w.dtype)
    poolw = (pool_w[:, :, None, :] * eye[:, None, :, None]).reshape(POOL_WIDTH, POOL_WIDTH).astype(_BF)
    wa2 = jnp.pad(gla_wa2, ((0, GAP - GLA_GATE_RANK), (0, KP - KW))).astype(_BF)
    ba = jnp.pad(gla_ba, (0, KP - KW)).reshape(1, KP)
    return (win, poolw, pool_scale.reshape(1, POOL_WIDTH), wa2, ba,
            gla_norm_g.reshape(1, VW), w_out.astype(_BF))


def _prep_ffn_params(w_up, conv_w, conv_b, w_down):
    cw = jnp.pad(jnp.concatenate([conv_w, conv_b[None, :]], axis=0),
                 ((0, SUBLANES - CONV_WIDTH - 1), (0, 0)))
    return w_up.astype(_BF), cw, w_down.astype(_BF)


def kernel(x, c, positions, ada_w, ada_b, norm1_g, w_in, pool_w, pool_scale, gla_wa2, gla_ba,
           gla_norm_g, w_out, norm2_g, w_up, conv_w, conv_b, w_down, final_g):
    L = ada_w.shape[0]
    B, T, D = x.shape
    tb = {k: jnp.asarray(v) for k, v in _tables().items()}
    for name in ("head_ind", "gla_sel", "ret_hmk", "ret_hmv", "gla_hmk", "gla_hmv"):
        tb[name] = tb[name].astype(_BF)
    mod = _modulation(c, ada_w, ada_b).reshape(L, B, 6, D)
    cos, sin = _rotary_tables(positions)
    fg = final_g.reshape(1, D)
    for l in range(L):
        mp = _prep_mixer_params(w_in[l], pool_w[l], pool_scale[l], gla_wa2[l], gla_ba[l],
                                gla_norm_g[l], w_out[l])
        x = _mixer(x, mod[l], norm1_g[l].reshape(1, D), mp[0], cos, sin, mp[1], mp[2],
                   mp[3], mp[4], mp[5], mp[6], tb)
        fp = _prep_ffn_params(w_up[l], conv_w[l], conv_b[l], w_down[l])
        x = _ffn(x, mod[l], norm2_g[l].reshape(1, D), *fp, fg, final_norm=(l == L - 1))
    return x
```

```python
import functools

import numpy as np
import jax
import jax.numpy as jnp
from jax import lax
from jax.experimental import pallas as pl
from jax.experimental.pallas import tpu as pltpu

POOL_WIDTH = 256
POOL_WINDOWS = (2, 4, 8, 16)
POOL_GROUP_DIM = 64
N_HEADS = 4
DK = 48
DV = 96
KW = N_HEADS * DK
VW = N_HEADS * DV
RET_CHUNK = 128
GLA_CHUNK = 64
GLA_GATE_RANK = 16
GLA_GATE_TAU = 16.0
ROPE_BASE = 10000.0
CONV_WIDTH = 3
EPS = 1e-6

LANES = 128
SUBLANES = 8
KP = 256
VMEM_LIMIT_BYTES = 56 * 1024 * 1024

OFF_POOL = 0
OFF_RQ = OFF_POOL + POOL_WIDTH
OFF_RK = OFF_RQ + KP
OFF_RV = OFF_RK + KP
OFF_RG = OFF_RV + VW
OFF_GQ = OFF_RG + VW
OFF_GK = OFF_GQ + KP
OFF_GV = OFF_GK + KP
OFF_GG = OFF_GV + VW
N_IN = OFF_GG + VW
GA_LANE = KW

MIX_TILE = 256
FFN_TILE = 512
FFN_BLOCK = 256
ROT_TILE = 512
ROT_ROWS = 32
POOL_HIST = 32
N_GLA_LEVELS = 6

_BF = jnp.bfloat16
_F32 = jnp.float32


def _dot(a, b):
    return jnp.dot(a, b, preferred_element_type=_F32)


def _dot_nt(a, b):
    return lax.dot_general(a, b, (((1,), (1,)), ((), ())), preferred_element_type=_F32)


def _dot_tn(a, b):
    return lax.dot_general(a, b, (((0,), (0,)), ((), ())), preferred_element_type=_F32)


def _split3(x):
    p1 = x.astype(_BF)
    r1 = x - p1.astype(_F32)
    p2 = r1.astype(_BF)
    p3 = (r1 - p2.astype(_F32)).astype(_BF)
    return p1, p2, p3


def _sigmoid(x):
    return 1.0 / (1.0 + jnp.exp(-x))


def _silu(x):
    return x * _sigmoid(x)


def _key_head(d):
    return np.where(d < KW, d // DK, -1)


@functools.lru_cache(maxsize=None)
def _tables():
    t = {}
    kd = _key_head(np.arange(KP))
    vd = np.arange(VW) // DV

    def head_masks(chunk):
        rows = np.repeat(np.arange(N_HEADS), chunk)
        return ((rows[:, None] == kd[None, :]).astype(np.float32),
                (rows[:, None] == vd[None, :]).astype(np.float32))

    C = RET_CHUNK
    lg = np.log(1.0 - 2.0 ** (-5.0 - np.arange(N_HEADS, dtype=np.float64)))
    i = np.arange(C)
    rel = i[:, None] - i[None, :]
    dec = np.where(rel[None] >= 0, np.exp(np.maximum(rel, 0)[None] * lg[:, None, None]), 0.0)
    t["ret_decay"] = np.transpose(dec, (1, 0, 2)).reshape(C, N_HEADS * C).astype(np.float32)
    t["ret_xi"] = np.exp((i[:, None] + 1.0) * lg[vd][None, :]).astype(np.float32)
    zeta = np.exp((C - 1.0 - i)[:, None] * lg[np.maximum(kd, 0)][None, :]) * (kd >= 0)[None, :]
    t["ret_zeta"] = zeta.astype(np.float32)
    t["ret_gc"] = (np.exp(C * lg[np.maximum(kd, 0)]) * (kd >= 0))[None, :].astype(np.float32)
    t["ret_hmk"], t["ret_hmv"] = head_masks(C)
    t["state_mask"] = (vd[:, None] == kd[None, :]).astype(np.float32)
    t["head_ind"] = (vd[:, None] == vd[None, :]).astype(np.float32)

    C = GLA_CHUNK
    i = np.arange(C)
    masks = []
    s = C // 2
    while s >= 1:
        parent = (i // (2 * s)) * (2 * s)
        upper = (i - parent) >= s
        masks.append(upper[:, None] & (~upper)[None, :] & (parent[:, None] == parent[None, :]))
        s //= 2
    masks.append(i[:, None] == i[None, :])
    t["gla_mask"] = np.stack([np.tile(m, (1, N_HEADS)) for m in masks]).astype(np.float32)
    t["gla_tri"] = (i[None, :] <= i[:, None]).astype(np.float32)
    t["gla_hmk"], t["gla_hmv"] = head_masks(C)

    t["pool_win"] = np.repeat(np.asarray(POOL_WINDOWS, np.float32), POOL_GROUP_DIM)[None, :]

    half = DK // 2
    l = np.arange(KP)
    hit = (np.arange(ROT_ROWS)[:, None] == (l % half)[None, :]) & (l < KW)[None, :]
    t["rot_cos_expand"] = hit.astype(np.float32)
    t["rot_sin_expand"] = hit * np.where((l % DK) < half, -1.0, 1.0)[None, :].astype(np.float32)
    t["rot_first_half"] = ((l % DK) < half).astype(np.float32)[None, :]
    return t


def _mod_kernel(c_ref, w_ref, b_ref, o_ref):
    ca = _silu(c_ref[...]).astype(_BF)
    o_ref[...] = _dot(ca, w_ref[...].astype(_BF)) + b_ref[...]


def _modulation(c, ada_w, ada_b):
    L, D, N = ada_w.shape
    B = c.shape[0]
    tn = D
    return pl.pallas_call(
        _mod_kernel,
        grid=(L, N // tn),
        in_specs=[
            pl.BlockSpec((B, D), lambda l, n: (0, 0)),
            pl.BlockSpec((None, D, tn), lambda l, n: (l, 0, n)),
            pl.BlockSpec((None, 1, tn), lambda l, n: (l, 0, n)),
        ],
        out_specs=pl.BlockSpec((None, B, tn), lambda l, n: (l, 0, n)),
        out_shape=jax.ShapeDtypeStruct((L, B, N), _F32),
        compiler_params=pltpu.CompilerParams(
            dimension_semantics=("arbitrary", "arbitrary"), vmem_limit_bytes=VMEM_LIMIT_BYTES),
        name="adaln_modulation",
    )(c, ada_w, ada_b.reshape(L, 1, N))


def _rot_kernel(pos_ref, freq_ref, ec_ref, es_ref, cos_ref, sin_ref):
    ang = freq_ref[...] * pos_ref[0]
    ec = ec_ref[...]
    es = es_ref[...]

    def widen(v, e):
        p1, p2, p3 = _split3(v)
        return _dot_tn(p1, e) + _dot_tn(p2, e) + _dot_tn(p3, e)

    cos_ref[0] = widen(jnp.cos(ang), ec)
    sin_ref[0] = widen(jnp.sin(ang), es)


def _rotary_tables(positions, tb):
    B, T = positions.shape
    tile = ROT_TILE if T % ROT_TILE == 0 else T
    inv_freq = ROPE_BASE ** (-jnp.arange(0, DK, 2, dtype=_F32) / DK)
    freq = jnp.pad(inv_freq, (0, ROT_ROWS - DK // 2)).reshape(ROT_ROWS, 1)
    pos = positions.astype(_F32).reshape(B, 1, T)
    const = lambda b, t: (0, 0)
    return pl.pallas_call(
        _rot_kernel,
        grid=(B, T // tile),
        in_specs=[
            pl.BlockSpec((1, 1, tile), lambda b, t: (b, 0, t)),
            pl.BlockSpec((ROT_ROWS, 1), const),
            pl.BlockSpec((ROT_ROWS, KP), const),
            pl.BlockSpec((ROT_ROWS, KP), const),
        ],
        out_specs=[pl.BlockSpec((1, tile, KP), lambda b, t: (b, t, 0))] * 2,
        out_shape=[jax.ShapeDtypeStruct((B, T, KP), _F32)] * 2,
        compiler_params=pltpu.CompilerParams(dimension_semantics=("arbitrary", "arbitrary")),
        name="rotary_table",
    )(pos, freq, tb["rot_cos_expand"], tb["rot_sin_expand"])


def _mixer_kernel(x_ref, mod_ref, ng_ref, win_ref, cos_ref, sin_ref,
                  poolw_ref, pools_ref, wa2_ref, ba_ref, gng_ref, wout_ref,
                  poolwin_ref, rothalf_ref,
                  rdecay_ref, rxi_ref, rzeta_ref, rgc_ref, rhmk_ref, rhmv_ref,
                  smask_ref, hind_ref,
                  gtri_ref, gmask_ref, ghmk_ref, ghmv_ref,
                  o_ref,
                  proj_ref, ycat_ref, b_ref, dec_ref, oscr_ref, ext_ref, s2_ref, s4_ref, s8_ref,
                  hist_ref, sret_ref, sgla_ref, *, tile):
    t_idx = pl.program_id(1)

    @pl.when(t_idx == 0)
    def _():
        hist_ref[...] = jnp.zeros_like(hist_ref)
        sret_ref[...] = jnp.zeros_like(sret_ref)
        sgla_ref[...] = jnp.zeros_like(sgla_ref)

    x = x_ref[0]
    shift1 = mod_ref[0, 0:1, :]
    scale1 = mod_ref[0, 1:2, :]
    gate1 = mod_ref[0, 2:3, :]
    ms = jnp.mean(x * x, axis=-1, keepdims=True)
    h = (x * lax.rsqrt(ms + EPS)) * (ng_ref[...] * (1.0 + scale1)) + shift1
    hb = h.astype(_BF)
    for off, width in ((OFF_POOL, POOL_WIDTH), (OFF_RQ, 2 * KP), (OFF_RV, VW), (OFF_RG, VW),
                       (OFF_GQ, 2 * KP), (OFF_GV, VW), (OFF_GG, VW)):
        proj_ref[:, off:off + width] = _dot(hb, win_ref[:, off:off + width])

    n = tile + POOL_HIST
    u = proj_ref[:, OFF_POOL:OFF_POOL + POOL_WIDTH]
    ext_ref[0:POOL_HIST, :] = hist_ref[...]
    ext_ref[POOL_HIST:n, :] = u
    hist_ref[...] = u[tile - POOL_HIST:tile, :]
    s2_ref[8:n, :] = ext_ref[8:n, :] + ext_ref[7:n - 1, :]
    s4_ref[16:n, :] = s2_ref[16:n, :] + s2_ref[14:n - 2, :]
    s8_ref[24:n, :] = s4_ref[24:n, :] + s4_ref[20:n - 4, :]
    s16 = s8_ref[32:n, :] + s8_ref[24:n - 8, :]
    lane = lax.broadcasted_iota(jnp.int32, (tile, POOL_WIDTH), 1)
    wsum = jnp.where(lane < POOL_GROUP_DIM, s2_ref[32:n, :],
                     jnp.where(lane < 2 * POOL_GROUP_DIM, s4_ref[32:n, :],
                               jnp.where(lane < 3 * POOL_GROUP_DIM, s8_ref[32:n, :], s16)))
    t_abs = (t_idx * tile + lax.broadcasted_iota(jnp.int32, (tile, POOL_WIDTH), 0)).astype(_F32)
    cnt = jnp.minimum(t_abs + 1.0, poolwin_ref[...])
    pooled = wsum / cnt - u
    ycat_ref[:, 0:POOL_WIDTH] = (_dot(pooled.astype(_BF), poolw_ref[...]) * pools_ref[...]).astype(_BF)

    inv_dv = 1.0 / DV
    k_scale = DK ** -0.5
    smask = smask_ref[...]
    hind = hind_ref[...]

    def head_rms(o):
        ms_h = _dot((o * o).astype(_BF), hind) * inv_dv
        return o * lax.rsqrt(ms_h + EPS)

    def stack_heads(a):
        return jnp.concatenate([a] * N_HEADS, axis=0)

    C = RET_CHUNK
    first_half = rothalf_ref[...] > 0.5

    def rotary(t, cosv, sinv):
        partner = jnp.where(first_half, pltpu.roll(t, KP - DK // 2, axis=1), pltpu.roll(t, DK // 2, axis=1))
        return t * cosv + partner * sinv

    ret_parts = []
    for c in range(tile // C):
        rows = slice(c * C, (c + 1) * C)
        cosv = cos_ref[0, rows, :]
        sinv = sin_ref[0, rows, :]
        q = rotary(proj_ref[rows, OFF_RQ:OFF_RQ + KP], cosv, sinv)
        k = rotary(proj_ref[rows, OFF_RK:OFF_RK + KP], cosv, sinv) * k_scale
        vb = proj_ref[rows, OFF_RV:OFF_RV + VW].astype(_BF)
        qb = q.astype(_BF)
        kbd = stack_heads(k.astype(_BF)) * rhmk_ref[...]
        scores = _dot_nt(qb, kbd) * rdecay_ref[...]
        vbd = stack_heads(vb) * rhmv_ref[...]
        o_intra = _dot(scores.astype(_BF), vbd)
        upd = _dot_tn(vb, (k * rzeta_ref[...]).astype(_BF)) * smask
        ret_parts.append((rows, qb, o_intra, upd))
    state = sret_ref[...]
    for rows, qb, o_intra, upd in ret_parts:
        oscr_ref[rows, 0:VW] = o_intra + _dot_nt(qb, state.astype(_BF)) * rxi_ref[...]
        state = state * rgc_ref[...] + upd
    sret_ref[...] = state

    logits = _dot(proj_ref[:, OFF_RQ:OFF_RQ + KP].astype(_BF), wa2_ref[...]) + ba_ref[...]
    log_sig = jnp.minimum(logits, 0.0) - jnp.log1p(jnp.exp(-jnp.abs(logits)))
    la = log_sig * (1.0 / GLA_GATE_TAU)
    q_scale = DK ** -0.5
    C = GLA_CHUNK
    n_chunks = tile // C
    W = n_chunks * KP

    la_l = jnp.concatenate([la[c * C:(c + 1) * C, :] for c in range(n_chunks)], axis=1)
    tri = gtri_ref[...]
    p1, p2, p3 = _split3(la_l)
    b = _dot(tri, p1) + _dot(tri, p2) + _dot(tri, p3)
    b_ref[...] = b

    row = lax.broadcasted_iota(jnp.int32, (C, W), 0)
    s = C // 2
    lvl = 0
    while s >= SUBLANES // 2:
        refs = [jnp.broadcast_to(b_ref[p + s - 1:p + s, :], (2 * s, W)) for p in range(0, C, 2 * s)]
        b_at_ref = refs[0] if len(refs) == 1 else jnp.concatenate(refs, axis=0)
        dec_ref[lvl * C:(lvl + 1) * C, :] = jnp.exp(-jnp.abs(b - b_at_ref))
        s //= 2
        lvl += 1
    up1 = pltpu.roll(b, 1, axis=0)
    up2 = pltpu.roll(b, 2, axis=0)
    dn1 = pltpu.roll(b, C - 1, axis=0)
    m4 = row % 4
    b_at_ref = jnp.where(m4 == 0, dn1, jnp.where(m4 == 1, b, jnp.where(m4 == 2, up1, up2)))
    dec_ref[lvl * C:(lvl + 1) * C, :] = jnp.exp(-jnp.abs(b - b_at_ref))
    lvl += 1
    b_at_ref = jnp.where(row % 2 == 0, b, up1)
    dec_ref[lvl * C:(lvl + 1) * C, :] = jnp.exp(-jnp.abs(b - b_at_ref))
    lvl += 1
    dec_ref[lvl * C:(lvl + 1) * C, :] = jnp.exp(b)
    b_last = jnp.broadcast_to(b_ref[C - 1:C, :], (C, W))
    dec_ref[(lvl + 1) * C:(lvl + 2) * C, :] = jnp.exp(b_last - b)

    hmk = ghmk_ref[...]
    gla_parts = []
    for c in range(n_chunks):
        rows = slice(c * C, (c + 1) * C)
        lanes = slice(c * KP, (c + 1) * KP)
        q = proj_ref[rows, OFF_GQ:OFF_GQ + KP] * q_scale
        k = proj_ref[rows, OFF_GK:OFF_GK + KP]
        vb = proj_ref[rows, OFF_GV:OFF_GV + VW].astype(_BF)
        scores = jnp.zeros((C, N_HEADS * C), _F32)
        for lvl in range(N_GLA_LEVELS + 1):
            if lvl < N_GLA_LEVELS:
                e = dec_ref[lvl * C:(lvl + 1) * C, lanes]
                ql, kl = q * e, k * e
            else:
                ql, kl = q, k
            kbd = stack_heads(kl.astype(_BF)) * hmk
            scores = scores + _dot_nt(ql.astype(_BF), kbd) * gmask_ref[lvl]
        vbd = stack_heads(vb) * ghmv_ref[...]
        e_cum = dec_ref[N_GLA_LEVELS * C:(N_GLA_LEVELS + 1) * C, lanes]
        e_rev = dec_ref[(N_GLA_LEVELS + 1) * C:(N_GLA_LEVELS + 2) * C, lanes]
        o_intra = _dot(scores.astype(_BF), vbd)
        upd = _dot_tn(vb, (k * e_rev).astype(_BF)) * smask
        gla_parts.append((rows, (q * e_cum).astype(_BF), o_intra, upd, e_cum[C - 1:C, :]))
    state = sgla_ref[...]
    for rows, qe, o_intra, upd, e_last in gla_parts:
        oscr_ref[rows, VW:2 * VW] = o_intra + _dot_nt(qe, state.astype(_BF))
        state = state * e_last + upd
    sgla_ref[...] = state

    y_ret = _silu(proj_ref[:, OFF_RG:OFF_RG + VW]) * head_rms(oscr_ref[:, 0:VW])
    ycat_ref[:, POOL_WIDTH:POOL_WIDTH + VW] = y_ret.astype(_BF)
    y_gla = _silu(proj_ref[:, OFF_GG:OFF_GG + VW]) * (head_rms(oscr_ref[:, VW:2 * VW]) * gng_ref[...])
    ycat_ref[:, POOL_WIDTH + VW:POOL_WIDTH + 2 * VW] = y_gla.astype(_BF)

    o_ref[0] = x_ref[0] + gate1 * _dot(ycat_ref[...], wout_ref[...])


def _whole(shape):
    nd = len(shape)
    return pl.BlockSpec(shape, lambda b, t: (0,) * nd, pipeline_mode=pl.Buffered(1))


def _layer_slab(shape, layer):
    nd = len(shape) - 1
    return pl.BlockSpec((None,) + tuple(shape[1:]), lambda b, t: (layer,) + (0,) * nd,
                        pipeline_mode=pl.Buffered(1))


def _mixer(x, mod, layer, params, cos, sin, tb):
    B, T, D = x.shape
    tile = MIX_TILE if T % MIX_TILE == 0 else T
    consts = [tb[k] for k in ("pool_win", "rot_first_half", "ret_decay", "ret_xi", "ret_zeta", "ret_gc",
                              "ret_hmk", "ret_hmv", "state_mask", "head_ind", "gla_tri", "gla_mask",
                              "gla_hmk", "gla_hmv")]
    norm_g, win = params[0], params[1]
    rest = list(params[2:])
    args = [x, mod, norm_g, win, cos, sin] + rest + consts
    in_specs = [
        pl.BlockSpec((1, tile, D), lambda b, t: (b, t, 0)),
        pl.BlockSpec((None, 1, 6, D), lambda b, t: (layer, b, 0, 0)),
        _layer_slab(norm_g.shape, layer),
        _layer_slab(win.shape, layer),
        pl.BlockSpec((1, tile, KP), lambda b, t: (b, t, 0)),
        pl.BlockSpec((1, tile, KP), lambda b, t: (b, t, 0)),
    ] + [_layer_slab(a.shape, layer) for a in rest] + [_whole(a.shape) for a in consts]
    n = tile + POOL_HIST
    n_chunks = tile // GLA_CHUNK
    return pl.pallas_call(
        functools.partial(_mixer_kernel, tile=tile),
        grid=(B, T // tile),
        in_specs=in_specs,
        out_specs=pl.BlockSpec((1, tile, D), lambda b, t: (b, t, 0)),
        out_shape=jax.ShapeDtypeStruct((B, T, D), _F32),
        scratch_shapes=[
            pltpu.VMEM((tile, N_IN), _F32),
            pltpu.VMEM((tile, POOL_WIDTH + 2 * VW), _BF),
            pltpu.VMEM((GLA_CHUNK, n_chunks * KP), _F32),
            pltpu.VMEM(((N_GLA_LEVELS + 2) * GLA_CHUNK, n_chunks * KP), _F32),
            pltpu.VMEM((tile, 2 * VW), _F32),
            pltpu.VMEM((n, POOL_WIDTH), _F32),
            pltpu.VMEM((n, POOL_WIDTH), _F32),
            pltpu.VMEM((n, POOL_WIDTH), _F32),
            pltpu.VMEM((n, POOL_WIDTH), _F32),
            pltpu.VMEM((POOL_HIST, POOL_WIDTH), _F32),
            pltpu.VMEM((VW, KP), _F32),
            pltpu.VMEM((VW, KP), _F32),
        ],
        compiler_params=pltpu.CompilerParams(
            dimension_semantics=("arbitrary", "arbitrary"), vmem_limit_bytes=VMEM_LIMIT_BYTES),
        name="token_mixer",
    )(*args)


def _ffn_kernel(x_ref, mod_ref, ng_ref, wup_ref, cw_ref, wd_ref, fg_ref,
                o_ref, h_ref, u_ref, carry_ref, act_ref, *, tile, d_ff, fb, final_norm):
    t_idx = pl.program_id(1)

    @pl.when(t_idx == 0)
    def _():
        carry_ref[...] = jnp.zeros_like(carry_ref)

    x = x_ref[0]
    shift2 = mod_ref[0, 3:4, :]
    scale2 = mod_ref[0, 4:5, :]
    gate2 = mod_ref[0, 5:6, :]
    ms = jnp.mean(x * x, axis=-1, keepdims=True)
    h = (x * lax.rsqrt(ms + EPS)) * (ng_ref[...] * (1.0 + scale2)) + shift2
    h_ref[...] = h.astype(_BF)
    S = SUBLANES

    def up_conv(slot, col):
        cols = slice(col, col + fb)
        u = _dot(h_ref[...], wup_ref[:, cols])
        u_ref[slot, 0:S, :] = carry_ref[:, cols]
        u_ref[slot, S:S + tile, :] = u
        carry_ref[:, cols] = u[tile - S:tile, :]
        cw = cw_ref[:, cols]
        return (u_ref[slot, S - 2:S - 2 + tile, :] * cw[0:1, :]
                + u_ref[slot, S - 1:S - 1 + tile, :] * cw[1:2, :] + u * cw[2:3, :] + cw[3:4, :])

    for j in range(d_ff // fb):
        ya = up_conv(2 * (j % 2), j * fb)
        yg = up_conv(2 * (j % 2) + 1, d_ff + j * fb)
        act_ref[:, j * fb:(j + 1) * fb] = (_silu(yg) * ya).astype(_BF)

    out = x_ref[0] + gate2 * _dot(act_ref[...], wd_ref[...])
    if final_norm:
        ms_o = jnp.mean(out * out, axis=-1, keepdims=True)
        out = out * lax.rsqrt(ms_o + EPS) * fg_ref[...]
    o_ref[0] = out


def _ffn(x, mod, layer, params, final_g, final_norm):
    B, T, D = x.shape
    tile = FFN_TILE if T % FFN_TILE == 0 else T
    norm_g, wup, cw, wd = params
    d_ff = wd.shape[1]
    in_specs = [
        pl.BlockSpec((1, tile, D), lambda b, t: (b, t, 0)),
        pl.BlockSpec((None, 1, 6, D), lambda b, t: (layer, b, 0, 0)),
    ] + [_layer_slab(a.shape, layer) for a in params] + [_whole(final_g.shape)]
    return pl.pallas_call(
        functools.partial(_ffn_kernel, tile=tile, d_ff=d_ff, fb=FFN_BLOCK, final_norm=final_norm),
        grid=(B, T // tile),
        in_specs=in_specs,
        out_specs=pl.BlockSpec((1, tile, D), lambda b, t: (b, t, 0)),
        out_shape=jax.ShapeDtypeStruct((B, T, D), _F32),
        scratch_shapes=[
            pltpu.VMEM((tile, D), _BF),
            pltpu.VMEM((4, tile + SUBLANES, FFN_BLOCK), _F32),
            pltpu.VMEM((SUBLANES, 2 * d_ff), _F32),
            pltpu.VMEM((tile, d_ff), _BF),
        ],
        compiler_params=pltpu.CompilerParams(
            dimension_semantics=("arbitrary", "arbitrary"), vmem_limit_bytes=VMEM_LIMIT_BYTES),
        name="conv_ffn",
    )(x, mod, *params, final_g)


def _prep_mixer_params(norm1_g, w_in, pool_w, pool_scale, gla_wa2, gla_ba, gla_norm_g, w_out):
    L, D, _ = w_in.shape
    wb = w_in.astype(_BF)
    o_rq = POOL_WIDTH
    o_rk = o_rq + KW
    o_rv = o_rk + KW
    o_gq = o_rv + 2 * VW
    o_gk = o_gq + KW
    o_gv = o_gk + KW
    o_ga = o_gv + VW
    o_gg = o_ga + GLA_GATE_RANK
    zeros = lambda n: jnp.zeros((L, D, n), _BF)
    pad_k = zeros(KP - KW)
    win = jnp.concatenate([
        wb[:, :, :o_rq],
        wb[:, :, o_rq:o_rk], wb[:, :, o_ga:o_gg], zeros(KP - KW - GLA_GATE_RANK),
        wb[:, :, o_rk:o_rv], pad_k,
        wb[:, :, o_rv:o_gq],
        wb[:, :, o_gq:o_gk], pad_k, wb[:, :, o_gk:o_gv], pad_k,
        wb[:, :, o_gv:o_ga], wb[:, :, o_gg:],
    ], axis=2)
    G = pool_w.shape[1]
    eye = jnp.eye(G, dtype=pool_w.dtype)
    poolw = (pool_w[:, :, :, None, :] * eye[None, :, None, :, None]).reshape(L, POOL_WIDTH, POOL_WIDTH)
    wa2 = jnp.pad(gla_wa2, ((0, 0), (GA_LANE, KP - GA_LANE - GLA_GATE_RANK), (0, KP - KW)))
    ba = jnp.pad(gla_ba, ((0, 0), (0, KP - KW))).reshape(L, 1, KP)
    return (norm1_g.reshape(L, 1, D), win, poolw.astype(_BF), pool_scale.reshape(L, 1, POOL_WIDTH),
            wa2.astype(_BF), ba, gla_norm_g.reshape(L, 1, VW), w_out.astype(_BF))


def _prep_ffn_params(norm2_g, w_up, conv_w, conv_b, w_down):
    L, D, _ = w_up.shape
    cw = jnp.pad(jnp.concatenate([conv_w, conv_b[:, None, :]], axis=1),
                 ((0, 0), (0, SUBLANES - CONV_WIDTH - 1), (0, 0)))
    return norm2_g.reshape(L, 1, D), w_up.astype(_BF), cw, w_down.astype(_BF)


def kernel(x, c, positions, ada_w, ada_b, norm1_g, w_in, pool_w, pool_scale, gla_wa2, gla_ba,
           gla_norm_g, w_out, norm2_g, w_up, conv_w, conv_b, w_down, final_g):
    L = ada_w.shape[0]
    B, T, D = x.shape
    tb = {k: jnp.asarray(v) for k, v in _tables().items()}
    for name in ("head_ind", "gla_tri", "ret_hmk", "ret_hmv", "gla_hmk", "gla_hmv",
                 "rot_cos_expand", "rot_sin_expand"):
        tb[name] = tb[name].astype(_BF)
    mod = _modulation(c, ada_w, ada_b).reshape(L, B, 6, D)
    cos, sin = _rotary_tables(positions, tb)
    mixer_params = _prep_mixer_params(norm1_g, w_in, pool_w, pool_scale, gla_wa2, gla_ba, gla_norm_g, w_out)
    ffn_params = _prep_ffn_params(norm2_g, w_up, conv_w, conv_b, w_down)
    fg = final_g.reshape(1, D)
    for l in range(L):
        x = _mixer(x, mod, l, mixer_params, cos, sin, tb)
        x = _ffn(x, mod, l, ffn_params, fg, final_norm=(l == L - 1))
    return x
```

```python
import functools

import numpy as np
import jax
import jax.numpy as jnp
from jax import lax
from jax.experimental import pallas as pl
from jax.experimental.pallas import tpu as pltpu

POOL_WIDTH = 256
POOL_WINDOWS = (2, 4, 8, 16)
POOL_GROUP_DIM = 64
N_HEADS = 4
DK = 48
DV = 96
KW = N_HEADS * DK
VW = N_HEADS * DV
RET_CHUNK = 128
GLA_CHUNK = 64
GLA_GATE_RANK = 16
GLA_GATE_TAU = 16.0
ROPE_BASE = 10000.0
CONV_WIDTH = 3
EPS = 1e-6

LANES = 128
SUBLANES = 8
KP = 256
VMEM_LIMIT_BYTES = 56 * 1024 * 1024

OFF_POOL = 0
OFF_RQ = OFF_POOL + POOL_WIDTH
OFF_RK = OFF_RQ + KP
OFF_RV = OFF_RK + KP
OFF_RG = OFF_RV + VW
OFF_GQ = OFF_RG + VW
OFF_GK = OFF_GQ + KP
OFF_GV = OFF_GK + KP
OFF_GG = OFF_GV + VW
N_IN = OFF_GG + VW
GA_LANE = KW

MIX_TILE = 512
MIX_SUB = 256
PROJ_PIECE = 256
PROJ_TICKS = 3
FFN_TILE = 512
FFN_SUB = 512
FFN_BLOCK = 256
ROT_TILE = 512
ROT_ROWS = 32
POOL_HIST = 32
N_GLA_LEVELS = 6

_BF = jnp.bfloat16
_F32 = jnp.float32


def _dot(a, b):
    return jnp.dot(a, b, preferred_element_type=_F32)


def _dot_nt(a, b):
    return lax.dot_general(a, b, (((1,), (1,)), ((), ())), preferred_element_type=_F32)


def _dot_tn(a, b):
    return lax.dot_general(a, b, (((0,), (0,)), ((), ())), preferred_element_type=_F32)


def _split3(x):
    p1 = x.astype(_BF)
    r1 = x - p1.astype(_F32)
    p2 = r1.astype(_BF)
    p3 = (r1 - p2.astype(_F32)).astype(_BF)
    return p1, p2, p3


def _sigmoid(x):
    return 1.0 / (1.0 + jnp.exp(-x))


def _silu(x):
    return x * _sigmoid(x)


def _key_head(d):
    return np.where(d < KW, d // DK, -1)


@functools.lru_cache(maxsize=None)
def _tables():
    t = {}
    kd = _key_head(np.arange(KP))
    vd = np.arange(VW) // DV

    def head_masks(chunk):
        rows = np.repeat(np.arange(N_HEADS), chunk)
        return ((rows[:, None] == kd[None, :]).astype(np.float32),
                (rows[:, None] == vd[None, :]).astype(np.float32))

    C = RET_CHUNK
    lg = np.log(1.0 - 2.0 ** (-5.0 - np.arange(N_HEADS, dtype=np.float64)))
    i = np.arange(C)
    rel = i[:, None] - i[None, :]
    dec = np.where(rel[None] >= 0, np.exp(np.maximum(rel, 0)[None] * lg[:, None, None]), 0.0)
    t["ret_decay"] = np.transpose(dec, (1, 0, 2)).reshape(C, N_HEADS * C).astype(np.float32)
    t["ret_xi"] = np.exp((i[:, None] + 1.0) * lg[vd][None, :]).astype(np.float32)
    zeta = np.exp((C - 1.0 - i)[:, None] * lg[np.maximum(kd, 0)][None, :]) * (kd >= 0)[None, :]
    t["ret_zeta"] = zeta.astype(np.float32)
    t["ret_gc"] = (np.exp(C * lg[np.maximum(kd, 0)]) * (kd >= 0))[None, :].astype(np.float32)
    t["ret_hmk"], t["ret_hmv"] = head_masks(C)
    t["state_mask"] = (vd[:, None] == kd[None, :]).astype(np.float32)
    t["head_ind"] = (vd[:, None] == vd[None, :]).astype(np.float32)

    C = GLA_CHUNK
    i = np.arange(C)
    masks = []
    s = C // 2
    while s >= 1:
        parent = (i // (2 * s)) * (2 * s)
        upper = (i - parent) >= s
        masks.append(upper[:, None] & (~upper)[None, :] & (parent[:, None] == parent[None, :]))
        s //= 2
    masks.append(i[:, None] == i[None, :])
    t["gla_mask"] = np.stack([np.tile(m, (1, N_HEADS)) for m in masks]).astype(np.float32)
    t["gla_tri"] = (i[None, :] <= i[:, None]).astype(np.float32)
    t["gla_hmk"], t["gla_hmv"] = head_masks(C)

    t["pool_win"] = np.repeat(np.asarray(POOL_WINDOWS, np.float32), POOL_GROUP_DIM)[None, :]

    half = DK // 2
    l = np.arange(KP)
    hit = (np.arange(ROT_ROWS)[:, None] == (l % half)[None, :]) & (l < KW)[None, :]
    t["rot_cos_expand"] = hit.astype(np.float32)
    t["rot_sin_expand"] = hit * np.where((l % DK) < half, -1.0, 1.0)[None, :].astype(np.float32)
    t["rot_first_half"] = ((l % DK) < half).astype(np.float32)[None, :]
    return t


def _mod_kernel(c_ref, w_ref, b_ref, o_ref):
    ca = _silu(c_ref[...]).astype(_BF)
    o_ref[...] = _dot(ca, w_ref[...].astype(_BF)) + b_ref[...]


def _modulation(c, ada_w, ada_b):
    L, D, N = ada_w.shape
    B = c.shape[0]
    tn = D
    return pl.pallas_call(
        _mod_kernel,
        grid=(L, N // tn),
        in_specs=[
            pl.BlockSpec((B, D), lambda l, n: (0, 0)),
            pl.BlockSpec((None, D, tn), lambda l, n: (l, 0, n)),
            pl.BlockSpec((None, 1, tn), lambda l, n: (l, 0, n)),
        ],
        out_specs=pl.BlockSpec((None, B, tn), lambda l, n: (l, 0, n)),
        out_shape=jax.ShapeDtypeStruct((L, B, N), _F32),
        compiler_params=pltpu.CompilerParams(
            dimension_semantics=("arbitrary", "arbitrary"), vmem_limit_bytes=VMEM_LIMIT_BYTES),
        name="adaln_modulation",
    )(c, ada_w, ada_b.reshape(L, 1, N))


def _rot_kernel(pos_ref, freq_ref, ec_ref, es_ref, cos_ref, sin_ref):
    ang = freq_ref[...] * pos_ref[0]
    ec = ec_ref[...]
    es = es_ref[...]

    def widen(v, e):
        p1, p2, p3 = _split3(v)
        return _dot_tn(p1, e) + _dot_tn(p2, e) + _dot_tn(p3, e)

    cos_ref[0] = widen(jnp.cos(ang), ec)
    sin_ref[0] = widen(jnp.sin(ang), es)


def _rotary_tables(positions, tb):
    B, T = positions.shape
    tile = ROT_TILE if T % ROT_TILE == 0 else T
    inv_freq = ROPE_BASE ** (-jnp.arange(0, DK, 2, dtype=_F32) / DK)
    freq = jnp.pad(inv_freq, (0, ROT_ROWS - DK // 2)).reshape(ROT_ROWS, 1)
    pos = positions.astype(_F32).reshape(B, 1, T)
    const = lambda b, t: (0, 0)
    return pl.pallas_call(
        _rot_kernel,
        grid=(B, T // tile),
        in_specs=[
            pl.BlockSpec((1, 1, tile), lambda b, t: (b, 0, t)),
            pl.BlockSpec((ROT_ROWS, 1), const),
            pl.BlockSpec((ROT_ROWS, KP), const),
            pl.BlockSpec((ROT_ROWS, KP), const),
        ],
        out_specs=[pl.BlockSpec((1, tile, KP), lambda b, t: (b, t, 0))] * 2,
        out_shape=[jax.ShapeDtypeStruct((B, T, KP), _F32)] * 2,
        compiler_params=pltpu.CompilerParams(dimension_semantics=("arbitrary", "arbitrary")),
        name="rotary_table",
    )(pos, freq, tb["rot_cos_expand"], tb["rot_sin_expand"])


def _mixer_kernel(x_ref, mod_ref, ng_ref, win_ref, cos_ref, sin_ref,
                  poolw_ref, pools_ref, wa2_ref, ba_ref, gng_ref, wout_ref,
                  poolwin_ref, rothalf_ref,
                  rdecay_ref, rxi_ref, rzeta_ref, rgc_ref, rhmk_ref, rhmv_ref,
                  smask_ref, hind_ref,
                  gtri_ref, gmask_ref, ghmk_ref, ghmv_ref,
                  o_ref,
                  hb_ref, proj_ref, ycat_ref, b_ref, dec_ref, oscr_ref, ext_ref, s2_ref, s4_ref, s8_ref,
                  hist_ref, sret_ref, sgla_ref, *, tile, sub):
    t_idx = pl.program_id(1)
    n_sub = tile // sub

    @pl.when(t_idx == 0)
    def _():
        hist_ref[...] = jnp.zeros_like(hist_ref)
        sret_ref[...] = jnp.zeros_like(sret_ref)
        sgla_ref[...] = jnp.zeros_like(sgla_ref)

    shift1 = mod_ref[0, 0:1, :]
    scale1 = mod_ref[0, 1:2, :]
    gate1 = mod_ref[0, 2:3, :]
    norm_scale = ng_ref[...] * (1.0 + scale1)

    def projection_tasks(i):
        srows = slice(i * sub, (i + 1) * sub)

        def norm():
            x = x_ref[0, srows, :]
            ms = jnp.mean(x * x, axis=-1, keepdims=True)
            hb_ref[srows, :] = ((x * lax.rsqrt(ms + EPS)) * norm_scale + shift1).astype(_BF)

        def segment(off, width):
            def run():
                proj_ref[srows, off:off + width] = _dot(hb_ref[srows, :], win_ref[:, off:off + width])
            return run

        return [norm] + [segment(off, PROJ_PIECE) for off in range(0, N_IN, PROJ_PIECE)]

    pending = []
    ticks = [0]

    def tick():
        ticks[0] += 1
        if pending and ticks[0] % PROJ_TICKS == 0:
            pending.pop(0)()

    def pooling(i):
        n = sub + POOL_HIST
        srows = slice(i * sub, (i + 1) * sub)
        u = proj_ref[srows, OFF_POOL:OFF_POOL + POOL_WIDTH]
        ext_ref[0:POOL_HIST, :] = hist_ref[...]
        ext_ref[POOL_HIST:n, :] = u
        hist_ref[...] = u[sub - POOL_HIST:sub, :]
        s2_ref[8:n, :] = ext_ref[8:n, :] + ext_ref[7:n - 1, :]
        s4_ref[16:n, :] = s2_ref[16:n, :] + s2_ref[14:n - 2, :]
        s8_ref[24:n, :] = s4_ref[24:n, :] + s4_ref[20:n - 4, :]
        s16 = s8_ref[32:n, :] + s8_ref[24:n - 8, :]
        lane = lax.broadcasted_iota(jnp.int32, (sub, POOL_WIDTH), 1)
        wsum = jnp.where(lane < POOL_GROUP_DIM, s2_ref[32:n, :],
                         jnp.where(lane < 2 * POOL_GROUP_DIM, s4_ref[32:n, :],
                                   jnp.where(lane < 3 * POOL_GROUP_DIM, s8_ref[32:n, :], s16)))
        t_abs = (t_idx * tile + i * sub
                 + lax.broadcasted_iota(jnp.int32, (sub, POOL_WIDTH), 0)).astype(_F32)
        cnt = jnp.minimum(t_abs + 1.0, poolwin_ref[...])
        pooled = wsum / cnt - u
        ycat_ref[srows, 0:POOL_WIDTH] = (_dot(pooled.astype(_BF), poolw_ref[...])
                                         * pools_ref[...]).astype(_BF)

    inv_dv = 1.0 / DV
    k_scale = DK ** -0.5
    q_scale = DK ** -0.5
    smask = smask_ref[...]
    hind = hind_ref[...]
    first_half = rothalf_ref[...] > 0.5

    def head_rms(o):
        ms_h = _dot((o * o).astype(_BF), hind) * inv_dv
        return o * lax.rsqrt(ms_h + EPS)

    def stack_heads(a):
        return jnp.concatenate([a] * N_HEADS, axis=0)

    def rotary(t, cosv, sinv):
        partner = jnp.where(first_half, pltpu.roll(t, KP - DK // 2, axis=1), pltpu.roll(t, DK // 2, axis=1))
        return t * cosv + partner * sinv

    def retention(r0, state):
        C = RET_CHUNK
        parts = []
        for c in range(sub // C):
            rows = slice(r0 + c * C, r0 + (c + 1) * C)
            cosv = cos_ref[0, rows, :]
            sinv = sin_ref[0, rows, :]
            q = rotary(proj_ref[rows, OFF_RQ:OFF_RQ + KP], cosv, sinv)
            k = rotary(proj_ref[rows, OFF_RK:OFF_RK + KP], cosv, sinv) * k_scale
            vb = proj_ref[rows, OFF_RV:OFF_RV + VW].astype(_BF)
            qb = q.astype(_BF)
            kbd = stack_heads(k.astype(_BF)) * rhmk_ref[...]
            scores = _dot_nt(qb, kbd) * rdecay_ref[...]
            vbd = stack_heads(vb) * rhmv_ref[...]
            o_intra = _dot(scores.astype(_BF), vbd)
            upd = _dot_tn(vb, (k * rzeta_ref[...]).astype(_BF)) * smask
            parts.append((rows, qb, o_intra, upd))
            tick()
        for rows, qb, o_intra, upd in parts:
            oscr_ref[rows, 0:VW] = o_intra + _dot_nt(qb, state.astype(_BF)) * rxi_ref[...]
            state = state * rgc_ref[...] + upd
            tick()
        return state

    def gla(i, r0, state):
        C = GLA_CHUNK
        n_chunks = sub // C
        W = n_chunks * KP
        srows = slice(r0, r0 + sub)
        logits = _dot(proj_ref[srows, OFF_RQ:OFF_RQ + KP].astype(_BF), wa2_ref[...]) + ba_ref[...]
        log_sig = jnp.minimum(logits, 0.0) - jnp.log1p(jnp.exp(-jnp.abs(logits)))
        la = log_sig * (1.0 / GLA_GATE_TAU)

        la_l = jnp.concatenate([la[c * C:(c + 1) * C, :] for c in range(n_chunks)], axis=1)
        tri = gtri_ref[...]
        p1, p2, p3 = _split3(la_l)
        b = _dot(tri, p1) + _dot(tri, p2) + _dot(tri, p3)
        b_ref[i] = b

        row = lax.broadcasted_iota(jnp.int32, (C, W), 0)
        s = C // 2
        lvl = 0
        while s >= SUBLANES // 2:
            refs = [jnp.broadcast_to(b_ref[i, p + s - 1:p + s, :], (2 * s, W)) for p in range(0, C, 2 * s)]
            b_at_ref = refs[0] if len(refs) == 1 else jnp.concatenate(refs, axis=0)
            dec_ref[i, lvl * C:(lvl + 1) * C, :] = jnp.exp(-jnp.abs(b - b_at_ref))
            s //= 2
            lvl += 1
            tick()
        up1 = pltpu.roll(b, 1, axis=0)
        up2 = pltpu.roll(b, 2, axis=0)
        dn1 = pltpu.roll(b, C - 1, axis=0)
        m4 = row % 4
        b_at_ref = jnp.where(m4 == 0, dn1, jnp.where(m4 == 1, b, jnp.where(m4 == 2, up1, up2)))
        dec_ref[i, lvl * C:(lvl + 1) * C, :] = jnp.exp(-jnp.abs(b - b_at_ref))
        lvl += 1
        tick()
        b_at_ref = jnp.where(row % 2 == 0, b, up1)
        dec_ref[i, lvl * C:(lvl + 1) * C, :] = jnp.exp(-jnp.abs(b - b_at_ref))
        lvl += 1
        tick()
        dec_ref[i, lvl * C:(lvl + 1) * C, :] = jnp.exp(b)
        tick()
        b_last = jnp.broadcast_to(b_ref[i, C - 1:C, :], (C, W))
        dec_ref[i, (lvl + 1) * C:(lvl + 2) * C, :] = jnp.exp(b_last - b)
        tick()

        hmk = ghmk_ref[...]
        parts = []
        for c in range(n_chunks):
            rows = slice(r0 + c * C, r0 + (c + 1) * C)
            lanes = slice(c * KP, (c + 1) * KP)
            q = proj_ref[rows, OFF_GQ:OFF_GQ + KP] * q_scale
            k = proj_ref[rows, OFF_GK:OFF_GK + KP]
            vb = proj_ref[rows, OFF_GV:OFF_GV + VW].astype(_BF)
            scores = jnp.zeros((C, N_HEADS * C), _F32)
            for lvl in range(N_GLA_LEVELS + 1):
                if lvl < N_GLA_LEVELS:
                    e = dec_ref[i, lvl * C:(lvl + 1) * C, lanes]
                    ql, kl = q * e, k * e
                else:
                    ql, kl = q, k
                kbd = stack_heads(kl.astype(_BF)) * hmk
                scores = scores + _dot_nt(ql.astype(_BF), kbd) * gmask_ref[lvl]
                tick()
            vbd = stack_heads(vb) * ghmv_ref[...]
            e_cum = dec_ref[i, N_GLA_LEVELS * C:(N_GLA_LEVELS + 1) * C, lanes]
            e_rev = dec_ref[i, (N_GLA_LEVELS + 1) * C:(N_GLA_LEVELS + 2) * C, lanes]
            o_intra = _dot(scores.astype(_BF), vbd)
            upd = _dot_tn(vb, (k * e_rev).astype(_BF)) * smask
            parts.append((rows, (q * e_cum).astype(_BF), o_intra, upd, e_cum[C - 1:C, :]))
        for rows, qe, o_intra, upd, e_last in parts:
            oscr_ref[rows, VW:2 * VW] = o_intra + _dot_nt(qe, state.astype(_BF))
            state = state * e_last + upd
            tick()
        return state

    for task in projection_tasks(0):
        task()
    ret_state = sret_ref[...]
    gla_state = sgla_ref[...]
    for i in range(n_sub):
        r0 = i * sub
        srows = slice(r0, r0 + sub)
        pending[:] = projection_tasks(i + 1) if i + 1 < n_sub else []
        tick()
        pooling(i)
        tick()
        ret_state = retention(r0, ret_state)
        gla_state = gla(i, r0, gla_state)
        while pending:
            pending.pop(0)()
        y_ret = _silu(proj_ref[srows, OFF_RG:OFF_RG + VW]) * head_rms(oscr_ref[srows, 0:VW])
        ycat_ref[srows, POOL_WIDTH:POOL_WIDTH + VW] = y_ret.astype(_BF)
        y_gla = (_silu(proj_ref[srows, OFF_GG:OFF_GG + VW])
                 * (head_rms(oscr_ref[srows, VW:2 * VW]) * gng_ref[...]))
        ycat_ref[srows, POOL_WIDTH + VW:POOL_WIDTH + 2 * VW] = y_gla.astype(_BF)
        o_ref[0, srows, :] = x_ref[0, srows, :] + gate1 * _dot(ycat_ref[srows, :], wout_ref[...])
    sret_ref[...] = ret_state
    sgla_ref[...] = gla_state


def _whole(shape):
    nd = len(shape)
    return pl.BlockSpec(shape, lambda b, t: (0,) * nd, pipeline_mode=pl.Buffered(1))


def _layer_slab(shape, layer):
    nd = len(shape) - 1
    return pl.BlockSpec((None,) + tuple(shape[1:]), lambda b, t: (layer,) + (0,) * nd,
                        pipeline_mode=pl.Buffered(1))


def _mixer(x, mod, layer, params, cos, sin, tb):
    B, T, D = x.shape
    tile = MIX_TILE if T % MIX_TILE == 0 else T
    sub = MIX_SUB if tile % MIX_SUB == 0 else tile
    consts = [tb[k] for k in ("pool_win", "rot_first_half", "ret_decay", "ret_xi", "ret_zeta", "ret_gc",
                              "ret_hmk", "ret_hmv", "state_mask", "head_ind", "gla_tri", "gla_mask",
                              "gla_hmk", "gla_hmv")]
    norm_g, win = params[0], params[1]
    rest = list(params[2:])
    args = [x, mod, norm_g, win, cos, sin] + rest + consts
    in_specs = [
        pl.BlockSpec((1, tile, D), lambda b, t: (b, t, 0)),
        pl.BlockSpec((None, 1, 6, D), lambda b, t: (layer, b, 0, 0)),
        _layer_slab(norm_g.shape, layer),
        _layer_slab(win.shape, layer),
        pl.BlockSpec((1, tile, KP), lambda b, t: (b, t, 0)),
        pl.BlockSpec((1, tile, KP), lambda b, t: (b, t, 0)),
    ] + [_layer_slab(a.shape, layer) for a in rest] + [_whole(a.shape) for a in consts]
    n = sub + POOL_HIST
    n_sub = tile // sub
    gla_lanes = (sub // GLA_CHUNK) * KP
    return pl.pallas_call(
        functools.partial(_mixer_kernel, tile=tile, sub=sub),
        grid=(B, T // tile),
        in_specs=in_specs,
        out_specs=pl.BlockSpec((1, tile, D), lambda b, t: (b, t, 0)),
        out_shape=jax.ShapeDtypeStruct((B, T, D), _F32),
        scratch_shapes=[
            pltpu.VMEM((tile, D), _BF),
            pltpu.VMEM((tile, N_IN), _F32),
            pltpu.VMEM((tile, POOL_WIDTH + 2 * VW), _BF),
            pltpu.VMEM((n_sub, GLA_CHUNK, gla_lanes), _F32),
            pltpu.VMEM((n_sub, (N_GLA_LEVELS + 2) * GLA_CHUNK, gla_lanes), _F32),
            pltpu.VMEM((tile, 2 * VW), _F32),
            pltpu.VMEM((n, POOL_WIDTH), _F32),
            pltpu.VMEM((n, POOL_WIDTH), _F32),
            pltpu.VMEM((n, POOL_WIDTH), _F32),
            pltpu.VMEM((n, POOL_WIDTH), _F32),
            pltpu.VMEM((POOL_HIST, POOL_WIDTH), _F32),
            pltpu.VMEM((VW, KP), _F32),
            pltpu.VMEM((VW, KP), _F32),
        ],
        compiler_params=pltpu.CompilerParams(
            dimension_semantics=("arbitrary", "arbitrary"), vmem_limit_bytes=VMEM_LIMIT_BYTES),
        name="token_mixer",
    )(*args)


def _ffn_kernel(x_ref, mod_ref, ng_ref, wup_ref, cw_ref, wd_ref, fg_ref,
                o_ref, h_ref, u_ref, carry_ref, act_ref, *, tile, sub, d_ff, fb, final_norm):
    t_idx = pl.program_id(1)
    n_sub = tile // sub

    @pl.when(t_idx == 0)
    def _():
        carry_ref[...] = jnp.zeros_like(carry_ref)

    shift2 = mod_ref[0, 3:4, :]
    scale2 = mod_ref[0, 4:5, :]
    gate2 = mod_ref[0, 5:6, :]
    norm_scale = ng_ref[...] * (1.0 + scale2)
    sub_rows = [slice(i * sub, (i + 1) * sub) for i in range(n_sub)]

    def norm(i):
        x = x_ref[0, sub_rows[i], :]
        ms = jnp.mean(x * x, axis=-1, keepdims=True)
        h_ref[sub_rows[i], :] = ((x * lax.rsqrt(ms + EPS)) * norm_scale + shift2).astype(_BF)

    def down(i):
        out = x_ref[0, sub_rows[i], :] + gate2 * _dot(act_ref[sub_rows[i], :], wd_ref[...])
        if final_norm:
            ms_o = jnp.mean(out * out, axis=-1, keepdims=True)
            out = out * lax.rsqrt(ms_o + EPS) * fg_ref[...]
        o_ref[0, sub_rows[i], :] = out

    S = SUBLANES

    def up_conv(srows, slot, col):
        cols = slice(col, col + fb)
        u = _dot(h_ref[srows, :], wup_ref[:, cols])
        u_ref[slot, 0:S, :] = carry_ref[:, cols]
        u_ref[slot, S:S + sub, :] = u
        carry_ref[:, cols] = u[sub - S:sub, :]
        cw = cw_ref[:, cols]
        return (u_ref[slot, S - 2:S - 2 + sub, :] * cw[0:1, :]
                + u_ref[slot, S - 1:S - 1 + sub, :] * cw[1:2, :] + u * cw[2:3, :] + cw[3:4, :])

    norm(0)
    for i in range(n_sub):
        for j in range(d_ff // fb):
            slot = 4 * i + 2 * (j % 2)
            ya = up_conv(sub_rows[i], slot, j * fb)
            yg = up_conv(sub_rows[i], slot + 1, d_ff + j * fb)
            act_ref[sub_rows[i], j * fb:(j + 1) * fb] = (_silu(yg) * ya).astype(_BF)
            if j == 0:
                if i + 1 < n_sub:
                    norm(i + 1)
                if i > 0:
                    down(i - 1)
    down(n_sub - 1)


def _ffn(x, mod, layer, params, final_g, final_norm):
    B, T, D = x.shape
    tile = FFN_TILE if T % FFN_TILE == 0 else T
    sub = FFN_SUB if tile % FFN_SUB == 0 else tile
    norm_g, wup, cw, wd = params
    d_ff = wd.shape[1]
    in_specs = [
        pl.BlockSpec((1, tile, D), lambda b, t: (b, t, 0)),
        pl.BlockSpec((None, 1, 6, D), lambda b, t: (layer, b, 0, 0)),
    ] + [_layer_slab(a.shape, layer) for a in params] + [_whole(final_g.shape)]
    return pl.pallas_call(
        functools.partial(_ffn_kernel, tile=tile, sub=sub, d_ff=d_ff, fb=FFN_BLOCK, final_norm=final_norm),
        grid=(B, T // tile),
        in_specs=in_specs,
        out_specs=pl.BlockSpec((1, tile, D), lambda b, t: (b, t, 0)),
        out_shape=jax.ShapeDtypeStruct((B, T, D), _F32),
        scratch_shapes=[
            pltpu.VMEM((tile, D), _BF),
            pltpu.VMEM((4 * (tile // sub), sub + SUBLANES, FFN_BLOCK), _F32),
            pltpu.VMEM((SUBLANES, 2 * d_ff), _F32),
            pltpu.VMEM((tile, d_ff), _BF),
        ],
        compiler_params=pltpu.CompilerParams(
            dimension_semantics=("arbitrary", "arbitrary"), vmem_limit_bytes=VMEM_LIMIT_BYTES),
        name="conv_ffn",
    )(x, mod, *params, final_g)


def _prep_mixer_params(norm1_g, w_in, pool_w, pool_scale, gla_wa2, gla_ba, gla_norm_g, w_out):
    L, D, _ = w_in.shape
    wb = w_in.astype(_BF)
    o_rq = POOL_WIDTH
    o_rk = o_rq + KW
    o_rv = o_rk + KW
    o_gq = o_rv + 2 * VW
    o_gk = o_gq + KW
    o_gv = o_gk + KW
    o_ga = o_gv + VW
    o_gg = o_ga + GLA_GATE_RANK
    zeros = lambda n: jnp.zeros((L, D, n), _BF)
    pad_k = zeros(KP - KW)
    win = jnp.concatenate([
        wb[:, :, :o_rq],
        wb[:, :, o_rq:o_rk], wb[:, :, o_ga:o_gg], zeros(KP - KW - GLA_GATE_RANK),
        wb[:, :, o_rk:o_rv], pad_k,
        wb[:, :, o_rv:o_gq],
        wb[:, :, o_gq:o_gk], pad_k, wb[:, :, o_gk:o_gv], pad_k,
        wb[:, :, o_gv:o_ga], wb[:, :, o_gg:],
    ], axis=2)
    G = pool_w.shape[1]
    eye = jnp.eye(G, dtype=pool_w.dtype)
    poolw = (pool_w[:, :, :, None, :] * eye[None, :, None, :, None]).reshape(L, POOL_WIDTH, POOL_WIDTH)
    wa2 = jnp.pad(gla_wa2, ((0, 0), (GA_LANE, KP - GA_LANE - GLA_GATE_RANK), (0, KP - KW)))
    ba = jnp.pad(gla_ba, ((0, 0), (0, KP - KW))).reshape(L, 1, KP)
    return (norm1_g.reshape(L, 1, D), win, poolw.astype(_BF), pool_scale.reshape(L, 1, POOL_WIDTH),
            wa2.astype(_BF), ba, gla_norm_g.reshape(L, 1, VW), w_out.astype(_BF))


def _prep_ffn_params(norm2_g, w_up, conv_w, conv_b, w_down):
    L, D, _ = w_up.shape
    cw = jnp.pad(jnp.concatenate([conv_w, conv_b[:, None, :]], axis=1),
                 ((0, 0), (0, SUBLANES - CONV_WIDTH - 1), (0, 0)))
    return norm2_g.reshape(L, 1, D), w_up.astype(_BF), cw, w_down.astype(_BF)


def kernel(x, c, positions, ada_w, ada_b, norm1_g, w_in, pool_w, pool_scale, gla_wa2, gla_ba,
           gla_norm_g, w_out, norm2_g, w_up, conv_w, conv_b, w_down, final_g):
    L = ada_w.shape[0]
    B, T, D = x.shape
    tb = {k: jnp.asarray(v) for k, v in _tables().items()}
    for name in ("head_ind", "gla_tri", "ret_hmk", "ret_hmv", "gla_hmk", "gla_hmv",
                 "rot_cos_expand", "rot_sin_expand"):
        tb[name] = tb[name].astype(_BF)
    mod = _modulation(c, ada_w, ada_b).reshape(L, B, 6, D)
    cos, sin = _rotary_tables(positions, tb)
    mixer_params = _prep_mixer_params(norm1_g, w_in, pool_w, pool_scale, gla_wa2, gla_ba, gla_norm_g, w_out)
    ffn_params = _prep_ffn_params(norm2_g, w_up, conv_w, conv_b, w_down)
    fg = final_g.reshape(1, D)
    for l in range(L):
        x = _mixer(x, mod, l, mixer_params, cos, sin, tb)
        x = _ffn(x, mod, l, ffn_params, fg, final_norm=(l == L - 1))
    return x
```

```python
import functools

import numpy as np
import jax
import jax.numpy as jnp
from jax import lax
from jax.experimental import pallas as pl
from jax.experimental.pallas import tpu as pltpu

POOL_WIDTH = 256
POOL_WINDOWS = (2, 4, 8, 16)
POOL_GROUP_DIM = 64
N_HEADS = 4
DK = 48
DV = 96
KW = N_HEADS * DK
VW = N_HEADS * DV
RET_CHUNK = 128
GLA_CHUNK = 64
GLA_GATE_RANK = 16
GLA_GATE_TAU = 16.0
ROPE_BASE = 10000.0
CONV_WIDTH = 3
EPS = 1e-6

LANES = 128
SUBLANES = 8
KP = 256
VMEM_LIMIT_BYTES = 56 * 1024 * 1024

OFF_POOL = 0
OFF_RQ = OFF_POOL + POOL_WIDTH
OFF_RK = OFF_RQ + KP
OFF_RV = OFF_RK + KP
OFF_RG = OFF_RV + VW
OFF_GQ = OFF_RG + VW
OFF_GK = OFF_GQ + KP
OFF_GV = OFF_GK + KP
OFF_GG = OFF_GV + VW
N_IN = OFF_GG + VW
GA_LANE = KW

MIX_TILE = 512
MIX_SUB = 256
PROJ_PIECE = 256
PROJ_TICKS = 3
FFN_TILE = 512
FFN_BLOCK = 256
ROT_TILE = 512
ROT_ROWS = 32
POOL_HIST = 32
N_GLA_LEVELS = 6

_BF = jnp.bfloat16
_F32 = jnp.float32


def _dot(a, b):
    return jnp.dot(a, b, preferred_element_type=_F32)


def _dot_nt(a, b):
    return lax.dot_general(a, b, (((1,), (1,)), ((), ())), preferred_element_type=_F32)


def _dot_tn(a, b):
    return lax.dot_general(a, b, (((0,), (0,)), ((), ())), preferred_element_type=_F32)


def _split3(x):
    p1 = x.astype(_BF)
    r1 = x - p1.astype(_F32)
    p2 = r1.astype(_BF)
    p3 = (r1 - p2.astype(_F32)).astype(_BF)
    return p1, p2, p3


def _sigmoid(x):
    return 1.0 / (1.0 + jnp.exp(-x))


def _silu(x):
    return x * _sigmoid(x)


def _key_head(d):
    return np.where(d < KW, d // DK, -1)


@functools.lru_cache(maxsize=None)
def _tables():
    t = {}
    kd = _key_head(np.arange(KP))
    vd = np.arange(VW) // DV

    def head_masks(chunk):
        rows = np.repeat(np.arange(N_HEADS), chunk)
        return ((rows[:, None] == kd[None, :]).astype(np.float32),
                (rows[:, None] == vd[None, :]).astype(np.float32))

    C = RET_CHUNK
    lg = np.log(1.0 - 2.0 ** (-5.0 - np.arange(N_HEADS, dtype=np.float64)))
    i = np.arange(C)
    rel = i[:, None] - i[None, :]
    dec = np.where(rel[None] >= 0, np.exp(np.maximum(rel, 0)[None] * lg[:, None, None]), 0.0)
    t["ret_decay"] = np.transpose(dec, (1, 0, 2)).reshape(C, N_HEADS * C).astype(np.float32)
    t["ret_xi"] = np.exp((i[:, None] + 1.0) * lg[vd][None, :]).astype(np.float32)
    zeta = np.exp((C - 1.0 - i)[:, None] * lg[np.maximum(kd, 0)][None, :]) * (kd >= 0)[None, :]
    t["ret_zeta"] = zeta.astype(np.float32)
    t["ret_gc"] = (np.exp(C * lg[np.maximum(kd, 0)]) * (kd >= 0))[None, :].astype(np.float32)
    t["ret_hmk"], t["ret_hmv"] = head_masks(C)
    t["state_mask"] = (vd[:, None] == kd[None, :]).astype(np.float32)
    t["head_ind"] = (vd[:, None] == vd[None, :]).astype(np.float32)

    C = GLA_CHUNK
    i = np.arange(C)
    masks = []
    s = C // 2
    while s >= 1:
        parent = (i // (2 * s)) * (2 * s)
        upper = (i - parent) >= s
        masks.append(upper[:, None] & (~upper)[None, :] & (parent[:, None] == parent[None, :]))
        s //= 2
    masks.append(i[:, None] == i[None, :])
    t["gla_mask"] = np.stack([np.tile(m, (1, N_HEADS)) for m in masks]).astype(np.float32)
    t["gla_tri"] = (i[None, :] <= i[:, None]).astype(np.float32)
    t["gla_hmk"], t["gla_hmv"] = head_masks(C)

    t["pool_win"] = np.repeat(np.asarray(POOL_WINDOWS, np.float32), POOL_GROUP_DIM)[None, :]

    half = DK // 2
    l = np.arange(KP)
    hit = (np.arange(ROT_ROWS)[:, None] == (l % half)[None, :]) & (l < KW)[None, :]
    t["rot_cos_expand"] = hit.astype(np.float32)
    t["rot_sin_expand"] = hit * np.where((l % DK) < half, -1.0, 1.0)[None, :].astype(np.float32)
    t["rot_first_half"] = ((l % DK) < half).astype(np.float32)[None, :]
    return t


def _mod_kernel(c_ref, w_ref, b_ref, o_ref):
    ca = _silu(c_ref[...]).astype(_BF)
    o_ref[...] = _dot(ca, w_ref[...].astype(_BF)) + b_ref[...]


def _modulation(c, ada_w, ada_b):
    L, D, N = ada_w.shape
    B = c.shape[0]
    tn = D
    return pl.pallas_call(
        _mod_kernel,
        grid=(L, N // tn),
        in_specs=[
            pl.BlockSpec((B, D), lambda l, n: (0, 0)),
            pl.BlockSpec((None, D, tn), lambda l, n: (l, 0, n)),
            pl.BlockSpec((None, 1, tn), lambda l, n: (l, 0, n)),
        ],
        out_specs=pl.BlockSpec((None, B, tn), lambda l, n: (l, 0, n)),
        out_shape=jax.ShapeDtypeStruct((L, B, N), _F32),
        compiler_params=pltpu.CompilerParams(
            dimension_semantics=("arbitrary", "arbitrary"), vmem_limit_bytes=VMEM_LIMIT_BYTES),
        name="adaln_modulation",
    )(c, ada_w, ada_b.reshape(L, 1, N))


def _rot_kernel(pos_ref, freq_ref, ec_ref, es_ref, cos_ref, sin_ref):
    ang = freq_ref[...] * pos_ref[0]
    ec = ec_ref[...]
    es = es_ref[...]

    def widen(v, e):
        p1, p2, p3 = _split3(v)
        return _dot_tn(p1, e) + _dot_tn(p2, e) + _dot_tn(p3, e)

    cos_ref[0] = widen(jnp.cos(ang), ec)
    sin_ref[0] = widen(jnp.sin(ang), es)


def _rotary_tables(positions, tb):
    B, T = positions.shape
    tile = ROT_TILE if T % ROT_TILE == 0 else T
    inv_freq = ROPE_BASE ** (-jnp.arange(0, DK, 2, dtype=_F32) / DK)
    freq = jnp.pad(inv_freq, (0, ROT_ROWS - DK // 2)).reshape(ROT_ROWS, 1)
    pos = positions.astype(_F32).reshape(B, 1, T)
    const = lambda b, t: (0, 0)
    return pl.pallas_call(
        _rot_kernel,
        grid=(B, T // tile),
        in_specs=[
            pl.BlockSpec((1, 1, tile), lambda b, t: (b, 0, t)),
            pl.BlockSpec((ROT_ROWS, 1), const),
            pl.BlockSpec((ROT_ROWS, KP), const),
            pl.BlockSpec((ROT_ROWS, KP), const),
        ],
        out_specs=[pl.BlockSpec((1, tile, KP), lambda b, t: (b, t, 0))] * 2,
        out_shape=[jax.ShapeDtypeStruct((B, T, KP), _F32)] * 2,
        compiler_params=pltpu.CompilerParams(dimension_semantics=("arbitrary", "arbitrary")),
        name="rotary_table",
    )(pos, freq, tb["rot_cos_expand"], tb["rot_sin_expand"])


def _mixer_kernel(x_ref, mod_ref, xn_ref, modn_ref, ng_ref, win_ref, cos_ref, sin_ref,
                  poolw_ref, pools_ref, wa2_ref, ba_ref, gng_ref, wout_ref,
                  poolwin_ref, rothalf_ref,
                  rdecay_ref, rxi_ref, rzeta_ref, rgc_ref, rhmk_ref, rhmv_ref,
                  smask_ref, hind_ref,
                  gtri_ref, gmask_ref, ghmk_ref, ghmv_ref,
                  o_ref,
                  hb_ref, proj_ref, ycat_ref, b_ref, dec_ref, oscr_ref, ext_ref, s2_ref, s4_ref, s8_ref,
                  hist_ref, sret_ref, sgla_ref, *, tile, sub):
    t_idx = pl.program_id(1)
    n_sub = tile // sub
    assert n_sub % 2 == 0

    @pl.when(t_idx == 0)
    def _():
        hist_ref[...] = jnp.zeros_like(hist_ref)
        sret_ref[...] = jnp.zeros_like(sret_ref)
        sgla_ref[...] = jnp.zeros_like(sgla_ref)

    gate1 = mod_ref[0, 2:3, :]

    def projection_tasks(i):
        slot = i % 2
        if i < n_sub:
            read_x = lambda: x_ref[0, i * sub:(i + 1) * sub, :]
            m_ref = mod_ref
        else:
            read_x = lambda: xn_ref[0]
            m_ref = modn_ref

        def norm():
            x = read_x()
            ms = jnp.mean(x * x, axis=-1, keepdims=True)
            norm_scale = ng_ref[...] * (1.0 + m_ref[0, 1:2, :])
            hb_ref[slot] = ((x * lax.rsqrt(ms + EPS)) * norm_scale + m_ref[0, 0:1, :]).astype(_BF)

        def segment(off, width):
            def run():
                proj_ref[slot, :, off:off + width] = _dot(hb_ref[slot], win_ref[:, off:off + width])
            return run

        return [norm] + [segment(off, PROJ_PIECE) for off in range(0, N_IN, PROJ_PIECE)]

    @pl.when((pl.program_id(0) == 0) & (t_idx == 0))
    def _():
        for task in projection_tasks(0):
            task()

    pending = []
    ticks = [0]

    def tick():
        ticks[0] += 1
        if pending and ticks[0] % PROJ_TICKS == 0:
            pending.pop(0)()

    def pooling(i):
        n = sub + POOL_HIST
        srows = slice(i * sub, (i + 1) * sub)
        u = proj_ref[i % 2, :, OFF_POOL:OFF_POOL + POOL_WIDTH]
        ext_ref[0:POOL_HIST, :] = hist_ref[...]
        ext_ref[POOL_HIST:n, :] = u
        hist_ref[...] = u[sub - POOL_HIST:sub, :]
        s2_ref[8:n, :] = ext_ref[8:n, :] + ext_ref[7:n - 1, :]
        s4_ref[16:n, :] = s2_ref[16:n, :] + s2_ref[14:n - 2, :]
        s8_ref[24:n, :] = s4_ref[24:n, :] + s4_ref[20:n - 4, :]
        s16 = s8_ref[32:n, :] + s8_ref[24:n - 8, :]
        lane = lax.broadcasted_iota(jnp.int32, (sub, POOL_WIDTH), 1)
        wsum = jnp.where(lane < POOL_GROUP_DIM, s2_ref[32:n, :],
                         jnp.where(lane < 2 * POOL_GROUP_DIM, s4_ref[32:n, :],
                                   jnp.where(lane < 3 * POOL_GROUP_DIM, s8_ref[32:n, :], s16)))
        t_abs = (t_idx * tile + i * sub
                 + lax.broadcasted_iota(jnp.int32, (sub, POOL_WIDTH), 0)).astype(_F32)
        cnt = jnp.minimum(t_abs + 1.0, poolwin_ref[...])
        pooled = wsum / cnt - u
        ycat_ref[srows, 0:POOL_WIDTH] = (_dot(pooled.astype(_BF), poolw_ref[...])
                                         * pools_ref[...]).astype(_BF)

    inv_dv = 1.0 / DV
    k_scale = DK ** -0.5
    q_scale = DK ** -0.5
    smask = smask_ref[...]
    hind = hind_ref[...]
    first_half = rothalf_ref[...] > 0.5

    def head_rms(o):
        ms_h = _dot((o * o).astype(_BF), hind) * inv_dv
        return o * lax.rsqrt(ms_h + EPS)

    def stack_heads(a):
        return jnp.concatenate([a] * N_HEADS, axis=0)

    def rotary(t, cosv, sinv):
        partner = jnp.where(first_half, pltpu.roll(t, KP - DK // 2, axis=1), pltpu.roll(t, DK // 2, axis=1))
        return t * cosv + partner * sinv

    def retention(i, r0, state):
        C = RET_CHUNK
        pv = proj_ref.at[i % 2]
        parts = []
        for c in range(sub // C):
            rows = slice(r0 + c * C, r0 + (c + 1) * C)
            lrows = slice(c * C, (c + 1) * C)
            cosv = cos_ref[0, rows, :]
            sinv = sin_ref[0, rows, :]
            q = rotary(pv[lrows, OFF_RQ:OFF_RQ + KP], cosv, sinv)
            k = rotary(pv[lrows, OFF_RK:OFF_RK + KP], cosv, sinv) * k_scale
            vb = pv[lrows, OFF_RV:OFF_RV + VW].astype(_BF)
            qb = q.astype(_BF)
            kbd = stack_heads(k.astype(_BF)) * rhmk_ref[...]
            scores = _dot_nt(qb, kbd) * rdecay_ref[...]
            vbd = stack_heads(vb) * rhmv_ref[...]
            o_intra = _dot(scores.astype(_BF), vbd)
            upd = _dot_tn(vb, (k * rzeta_ref[...]).astype(_BF)) * smask
            parts.append((rows, qb, o_intra, upd))
            tick()
        for rows, qb, o_intra, upd in parts:
            oscr_ref[rows, 0:VW] = o_intra + _dot_nt(qb, state.astype(_BF)) * rxi_ref[...]
            state = state * rgc_ref[...] + upd
            tick()
        return state

    def gla(i, r0, state):
        C = GLA_CHUNK
        n_chunks = sub // C
        W = n_chunks * KP
        pv = proj_ref.at[i % 2]
        logits = _dot(pv[:, OFF_RQ:OFF_RQ + KP].astype(_BF), wa2_ref[...]) + ba_ref[...]
        log_sig = jnp.minimum(logits, 0.0) - jnp.log1p(jnp.exp(-jnp.abs(logits)))
        la = log_sig * (1.0 / GLA_GATE_TAU)

        la_l = jnp.concatenate([la[c * C:(c + 1) * C, :] for c in range(n_chunks)], axis=1)
        tri = gtri_ref[...]
        p1, p2, p3 = _split3(la_l)
        b = _dot(tri, p1) + _dot(tri, p2) + _dot(tri, p3)
        b_ref[i] = b

        row = lax.broadcasted_iota(jnp.int32, (C, W), 0)
        s = C // 2
        lvl = 0
        while s >= SUBLANES // 2:
            refs = [jnp.broadcast_to(b_ref[i, p + s - 1:p + s, :], (2 * s, W)) for p in range(0, C, 2 * s)]
            b_at_ref = refs[0] if len(refs) == 1 else jnp.concatenate(refs, axis=0)
            dec_ref[i, lvl * C:(lvl + 1) * C, :] = jnp.exp(-jnp.abs(b - b_at_ref))
            s //= 2
            lvl += 1
            tick()
        up1 = pltpu.roll(b, 1, axis=0)
        up2 = pltpu.roll(b, 2, axis=0)
        dn1 = pltpu.roll(b, C - 1, axis=0)
        m4 = row % 4
        b_at_ref = jnp.where(m4 == 0, dn1, jnp.where(m4 == 1, b, jnp.where(m4 == 2, up1, up2)))
        dec_ref[i, lvl * C:(lvl + 1) * C, :] = jnp.exp(-jnp.abs(b - b_at_ref))
        lvl += 1
        tick()
        b_at_ref = jnp.where(row % 2 == 0, b, up1)
        dec_ref[i, lvl * C:(lvl + 1) * C, :] = jnp.exp(-jnp.abs(b - b_at_ref))
        lvl += 1
        tick()
        dec_ref[i, lvl * C:(lvl + 1) * C, :] = jnp.exp(b)
        tick()
        b_last = jnp.broadcast_to(b_ref[i, C - 1:C, :], (C, W))
        dec_ref[i, (lvl + 1) * C:(lvl + 2) * C, :] = jnp.exp(b_last - b)
        tick()

        hmk = ghmk_ref[...]
        parts = []
        for c in range(n_chunks):
            rows = slice(r0 + c * C, r0 + (c + 1) * C)
            lrows = slice(c * C, (c + 1) * C)
            lanes = slice(c * KP, (c + 1) * KP)
            q = pv[lrows, OFF_GQ:OFF_GQ + KP] * q_scale
            k = pv[lrows, OFF_GK:OFF_GK + KP]
            vb = pv[lrows, OFF_GV:OFF_GV + VW].astype(_BF)
            scores = jnp.zeros((C, N_HEADS * C), _F32)
            for lvl in range(N_GLA_LEVELS + 1):
                if lvl < N_GLA_LEVELS:
                    e = dec_ref[i, lvl * C:(lvl + 1) * C, lanes]
                    ql, kl = q * e, k * e
                else:
                    ql, kl = q, k
                kbd = stack_heads(kl.astype(_BF)) * hmk
                scores = scores + _dot_nt(ql.astype(_BF), kbd) * gmask_ref[lvl]
                tick()
            vbd = stack_heads(vb) * ghmv_ref[...]
            e_cum = dec_ref[i, N_GLA_LEVELS * C:(N_GLA_LEVELS + 1) * C, lanes]
            e_rev = dec_ref[i, (N_GLA_LEVELS + 1) * C:(N_GLA_LEVELS + 2) * C, lanes]
            o_intra = _dot(scores.astype(_BF), vbd)
            upd = _dot_tn(vb, (k * e_rev).astype(_BF)) * smask
            parts.append((rows, (q * e_cum).astype(_BF), o_intra, upd, e_cum[C - 1:C, :]))
        for rows, qe, o_intra, upd, e_last in parts:
            oscr_ref[rows, VW:2 * VW] = o_intra + _dot_nt(qe, state.astype(_BF))
            state = state * e_last + upd
            tick()
        return state

    ret_state = sret_ref[...]
    gla_state = sgla_ref[...]
    for i in range(n_sub):
        r0 = i * sub
        srows = slice(r0, r0 + sub)
        pending[:] = projection_tasks(i + 1)
        tick()
        pooling(i)
        tick()
        ret_state = retention(i, r0, ret_state)
        gla_state = gla(i, r0, gla_state)
        y_ret = _silu(proj_ref[i % 2, :, OFF_RG:OFF_RG + VW]) * head_rms(oscr_ref[srows, 0:VW])
        ycat_ref[srows, POOL_WIDTH:POOL_WIDTH + VW] = y_ret.astype(_BF)
        y_gla = (_silu(proj_ref[i % 2, :, OFF_GG:OFF_GG + VW])
                 * (head_rms(oscr_ref[srows, VW:2 * VW]) * gng_ref[...]))
        while pending:
            pending.pop(0)()
        ycat_ref[srows, POOL_WIDTH + VW:POOL_WIDTH + 2 * VW] = y_gla.astype(_BF)
        o_ref[0, srows, :] = x_ref[0, srows, :] + gate1 * _dot(ycat_ref[srows, :], wout_ref[...])
    sret_ref[...] = ret_state
    sgla_ref[...] = gla_state


def _whole(shape):
    nd = len(shape)
    return pl.BlockSpec(shape, lambda b, t: (0,) * nd, pipeline_mode=pl.Buffered(1))


def _layer_slab(shape, layer):
    nd = len(shape) - 1
    return pl.BlockSpec((None,) + tuple(shape[1:]), lambda b, t: (layer,) + (0,) * nd,
                        pipeline_mode=pl.Buffered(1))


def _mixer(x, mod, layer, params, cos, sin, tb):
    B, T, D = x.shape
    tile = MIX_TILE if T % MIX_TILE == 0 else T
    sub = MIX_SUB if tile % MIX_SUB == 0 else tile
    consts = [tb[k] for k in ("pool_win", "rot_first_half", "ret_decay", "ret_xi", "ret_zeta", "ret_gc",
                              "ret_hmk", "ret_hmv", "state_mask", "head_ind", "gla_tri", "gla_mask",
                              "gla_hmk", "gla_hmv")]
    norm_g, win = params[0], params[1]
    rest = list(params[2:])
    n_sub = tile // sub
    n_tiles = T // tile

    def next_tile(b, t):
        flat = jnp.minimum(b * n_tiles + t + 1, B * n_tiles - 1)
        return flat // n_tiles, flat % n_tiles

    def next_x(b, t):
        nb, nt = next_tile(b, t)
        return nb, nt * n_sub, 0

    def next_mod(b, t):
        return layer, next_tile(b, t)[0], 0, 0

    args = [x, mod, x, mod, norm_g, win, cos, sin] + rest + consts
    in_specs = [
        pl.BlockSpec((1, tile, D), lambda b, t: (b, t, 0)),
        pl.BlockSpec((None, 1, 6, D), lambda b, t: (layer, b, 0, 0)),
        pl.BlockSpec((1, sub, D), next_x),
        pl.BlockSpec((None, 1, 6, D), next_mod),
        _layer_slab(norm_g.shape, layer),
        _layer_slab(win.shape, layer),
        pl.BlockSpec((1, tile, KP), lambda b, t: (b, t, 0)),
        pl.BlockSpec((1, tile, KP), lambda b, t: (b, t, 0)),
    ] + [_layer_slab(a.shape, layer) for a in rest] + [_whole(a.shape) for a in consts]
    n = sub + POOL_HIST
    gla_lanes = (sub // GLA_CHUNK) * KP
    return pl.pallas_call(
        functools.partial(_mixer_kernel, tile=tile, sub=sub),
        grid=(B, T // tile),
        in_specs=in_specs,
        out_specs=pl.BlockSpec((1, tile, D), lambda b, t: (b, t, 0)),
        out_shape=jax.ShapeDtypeStruct((B, T, D), _F32),
        scratch_shapes=[
            pltpu.VMEM((2, sub, D), _BF),
            pltpu.VMEM((2, sub, N_IN), _F32),
            pltpu.VMEM((tile, POOL_WIDTH + 2 * VW), _BF),
            pltpu.VMEM((n_sub, GLA_CHUNK, gla_lanes), _F32),
            pltpu.VMEM((n_sub, (N_GLA_LEVELS + 2) * GLA_CHUNK, gla_lanes), _F32),
            pltpu.VMEM((tile, 2 * VW), _F32),
            pltpu.VMEM((n, POOL_WIDTH), _F32),
            pltpu.VMEM((n, POOL_WIDTH), _F32),
            pltpu.VMEM((n, POOL_WIDTH), _F32),
            pltpu.VMEM((n, POOL_WIDTH), _F32),
            pltpu.VMEM((POOL_HIST, POOL_WIDTH), _F32),
            pltpu.VMEM((VW, KP), _F32),
            pltpu.VMEM((VW, KP), _F32),
        ],
        compiler_params=pltpu.CompilerParams(
            dimension_semantics=("arbitrary", "arbitrary"), vmem_limit_bytes=VMEM_LIMIT_BYTES),
        name="token_mixer",
    )(*args)


def _ffn_kernel(x_ref, mod_ref, ng_ref, wup_ref, cw_ref, wd_ref, fg_ref,
                o_ref, h_ref, carry_ref, act_ref, *, tile, d_ff, fb, final_norm):
    t_idx = pl.program_id(1)
    S = SUBLANES

    @pl.when(t_idx == 0)
    def _():
        carry_ref[...] = jnp.zeros_like(carry_ref)

    shift2 = mod_ref[0, 3:4, :]
    scale2 = mod_ref[0, 4:5, :]
    gate2 = mod_ref[0, 5:6, :]
    norm_scale = ng_ref[...] * (1.0 + scale2)
    x = pltpu.einshape("(sv)d->(vs)d", x_ref[0], s=S)
    ms = jnp.mean(x * x, axis=-1, keepdims=True)
    h_ref[...] = ((x * lax.rsqrt(ms + EPS)) * norm_scale + shift2).astype(_BF)
    first_sublane = lax.broadcasted_iota(jnp.int32, (S, fb), 0) == 0

    def up_conv(col):
        cols = slice(col, col + fb)
        u = _dot(h_ref[...], wup_ref[:, cols])
        back1 = jnp.where(first_sublane, pltpu.roll(carry_ref[S:2 * S, cols], 1, axis=0),
                          pltpu.roll(u[tile - S:tile, :], 1, axis=0))
        back2 = jnp.where(first_sublane, pltpu.roll(carry_ref[0:S, cols], 1, axis=0),
                          pltpu.roll(u[tile - 2 * S:tile - S, :], 1, axis=0))
        carry_ref[:, cols] = u[tile - 2 * S:tile, :]
        prev1 = jnp.concatenate([back1, u[0:tile - S, :]], axis=0)
        prev2 = jnp.concatenate([back2, back1, u[0:tile - 2 * S, :]], axis=0)
        cw = cw_ref[:, cols]
        return prev2 * cw[0:1, :] + prev1 * cw[1:2, :] + u * cw[2:3, :] + cw[3:4, :]

    for j in range(d_ff // fb):
        ya = up_conv(j * fb)
        yg = up_conv(d_ff + j * fb)
        act_ref[:, j * fb:(j + 1) * fb] = (_silu(yg) * ya).astype(_BF)

    out = x + gate2 * _dot(act_ref[...], wd_ref[...])
    if final_norm:
        ms_o = jnp.mean(out * out, axis=-1, keepdims=True)
        out = out * lax.rsqrt(ms_o + EPS) * fg_ref[...]
    o_ref[0] = pltpu.einshape("(vs)d->(sv)d", out, s=S)


def _ffn(x, mod, layer, params, final_g, final_norm):
    B, T, D = x.shape
    tile = FFN_TILE if T % FFN_TILE == 0 else T
    norm_g, wup, cw, wd = params
    d_ff = wd.shape[1]
    in_specs = [
        pl.BlockSpec((1, tile, D), lambda b, t: (b, t, 0)),
        pl.BlockSpec((None, 1, 6, D), lambda b, t: (layer, b, 0, 0)),
    ] + [_layer_slab(a.shape, layer) for a in params] + [_whole(final_g.shape)]
    return pl.pallas_call(
        functools.partial(_ffn_kernel, tile=tile, d_ff=d_ff, fb=FFN_BLOCK, final_norm=final_norm),
        grid=(B, T // tile),
        in_specs=in_specs,
        out_specs=pl.BlockSpec((1, tile, D), lambda b, t: (b, t, 0)),
        out_shape=jax.ShapeDtypeStruct((B, T, D), _F32),
        scratch_shapes=[
            pltpu.VMEM((tile, D), _BF),
            pltpu.VMEM((2 * SUBLANES, 2 * d_ff), _F32),
            pltpu.VMEM((tile, d_ff), _BF),
        ],
        compiler_params=pltpu.CompilerParams(
            dimension_semantics=("arbitrary", "arbitrary"), vmem_limit_bytes=VMEM_LIMIT_BYTES),
        name="conv_ffn",
    )(x, mod, *params, final_g)


def _prep_mixer_params(norm1_g, w_in, pool_w, pool_scale, gla_wa2, gla_ba, gla_norm_g, w_out):
    L, D, _ = w_in.shape
    wb = w_in.astype(_BF)
    o_rq = POOL_WIDTH
    o_rk = o_rq + KW
    o_rv = o_rk + KW
    o_gq = o_rv + 2 * VW
    o_gk = o_gq + KW
    o_gv = o_gk + KW
    o_ga = o_gv + VW
    o_gg = o_ga + GLA_GATE_RANK
    zeros = lambda n: jnp.zeros((L, D, n), _BF)
    pad_k = zeros(KP - KW)
    win = jnp.concatenate([
        wb[:, :, :o_rq],
        wb[:, :, o_rq:o_rk], wb[:, :, o_ga:o_gg], zeros(KP - KW - GLA_GATE_RANK),
        wb[:, :, o_rk:o_rv], pad_k,
        wb[:, :, o_rv:o_gq],
        wb[:, :, o_gq:o_gk], pad_k, wb[:, :, o_gk:o_gv], pad_k,
        wb[:, :, o_gv:o_ga], wb[:, :, o_gg:],
    ], axis=2)
    G = pool_w.shape[1]
    eye = jnp.eye(G, dtype=pool_w.dtype)
    poolw = (pool_w[:, :, :, None, :] * eye[None, :, None, :, None]).reshape(L, POOL_WIDTH, POOL_WIDTH)
    wa2 = jnp.pad(gla_wa2, ((0, 0), (GA_LANE, KP - GA_LANE - GLA_GATE_RANK), (0, KP - KW)))
    ba = jnp.pad(gla_ba, ((0, 0), (0, KP - KW))).reshape(L, 1, KP)
    return (norm1_g.reshape(L, 1, D), win, poolw.astype(_BF), pool_scale.reshape(L, 1, POOL_WIDTH),
            wa2.astype(_BF), ba, gla_norm_g.reshape(L, 1, VW), w_out.astype(_BF))


def _prep_ffn_params(norm2_g, w_up, conv_w, conv_b, w_down):
    L, D, _ = w_up.shape
    cw = jnp.pad(jnp.concatenate([conv_w, conv_b[:, None, :]], axis=1),
                 ((0, 0), (0, SUBLANES - CONV_WIDTH - 1), (0, 0)))
    return norm2_g.reshape(L, 1, D), w_up.astype(_BF), cw, w_down.astype(_BF)


def kernel(x, c, positions, ada_w, ada_b, norm1_g, w_in, pool_w, pool_scale, gla_wa2, gla_ba,
           gla_norm_g, w_out, norm2_g, w_up, conv_w, conv_b, w_down, final_g):
    L = ada_w.shape[0]
    B, T, D = x.shape
    tb = {k: jnp.asarray(v) for k, v in _tables().items()}
    for name in ("head_ind", "gla_tri", "ret_hmk", "ret_hmv", "gla_hmk", "gla_hmv",
                 "rot_cos_expand", "rot_sin_expand"):
        tb[name] = tb[name].astype(_BF)
    mod = _modulation(c, ada_w, ada_b).reshape(L, B, 6, D)
    cos, sin = _rotary_tables(positions, tb)
    mixer_params = _prep_mixer_params(norm1_g, w_in, pool_w, pool_scale, gla_wa2, gla_ba, gla_norm_g, w_out)
    ffn_params = _prep_ffn_params(norm2_g, w_up, conv_w, conv_b, w_down)
    fg = final_g.reshape(1, D)
    for l in range(L):
        x = _mixer(x, mod, l, mixer_params, cos, sin, tb)
        x = _ffn(x, mod, l, ffn_params, fg, final_norm=(l == L - 1))
    return x
```

```python
import functools

import numpy as np
import jax
import jax.numpy as jnp
from jax import lax
from jax.experimental import pallas as pl
from jax.experimental.pallas import tpu as pltpu

POOL_WIDTH = 256
POOL_WINDOWS = (2, 4, 8, 16)
POOL_GROUP_DIM = 64
N_HEADS = 4
DK = 48
DV = 96
KW = N_HEADS * DK
VW = N_HEADS * DV
RET_CHUNK = 128
GLA_CHUNK = 64
GLA_GATE_RANK = 16
GLA_GATE_TAU = 16.0
ROPE_BASE = 10000.0
CONV_WIDTH = 3
EPS = 1e-6

LANES = 128
SUBLANES = 8
KP = 256
VMEM_LIMIT_BYTES = 56 * 1024 * 1024

OFF_POOL = 0
OFF_RQ = OFF_POOL + POOL_WIDTH
OFF_RK = OFF_RQ + KP
OFF_RV = OFF_RK + KP
OFF_RG = OFF_RV + VW
OFF_GQ = OFF_RG + VW
OFF_GK = OFF_GQ + KP
OFF_GV = OFF_GK + KP
OFF_GG = OFF_GV + VW
N_IN = OFF_GG + VW
GA_LANE = KW

MIX_TILE = 1024
MIX_SUB = 256
PROJ_PIECE = 256
PROJ_TICKS = 3
FFN_TILE = 512
FFN_BLOCK = 256
ROT_TILE = 512
ROT_ROWS = 32
POOL_HIST = 32
N_GLA_LEVELS = 6

_BF = jnp.bfloat16
_F32 = jnp.float32


def _dot(a, b):
    return jnp.dot(a, b, preferred_element_type=_F32)


def _dot_nt(a, b):
    return lax.dot_general(a, b, (((1,), (1,)), ((), ())), preferred_element_type=_F32)


def _dot_tn(a, b):
    return lax.dot_general(a, b, (((0,), (0,)), ((), ())), preferred_element_type=_F32)


def _split3(x):
    p1 = x.astype(_BF)
    r1 = x - p1.astype(_F32)
    p2 = r1.astype(_BF)
    p3 = (r1 - p2.astype(_F32)).astype(_BF)
    return p1, p2, p3


def _sigmoid(x):
    return 1.0 / (1.0 + jnp.exp(-x))


def _silu(x):
    return x * _sigmoid(x)


def _key_head(d):
    return np.where(d < KW, d // DK, -1)


@functools.lru_cache(maxsize=None)
def _tables():
    t = {}
    kd = _key_head(np.arange(KP))
    vd = np.arange(VW) // DV

    def head_masks(chunk):
        rows = np.repeat(np.arange(N_HEADS), chunk)
        return ((rows[:, None] == kd[None, :]).astype(np.float32),
                (rows[:, None] == vd[None, :]).astype(np.float32))

    C = RET_CHUNK
    lg = np.log(1.0 - 2.0 ** (-5.0 - np.arange(N_HEADS, dtype=np.float64)))
    i = np.arange(C)
    rel = i[:, None] - i[None, :]
    dec = np.where(rel[None] >= 0, np.exp(np.maximum(rel, 0)[None] * lg[:, None, None]), 0.0)
    t["ret_decay"] = np.transpose(dec, (1, 0, 2)).reshape(C, N_HEADS * C).astype(np.float32)
    t["ret_xi"] = np.exp((i[:, None] + 1.0) * lg[vd][None, :]).astype(np.float32)
    zeta = np.exp((C - 1.0 - i)[:, None] * lg[np.maximum(kd, 0)][None, :]) * (kd >= 0)[None, :]
    t["ret_zeta"] = zeta.astype(np.float32)
    t["ret_gc"] = (np.exp(C * lg[np.maximum(kd, 0)]) * (kd >= 0))[None, :].astype(np.float32)
    t["ret_hmk"], t["ret_hmv"] = head_masks(C)
    t["state_mask"] = (vd[:, None] == kd[None, :]).astype(np.float32)
    t["head_ind"] = (vd[:, None] == vd[None, :]).astype(np.float32)

    C = GLA_CHUNK
    i = np.arange(C)
    masks = []
    s = C // 2
    while s >= 1:
        parent = (i // (2 * s)) * (2 * s)
        upper = (i - parent) >= s
        masks.append(upper[:, None] & (~upper)[None, :] & (parent[:, None] == parent[None, :]))
        s //= 2
    masks.append(i[:, None] == i[None, :])
    t["gla_mask"] = np.stack([np.tile(m, (1, N_HEADS)) for m in masks]).astype(np.float32)
    t["gla_tri"] = (i[None, :] <= i[:, None]).astype(np.float32)
    t["gla_hmk"], t["gla_hmv"] = head_masks(C)

    t["pool_win"] = np.repeat(np.asarray(POOL_WINDOWS, np.float32), POOL_GROUP_DIM)[None, :]

    half = DK // 2
    l = np.arange(KP)
    hit = (np.arange(ROT_ROWS)[:, None] == (l % half)[None, :]) & (l < KW)[None, :]
    t["rot_cos_expand"] = hit.astype(np.float32)
    t["rot_sin_expand"] = hit * np.where((l % DK) < half, -1.0, 1.0)[None, :].astype(np.float32)
    t["rot_first_half"] = ((l % DK) < half).astype(np.float32)[None, :]
    return t


def _mod_kernel(c_ref, w_ref, b_ref, o_ref):
    ca = _silu(c_ref[...]).astype(_BF)
    o_ref[...] = _dot(ca, w_ref[...].astype(_BF)) + b_ref[...]


def _modulation(c, ada_w, ada_b):
    L, D, N = ada_w.shape
    B = c.shape[0]
    tn = D
    return pl.pallas_call(
        _mod_kernel,
        grid=(L, N // tn),
        in_specs=[
            pl.BlockSpec((B, D), lambda l, n: (0, 0)),
            pl.BlockSpec((None, D, tn), lambda l, n: (l, 0, n)),
            pl.BlockSpec((None, 1, tn), lambda l, n: (l, 0, n)),
        ],
        out_specs=pl.BlockSpec((None, B, tn), lambda l, n: (l, 0, n)),
        out_shape=jax.ShapeDtypeStruct((L, B, N), _F32),
        compiler_params=pltpu.CompilerParams(
            dimension_semantics=("arbitrary", "arbitrary"), vmem_limit_bytes=VMEM_LIMIT_BYTES),
        name="adaln_modulation",
    )(c, ada_w, ada_b.reshape(L, 1, N))


def _rot_kernel(pos_ref, freq_ref, ec_ref, es_ref, cos_ref, sin_ref):
    ang = freq_ref[...] * pos_ref[0]
    ec = ec_ref[...]
    es = es_ref[...]

    def widen(v, e):
        p1, p2, p3 = _split3(v)
        return _dot_tn(p1, e) + _dot_tn(p2, e) + _dot_tn(p3, e)

    cos_ref[0] = widen(jnp.cos(ang), ec)
    sin_ref[0] = widen(jnp.sin(ang), es)


def _rotary_tables(positions, tb):
    B, T = positions.shape
    tile = ROT_TILE if T % ROT_TILE == 0 else T
    inv_freq = ROPE_BASE ** (-jnp.arange(0, DK, 2, dtype=_F32) / DK)
    freq = jnp.pad(inv_freq, (0, ROT_ROWS - DK // 2)).reshape(ROT_ROWS, 1)
    pos = positions.astype(_F32).reshape(B, 1, T)
    const = lambda b, t: (0, 0)
    return pl.pallas_call(
        _rot_kernel,
        grid=(B, T // tile),
        in_specs=[
            pl.BlockSpec((1, 1, tile), lambda b, t: (b, 0, t)),
            pl.BlockSpec((ROT_ROWS, 1), const),
            pl.BlockSpec((ROT_ROWS, KP), const),
            pl.BlockSpec((ROT_ROWS, KP), const),
        ],
        out_specs=[pl.BlockSpec((1, tile, KP), lambda b, t: (b, t, 0))] * 2,
        out_shape=[jax.ShapeDtypeStruct((B, T, KP), _F32)] * 2,
        compiler_params=pltpu.CompilerParams(dimension_semantics=("arbitrary", "arbitrary")),
        name="rotary_table",
    )(pos, freq, tb["rot_cos_expand"], tb["rot_sin_expand"])


def _mixer_kernel(x_ref, mod_ref, ng_ref, win_ref, cos_ref, sin_ref,
                  poolw_ref, pools_ref, wa2_ref, ba_ref, gng_ref, wout_ref,
                  poolwin_ref, rothalf_ref,
                  rdecay_ref, rxi_ref, rzeta_ref, rgc_ref, rhmk_ref, rhmv_ref,
                  smask_ref, hind_ref,
                  gtri_ref, gmask_ref, ghmk_ref, ghmv_ref,
                  o_ref,
                  hb_ref, proj_ref, ycat_ref, b_ref, dec_ref, oscr_ref, ext_ref, s2_ref, s4_ref, s8_ref,
                  hist_ref, sret_ref, sgla_ref, *, tile, sub):
    t_idx = pl.program_id(1)
    n_sub = tile // sub

    @pl.when(t_idx == 0)
    def _():
        hist_ref[...] = jnp.zeros_like(hist_ref)
        sret_ref[...] = jnp.zeros_like(sret_ref)
        sgla_ref[...] = jnp.zeros_like(sgla_ref)

    gate1 = mod_ref[0, 2:3, :]

    norm_scale = ng_ref[...] * (1.0 + mod_ref[0, 1:2, :])
    shift1 = mod_ref[0, 0:1, :]

    def projection_tasks(i):
        slot = i % 2

        def norm():
            x = x_ref[0, i * sub:(i + 1) * sub, :]
            ms = jnp.mean(x * x, axis=-1, keepdims=True)
            hb_ref[slot] = ((x * lax.rsqrt(ms + EPS)) * norm_scale + shift1).astype(_BF)

        def segment(off, width):
            def run():
                proj_ref[slot, :, off:off + width] = _dot(hb_ref[slot], win_ref[:, off:off + width])
            return run

        return [norm] + [segment(off, PROJ_PIECE) for off in range(0, N_IN, PROJ_PIECE)]

    for task in projection_tasks(0):
        task()

    pending = []
    ticks = [0]

    def tick():
        ticks[0] += 1
        if pending and ticks[0] % PROJ_TICKS == 0:
            pending.pop(0)()

    def pooling(i):
        n = sub + POOL_HIST
        srows = slice(i * sub, (i + 1) * sub)
        u = proj_ref[i % 2, :, OFF_POOL:OFF_POOL + POOL_WIDTH]
        ext_ref[0:POOL_HIST, :] = hist_ref[...]
        ext_ref[POOL_HIST:n, :] = u
        hist_ref[...] = u[sub - POOL_HIST:sub, :]
        s2_ref[8:n, :] = ext_ref[8:n, :] + ext_ref[7:n - 1, :]
        s4_ref[16:n, :] = s2_ref[16:n, :] + s2_ref[14:n - 2, :]
        s8_ref[24:n, :] = s4_ref[24:n, :] + s4_ref[20:n - 4, :]
        s16 = s8_ref[32:n, :] + s8_ref[24:n - 8, :]
        lane = lax.broadcasted_iota(jnp.int32, (sub, POOL_WIDTH), 1)
        wsum = jnp.where(lane < POOL_GROUP_DIM, s2_ref[32:n, :],
                         jnp.where(lane < 2 * POOL_GROUP_DIM, s4_ref[32:n, :],
                                   jnp.where(lane < 3 * POOL_GROUP_DIM, s8_ref[32:n, :], s16)))
        t_abs = (t_idx * tile + i * sub
                 + lax.broadcasted_iota(jnp.int32, (sub, POOL_WIDTH), 0)).astype(_F32)
        cnt = jnp.minimum(t_abs + 1.0, poolwin_ref[...])
        pooled = wsum / cnt - u
        ycat_ref[srows, 0:POOL_WIDTH] = (_dot(pooled.astype(_BF), poolw_ref[...])
                                         * pools_ref[...]).astype(_BF)

    inv_dv = 1.0 / DV
    k_scale = DK ** -0.5
    q_scale = DK ** -0.5
    smask = smask_ref[...]
    hind = hind_ref[...]
    first_half = rothalf_ref[...] > 0.5

    def head_rms(o):
        ms_h = _dot((o * o).astype(_BF), hind) * inv_dv
        return o * lax.rsqrt(ms_h + EPS)

    def stack_heads(a):
        return jnp.concatenate([a] * N_HEADS, axis=0)

    def rotary(t, cosv, sinv):
        partner = jnp.where(first_half, pltpu.roll(t, KP - DK // 2, axis=1), pltpu.roll(t, DK // 2, axis=1))
        return t * cosv + partner * sinv

    def retention(i, r0, state):
        C = RET_CHUNK
        pv = proj_ref.at[i % 2]
        parts = []
        for c in range(sub // C):
            rows = slice(r0 + c * C, r0 + (c + 1) * C)
            lrows = slice(c * C, (c + 1) * C)
            cosv = cos_ref[0, rows, :]
            sinv = sin_ref[0, rows, :]
            q = rotary(pv[lrows, OFF_RQ:OFF_RQ + KP], cosv, sinv)
            k = rotary(pv[lrows, OFF_RK:OFF_RK + KP], cosv, sinv) * k_scale
            vb = pv[lrows, OFF_RV:OFF_RV + VW].astype(_BF)
            qb = q.astype(_BF)
            kbd = stack_heads(k.astype(_BF)) * rhmk_ref[...]
            scores = _dot_nt(qb, kbd) * rdecay_ref[...]
            vbd = stack_heads(vb) * rhmv_ref[...]
            o_intra = _dot(scores.astype(_BF), vbd)
            upd = _dot_tn(vb, (k * rzeta_ref[...]).astype(_BF)) * smask
            parts.append((rows, qb, o_intra, upd))
            tick()
        for rows, qb, o_intra, upd in parts:
            oscr_ref[rows, 0:VW] = o_intra + _dot_nt(qb, state.astype(_BF)) * rxi_ref[...]
            state = state * rgc_ref[...] + upd
            tick()
        return state

    def gla(i, r0, state):
        C = GLA_CHUNK
        n_chunks = sub // C
        W = n_chunks * KP
        pv = proj_ref.at[i % 2]
        logits = _dot(pv[:, OFF_RQ:OFF_RQ + KP].astype(_BF), wa2_ref[...]) + ba_ref[...]
        log_sig = jnp.minimum(logits, 0.0) - jnp.log1p(jnp.exp(-jnp.abs(logits)))
        la = log_sig * (1.0 / GLA_GATE_TAU)

        la_l = jnp.concatenate([la[c * C:(c + 1) * C, :] for c in range(n_chunks)], axis=1)
        tri = gtri_ref[...]
        p1, p2, p3 = _split3(la_l)
        b = _dot(tri, p1) + _dot(tri, p2) + _dot(tri, p3)
        b_ref[i % 2] = b

        row = lax.broadcasted_iota(jnp.int32, (C, W), 0)
        s = C // 2
        lvl = 0
        while s >= SUBLANES // 2:
            refs = [jnp.broadcast_to(b_ref[i % 2, p + s - 1:p + s, :], (2 * s, W)) for p in range(0, C, 2 * s)]
            b_at_ref = refs[0] if len(refs) == 1 else jnp.concatenate(refs, axis=0)
            dec_ref[i % 2, lvl * C:(lvl + 1) * C, :] = jnp.exp(-jnp.abs(b - b_at_ref))
            s //= 2
            lvl += 1
            tick()
        up1 = pltpu.roll(b, 1, axis=0)
        up2 = pltpu.roll(b, 2, axis=0)
        dn1 = pltpu.roll(b, C - 1, axis=0)
        m4 = row % 4
        b_at_ref = jnp.where(m4 == 0, dn1, jnp.where(m4 == 1, b, jnp.where(m4 == 2, up1, up2)))
        dec_ref[i % 2, lvl * C:(lvl + 1) * C, :] = jnp.exp(-jnp.abs(b - b_at_ref))
        lvl += 1
        tick()
        b_at_ref = jnp.where(row % 2 == 0, b, up1)
        dec_ref[i % 2, lvl * C:(lvl + 1) * C, :] = jnp.exp(-jnp.abs(b - b_at_ref))
        lvl += 1
        tick()
        dec_ref[i % 2, lvl * C:(lvl + 1) * C, :] = jnp.exp(b)
        tick()
        b_last = jnp.broadcast_to(b_ref[i % 2, C - 1:C, :], (C, W))
        dec_ref[i % 2, (lvl + 1) * C:(lvl + 2) * C, :] = jnp.exp(b_last - b)
        tick()

        hmk = ghmk_ref[...]
        parts = []
        for c in range(n_chunks):
            rows = slice(r0 + c * C, r0 + (c + 1) * C)
            lrows = slice(c * C, (c + 1) * C)
            lanes = slice(c * KP, (c + 1) * KP)
            q = pv[lrows, OFF_GQ:OFF_GQ + KP] * q_scale
            k = pv[lrows, OFF_GK:OFF_GK + KP]
            vb = pv[lrows, OFF_GV:OFF_GV + VW].astype(_BF)
            scores = jnp.zeros((C, N_HEADS * C), _F32)
            for lvl in range(N_GLA_LEVELS + 1):
                if lvl < N_GLA_LEVELS:
                    e = dec_ref[i % 2, lvl * C:(lvl + 1) * C, lanes]
                    ql, kl = q * e, k * e
                else:
                    ql, kl = q, k
                kbd = stack_heads(kl.astype(_BF)) * hmk
                scores = scores + _dot_nt(ql.astype(_BF), kbd) * gmask_ref[lvl]
                tick()
            vbd = stack_heads(vb) * ghmv_ref[...]
            e_cum = dec_ref[i % 2, N_GLA_LEVELS * C:(N_GLA_LEVELS + 1) * C, lanes]
            e_rev = dec_ref[i % 2, (N_GLA_LEVELS + 1) * C:(N_GLA_LEVELS + 2) * C, lanes]
            o_intra = _dot(scores.astype(_BF), vbd)
            upd = _dot_tn(vb, (k * e_rev).astype(_BF)) * smask
            parts.append((rows, (q * e_cum).astype(_BF), o_intra, upd, e_cum[C - 1:C, :]))
        for rows, qe, o_intra, upd, e_last in parts:
            oscr_ref[rows, VW:2 * VW] = o_intra + _dot_nt(qe, state.astype(_BF))
            state = state * e_last + upd
            tick()
        return state

    ret_state = sret_ref[...]
    gla_state = sgla_ref[...]
    for i in range(n_sub):
        r0 = i * sub
        srows = slice(r0, r0 + sub)
        pending[:] = projection_tasks(i + 1) if i + 1 < n_sub else []
        tick()
        pooling(i)
        tick()
        ret_state = retention(i, r0, ret_state)
        gla_state = gla(i, r0, gla_state)
        y_ret = _silu(proj_ref[i % 2, :, OFF_RG:OFF_RG + VW]) * head_rms(oscr_ref[srows, 0:VW])
        ycat_ref[srows, POOL_WIDTH:POOL_WIDTH + VW] = y_ret.astype(_BF)
        y_gla = (_silu(proj_ref[i % 2, :, OFF_GG:OFF_GG + VW])
                 * (head_rms(oscr_ref[srows, VW:2 * VW]) * gng_ref[...]))
        while pending:
            pending.pop(0)()
        ycat_ref[srows, POOL_WIDTH + VW:POOL_WIDTH + 2 * VW] = y_gla.astype(_BF)
        o_ref[0, srows, :] = x_ref[0, srows, :] + gate1 * _dot(ycat_ref[srows, :], wout_ref[...])
    sret_ref[...] = ret_state
    sgla_ref[...] = gla_state


def _whole(shape):
    nd = len(shape)
    return pl.BlockSpec(shape, lambda b, t: (0,) * nd, pipeline_mode=pl.Buffered(1))


def _layer_slab(shape, layer):
    nd = len(shape) - 1
    return pl.BlockSpec((None,) + tuple(shape[1:]), lambda b, t: (layer,) + (0,) * nd,
                        pipeline_mode=pl.Buffered(1))


def _mixer(x, mod, layer, params, cos, sin, tb):
    B, T, D = x.shape
    tile = MIX_TILE if T % MIX_TILE == 0 else T
    sub = MIX_SUB if tile % MIX_SUB == 0 else tile
    consts = [tb[k] for k in ("pool_win", "rot_first_half", "ret_decay", "ret_xi", "ret_zeta", "ret_gc",
                              "ret_hmk", "ret_hmv", "state_mask", "head_ind", "gla_tri", "gla_mask",
                              "gla_hmk", "gla_hmv")]
    norm_g, win = params[0], params[1]
    rest = list(params[2:])
    n_sub = tile // sub
    args = [x, mod, norm_g, win, cos, sin] + rest + consts
    in_specs = [
        pl.BlockSpec((1, tile, D), lambda b, t: (b, t, 0)),
        pl.BlockSpec((None, 1, 6, D), lambda b, t: (layer, b, 0, 0)),
        _layer_slab(norm_g.shape, layer),
        _layer_slab(win.shape, layer),
        pl.BlockSpec((1, tile, KP), lambda b, t: (b, t, 0)),
        pl.BlockSpec((1, tile, KP), lambda b, t: (b, t, 0)),
    ] + [_layer_slab(a.shape, layer) for a in rest] + [_whole(a.shape) for a in consts]
    n = sub + POOL_HIST
    gla_lanes = (sub // GLA_CHUNK) * KP
    return pl.pallas_call(
        functools.partial(_mixer_kernel, tile=tile, sub=sub),
        grid=(B, T // tile),
        in_specs=in_specs,
        out_specs=pl.BlockSpec((1, tile, D), lambda b, t: (b, t, 0)),
        out_shape=jax.ShapeDtypeStruct((B, T, D), _F32),
        scratch_shapes=[
            pltpu.VMEM((2, sub, D), _BF),
            pltpu.VMEM((2, sub, N_IN), _F32),
            pltpu.VMEM((tile, POOL_WIDTH + 2 * VW), _BF),
            pltpu.VMEM((2, GLA_CHUNK, gla_lanes), _F32),
            pltpu.VMEM((2, (N_GLA_LEVELS + 2) * GLA_CHUNK, gla_lanes), _F32),
            pltpu.VMEM((tile, 2 * VW), _F32),
            pltpu.VMEM((n, POOL_WIDTH), _F32),
            pltpu.VMEM((n, POOL_WIDTH), _F32),
            pltpu.VMEM((n, POOL_WIDTH), _F32),
            pltpu.VMEM((n, POOL_WIDTH), _F32),
            pltpu.VMEM((POOL_HIST, POOL_WIDTH), _F32),
            pltpu.VMEM((VW, KP), _F32),
            pltpu.VMEM((VW, KP), _F32),
        ],
        compiler_params=pltpu.CompilerParams(
            dimension_semantics=("arbitrary", "arbitrary"), vmem_limit_bytes=VMEM_LIMIT_BYTES),
        name="token_mixer",
    )(*args)


def _ffn_kernel(x_ref, mod_ref, ng_ref, wup_ref, cw_ref, wd_ref, fg_ref,
                o_ref, h_ref, carry_ref, act_ref, *, tile, d_ff, fb, final_norm):
    t_idx = pl.program_id(1)
    S = SUBLANES

    @pl.when(t_idx == 0)
    def _():
        carry_ref[...] = jnp.zeros_like(carry_ref)

    shift2 = mod_ref[0, 3:4, :]
    scale2 = mod_ref[0, 4:5, :]
    gate2 = mod_ref[0, 5:6, :]
    norm_scale = ng_ref[...] * (1.0 + scale2)
    R = tile // S
    D = x_ref.shape[-1]
    x = x_ref[0].reshape(S, R, D).swapaxes(0, 1).reshape(tile, D)
    ms = jnp.mean(x * x, axis=-1, keepdims=True)
    h_ref[...] = ((x * lax.rsqrt(ms + EPS)) * norm_scale + shift2).astype(_BF)
    first_sublane = lax.broadcasted_iota(jnp.int32, (S, fb), 0) == 0

    def up_conv(col):
        cols = slice(col, col + fb)
        u = _dot(h_ref[...], wup_ref[:, cols])
        back1 = jnp.where(first_sublane, pltpu.roll(carry_ref[S:2 * S, cols], 1, axis=0),
                          pltpu.roll(u[tile - S:tile, :], 1, axis=0))
        back2 = jnp.where(first_sublane, pltpu.roll(carry_ref[0:S, cols], 1, axis=0),
                          pltpu.roll(u[tile - 2 * S:tile - S, :], 1, axis=0))
        carry_ref[:, cols] = u[tile - 2 * S:tile, :]
        prev1 = jnp.concatenate([back1, u[0:tile - S, :]], axis=0)
        prev2 = jnp.concatenate([back2, back1, u[0:tile - 2 * S, :]], axis=0)
        cw = cw_ref[:, cols]
        return prev2 * cw[0:1, :] + prev1 * cw[1:2, :] + u * cw[2:3, :] + cw[3:4, :]

    for j in range(d_ff // fb):
        ya = up_conv(j * fb)
        yg = up_conv(d_ff + j * fb)
        act_ref[:, j * fb:(j + 1) * fb] = (_silu(yg) * ya).astype(_BF)

    out = x + gate2 * _dot(act_ref[...], wd_ref[...])
    if final_norm:
        ms_o = jnp.mean(out * out, axis=-1, keepdims=True)
        out = out * lax.rsqrt(ms_o + EPS) * fg_ref[...]
    o_ref[0] = out.reshape(R, S, D).swapaxes(0, 1).reshape(tile, D)


def _ffn(x, mod, layer, params, final_g, final_norm):
    B, T, D = x.shape
    tile = FFN_TILE if T % FFN_TILE == 0 else T
    norm_g, wup, cw, wd = params
    d_ff = wd.shape[1]
    in_specs = [
        pl.BlockSpec((1, tile, D), lambda b, t: (b, t, 0)),
        pl.BlockSpec((None, 1, 6, D), lambda b, t: (layer, b, 0, 0)),
    ] + [_layer_slab(a.shape, layer) for a in params] + [_whole(final_g.shape)]
    return pl.pallas_call(
        functools.partial(_ffn_kernel, tile=tile, d_ff=d_ff, fb=FFN_BLOCK, final_norm=final_norm),
        grid=(B, T // tile),
        in_specs=in_specs,
        out_specs=pl.BlockSpec((1, tile, D), lambda b, t: (b, t, 0)),
        out_shape=jax.ShapeDtypeStruct((B, T, D), _F32),
        scratch_shapes=[
            pltpu.VMEM((tile, D), _BF),
            pltpu.VMEM((2 * SUBLANES, 2 * d_ff), _F32),
            pltpu.VMEM((tile, d_ff), _BF),
        ],
        compiler_params=pltpu.CompilerParams(
            dimension_semantics=("arbitrary", "arbitrary"), vmem_limit_bytes=VMEM_LIMIT_BYTES),
        name="conv_ffn",
    )(x, mod, *params, final_g)


def _prep_mixer_params(norm1_g, w_in, pool_w, pool_scale, gla_wa2, gla_ba, gla_norm_g, w_out):
    L, D, _ = w_in.shape
    wb = w_in.astype(_BF)
    o_rq = POOL_WIDTH
    o_rk = o_rq + KW
    o_rv = o_rk + KW
    o_gq = o_rv + 2 * VW
    o_gk = o_gq + KW
    o_gv = o_gk + KW
    o_ga = o_gv + VW
    o_gg = o_ga + GLA_GATE_RANK
    zeros = lambda n: jnp.zeros((L, D, n), _BF)
    pad_k = zeros(KP - KW)
    win = jnp.concatenate([
        wb[:, :, :o_rq],
        wb[:, :, o_rq:o_rk], wb[:, :, o_ga:o_gg], zeros(KP - KW - GLA_GATE_RANK),
        wb[:, :, o_rk:o_rv], pad_k,
        wb[:, :, o_rv:o_gq],
        wb[:, :, o_gq:o_gk], pad_k, wb[:, :, o_gk:o_gv], pad_k,
        wb[:, :, o_gv:o_ga], wb[:, :, o_gg:],
    ], axis=2)
    G = pool_w.shape[1]
    eye = jnp.eye(G, dtype=pool_w.dtype)
    poolw = (pool_w[:, :, :, None, :] * eye[None, :, None, :, None]).reshape(L, POOL_WIDTH, POOL_WIDTH)
    wa2 = jnp.pad(gla_wa2, ((0, 0), (GA_LANE, KP - GA_LANE - GLA_GATE_RANK), (0, KP - KW)))
    ba = jnp.pad(gla_ba, ((0, 0), (0, KP - KW))).reshape(L, 1, KP)
    return (norm1_g.reshape(L, 1, D), win, poolw.astype(_BF), pool_scale.reshape(L, 1, POOL_WIDTH),
            wa2.astype(_BF), ba, gla_norm_g.reshape(L, 1, VW), w_out.astype(_BF))


def _prep_ffn_params(norm2_g, w_up, conv_w, conv_b, w_down):
    L, D, _ = w_up.shape
    cw = jnp.pad(jnp.concatenate([conv_w, conv_b[:, None, :]], axis=1),
                 ((0, 0), (0, SUBLANES - CONV_WIDTH - 1), (0, 0)))
    return norm2_g.reshape(L, 1, D), w_up.astype(_BF), cw, w_down.astype(_BF)


def kernel(x, c, positions, ada_w, ada_b, norm1_g, w_in, pool_w, pool_scale, gla_wa2, gla_ba,
           gla_norm_g, w_out, norm2_g, w_up, conv_w, conv_b, w_down, final_g):
    L = ada_w.shape[0]
    B, T, D = x.shape
    tb = {k: jnp.asarray(v) for k, v in _tables().items()}
    for name in ("head_ind", "gla_tri", "ret_hmk", "ret_hmv", "gla_hmk", "gla_hmv",
                 "rot_cos_expand", "rot_sin_expand"):
        tb[name] = tb[name].astype(_BF)
    mod = _modulation(c, ada_w, ada_b).reshape(L, B, 6, D)
    cos, sin = _rotary_tables(positions, tb)
    mixer_params = _prep_mixer_params(norm1_g, w_in, pool_w, pool_scale, gla_wa2, gla_ba, gla_norm_g, w_out)
    ffn_params = _prep_ffn_params(norm2_g, w_up, conv_w, conv_b, w_down)
    fg = final_g.reshape(1, D)
    for l in range(L):
        x = _mixer(x, mod, l, mixer_params, cos, sin, tb)
        x = _ffn(x, mod, l, ffn_params, fg, final_norm=(l == L - 1))
    return x
```

```python
import functools

import numpy as np
import jax
import jax.numpy as jnp
from jax import lax
from jax.experimental import pallas as pl
from jax.experimental.pallas import tpu as pltpu

POOL_WIDTH = 256
POOL_WINDOWS = (2, 4, 8, 16)
POOL_GROUP_DIM = 64
N_HEADS = 4
DK = 48
DV = 96
KW = N_HEADS * DK
VW = N_HEADS * DV
RET_CHUNK = 128
GLA_CHUNK = 64
GLA_GATE_RANK = 16
GLA_GATE_TAU = 16.0
ROPE_BASE = 10000.0
CONV_WIDTH = 3
EPS = 1e-6

LANES = 128
SUBLANES = 8
KP = 256
VMEM_LIMIT_BYTES = 56 * 1024 * 1024

OFF_POOL = 0
OFF_RQ = OFF_POOL + POOL_WIDTH
OFF_RK = OFF_RQ + KW
OFF_RV = OFF_RK + KW
OFF_RG = OFF_RV + VW
OFF_GQ = OFF_RG + VW
OFF_GK = OFF_GQ + KW
OFF_GV = OFF_GK + KW
OFF_GA = OFF_GV + VW
OFF_GG = OFF_GA + GLA_GATE_RANK
D_IN = OFF_GG + VW
N_IN = -(-D_IN // LANES) * LANES

MIX_TILE = 512
MIX_SUB = 256
PROJ_PIECE = 256
PROJ_TICKS = 3
FFN_TILE = 512
FFN_BLOCK = 256
ROT_TILE = 1024
ROT_ROWS = 32
POOL_HIST = 32
N_GLA_LEVELS = 6
LOG2_E = 1.4426950408889634

_BF = jnp.bfloat16
_F32 = jnp.float32


def _dot(a, b):
    return jnp.dot(a, b, preferred_element_type=_F32)


def _dot_nt(a, b):
    return lax.dot_general(a, b, (((1,), (1,)), ((), ())), preferred_element_type=_F32)


def _dot_tn(a, b):
    return lax.dot_general(a, b, (((0,), (0,)), ((), ())), preferred_element_type=_F32)


def _split3(x):
    p1 = x.astype(_BF)
    r1 = x - p1.astype(_F32)
    p2 = r1.astype(_BF)
    p3 = (r1 - p2.astype(_F32)).astype(_BF)
    return p1, p2, p3


def _sigmoid(x):
    return 1.0 / (1.0 + jnp.exp(-x))


def _silu(x):
    return x * _sigmoid(x)


def _key_head(d):
    return np.where(d < KW, d // DK, -1)


@functools.lru_cache(maxsize=None)
def _tables():
    t = {}
    kd = _key_head(np.arange(KP))
    vd = np.arange(VW) // DV

    def head_masks(chunk):
        rows = np.repeat(np.arange(N_HEADS), chunk)
        return ((rows[:, None] == kd[None, :]).astype(np.float32),
                (rows[:, None] == vd[None, :]).astype(np.float32))

    C = RET_CHUNK
    lg = np.log(1.0 - 2.0 ** (-5.0 - np.arange(N_HEADS, dtype=np.float64)))
    i = np.arange(C)
    rel = i[:, None] - i[None, :]
    dec = np.where(rel[None] >= 0, np.exp(np.maximum(rel, 0)[None] * lg[:, None, None]), 0.0)
    t["ret_decay"] = np.transpose(dec, (1, 0, 2)).reshape(C, N_HEADS * C).astype(np.float32)
    t["ret_xi"] = np.exp((i[:, None] + 1.0) * lg[vd][None, :]).astype(np.float32)
    zeta = np.exp((C - 1.0 - i)[:, None] * lg[np.maximum(kd, 0)][None, :]) * (kd >= 0)[None, :]
    t["ret_zeta"] = zeta.astype(np.float32)
    t["ret_gc"] = (np.exp(C * lg[np.maximum(kd, 0)]) * (kd >= 0))[None, :].astype(np.float32)
    t["ret_hmk"], t["ret_hmv"] = head_masks(C)
    t["state_mask"] = (vd[:, None] == kd[None, :]).astype(np.float32)
    t["head_ind"] = (vd[:, None] == vd[None, :]).astype(np.float32)

    C = GLA_CHUNK
    i = np.arange(C)
    masks = []
    s = C // 2
    while s >= 1:
        parent = (i // (2 * s)) * (2 * s)
        upper = (i - parent) >= s
        masks.append(upper[:, None] & (~upper)[None, :] & (parent[:, None] == parent[None, :]))
        s //= 2
    masks.append(i[:, None] == i[None, :])
    t["gla_mask"] = np.stack([np.tile(m, (1, N_HEADS)) for m in masks]).astype(np.float32)
    t["gla_tri"] = (i[None, :] <= i[:, None]).astype(np.float32)
    t["gla_hmk"], t["gla_hmv"] = head_masks(C)

    t["pool_win"] = np.repeat(np.asarray(POOL_WINDOWS, np.float32), POOL_GROUP_DIM)[None, :]

    half = DK // 2
    l = np.arange(KP)
    hit = (np.arange(ROT_ROWS)[:, None] == (l % half)[None, :]) & (l < KW)[None, :]
    t["rot_cos_expand"] = hit.astype(np.float32)
    t["rot_sin_expand"] = hit * np.where((l % DK) < half, -1.0, 1.0)[None, :].astype(np.float32)
    t["rot_first_half"] = ((l % DK) < half).astype(np.float32)[None, :]
    return t


def _mod_kernel(c_ref, w_ref, b_ref, o_ref):
    ca = _silu(c_ref[...]).astype(_BF)
    o_ref[...] = _dot(ca, w_ref[...].astype(_BF)) + b_ref[...]


def _modulation(c, ada_w, ada_b):
    L, D, N = ada_w.shape
    B = c.shape[0]
    tn = D
    return pl.pallas_call(
        _mod_kernel,
        grid=(L, N // tn),
        in_specs=[
            pl.BlockSpec((B, D), lambda l, n: (0, 0)),
            pl.BlockSpec((None, D, tn), lambda l, n: (l, 0, n)),
            pl.BlockSpec((None, 1, tn), lambda l, n: (l, 0, n)),
        ],
        out_specs=pl.BlockSpec((None, B, tn), lambda l, n: (l, 0, n)),
        out_shape=jax.ShapeDtypeStruct((L, B, N), _F32),
        compiler_params=pltpu.CompilerParams(
            dimension_semantics=("arbitrary", "arbitrary"), vmem_limit_bytes=VMEM_LIMIT_BYTES),
        name="adaln_modulation",
    )(c, ada_w, ada_b.reshape(L, 1, N))


def _rot_kernel(pos_ref, freq_ref, ec_ref, es_ref, cos_ref, sin_ref):
    ang = freq_ref[...] * pos_ref[0]
    ec = ec_ref[...]
    es = es_ref[...]

    def widen(v, e):
        p1, p2, p3 = _split3(v)
        return _dot_tn(p1, e) + _dot_tn(p2, e) + _dot_tn(p3, e)

    cos_ref[0] = widen(jnp.cos(ang), ec)
    sin_ref[0] = widen(jnp.sin(ang), es)


def _rotary_tables(positions, tb):
    B, T = positions.shape
    tile = ROT_TILE if T % ROT_TILE == 0 else T
    inv_freq = ROPE_BASE ** (-jnp.arange(0, DK, 2, dtype=_F32) / DK)
    freq = jnp.pad(inv_freq, (0, ROT_ROWS - DK // 2)).reshape(ROT_ROWS, 1)
    pos = positions.astype(_F32).reshape(B, 1, T)
    const = lambda b, t: (0, 0)
    return pl.pallas_call(
        _rot_kernel,
        grid=(B, T // tile),
        in_specs=[
            pl.BlockSpec((1, 1, tile), lambda b, t: (b, 0, t)),
            pl.BlockSpec((ROT_ROWS, 1), const),
            pl.BlockSpec((ROT_ROWS, KP), const),
            pl.BlockSpec((ROT_ROWS, KP), const),
        ],
        out_specs=[pl.BlockSpec((1, tile, KP), lambda b, t: (b, t, 0))] * 2,
        out_shape=[jax.ShapeDtypeStruct((B, T, KP), _F32)] * 2,
        compiler_params=pltpu.CompilerParams(dimension_semantics=("arbitrary", "arbitrary")),
        name="rotary_table",
    )(pos, freq, tb["rot_cos_expand"], tb["rot_sin_expand"])


def _mixer_kernel(x_ref, mod_ref, ng_ref, win_ref, cos_ref, sin_ref,
                  poolw_ref, pools_ref, wa2_ref, ba_ref, gng_ref, wout_ref,
                  poolwin_ref, rothalf_ref,
                  rdecay_ref, rxi_ref, rzeta_ref, rgc_ref, rhmk_ref, rhmv_ref,
                  smask_ref, hind_ref,
                  gtri_ref, gmask_ref, ghmk_ref, ghmv_ref,
                  o_ref,
                  hb_ref, proj_ref, ycat_ref, b_ref, dec_ref, oscr_ref, ext_ref, s2_ref, s4_ref, s8_ref,
                  hist_ref, sret_ref, sgla_ref, *, tile, sub):
    t_idx = pl.program_id(1)
    n_sub = tile // sub

    @pl.when(t_idx == 0)
    def _():
        hist_ref[...] = jnp.zeros_like(hist_ref)
        sret_ref[...] = jnp.zeros_like(sret_ref)
        sgla_ref[...] = jnp.zeros_like(sgla_ref)

    gate1 = mod_ref[0, 2:3, :]

    norm_scale = ng_ref[...] * (1.0 + mod_ref[0, 1:2, :])
    shift1 = mod_ref[0, 0:1, :]

    def projection_tasks(i):
        slot = i % 2

        def norm():
            x = x_ref[0, i * sub:(i + 1) * sub, :]
            ms = jnp.mean(x * x, axis=-1, keepdims=True)
            hb_ref[slot] = ((x * lax.rsqrt(ms + EPS)) * norm_scale + shift1).astype(_BF)

        def segment(off, width):
            def run():
                proj_ref[slot, :, off:off + width] = _dot(hb_ref[slot], win_ref[:, off:off + width])
            return run

        return [norm] + [segment(off, min(PROJ_PIECE, N_IN - off)) for off in range(0, N_IN, PROJ_PIECE)]

    def key_block(pv, lrows, off):
        shift = off % LANES
        if shift == 0:
            return pv[lrows, off:off + KP]
        return pltpu.roll(pv[lrows, off - shift:off - shift + KP], KP - shift, axis=1)

    for task in projection_tasks(0):
        task()

    pending = []
    ticks = [0]

    def tick():
        ticks[0] += 1
        if pending and ticks[0] % PROJ_TICKS == 0:
            pending.pop(0)()

    def pooling(i):
        n = sub + POOL_HIST
        srows = slice(i * sub, (i + 1) * sub)
        u = proj_ref[i % 2, :, OFF_POOL:OFF_POOL + POOL_WIDTH]
        ext_ref[0:POOL_HIST, :] = hist_ref[...]
        ext_ref[POOL_HIST:n, :] = u
        hist_ref[...] = u[sub - POOL_HIST:sub, :]
        s2_ref[8:n, :] = ext_ref[8:n, :] + ext_ref[7:n - 1, :]
        s4_ref[16:n, :] = s2_ref[16:n, :] + s2_ref[14:n - 2, :]
        s8_ref[24:n, :] = s4_ref[24:n, :] + s4_ref[20:n - 4, :]
        s16 = s8_ref[32:n, :] + s8_ref[24:n - 8, :]
        lane = lax.broadcasted_iota(jnp.int32, (sub, POOL_WIDTH), 1)
        wsum = jnp.where(lane < POOL_GROUP_DIM, s2_ref[32:n, :],
                         jnp.where(lane < 2 * POOL_GROUP_DIM, s4_ref[32:n, :],
                                   jnp.where(lane < 3 * POOL_GROUP_DIM, s8_ref[32:n, :], s16)))
        t_abs = (t_idx * tile + i * sub
                 + lax.broadcasted_iota(jnp.int32, (sub, POOL_WIDTH), 0)).astype(_F32)
        cnt = jnp.minimum(t_abs + 1.0, poolwin_ref[...])
        pooled = wsum / cnt - u
        ycat_ref[srows, 0:POOL_WIDTH] = (_dot(pooled.astype(_BF), poolw_ref[...])
                                         * pools_ref[...]).astype(_BF)

    inv_dv = 1.0 / DV
    k_scale = DK ** -0.5
    q_scale = DK ** -0.5
    smask = smask_ref[...]
    hind = hind_ref[...]
    first_half = rothalf_ref[...] > 0.5

    def head_rms(o):
        ms_h = _dot((o * o).astype(_BF), hind) * inv_dv
        return o * lax.rsqrt(ms_h + EPS)

    def stack_heads(a):
        return jnp.concatenate([a] * N_HEADS, axis=0)

    def rotary(t, cosv, sinv):
        partner = jnp.where(first_half, pltpu.roll(t, KP - DK // 2, axis=1), pltpu.roll(t, DK // 2, axis=1))
        return t * cosv + partner * sinv

    def retention(i, r0, state):
        C = RET_CHUNK
        pv = proj_ref.at[i % 2]
        parts = []
        for c in range(sub // C):
            rows = slice(r0 + c * C, r0 + (c + 1) * C)
            lrows = slice(c * C, (c + 1) * C)
            cosv = cos_ref[0, rows, :]
            sinv = sin_ref[0, rows, :]
            q = rotary(key_block(pv, lrows, OFF_RQ), cosv, sinv)
            k = rotary(key_block(pv, lrows, OFF_RK), cosv, sinv) * k_scale
            vb = pv[lrows, OFF_RV:OFF_RV + VW].astype(_BF)
            qb = q.astype(_BF)
            kbd = stack_heads(k.astype(_BF)) * rhmk_ref[...]
            scores = _dot_nt(qb, kbd) * rdecay_ref[...]
            vbd = stack_heads(vb) * rhmv_ref[...]
            o_intra = _dot(scores.astype(_BF), vbd)
            upd = _dot_tn(vb, (k * rzeta_ref[...]).astype(_BF)) * smask
            parts.append((rows, qb, o_intra, upd))
            tick()
        for rows, qb, o_intra, upd in parts:
            oscr_ref[rows, 0:VW] = o_intra + _dot_nt(qb, state.astype(_BF)) * rxi_ref[...]
            state = state * rgc_ref[...] + upd
            tick()
        return state

    def gla(i, r0, state):
        C = GLA_CHUNK
        n_chunks = sub // C
        W = n_chunks * KP
        pv = proj_ref.at[i % 2]
        logits = _dot(pv[:, OFF_GA:OFF_GA + LANES].astype(_BF), wa2_ref[...]) + ba_ref[...]
        log_sig = jnp.minimum(logits, 0.0) - jnp.log(1.0 + jnp.exp(-jnp.abs(logits)))
        la = log_sig * (LOG2_E / GLA_GATE_TAU)

        la_l = jnp.concatenate([la[c * C:(c + 1) * C, :] for c in range(n_chunks)], axis=1)
        tri = gtri_ref[...]
        p1, p2, p3 = _split3(la_l)
        b = _dot(tri, p1) + _dot(tri, p2) + _dot(tri, p3)
        b_ref[i % 2] = b

        row = lax.broadcasted_iota(jnp.int32, (C, W), 0)
        s = C // 2
        lvl = 0
        while s >= SUBLANES // 2:
            refs = [jnp.broadcast_to(b_ref[i % 2, p + s - 1:p + s, :], (2 * s, W)) for p in range(0, C, 2 * s)]
            b_at_ref = refs[0] if len(refs) == 1 else jnp.concatenate(refs, axis=0)
            dec_ref[i % 2, lvl * C:(lvl + 1) * C, :] = jnp.exp2(-jnp.abs(b - b_at_ref))
            s //= 2
            lvl += 1
            tick()
        up1 = pltpu.roll(b, 1, axis=0)
        up2 = pltpu.roll(b, 2, axis=0)
        dn1 = pltpu.roll(b, C - 1, axis=0)
        m4 = row % 4
        b_at_ref = jnp.where(m4 == 0, dn1, jnp.where(m4 == 1, b, jnp.where(m4 == 2, up1, up2)))
        dec_ref[i % 2, lvl * C:(lvl + 1) * C, :] = jnp.exp2(-jnp.abs(b - b_at_ref))
        lvl += 1
        tick()
        b_at_ref = jnp.where(row % 2 == 0, b, up1)
        dec_ref[i % 2, lvl * C:(lvl + 1) * C, :] = jnp.exp2(-jnp.abs(b - b_at_ref))
        lvl += 1
        tick()
        dec_ref[i % 2, lvl * C:(lvl + 1) * C, :] = jnp.exp2(b)
        tick()
        b_last = jnp.broadcast_to(b_ref[i % 2, C - 1:C, :], (C, W))
        dec_ref[i % 2, (lvl + 1) * C:(lvl + 2) * C, :] = jnp.exp2(b_last - b)
        tick()

        hmk = ghmk_ref[...]
        parts = []
        for c in range(n_chunks):
            rows = slice(r0 + c * C, r0 + (c + 1) * C)
            lrows = slice(c * C, (c + 1) * C)
            lanes = slice(c * KP, (c + 1) * KP)
            q = key_block(pv, lrows, OFF_GQ) * q_scale
            k = key_block(pv, lrows, OFF_GK)
            vb = pv[lrows, OFF_GV:OFF_GV + VW].astype(_BF)
            qb = q.astype(_BF)
            kbd0 = stack_heads(k.astype(_BF)) * hmk
            scores = jnp.zeros((C, N_HEADS * C), _F32)
            for lvl in range(N_GLA_LEVELS + 1):
                if lvl < N_GLA_LEVELS:
                    eb = dec_ref[i % 2, lvl * C:(lvl + 1) * C, lanes].astype(_BF)
                    ql, kbd = qb * eb, kbd0 * stack_heads(eb)
                else:
                    ql, kbd = qb, kbd0
                scores = scores + _dot_nt(ql, kbd) * gmask_ref[lvl]
                tick()
            vbd = stack_heads(vb) * ghmv_ref[...]
            e_cum = dec_ref[i % 2, N_GLA_LEVELS * C:(N_GLA_LEVELS + 1) * C, lanes]
            e_rev = dec_ref[i % 2, (N_GLA_LEVELS + 1) * C:(N_GLA_LEVELS + 2) * C, lanes]
            o_intra = _dot(scores.astype(_BF), vbd)
            upd = _dot_tn(vb, (k * e_rev).astype(_BF)) * smask
            parts.append((rows, (q * e_cum).astype(_BF), o_intra, upd, e_cum[C - 1:C, :]))
        for rows, qe, o_intra, upd, e_last in parts:
            oscr_ref[rows, VW:2 * VW] = o_intra + _dot_nt(qe, state.astype(_BF))
            state = state * e_last + upd
            tick()
        return state

    ret_state = sret_ref[...]
    gla_state = sgla_ref[...]
    for i in range(n_sub):
        r0 = i * sub
        srows = slice(r0, r0 + sub)
        pending[:] = projection_tasks(i + 1) if i + 1 < n_sub else []
        tick()
        pooling(i)
        tick()
        ret_state = retention(i, r0, ret_state)
        gla_state = gla(i, r0, gla_state)
        y_ret = _silu(proj_ref[i % 2, :, OFF_RG:OFF_RG + VW]) * head_rms(oscr_ref[srows, 0:VW])
        ycat_ref[srows, POOL_WIDTH:POOL_WIDTH + VW] = y_ret.astype(_BF)
        gg_off = OFF_GG % LANES
        gg = pltpu.roll(proj_ref[i % 2, :, OFF_GG - gg_off:N_IN], N_IN - OFF_GG, axis=1)[:, 0:VW]
        y_gla = (_silu(gg)
                 * (head_rms(oscr_ref[srows, VW:2 * VW]) * gng_ref[...]))
        while pending:
            pending.pop(0)()
        ycat_ref[srows, POOL_WIDTH + VW:POOL_WIDTH + 2 * VW] = y_gla.astype(_BF)
        o_ref[0, srows, :] = x_ref[0, srows, :] + gate1 * _dot(ycat_ref[srows, :], wout_ref[...])
    sret_ref[...] = ret_state
    sgla_ref[...] = gla_state


def _whole(shape):
    nd = len(shape)
    return pl.BlockSpec(shape, lambda b, t: (0,) * nd, pipeline_mode=pl.Buffered(1))


def _layer_slab(shape, layer):
    nd = len(shape) - 1
    return pl.BlockSpec((None,) + tuple(shape[1:]), lambda b, t: (layer,) + (0,) * nd,
                        pipeline_mode=pl.Buffered(1))


def _mixer(x, mod, layer, params, cos, sin, tb):
    B, T, D = x.shape
    tile = MIX_TILE if T % MIX_TILE == 0 else T
    sub = MIX_SUB if tile % MIX_SUB == 0 else tile
    consts = [tb[k] for k in ("pool_win", "rot_first_half", "ret_decay", "ret_xi", "ret_zeta", "ret_gc",
                              "ret_hmk", "ret_hmv", "state_mask", "head_ind", "gla_tri", "gla_mask",
                              "gla_hmk", "gla_hmv")]
    norm_g, win = params[0], params[1]
    rest = list(params[2:])
    n_sub = tile // sub
    args = [x, mod, norm_g, win, cos, sin] + rest + consts
    in_specs = [
        pl.BlockSpec((1, tile, D), lambda b, t: (b, t, 0)),
        pl.BlockSpec((None, 1, 6, D), lambda b, t: (layer, b, 0, 0)),
        _layer_slab(norm_g.shape, layer),
        _layer_slab(win.shape, layer),
        pl.BlockSpec((1, tile, KP), lambda b, t: (b, t, 0)),
        pl.BlockSpec((1, tile, KP), lambda b, t: (b, t, 0)),
    ] + [_layer_slab(a.shape, layer) for a in rest] + [_whole(a.shape) for a in consts]
    n = sub + POOL_HIST
    gla_lanes = (sub // GLA_CHUNK) * KP
    return pl.pallas_call(
        functools.partial(_mixer_kernel, tile=tile, sub=sub),
        grid=(B, T // tile),
        in_specs=in_specs,
        out_specs=pl.BlockSpec((1, tile, D), lambda b, t: (b, t, 0)),
        out_shape=jax.ShapeDtypeStruct((B, T, D), _F32),
        scratch_shapes=[
            pltpu.VMEM((2, sub, D), _BF),
            pltpu.VMEM((2, sub, N_IN), _F32),
            pltpu.VMEM((tile, POOL_WIDTH + 2 * VW), _BF),
            pltpu.VMEM((2, GLA_CHUNK, gla_lanes), _F32),
            pltpu.VMEM((2, (N_GLA_LEVELS + 2) * GLA_CHUNK, gla_lanes), _F32),
            pltpu.VMEM((tile, 2 * VW), _F32),
            pltpu.VMEM((n, POOL_WIDTH), _F32),
            pltpu.VMEM((n, POOL_WIDTH), _F32),
            pltpu.VMEM((n, POOL_WIDTH), _F32),
            pltpu.VMEM((n, POOL_WIDTH), _F32),
            pltpu.VMEM((POOL_HIST, POOL_WIDTH), _F32),
            pltpu.VMEM((VW, KP), _F32),
            pltpu.VMEM((VW, KP), _F32),
        ],
        compiler_params=pltpu.CompilerParams(
            dimension_semantics=("arbitrary", "arbitrary"), vmem_limit_bytes=VMEM_LIMIT_BYTES),
        name="token_mixer",
    )(*args)


def _ffn_kernel(x_ref, mod_ref, ng_ref, wup_ref, cw_ref, wd_ref, fg_ref,
                o_ref, h_ref, carry_ref, act_ref, *, tile, d_ff, fb, final_norm):
    t_idx = pl.program_id(1)
    S = SUBLANES

    @pl.when(t_idx == 0)
    def _():
        carry_ref[...] = jnp.zeros_like(carry_ref)

    shift2 = mod_ref[0, 3:4, :]
    scale2 = mod_ref[0, 4:5, :]
    gate2 = mod_ref[0, 5:6, :]
    norm_scale = ng_ref[...] * (1.0 + scale2)
    R = tile // S
    D = x_ref.shape[-1]
    x = x_ref[0].reshape(S, R, D).swapaxes(0, 1).reshape(tile, D)
    ms = jnp.mean(x * x, axis=-1, keepdims=True)
    h_ref[...] = ((x * lax.rsqrt(ms + EPS)) * norm_scale + shift2).astype(_BF)
    first_sublane = lax.broadcasted_iota(jnp.int32, (S, fb), 0) == 0

    def up_conv(col):
        cols = slice(col, col + fb)
        u = _dot(h_ref[...], wup_ref[:, cols])
        back1 = jnp.where(first_sublane, pltpu.roll(carry_ref[S:2 * S, cols], 1, axis=0),
                          pltpu.roll(u[tile - S:tile, :], 1, axis=0))
        back2 = jnp.where(first_sublane, pltpu.roll(carry_ref[0:S, cols], 1, axis=0),
                          pltpu.roll(u[tile - 2 * S:tile - S, :], 1, axis=0))
        carry_ref[:, cols] = u[tile - 2 * S:tile, :]
        prev1 = jnp.concatenate([back1, u[0:tile - S, :]], axis=0)
        prev2 = jnp.concatenate([back2, back1, u[0:tile - 2 * S, :]], axis=0)
        cw = cw_ref[:, cols]
        return prev2 * cw[0:1, :] + prev1 * cw[1:2, :] + u * cw[2:3, :] + cw[3:4, :]

    for j in range(d_ff // fb):
        ya = up_conv(j * fb)
        yg = up_conv(d_ff + j * fb)
        act_ref[:, j * fb:(j + 1) * fb] = (_silu(yg) * ya).astype(_BF)

    out = x + gate2 * _dot(act_ref[...], wd_ref[...])
    if final_norm:
        ms_o = jnp.mean(out * out, axis=-1, keepdims=True)
        out = out * lax.rsqrt(ms_o + EPS) * fg_ref[...]
    o_ref[0] = out.reshape(R, S, D).swapaxes(0, 1).reshape(tile, D)


def _ffn(x, mod, layer, params, final_g, final_norm):
    B, T, D = x.shape
    tile = FFN_TILE if T % FFN_TILE == 0 else T
    norm_g, wup, cw, wd = params
    d_ff = wd.shape[1]
    in_specs = [
        pl.BlockSpec((1, tile, D), lambda b, t: (b, t, 0)),
        pl.BlockSpec((None, 1, 6, D), lambda b, t: (layer, b, 0, 0)),
    ] + [_layer_slab(a.shape, layer) for a in params] + [_whole(final_g.shape)]
    return pl.pallas_call(
        functools.partial(_ffn_kernel, tile=tile, d_ff=d_ff, fb=FFN_BLOCK, final_norm=final_norm),
        grid=(B, T // tile),
        in_specs=in_specs,
        out_specs=pl.BlockSpec((1, tile, D), lambda b, t: (b, t, 0)),
        out_shape=jax.ShapeDtypeStruct((B, T, D), _F32),
        scratch_shapes=[
            pltpu.VMEM((tile, D), _BF),
            pltpu.VMEM((2 * SUBLANES, 2 * d_ff), _F32),
            pltpu.VMEM((tile, d_ff), _BF),
        ],
        compiler_params=pltpu.CompilerParams(
            dimension_semantics=("arbitrary", "arbitrary"), vmem_limit_bytes=VMEM_LIMIT_BYTES),
        name="conv_ffn",
    )(x, mod, *params, final_g)


def _prep_mixer_params(norm1_g, w_in, pool_w, pool_scale, gla_wa2, gla_ba, gla_norm_g, w_out):
    L, D, _ = w_in.shape
    win = jnp.pad(w_in.astype(_BF), ((0, 0), (0, 0), (0, N_IN - D_IN)))
    G = pool_w.shape[1]
    eye = jnp.eye(G, dtype=pool_w.dtype)
    poolw = (pool_w[:, :, :, None, :] * eye[None, :, None, :, None]).reshape(L, POOL_WIDTH, POOL_WIDTH)
    wa2 = jnp.pad(gla_wa2, ((0, 0), (0, LANES - GLA_GATE_RANK), (0, KP - KW)))
    ba = jnp.pad(gla_ba, ((0, 0), (0, KP - KW))).reshape(L, 1, KP)
    return (norm1_g.reshape(L, 1, D), win, poolw.astype(_BF), pool_scale.reshape(L, 1, POOL_WIDTH),
            wa2.astype(_BF), ba, gla_norm_g.reshape(L, 1, VW), w_out.astype(_BF))


def _prep_ffn_params(norm2_g, w_up, conv_w, conv_b, w_down):
    L, D, _ = w_up.shape
    cw = jnp.pad(jnp.concatenate([conv_w, conv_b[:, None, :]], axis=1),
                 ((0, 0), (0, SUBLANES - CONV_WIDTH - 1), (0, 0)))
    return norm2_g.reshape(L, 1, D), w_up.astype(_BF), cw, w_down.astype(_BF)


def kernel(x, c, positions, ada_w, ada_b, norm1_g, w_in, pool_w, pool_scale, gla_wa2, gla_ba,
           gla_norm_g, w_out, norm2_g, w_up, conv_w, conv_b, w_down, final_g):
    L = ada_w.shape[0]
    B, T, D = x.shape
    tb = {k: jnp.asarray(v) for k, v in _tables().items()}
    for name in ("head_ind", "gla_tri", "ret_hmk", "ret_hmv", "gla_hmk", "gla_hmv",
                 "rot_cos_expand", "rot_sin_expand"):
        tb[name] = tb[name].astype(_BF)
    mod = _modulation(c, ada_w, ada_b).reshape(L, B, 6, D)
    cos, sin = _rotary_tables(positions, tb)
    mixer_params = _prep_mixer_params(norm1_g, w_in, pool_w, pool_scale, gla_wa2, gla_ba, gla_norm_g, w_out)
    ffn_params = _prep_ffn_params(norm2_g, w_up, conv_w, conv_b, w_down)
    fg = final_g.reshape(1, D)
    for l in range(L):
        x = _mixer(x, mod, l, mixer_params, cos, sin, tb)
        x = _ffn(x, mod, l, ffn_params, fg, final_norm=(l == L - 1))
    return x
```

```python
import functools

import numpy as np
import jax
import jax.numpy as jnp
from jax import lax
from jax.experimental import pallas as pl
from jax.experimental.pallas import tpu as pltpu

POOL_WIDTH = 256
POOL_WINDOWS = (2, 4, 8, 16)
POOL_GROUP_DIM = 64
N_HEADS = 4
DK = 48
DV = 96
KW = N_HEADS * DK
VW = N_HEADS * DV
RET_CHUNK = 128
GLA_CHUNK = 64
GLA_GATE_RANK = 16
GLA_GATE_TAU = 16.0
ROPE_BASE = 10000.0
CONV_WIDTH = 3
EPS = 1e-6

LANES = 128
SUBLANES = 8
KP = 256
VMEM_LIMIT_BYTES = 56 * 1024 * 1024

OFF_POOL = 0
OFF_RQ = OFF_POOL + POOL_WIDTH
OFF_RK = OFF_RQ + KW
OFF_RV = OFF_RK + KW
OFF_RG = OFF_RV + VW
OFF_GQ = OFF_RG + VW
OFF_GK = OFF_GQ + KW
OFF_GV = OFF_GK + KW
OFF_GA = OFF_GV + VW
OFF_GG = OFF_GA + GLA_GATE_RANK
D_IN = OFF_GG + VW
N_IN = -(-D_IN // LANES) * LANES

MIX_TILE = 512
MIX_SUB = 256
PROJ_PIECE = 256
PROJ_TICKS = 3
FFN_TILE = 512
FFN_BLOCK = 256
FFN_NORM_ROWS = 128
FFN_EARLY_BLOCKS = 2
ROT_TILE = 1024
ROT_ROWS = 32
POOL_HIST = 32
N_GLA_LEVELS = 6
LOG2_E = 1.4426950408889634

_BF = jnp.bfloat16
_F32 = jnp.float32


def _dot(a, b):
    return jnp.dot(a, b, preferred_element_type=_F32)


def _dot_nt(a, b):
    return lax.dot_general(a, b, (((1,), (1,)), ((), ())), preferred_element_type=_F32)


def _dot_tn(a, b):
    return lax.dot_general(a, b, (((0,), (0,)), ((), ())), preferred_element_type=_F32)


def _split3(x):
    p1 = x.astype(_BF)
    r1 = x - p1.astype(_F32)
    p2 = r1.astype(_BF)
    p3 = (r1 - p2.astype(_F32)).astype(_BF)
    return p1, p2, p3


def _sigmoid(x):
    return 1.0 / (1.0 + jnp.exp(-x))


def _silu(x):
    return x * _sigmoid(x)


def _key_head(d):
    return np.where(d < KW, d // DK, -1)


@functools.lru_cache(maxsize=None)
def _tables():
    t = {}
    kd = _key_head(np.arange(KP))
    vd = np.arange(VW) // DV

    def head_masks(chunk):
        rows = np.repeat(np.arange(N_HEADS), chunk)
        return ((rows[:, None] == kd[None, :]).astype(np.float32),
                (rows[:, None] == vd[None, :]).astype(np.float32))

    C = RET_CHUNK
    lg = np.log(1.0 - 2.0 ** (-5.0 - np.arange(N_HEADS, dtype=np.float64)))
    i = np.arange(C)
    rel = i[:, None] - i[None, :]
    dec = np.where(rel[None] >= 0, np.exp(np.maximum(rel, 0)[None] * lg[:, None, None]), 0.0)
    t["ret_decay"] = np.transpose(dec, (1, 0, 2)).reshape(C, N_HEADS * C).astype(np.float32)
    t["ret_xi"] = np.exp((i[:, None] + 1.0) * lg[vd][None, :]).astype(np.float32)
    zeta = np.exp((C - 1.0 - i)[:, None] * lg[np.maximum(kd, 0)][None, :]) * (kd >= 0)[None, :]
    t["ret_zeta"] = zeta.astype(np.float32)
    t["ret_gc"] = (np.exp(C * lg[np.maximum(kd, 0)]) * (kd >= 0))[None, :].astype(np.float32)
    t["ret_hmk"], t["ret_hmv"] = head_masks(C)
    t["state_mask"] = (vd[:, None] == kd[None, :]).astype(np.float32)
    t["head_ind"] = (vd[:, None] == vd[None, :]).astype(np.float32)

    C = GLA_CHUNK
    i = np.arange(C)
    masks = []
    s = C // 2
    while s >= 1:
        parent = (i // (2 * s)) * (2 * s)
        upper = (i - parent) >= s
        masks.append(upper[:, None] & (~upper)[None, :] & (parent[:, None] == parent[None, :]))
        s //= 2
    masks.append(i[:, None] == i[None, :])
    t["gla_mask"] = np.stack([np.tile(m, (1, N_HEADS)) for m in masks]).astype(np.float32)
    t["gla_tri"] = (i[None, :] <= i[:, None]).astype(np.float32)
    t["gla_hmk"], t["gla_hmv"] = head_masks(C)

    t["pool_win"] = np.repeat(np.asarray(POOL_WINDOWS, np.float32), POOL_GROUP_DIM)[None, :]

    half = DK // 2
    l = np.arange(KP)
    hit = (np.arange(ROT_ROWS)[:, None] == (l % half)[None, :]) & (l < KW)[None, :]
    t["rot_cos_expand"] = hit.astype(np.float32)
    t["rot_sin_expand"] = hit * np.where((l % DK) < half, -1.0, 1.0)[None, :].astype(np.float32)
    t["rot_first_half"] = ((l % DK) < half).astype(np.float32)[None, :]
    return t


def _mod_kernel(c_ref, w_ref, b_ref, o_ref):
    ca = _silu(c_ref[...]).astype(_BF)
    o_ref[...] = _dot(ca, w_ref[...].astype(_BF)) + b_ref[...]


def _modulation(c, ada_w, ada_b):
    L, D, N = ada_w.shape
    B = c.shape[0]
    tn = D
    return pl.pallas_call(
        _mod_kernel,
        grid=(L, N // tn),
        in_specs=[
            pl.BlockSpec((B, D), lambda l, n: (0, 0)),
            pl.BlockSpec((None, D, tn), lambda l, n: (l, 0, n)),
            pl.BlockSpec((None, 1, tn), lambda l, n: (l, 0, n)),
        ],
        out_specs=pl.BlockSpec((None, B, tn), lambda l, n: (l, 0, n)),
        out_shape=jax.ShapeDtypeStruct((L, B, N), _F32),
        compiler_params=pltpu.CompilerParams(
            dimension_semantics=("arbitrary", "arbitrary"), vmem_limit_bytes=VMEM_LIMIT_BYTES),
        name="adaln_modulation",
    )(c, ada_w, ada_b.reshape(L, 1, N))


def _rot_kernel(pos_ref, freq_ref, ec_ref, es_ref, cos_ref, sin_ref):
    ang = freq_ref[...] * pos_ref[0]
    ec = ec_ref[...]
    es = es_ref[...]

    def widen(v, e):
        p1, p2, p3 = _split3(v)
        return _dot_tn(p1, e) + _dot_tn(p2, e) + _dot_tn(p3, e)

    cos_ref[0] = widen(jnp.cos(ang), ec)
    sin_ref[0] = widen(jnp.sin(ang), es)


def _rotary_tables(positions, tb):
    B, T = positions.shape
    tile = ROT_TILE if T % ROT_TILE == 0 else T
    inv_freq = ROPE_BASE ** (-jnp.arange(0, DK, 2, dtype=_F32) / DK)
    freq = jnp.pad(inv_freq, (0, ROT_ROWS - DK // 2)).reshape(ROT_ROWS, 1)
    pos = positions.astype(_F32).reshape(B, 1, T)
    const = lambda b, t: (0, 0)
    return pl.pallas_call(
        _rot_kernel,
        grid=(B, T // tile),
        in_specs=[
            pl.BlockSpec((1, 1, tile), lambda b, t: (b, 0, t)),
            pl.BlockSpec((ROT_ROWS, 1), const),
            pl.BlockSpec((ROT_ROWS, KP), const),
            pl.BlockSpec((ROT_ROWS, KP), const),
        ],
        out_specs=[pl.BlockSpec((1, tile, KP), lambda b, t: (b, t, 0))] * 2,
        out_shape=[jax.ShapeDtypeStruct((B, T, KP), _F32)] * 2,
        compiler_params=pltpu.CompilerParams(dimension_semantics=("arbitrary", "arbitrary")),
        name="rotary_table",
    )(pos, freq, tb["rot_cos_expand"], tb["rot_sin_expand"])


def _mixer_kernel(x_ref, mod_ref, ng_ref, win_ref, cos_ref, sin_ref,
                  poolw_ref, pools_ref, wa2_ref, ba_ref, gng_ref, wout_ref,
                  poolwin_ref, rothalf_ref,
                  rdecay_ref, rxi_ref, rzeta_ref, rgc_ref, rhmk_ref, rhmv_ref,
                  smask_ref, hind_ref,
                  gtri_ref, gmask_ref, ghmk_ref, ghmv_ref,
                  o_ref,
                  hb_ref, proj_ref, ycat_ref, b_ref, dec_ref, oscr_ref, ext_ref, s2_ref, s4_ref, s8_ref,
                  hist_ref, sret_ref, sgla_ref, *, tile, sub):
    t_idx = pl.program_id(1)
    n_sub = tile // sub

    @pl.when(t_idx == 0)
    def _():
        hist_ref[...] = jnp.zeros_like(hist_ref)
        sret_ref[...] = jnp.zeros_like(sret_ref)
        sgla_ref[...] = jnp.zeros_like(sgla_ref)

    gate1 = mod_ref[0, 2:3, :]

    norm_scale = ng_ref[...] * (1.0 + mod_ref[0, 1:2, :])
    shift1 = mod_ref[0, 0:1, :]

    def projection_tasks(i):
        slot = i % 2

        def norm():
            x = x_ref[0, i * sub:(i + 1) * sub, :]
            ms = jnp.mean(x * x, axis=-1, keepdims=True)
            hb_ref[slot] = ((x * lax.rsqrt(ms + EPS)) * norm_scale + shift1).astype(_BF)

        def segment(off, width):
            def run():
                proj_ref[slot, :, off:off + width] = _dot(hb_ref[slot], win_ref[:, off:off + width])
            return run

        tasks = [(i, 0, norm)]
        for off in range(0, N_IN, PROJ_PIECE):
            width = min(PROJ_PIECE, N_IN - off)
            tasks.append((i, off + width, segment(off, width)))
        return tasks

    def key_block(pv, lrows, off):
        shift = off % LANES
        if shift == 0:
            return pv[lrows, off:off + KP]
        return pltpu.roll(pv[lrows, off - shift:off - shift + KP], KP - shift, axis=1)

    pending = projection_tasks(0)
    ticks = [0]

    def tick(weight=1):
        ticks[0] += weight
        if pending and ticks[0] >= PROJ_TICKS:
            ticks[0] = 0
            pending.pop(0)[2]()

    def require(i, column):
        while pending and (pending[0][0], pending[0][1]) <= (i, column):
            pending.pop(0)[2]()

    require(0, N_IN)

    def pooling(i):
        n = sub + POOL_HIST
        srows = slice(i * sub, (i + 1) * sub)
        u = proj_ref[i % 2, :, OFF_POOL:OFF_POOL + POOL_WIDTH]
        ext_ref[0:POOL_HIST, :] = hist_ref[...]
        ext_ref[POOL_HIST:n, :] = u
        hist_ref[...] = u[sub - POOL_HIST:sub, :]
        s2_ref[8:n, :] = ext_ref[8:n, :] + ext_ref[7:n - 1, :]
        s4_ref[16:n, :] = s2_ref[16:n, :] + s2_ref[14:n - 2, :]
        s8_ref[24:n, :] = s4_ref[24:n, :] + s4_ref[20:n - 4, :]
        s16 = s8_ref[32:n, :] + s8_ref[24:n - 8, :]
        lane = lax.broadcasted_iota(jnp.int32, (sub, POOL_WIDTH), 1)
        wsum = jnp.where(lane < POOL_GROUP_DIM, s2_ref[32:n, :],
                         jnp.where(lane < 2 * POOL_GROUP_DIM, s4_ref[32:n, :],
                                   jnp.where(lane < 3 * POOL_GROUP_DIM, s8_ref[32:n, :], s16)))
        t_abs = (t_idx * tile + i * sub
                 + lax.broadcasted_iota(jnp.int32, (sub, POOL_WIDTH), 0)).astype(_F32)
        cnt = jnp.minimum(t_abs + 1.0, poolwin_ref[...])
        pooled = wsum / cnt - u
        ycat_ref[srows, 0:POOL_WIDTH] = (_dot(pooled.astype(_BF), poolw_ref[...])
                                         * pools_ref[...]).astype(_BF)

    inv_dv = 1.0 / DV
    k_scale = DK ** -0.5
    q_scale = DK ** -0.5
    smask = smask_ref[...]

    def own_head(state):
        return state.astype(_BF) * smask
    hind = hind_ref[...]
    first_half = rothalf_ref[...] > 0.5

    def head_rms(o):
        ms_h = _dot((o * o).astype(_BF), hind) * inv_dv
        return o * lax.rsqrt(ms_h + EPS)

    def stack_heads(a):
        return jnp.concatenate([a] * N_HEADS, axis=0)

    def rotary(t, cosv, sinv):
        partner = jnp.where(first_half, pltpu.roll(t, KP - DK // 2, axis=1), pltpu.roll(t, DK // 2, axis=1))
        return t * cosv + partner * sinv

    def retention(i, r0, state):
        C = RET_CHUNK
        pv = proj_ref.at[i % 2]
        parts = []
        for c in range(sub // C):
            rows = slice(r0 + c * C, r0 + (c + 1) * C)
            lrows = slice(c * C, (c + 1) * C)
            cosv = cos_ref[0, rows, :]
            sinv = sin_ref[0, rows, :]
            q = rotary(key_block(pv, lrows, OFF_RQ), cosv, sinv)
            k = rotary(key_block(pv, lrows, OFF_RK), cosv, sinv) * k_scale
            vb = pv[lrows, OFF_RV:OFF_RV + VW].astype(_BF)
            qb = q.astype(_BF)
            kbd = stack_heads(k.astype(_BF)) * rhmk_ref[...]
            scores = _dot_nt(qb, kbd) * rdecay_ref[...]
            vbd = stack_heads(vb) * rhmv_ref[...]
            o_intra = _dot(scores.astype(_BF), vbd)
            upd = _dot_tn(vb, (k * rzeta_ref[...]).astype(_BF))
            parts.append((rows, qb, o_intra, upd))
            tick()
        for rows, qb, o_intra, upd in parts:
            oscr_ref[rows, 0:VW] = o_intra + _dot_nt(qb, own_head(state)) * rxi_ref[...]
            state = state * rgc_ref[...] + upd
            tick()
        return state

    def gla(i, r0, state):
        C = GLA_CHUNK
        n_chunks = sub // C
        W = n_chunks * KP
        pv = proj_ref.at[i % 2]
        logits = _dot(pv[:, OFF_GA:OFF_GA + LANES].astype(_BF), wa2_ref[...]) + ba_ref[...]
        log_sig = jnp.minimum(logits, 0.0) - jnp.log(1.0 + jnp.exp(-jnp.abs(logits)))
        la = log_sig * (LOG2_E / GLA_GATE_TAU)

        la_l = jnp.concatenate([la[c * C:(c + 1) * C, :] for c in range(n_chunks)], axis=1)
        tri = gtri_ref[...]
        p1, p2, p3 = _split3(la_l)
        b = _dot(tri, p1) + _dot(tri, p2) + _dot(tri, p3)
        b_ref[i % 2] = b

        row = lax.broadcasted_iota(jnp.int32, (C, W), 0)
        s = C // 2
        lvl = 0
        while s >= SUBLANES // 2:
            refs = [jnp.broadcast_to(b_ref[i % 2, p + s - 1:p + s, :], (2 * s, W)) for p in range(0, C, 2 * s)]
            b_at_ref = refs[0] if len(refs) == 1 else jnp.concatenate(refs, axis=0)
            dec_ref[i % 2, lvl * C:(lvl + 1) * C, :] = jnp.exp2(-jnp.abs(b - b_at_ref))
            s //= 2
            lvl += 1
            tick()
        up1 = pltpu.roll(b, 1, axis=0)
        up2 = pltpu.roll(b, 2, axis=0)
        dn1 = pltpu.roll(b, C - 1, axis=0)
        m4 = row % 4
        b_at_ref = jnp.where(m4 == 0, dn1, jnp.where(m4 == 1, b, jnp.where(m4 == 2, up1, up2)))
        dec_ref[i % 2, lvl * C:(lvl + 1) * C, :] = jnp.exp2(-jnp.abs(b - b_at_ref))
        lvl += 1
        tick()
        b_at_ref = jnp.where(row % 2 == 0, b, up1)
        dec_ref[i % 2, lvl * C:(lvl + 1) * C, :] = jnp.exp2(-jnp.abs(b - b_at_ref))
        lvl += 1
        tick()
        dec_ref[i % 2, lvl * C:(lvl + 1) * C, :] = jnp.exp2(b)
        tick()
        b_last = jnp.broadcast_to(b_ref[i % 2, C - 1:C, :], (C, W))
        dec_ref[i % 2, (lvl + 1) * C:(lvl + 2) * C, :] = jnp.exp2(b_last - b)
        tick()

        hmk = ghmk_ref[...]
        parts = []
        for c in range(n_chunks):
            rows = slice(r0 + c * C, r0 + (c + 1) * C)
            lrows = slice(c * C, (c + 1) * C)
            lanes = slice(c * KP, (c + 1) * KP)
            q = key_block(pv, lrows, OFF_GQ) * q_scale
            k = key_block(pv, lrows, OFF_GK)
            vb = pv[lrows, OFF_GV:OFF_GV + VW].astype(_BF)
            qb = q.astype(_BF)
            kbd0 = stack_heads(k.astype(_BF)) * hmk
            scores = jnp.zeros((C, N_HEADS * C), _F32)
            for lvl in range(N_GLA_LEVELS + 1):
                if lvl < N_GLA_LEVELS:
                    eb = dec_ref[i % 2, lvl * C:(lvl + 1) * C, lanes].astype(_BF)
                    ql, kbd = qb * eb, kbd0 * stack_heads(eb)
                else:
                    ql, kbd = qb, kbd0
                scores = scores + _dot_nt(ql, kbd) * gmask_ref[lvl]
                tick()
            vbd = stack_heads(vb) * ghmv_ref[...]
            e_cum = dec_ref[i % 2, N_GLA_LEVELS * C:(N_GLA_LEVELS + 1) * C, lanes]
            e_rev = dec_ref[i % 2, (N_GLA_LEVELS + 1) * C:(N_GLA_LEVELS + 2) * C, lanes]
            o_intra = _dot(scores.astype(_BF), vbd)
            upd = _dot_tn(vb, (k * e_rev).astype(_BF))
            parts.append((rows, (q * e_cum).astype(_BF), o_intra, upd, e_cum[C - 1:C, :]))
        for rows, qe, o_intra, upd, e_last in parts:
            oscr_ref[rows, VW:2 * VW] = o_intra + _dot_nt(qe, own_head(state))
            state = state * e_last + upd
            tick()
        return state

    ret_state = sret_ref[...]
    gla_state = sgla_ref[...]
    for i in range(n_sub):
        r0 = i * sub
        srows = slice(r0, r0 + sub)
        if i + 1 < n_sub:
            pending.extend(projection_tasks(i + 1))
        tick()
        pooling(i)
        tick()
        ret_state = retention(i, r0, ret_state)
        gla_state = gla(i, r0, gla_state)
        y_ret = _silu(proj_ref[i % 2, :, OFF_RG:OFF_RG + VW]) * head_rms(oscr_ref[srows, 0:VW])
        ycat_ref[srows, POOL_WIDTH:POOL_WIDTH + VW] = y_ret.astype(_BF)
        gg_off = OFF_GG % LANES
        gg = pltpu.roll(proj_ref[i % 2, :, OFF_GG - gg_off:N_IN], N_IN - OFF_GG, axis=1)[:, 0:VW]
        y_gla = (_silu(gg)
                 * (head_rms(oscr_ref[srows, VW:2 * VW]) * gng_ref[...]))
        require(i + 1, N_IN)
        ycat_ref[srows, POOL_WIDTH + VW:POOL_WIDTH + 2 * VW] = y_gla.astype(_BF)
        o_ref[0, srows, :] = x_ref[0, srows, :] + gate1 * _dot(ycat_ref[srows, :], wout_ref[...])
    sret_ref[...] = ret_state
    sgla_ref[...] = gla_state


def _whole(shape):
    nd = len(shape)
    return pl.BlockSpec(shape, lambda b, t: (0,) * nd, pipeline_mode=pl.Buffered(1))


def _layer_slab(shape, layer):
    nd = len(shape) - 1
    return pl.BlockSpec((None,) + tuple(shape[1:]), lambda b, t: (layer,) + (0,) * nd,
                        pipeline_mode=pl.Buffered(1))


def _mixer(x, mod, layer, params, cos, sin, tb):
    B, T, D = x.shape
    tile = MIX_TILE if T % MIX_TILE == 0 else T
    sub = MIX_SUB if tile % MIX_SUB == 0 else tile
    consts = [tb[k] for k in ("pool_win", "rot_first_half", "ret_decay", "ret_xi", "ret_zeta", "ret_gc",
                              "ret_hmk", "ret_hmv", "state_mask", "head_ind", "gla_tri", "gla_mask",
                              "gla_hmk", "gla_hmv")]
    norm_g, win = params[0], params[1]
    rest = list(params[2:])
    n_sub = tile // sub
    args = [x, mod, norm_g, win, cos, sin] + rest + consts
    in_specs = [
        pl.BlockSpec((1, tile, D), lambda b, t: (b, t, 0)),
        pl.BlockSpec((None, 1, 6, D), lambda b, t: (layer, b, 0, 0)),
        _layer_slab(norm_g.shape, layer),
        _layer_slab(win.shape, layer),
        pl.BlockSpec((1, tile, KP), lambda b, t: (b, t, 0)),
        pl.BlockSpec((1, tile, KP), lambda b, t: (b, t, 0)),
    ] + [_layer_slab(a.shape, layer) for a in rest] + [_whole(a.shape) for a in consts]
    n = sub + POOL_HIST
    gla_lanes = (sub // GLA_CHUNK) * KP
    return pl.pallas_call(
        functools.partial(_mixer_kernel, tile=tile, sub=sub),
        grid=(B, T // tile),
        in_specs=in_specs,
        out_specs=pl.BlockSpec((1, tile, D), lambda b, t: (b, t, 0)),
        out_shape=jax.ShapeDtypeStruct((B, T, D), _F32),
        scratch_shapes=[
            pltpu.VMEM((2, sub, D), _BF),
            pltpu.VMEM((2, sub, N_IN), _F32),
            pltpu.VMEM((tile, POOL_WIDTH + 2 * VW), _BF),
            pltpu.VMEM((2, GLA_CHUNK, gla_lanes), _F32),
            pltpu.VMEM((2, (N_GLA_LEVELS + 2) * GLA_CHUNK, gla_lanes), _F32),
            pltpu.VMEM((tile, 2 * VW), _F32),
            pltpu.VMEM((n, POOL_WIDTH), _F32),
            pltpu.VMEM((n, POOL_WIDTH), _F32),
            pltpu.VMEM((n, POOL_WIDTH), _F32),
            pltpu.VMEM((n, POOL_WIDTH), _F32),
            pltpu.VMEM((POOL_HIST, POOL_WIDTH), _F32),
            pltpu.VMEM((VW, KP), _F32),
            pltpu.VMEM((VW, KP), _F32),
        ],
        compiler_params=pltpu.CompilerParams(
            dimension_semantics=("arbitrary", "arbitrary"), vmem_limit_bytes=VMEM_LIMIT_BYTES),
        name="token_mixer",
    )(*args)


def _ffn_kernel(x_ref, mod_ref, ng_ref, wup_ref, cw_ref, wd_ref, fg_ref,
                o_ref, h_ref, carry_ref, act_ref, *, tile, d_ff, fb, final_norm):
    t_idx = pl.program_id(1)
    S = SUBLANES
    R = tile // S
    D = x_ref.shape[-1]

    @pl.when(t_idx == 0)
    def _():
        carry_ref[...] = jnp.zeros_like(carry_ref)

    shift2 = mod_ref[0, 3:4, :]
    scale2 = mod_ref[0, 4:5, :]
    gate2 = mod_ref[0, 5:6, :]
    norm_scale = ng_ref[...] * (1.0 + scale2)
    groups = FFN_NORM_ROWS // S
    early_cols = ([j * fb for j in range(FFN_EARLY_BLOCKS)]
                  + [d_ff + j * fb for j in range(FFN_EARLY_BLOCKS)])
    early = {col: [] for col in early_cols}
    x_parts = []
    for p in range(tile // FFN_NORM_ROWS):
        prow = slice(p * FFN_NORM_ROWS, (p + 1) * FFN_NORM_ROWS)
        xs = jnp.concatenate([x_ref[0, s * R + p * groups:s * R + (p + 1) * groups, :]
                              for s in range(S)], axis=0)
        xs = xs.reshape(S, groups, D).swapaxes(0, 1).reshape(FFN_NORM_ROWS, D)
        ms = jnp.mean(xs * xs, axis=-1, keepdims=True)
        h_ref[prow, :] = ((xs * lax.rsqrt(ms + EPS)) * norm_scale + shift2).astype(_BF)
        x_parts.append(xs)
        for col in early_cols:
            early[col].append(_dot(h_ref[prow, :], wup_ref[:, col:col + fb]))
    x = jnp.concatenate(x_parts, axis=0)
    first_sublane = lax.broadcasted_iota(jnp.int32, (S, fb), 0) == 0

    def up_conv(col):
        cols = slice(col, col + fb)
        if col in early:
            u = jnp.concatenate(early[col], axis=0)
        else:
            u = _dot(h_ref[...], wup_ref[:, cols])
        back1 = jnp.where(first_sublane, pltpu.roll(carry_ref[S:2 * S, cols], 1, axis=0),
                          pltpu.roll(u[tile - S:tile, :], 1, axis=0))
        back2 = jnp.where(first_sublane, pltpu.roll(carry_ref[0:S, cols], 1, axis=0),
                          pltpu.roll(u[tile - 2 * S:tile - S, :], 1, axis=0))
        carry_ref[:, cols] = u[tile - 2 * S:tile, :]
        prev1 = jnp.concatenate([back1, u[0:tile - S, :]], axis=0)
        prev2 = jnp.concatenate([back2, back1, u[0:tile - 2 * S, :]], axis=0)
        cw = cw_ref[:, cols]
        return prev2 * cw[0:1, :] + prev1 * cw[1:2, :] + u * cw[2:3, :] + cw[3:4, :]

    for j in range(d_ff // fb):
        ya = up_conv(j * fb)
        yg = up_conv(d_ff + j * fb)
        act_ref[:, j * fb:(j + 1) * fb] = (_silu(yg) * ya).astype(_BF)

    out = x + gate2 * _dot(act_ref[...], wd_ref[...])
    if final_norm:
        ms_o = jnp.mean(out * out, axis=-1, keepdims=True)
        out = out * lax.rsqrt(ms_o + EPS) * fg_ref[...]
    o_ref[0] = out.reshape(R, S, D).swapaxes(0, 1).reshape(tile, D)


def _ffn(x, mod, layer, params, final_g, final_norm):
    B, T, D = x.shape
    tile = FFN_TILE if T % FFN_TILE == 0 else T
    norm_g, wup, cw, wd = params
    d_ff = wd.shape[1]
    in_specs = [
        pl.BlockSpec((1, tile, D), lambda b, t: (b, t, 0)),
        pl.BlockSpec((None, 1, 6, D), lambda b, t: (layer, b, 0, 0)),
    ] + [_layer_slab(a.shape, layer) for a in params] + [_whole(final_g.shape)]
    return pl.pallas_call(
        functools.partial(_ffn_kernel, tile=tile, d_ff=d_ff, fb=FFN_BLOCK, final_norm=final_norm),
        grid=(B, T // tile),
        in_specs=in_specs,
        out_specs=pl.BlockSpec((1, tile, D), lambda b, t: (b, t, 0)),
        out_shape=jax.ShapeDtypeStruct((B, T, D), _F32),
        scratch_shapes=[
            pltpu.VMEM((tile, D), _BF),
            pltpu.VMEM((2 * SUBLANES, 2 * d_ff), _F32),
            pltpu.VMEM((tile, d_ff), _BF),
        ],
        compiler_params=pltpu.CompilerParams(
            dimension_semantics=("arbitrary", "arbitrary"), vmem_limit_bytes=VMEM_LIMIT_BYTES),
        name="conv_ffn",
    )(x, mod, *params, final_g)


def _prep_mixer_params(norm1_g, w_in, pool_w, pool_scale, gla_wa2, gla_ba, gla_norm_g, w_out):
    L, D, _ = w_in.shape
    win = jnp.pad(w_in.astype(_BF), ((0, 0), (0, 0), (0, N_IN - D_IN)))
    G = pool_w.shape[1]
    eye = jnp.eye(G, dtype=pool_w.dtype)
    poolw = (pool_w[:, :, :, None, :] * eye[None, :, None, :, None]).reshape(L, POOL_WIDTH, POOL_WIDTH)
    wa2 = jnp.pad(gla_wa2, ((0, 0), (0, LANES - GLA_GATE_RANK), (0, KP - KW)))
    ba = jnp.pad(gla_ba, ((0, 0), (0, KP - KW))).reshape(L, 1, KP)
    return (norm1_g.reshape(L, 1, D), win, poolw.astype(_BF), pool_scale.reshape(L, 1, POOL_WIDTH),
            wa2.astype(_BF), ba, gla_norm_g.reshape(L, 1, VW), w_out.astype(_BF))


def _prep_ffn_params(norm2_g, w_up, conv_w, conv_b, w_down):
    L, D, _ = w_up.shape
    cw = jnp.pad(jnp.concatenate([conv_w, conv_b[:, None, :]], axis=1),
                 ((0, 0), (0, SUBLANES - CONV_WIDTH - 1), (0, 0)))
    return norm2_g.reshape(L, 1, D), w_up.astype(_BF), cw, w_down.astype(_BF)


def kernel(x, c, positions, ada_w, ada_b, norm1_g, w_in, pool_w, pool_scale, gla_wa2, gla_ba,
           gla_norm_g, w_out, norm2_g, w_up, conv_w, conv_b, w_down, final_g):
    L = ada_w.shape[0]
    B, T, D = x.shape
    tb = {k: jnp.asarray(v) for k, v in _tables().items()}
    for name in ("head_ind", "gla_tri", "ret_hmk", "ret_hmv", "gla_hmk", "gla_hmv", "state_mask",
                 "rot_cos_expand", "rot_sin_expand"):
        tb[name] = tb[name].astype(_BF)
    mod = _modulation(c, ada_w, ada_b).reshape(L, B, 6, D)
    cos, sin = _rotary_tables(positions, tb)
    mixer_params = _prep_mixer_params(norm1_g, w_in, pool_w, pool_scale, gla_wa2, gla_ba, gla_norm_g, w_out)
    ffn_params = _prep_ffn_params(norm2_g, w_up, conv_w, conv_b, w_down)
    fg = final_g.reshape(1, D)
    for l in range(L):
        x = _mixer(x, mod, l, mixer_params, cos, sin, tb)
        x = _ffn(x, mod, l, ffn_params, fg, final_norm=(l == L - 1))
    return x
```

```python
import functools

import numpy as np
import jax
import jax.numpy as jnp
from jax import lax
from jax.experimental import pallas as pl
from jax.experimental.pallas import tpu as pltpu

POOL_WIDTH = 256
POOL_WINDOWS = (2, 4, 8, 16)
POOL_GROUP_DIM = 64
N_HEADS = 4
DK = 48
DV = 96
KW = N_HEADS * DK
VW = N_HEADS * DV
RET_CHUNK = 128
GLA_CHUNK = 64
GLA_GATE_RANK = 16
GLA_GATE_TAU = 16.0
ROPE_BASE = 10000.0
CONV_WIDTH = 3
EPS = 1e-6

LANES = 128
SUBLANES = 8
KP = 256
VMEM_LIMIT_BYTES = 56 * 1024 * 1024

OFF_POOL = 0
OFF_RQ = OFF_POOL + POOL_WIDTH
OFF_RK = OFF_RQ + KW
OFF_RV = OFF_RK + KW
OFF_RG = OFF_RV + VW
OFF_GQ = OFF_RG + VW
OFF_GK = OFF_GQ + KW
OFF_GV = OFF_GK + KW
OFF_GA = OFF_GV + VW
OFF_GG = OFF_GA + GLA_GATE_RANK
D_IN = OFF_GG + VW
N_IN = -(-D_IN // LANES) * LANES

MIX_TILE = 512
MIX_SUB = 256
PROJ_PIECE = 256
PROJ_TICKS = 3
FFN_TILE = 512
FFN_BLOCK = 256
FFN_NORM_ROWS = 128
FFN_EARLY_BLOCKS = 2
ROT_TILE = 1024
ROT_ROWS = 32
POOL_HIST = 32
N_GLA_LEVELS = 6
CAST_ROWS = 128
LOG2_E = 1.4426950408889634

_BF = jnp.bfloat16
_F32 = jnp.float32


def _dot(a, b):
    return jnp.dot(a, b, preferred_element_type=_F32)


def _dot_nt(a, b):
    return lax.dot_general(a, b, (((1,), (1,)), ((), ())), preferred_element_type=_F32)


def _dot_tn(a, b):
    return lax.dot_general(a, b, (((0,), (0,)), ((), ())), preferred_element_type=_F32)


def _split3(x):
    p1 = x.astype(_BF)
    r1 = x - p1.astype(_F32)
    p2 = r1.astype(_BF)
    p3 = (r1 - p2.astype(_F32)).astype(_BF)
    return p1, p2, p3


def _cast_weight(src_ref, dst_ref):
    n_rows, n_cols = src_ref.shape
    whole = (n_cols // LANES) * LANES
    for r in range(0, n_rows, CAST_ROWS):
        rows = slice(r, r + CAST_ROWS)
        dst_ref[rows, 0:whole] = src_ref[rows, 0:whole].astype(_BF)
        if dst_ref.shape[1] > whole:
            dst_ref[rows, whole:] = jnp.zeros((CAST_ROWS, dst_ref.shape[1] - whole), _BF)
        if n_cols > whole:
            dst_ref[rows, whole:n_cols] = src_ref[rows, whole:n_cols].astype(_BF)


def _sigmoid(x):
    return 1.0 / (1.0 + jnp.exp(-x))


def _silu(x):
    return x * _sigmoid(x)


def _key_head(d):
    return np.where(d < KW, d // DK, -1)


@functools.lru_cache(maxsize=None)
def _tables():
    t = {}
    kd = _key_head(np.arange(KP))
    vd = np.arange(VW) // DV

    def head_masks(chunk):
        rows = np.repeat(np.arange(N_HEADS), chunk)
        return ((rows[:, None] == kd[None, :]).astype(np.float32),
                (rows[:, None] == vd[None, :]).astype(np.float32))

    C = RET_CHUNK
    lg = np.log(1.0 - 2.0 ** (-5.0 - np.arange(N_HEADS, dtype=np.float64)))
    i = np.arange(C)
    rel = i[:, None] - i[None, :]
    dec = np.where(rel[None] >= 0, np.exp(np.maximum(rel, 0)[None] * lg[:, None, None]), 0.0)
    t["ret_decay"] = np.transpose(dec, (1, 0, 2)).reshape(C, N_HEADS * C).astype(np.float32)
    t["ret_xi"] = np.exp((i[:, None] + 1.0) * lg[vd][None, :]).astype(np.float32)
    zeta = np.exp((C - 1.0 - i)[:, None] * lg[np.maximum(kd, 0)][None, :]) * (kd >= 0)[None, :]
    t["ret_zeta"] = zeta.astype(np.float32)
    t["ret_gc"] = (np.exp(C * lg[np.maximum(kd, 0)]) * (kd >= 0))[None, :].astype(np.float32)
    t["ret_hmk"], t["ret_hmv"] = head_masks(C)
    t["state_mask"] = (vd[:, None] == kd[None, :]).astype(np.float32)
    t["head_ind"] = (vd[:, None] == vd[None, :]).astype(np.float32)

    C = GLA_CHUNK
    i = np.arange(C)
    masks = []
    s = C // 2
    while s >= 1:
        parent = (i // (2 * s)) * (2 * s)
        upper = (i - parent) >= s
        masks.append(upper[:, None] & (~upper)[None, :] & (parent[:, None] == parent[None, :]))
        s //= 2
    masks.append(i[:, None] == i[None, :])
    t["gla_mask"] = np.stack([np.tile(m, (1, N_HEADS)) for m in masks]).astype(np.float32)
    t["gla_tri"] = (i[None, :] <= i[:, None]).astype(np.float32)
    t["gla_hmk"], t["gla_hmv"] = head_masks(C)

    t["pool_win"] = np.repeat(np.asarray(POOL_WINDOWS, np.float32), POOL_GROUP_DIM)[None, :]

    half = DK // 2
    l = np.arange(KP)
    hit = (np.arange(ROT_ROWS)[:, None] == (l % half)[None, :]) & (l < KW)[None, :]
    t["rot_cos_expand"] = hit.astype(np.float32)
    t["rot_sin_expand"] = hit * np.where((l % DK) < half, -1.0, 1.0)[None, :].astype(np.float32)
    t["rot_first_half"] = ((l % DK) < half).astype(np.float32)[None, :]
    return t


def _mod_kernel(c_ref, w_ref, b_ref, o_ref):
    ca = _silu(c_ref[...]).astype(_BF)
    o_ref[...] = _dot(ca, w_ref[...].astype(_BF)) + b_ref[...]


def _modulation(c, ada_w, ada_b):
    L, D, N = ada_w.shape
    B = c.shape[0]
    tn = D
    return pl.pallas_call(
        _mod_kernel,
        grid=(L, N // tn),
        in_specs=[
            pl.BlockSpec((B, D), lambda l, n: (0, 0)),
            pl.BlockSpec((None, D, tn), lambda l, n: (l, 0, n)),
            pl.BlockSpec((None, 1, tn), lambda l, n: (l, 0, n)),
        ],
        out_specs=pl.BlockSpec((None, B, tn), lambda l, n: (l, 0, n)),
        out_shape=jax.ShapeDtypeStruct((L, B, N), _F32),
        compiler_params=pltpu.CompilerParams(
            dimension_semantics=("arbitrary", "arbitrary"), vmem_limit_bytes=VMEM_LIMIT_BYTES),
        name="adaln_modulation",
    )(c, ada_w, ada_b.reshape(L, 1, N))


def _rot_kernel(pos_ref, freq_ref, ec_ref, es_ref, cos_ref, sin_ref):
    ang = freq_ref[...] * pos_ref[0]
    ec = ec_ref[...]
    es = es_ref[...]

    def widen(v, e):
        p1, p2, p3 = _split3(v)
        return _dot_tn(p1, e) + _dot_tn(p2, e) + _dot_tn(p3, e)

    cos_ref[0] = widen(jnp.cos(ang), ec)
    sin_ref[0] = widen(jnp.sin(ang), es)


def _rotary_tables(positions, tb):
    B, T = positions.shape
    tile = ROT_TILE if T % ROT_TILE == 0 else T
    inv_freq = ROPE_BASE ** (-jnp.arange(0, DK, 2, dtype=_F32) / DK)
    freq = jnp.pad(inv_freq, (0, ROT_ROWS - DK // 2)).reshape(ROT_ROWS, 1)
    pos = positions.astype(_F32).reshape(B, 1, T)
    const = lambda b, t: (0, 0)
    return pl.pallas_call(
        _rot_kernel,
        grid=(B, T // tile),
        in_specs=[
            pl.BlockSpec((1, 1, tile), lambda b, t: (b, 0, t)),
            pl.BlockSpec((ROT_ROWS, 1), const),
            pl.BlockSpec((ROT_ROWS, KP), const),
            pl.BlockSpec((ROT_ROWS, KP), const),
        ],
        out_specs=[pl.BlockSpec((1, tile, KP), lambda b, t: (b, t, 0))] * 2,
        out_shape=[jax.ShapeDtypeStruct((B, T, KP), _F32)] * 2,
        compiler_params=pltpu.CompilerParams(dimension_semantics=("arbitrary", "arbitrary")),
        name="rotary_table",
    )(pos, freq, tb["rot_cos_expand"], tb["rot_sin_expand"])


def _mixer_kernel(x_ref, mod_ref, ng_ref, win32_ref, cos_ref, sin_ref,
                  poolw_ref, pools_ref, wa2_ref, ba_ref, gng_ref, wout32_ref,
                  poolwin_ref, rothalf_ref,
                  rdecay_ref, rxi_ref, rzeta_ref, rgc_ref, rhmk_ref, rhmv_ref,
                  smask_ref, hind_ref,
                  gtri_ref, gmask_ref, ghmk_ref, ghmv_ref,
                  o_ref,
                  win_ref, wout_ref,
                  hb_ref, proj_ref, ycat_ref, b_ref, dec_ref, oscr_ref, ext_ref, s2_ref, s4_ref, s8_ref,
                  hist_ref, sret_ref, sgla_ref, *, tile, sub):
    t_idx = pl.program_id(1)
    n_sub = tile // sub

    @pl.when((pl.program_id(0) == 0) & (t_idx == 0))
    def _():
        _cast_weight(win32_ref, win_ref)
        _cast_weight(wout32_ref, wout_ref)

    @pl.when(t_idx == 0)
    def _():
        hist_ref[...] = jnp.zeros_like(hist_ref)
        sret_ref[...] = jnp.zeros_like(sret_ref)
        sgla_ref[...] = jnp.zeros_like(sgla_ref)

    gate1 = mod_ref[0, 2:3, :]

    norm_scale = ng_ref[...] * (1.0 + mod_ref[0, 1:2, :])
    shift1 = mod_ref[0, 0:1, :]

    def projection_tasks(i):
        slot = i % 2

        def norm():
            x = x_ref[0, i * sub:(i + 1) * sub, :]
            ms = jnp.mean(x * x, axis=-1, keepdims=True)
            hb_ref[slot] = ((x * lax.rsqrt(ms + EPS)) * norm_scale + shift1).astype(_BF)

        def segment(off, width):
            def run():
                proj_ref[slot, :, off:off + width] = _dot(hb_ref[slot], win_ref[:, off:off + width])
            return run

        tasks = [(i, 0, norm)]
        for off in range(0, N_IN, PROJ_PIECE):
            width = min(PROJ_PIECE, N_IN - off)
            tasks.append((i, off + width, segment(off, width)))
        return tasks

    def key_block(pv, lrows, off):
        shift = off % LANES
        if shift == 0:
            return pv[lrows, off:off + KP]
        return pltpu.roll(pv[lrows, off - shift:off - shift + KP], KP - shift, axis=1)

    pending = projection_tasks(0)
    ticks = [0]

    def tick(weight=1):
        ticks[0] += weight
        if pending and ticks[0] >= PROJ_TICKS:
            ticks[0] = 0
            pending.pop(0)[2]()

    def require(i, column):
        while pending and (pending[0][0], pending[0][1]) <= (i, column):
            pending.pop(0)[2]()

    require(0, N_IN)

    def pooling(i):
        n = sub + POOL_HIST
        srows = slice(i * sub, (i + 1) * sub)
        u = proj_ref[i % 2, :, OFF_POOL:OFF_POOL + POOL_WIDTH]
        ext_ref[0:POOL_HIST, :] = hist_ref[...]
        ext_ref[POOL_HIST:n, :] = u
        hist_ref[...] = u[sub - POOL_HIST:sub, :]
        s2_ref[8:n, :] = ext_ref[8:n, :] + ext_ref[7:n - 1, :]
        s4_ref[16:n, :] = s2_ref[16:n, :] + s2_ref[14:n - 2, :]
        s8_ref[24:n, :] = s4_ref[24:n, :] + s4_ref[20:n - 4, :]
        s16 = s8_ref[32:n, :] + s8_ref[24:n - 8, :]
        lane = lax.broadcasted_iota(jnp.int32, (sub, POOL_WIDTH), 1)
        wsum = jnp.where(lane < POOL_GROUP_DIM, s2_ref[32:n, :],
                         jnp.where(lane < 2 * POOL_GROUP_DIM, s4_ref[32:n, :],
                                   jnp.where(lane < 3 * POOL_GROUP_DIM, s8_ref[32:n, :], s16)))
        t_abs = (t_idx * tile + i * sub
                 + lax.broadcasted_iota(jnp.int32, (sub, POOL_WIDTH), 0)).astype(_F32)
        cnt = jnp.minimum(t_abs + 1.0, poolwin_ref[...])
        pooled = wsum / cnt - u
        ycat_ref[srows, 0:POOL_WIDTH] = (_dot(pooled.astype(_BF), poolw_ref[...])
                                         * pools_ref[...]).astype(_BF)

    inv_dv = 1.0 / DV
    k_scale = DK ** -0.5
    q_scale = DK ** -0.5
    smask = smask_ref[...]

    def own_head(state):
        return state.astype(_BF) * smask
    hind = hind_ref[...]
    first_half = rothalf_ref[...] > 0.5

    def head_rms(o):
        ms_h = _dot((o * o).astype(_BF), hind) * inv_dv
        return o * lax.rsqrt(ms_h + EPS)

    def stack_heads(a):
        return jnp.concatenate([a] * N_HEADS, axis=0)

    def rotary(t, cosv, sinv):
        partner = jnp.where(first_half, pltpu.roll(t, KP - DK // 2, axis=1), pltpu.roll(t, DK // 2, axis=1))
        return t * cosv + partner * sinv

    def retention(i, r0, state):
        C = RET_CHUNK
        pv = proj_ref.at[i % 2]
        parts = []
        for c in range(sub // C):
            rows = slice(r0 + c * C, r0 + (c + 1) * C)
            lrows = slice(c * C, (c + 1) * C)
            cosv = cos_ref[0, rows, :]
            sinv = sin_ref[0, rows, :]
            q = rotary(key_block(pv, lrows, OFF_RQ), cosv, sinv)
            k = rotary(key_block(pv, lrows, OFF_RK), cosv, sinv) * k_scale
            vb = pv[lrows, OFF_RV:OFF_RV + VW].astype(_BF)
            qb = q.astype(_BF)
            kbd = stack_heads(k.astype(_BF)) * rhmk_ref[...]
            scores = _dot_nt(qb, kbd) * rdecay_ref[...]
            vbd = stack_heads(vb) * rhmv_ref[...]
            o_intra = _dot(scores.astype(_BF), vbd)
            upd = _dot_tn(vb, (k * rzeta_ref[...]).astype(_BF))
            parts.append((rows, qb, o_intra, upd))
            tick()
        for rows, qb, o_intra, upd in parts:
            oscr_ref[rows, 0:VW] = o_intra + _dot_nt(qb, own_head(state)) * rxi_ref[...]
            state = state * rgc_ref[...] + upd
            tick()
        return state

    def gla(i, r0, state):
        C = GLA_CHUNK
        n_chunks = sub // C
        W = n_chunks * KP
        pv = proj_ref.at[i % 2]
        logits = _dot(pv[:, OFF_GA:OFF_GA + LANES].astype(_BF), wa2_ref[...]) + ba_ref[...]
        log_sig = jnp.minimum(logits, 0.0) - jnp.log(1.0 + jnp.exp(-jnp.abs(logits)))
        la = log_sig * (LOG2_E / GLA_GATE_TAU)

        la_l = jnp.concatenate([la[c * C:(c + 1) * C, :] for c in range(n_chunks)], axis=1)
        tri = gtri_ref[...]
        p1, p2, p3 = _split3(la_l)
        b = _dot(tri, p1) + _dot(tri, p2) + _dot(tri, p3)
        b_ref[i % 2] = b

        row = lax.broadcasted_iota(jnp.int32, (C, W), 0)
        s = C // 2
        lvl = 0
        while s >= SUBLANES // 2:
            refs = [jnp.broadcast_to(b_ref[i % 2, p + s - 1:p + s, :], (2 * s, W)) for p in range(0, C, 2 * s)]
            b_at_ref = refs[0] if len(refs) == 1 else jnp.concatenate(refs, axis=0)
            dec_ref[i % 2, lvl * C:(lvl + 1) * C, :] = jnp.exp2(-jnp.abs(b - b_at_ref))
            s //= 2
            lvl += 1
            tick()
        up1 = pltpu.roll(b, 1, axis=0)
        up2 = pltpu.roll(b, 2, axis=0)
        dn1 = pltpu.roll(b, C - 1, axis=0)
        m4 = row % 4
        b_at_ref = jnp.where(m4 == 0, dn1, jnp.where(m4 == 1, b, jnp.where(m4 == 2, up1, up2)))
        dec_ref[i % 2, lvl * C:(lvl + 1) * C, :] = jnp.exp2(-jnp.abs(b - b_at_ref))
        lvl += 1
        tick()
        b_at_ref = jnp.where(row % 2 == 0, b, up1)
        dec_ref[i % 2, lvl * C:(lvl + 1) * C, :] = jnp.exp2(-jnp.abs(b - b_at_ref))
        lvl += 1
        tick()
        dec_ref[i % 2, lvl * C:(lvl + 1) * C, :] = jnp.exp2(b)
        tick()
        b_last = jnp.broadcast_to(b_ref[i % 2, C - 1:C, :], (C, W))
        dec_ref[i % 2, (lvl + 1) * C:(lvl + 2) * C, :] = jnp.exp2(b_last - b)
        tick()

        hmk = ghmk_ref[...]
        parts = []
        for c in range(n_chunks):
            rows = slice(r0 + c * C, r0 + (c + 1) * C)
            lrows = slice(c * C, (c + 1) * C)
            lanes = slice(c * KP, (c + 1) * KP)
            q = key_block(pv, lrows, OFF_GQ) * q_scale
            k = key_block(pv, lrows, OFF_GK)
            vb = pv[lrows, OFF_GV:OFF_GV + VW].astype(_BF)
            qb = q.astype(_BF)
            kbd0 = stack_heads(k.astype(_BF)) * hmk
            scores = jnp.zeros((C, N_HEADS * C), _F32)
            for lvl in range(N_GLA_LEVELS + 1):
                if lvl < N_GLA_LEVELS:
                    eb = dec_ref[i % 2, lvl * C:(lvl + 1) * C, lanes].astype(_BF)
                    ql, kbd = qb * eb, kbd0 * stack_heads(eb)
                else:
                    ql, kbd = qb, kbd0
                scores = scores + _dot_nt(ql, kbd) * gmask_ref[lvl]
                tick()
            vbd = stack_heads(vb) * ghmv_ref[...]
            e_cum = dec_ref[i % 2, N_GLA_LEVELS * C:(N_GLA_LEVELS + 1) * C, lanes]
            e_rev = dec_ref[i % 2, (N_GLA_LEVELS + 1) * C:(N_GLA_LEVELS + 2) * C, lanes]
            o_intra = _dot(scores.astype(_BF), vbd)
            upd = _dot_tn(vb, (k * e_rev).astype(_BF))
            parts.append((rows, (q * e_cum).astype(_BF), o_intra, upd, e_cum[C - 1:C, :]))
        for rows, qe, o_intra, upd, e_last in parts:
            oscr_ref[rows, VW:2 * VW] = o_intra + _dot_nt(qe, own_head(state))
            state = state * e_last + upd
            tick()
        return state

    ret_state = sret_ref[...]
    gla_state = sgla_ref[...]
    for i in range(n_sub):
        r0 = i * sub
        srows = slice(r0, r0 + sub)
        if i + 1 < n_sub:
            pending.extend(projection_tasks(i + 1))
        tick()
        pooling(i)
        tick()
        ret_state = retention(i, r0, ret_state)
        gla_state = gla(i, r0, gla_state)
        y_ret = _silu(proj_ref[i % 2, :, OFF_RG:OFF_RG + VW]) * head_rms(oscr_ref[srows, 0:VW])
        ycat_ref[srows, POOL_WIDTH:POOL_WIDTH + VW] = y_ret.astype(_BF)
        gg_off = OFF_GG % LANES
        gg = pltpu.roll(proj_ref[i % 2, :, OFF_GG - gg_off:N_IN], N_IN - OFF_GG, axis=1)[:, 0:VW]
        y_gla = (_silu(gg)
                 * (head_rms(oscr_ref[srows, VW:2 * VW]) * gng_ref[...]))
        require(i + 1, N_IN)
        ycat_ref[srows, POOL_WIDTH + VW:POOL_WIDTH + 2 * VW] = y_gla.astype(_BF)
        o_ref[0, srows, :] = x_ref[0, srows, :] + gate1 * _dot(ycat_ref[srows, :], wout_ref[...])
    sret_ref[...] = ret_state
    sgla_ref[...] = gla_state


def _whole(shape):
    nd = len(shape)
    return pl.BlockSpec(shape, lambda b, t: (0,) * nd, pipeline_mode=pl.Buffered(1))


def _layer_slab(shape, layer):
    nd = len(shape) - 1
    return pl.BlockSpec((None,) + tuple(shape[1:]), lambda b, t: (layer,) + (0,) * nd,
                        pipeline_mode=pl.Buffered(1))


def _mixer(x, mod, layer, params, cos, sin, tb):
    B, T, D = x.shape
    tile = MIX_TILE if T % MIX_TILE == 0 else T
    sub = MIX_SUB if tile % MIX_SUB == 0 else tile
    consts = [tb[k] for k in ("pool_win", "rot_first_half", "ret_decay", "ret_xi", "ret_zeta", "ret_gc",
                              "ret_hmk", "ret_hmv", "state_mask", "head_ind", "gla_tri", "gla_mask",
                              "gla_hmk", "gla_hmv")]
    norm_g, win = params[0], params[1]
    rest = list(params[2:])
    n_sub = tile // sub
    args = [x, mod, norm_g, win, cos, sin] + rest + consts
    in_specs = [
        pl.BlockSpec((1, tile, D), lambda b, t: (b, t, 0)),
        pl.BlockSpec((None, 1, 6, D), lambda b, t: (layer, b, 0, 0)),
        _layer_slab(norm_g.shape, layer),
        _layer_slab(win.shape, layer),
        pl.BlockSpec((1, tile, KP), lambda b, t: (b, t, 0)),
        pl.BlockSpec((1, tile, KP), lambda b, t: (b, t, 0)),
    ] + [_layer_slab(a.shape, layer) for a in rest] + [_whole(a.shape) for a in consts]
    n = sub + POOL_HIST
    gla_lanes = (sub // GLA_CHUNK) * KP
    return pl.pallas_call(
        functools.partial(_mixer_kernel, tile=tile, sub=sub),
        grid=(B, T // tile),
        in_specs=in_specs,
        out_specs=pl.BlockSpec((1, tile, D), lambda b, t: (b, t, 0)),
        out_shape=jax.ShapeDtypeStruct((B, T, D), _F32),
        scratch_shapes=[
            pltpu.VMEM((D, N_IN), _BF),
            pltpu.VMEM((POOL_WIDTH + 2 * VW, D), _BF),
            pltpu.VMEM((2, sub, D), _BF),
            pltpu.VMEM((2, sub, N_IN), _F32),
            pltpu.VMEM((tile, POOL_WIDTH + 2 * VW), _BF),
            pltpu.VMEM((2, GLA_CHUNK, gla_lanes), _F32),
            pltpu.VMEM((2, (N_GLA_LEVELS + 2) * GLA_CHUNK, gla_lanes), _F32),
            pltpu.VMEM((tile, 2 * VW), _F32),
            pltpu.VMEM((n, POOL_WIDTH), _F32),
            pltpu.VMEM((n, POOL_WIDTH), _F32),
            pltpu.VMEM((n, POOL_WIDTH), _F32),
            pltpu.VMEM((n, POOL_WIDTH), _F32),
            pltpu.VMEM((POOL_HIST, POOL_WIDTH), _F32),
            pltpu.VMEM((VW, KP), _F32),
            pltpu.VMEM((VW, KP), _F32),
        ],
        compiler_params=pltpu.CompilerParams(
            dimension_semantics=("arbitrary", "arbitrary"), vmem_limit_bytes=VMEM_LIMIT_BYTES),
        name="token_mixer",
    )(*args)


def _ffn_kernel(x_ref, mod_ref, ng_ref, wup_ref, cw_ref, wd32_ref, fg_ref,
                o_ref, wd_ref, h_ref, carry_ref, act_ref, *, tile, d_ff, fb, final_norm):
    t_idx = pl.program_id(1)
    S = SUBLANES
    R = tile // S
    D = x_ref.shape[-1]

    @pl.when(t_idx == 0)
    def _():
        carry_ref[...] = jnp.zeros_like(carry_ref)

    @pl.when((pl.program_id(0) == 0) & (t_idx == 0))
    def _():
        _cast_weight(wd32_ref, wd_ref)

    shift2 = mod_ref[0, 3:4, :]
    scale2 = mod_ref[0, 4:5, :]
    gate2 = mod_ref[0, 5:6, :]
    norm_scale = ng_ref[...] * (1.0 + scale2)
    groups = FFN_NORM_ROWS // S
    early_cols = ([j * fb for j in range(FFN_EARLY_BLOCKS)]
                  + [d_ff + j * fb for j in range(FFN_EARLY_BLOCKS)])
    early = {col: [] for col in early_cols}
    x_parts = []
    for p in range(tile // FFN_NORM_ROWS):
        prow = slice(p * FFN_NORM_ROWS, (p + 1) * FFN_NORM_ROWS)
        xs = jnp.concatenate([x_ref[0, s * R + p * groups:s * R + (p + 1) * groups, :]
                              for s in range(S)], axis=0)
        xs = xs.reshape(S, groups, D).swapaxes(0, 1).reshape(FFN_NORM_ROWS, D)
        ms = jnp.mean(xs * xs, axis=-1, keepdims=True)
        h_ref[prow, :] = ((xs * lax.rsqrt(ms + EPS)) * norm_scale + shift2).astype(_BF)
        x_parts.append(xs)
        for col in early_cols:
            early[col].append(_dot(h_ref[prow, :], wup_ref[:, col:col + fb]))
    x = jnp.concatenate(x_parts, axis=0)
    first_sublane = lax.broadcasted_iota(jnp.int32, (S, fb), 0) == 0

    def up_conv(col):
        cols = slice(col, col + fb)
        if col in early:
            u = jnp.concatenate(early[col], axis=0)
        else:
            u = _dot(h_ref[...], wup_ref[:, cols])
        back1 = jnp.where(first_sublane, pltpu.roll(carry_ref[S:2 * S, cols], 1, axis=0),
                          pltpu.roll(u[tile - S:tile, :], 1, axis=0))
        back2 = jnp.where(first_sublane, pltpu.roll(carry_ref[0:S, cols], 1, axis=0),
                          pltpu.roll(u[tile - 2 * S:tile - S, :], 1, axis=0))
        carry_ref[:, cols] = u[tile - 2 * S:tile, :]
        prev1 = jnp.concatenate([back1, u[0:tile - S, :]], axis=0)
        prev2 = jnp.concatenate([back2, back1, u[0:tile - 2 * S, :]], axis=0)
        cw = cw_ref[:, cols]
        return prev2 * cw[0:1, :] + prev1 * cw[1:2, :] + u * cw[2:3, :] + cw[3:4, :]

    for j in range(d_ff // fb):
        ya = up_conv(j * fb)
        yg = up_conv(d_ff + j * fb)
        act_ref[:, j * fb:(j + 1) * fb] = (_silu(yg) * ya).astype(_BF)

    out = x + gate2 * _dot(act_ref[...], wd_ref[...])
    if final_norm:
        ms_o = jnp.mean(out * out, axis=-1, keepdims=True)
        out = out * lax.rsqrt(ms_o + EPS) * fg_ref[...]
    o_ref[0] = out.reshape(R, S, D).swapaxes(0, 1).reshape(tile, D)


def _ffn(x, mod, layer, params, final_g, final_norm):
    B, T, D = x.shape
    tile = FFN_TILE if T % FFN_TILE == 0 else T
    norm_g, wup, cw, wd = params
    d_ff = wd.shape[1]
    in_specs = [
        pl.BlockSpec((1, tile, D), lambda b, t: (b, t, 0)),
        pl.BlockSpec((None, 1, 6, D), lambda b, t: (layer, b, 0, 0)),
    ] + [_layer_slab(a.shape, layer) for a in params] + [_whole(final_g.shape)]
    return pl.pallas_call(
        functools.partial(_ffn_kernel, tile=tile, d_ff=d_ff, fb=FFN_BLOCK, final_norm=final_norm),
        grid=(B, T // tile),
        in_specs=in_specs,
        out_specs=pl.BlockSpec((1, tile, D), lambda b, t: (b, t, 0)),
        out_shape=jax.ShapeDtypeStruct((B, T, D), _F32),
        scratch_shapes=[
            pltpu.VMEM((d_ff, D), _BF),
            pltpu.VMEM((tile, D), _BF),
            pltpu.VMEM((2 * SUBLANES, 2 * d_ff), _F32),
            pltpu.VMEM((tile, d_ff), _BF),
        ],
        compiler_params=pltpu.CompilerParams(
            dimension_semantics=("arbitrary", "arbitrary"), vmem_limit_bytes=VMEM_LIMIT_BYTES),
        name="conv_ffn",
    )(x, mod, *params, final_g)


def _prep_mixer_params(norm1_g, w_in, pool_w, pool_scale, gla_wa2, gla_ba, gla_norm_g, w_out):
    L, D, _ = w_in.shape
    G = pool_w.shape[1]
    eye = jnp.eye(G, dtype=pool_w.dtype)
    poolw = (pool_w[:, :, :, None, :] * eye[None, :, None, :, None]).reshape(L, POOL_WIDTH, POOL_WIDTH)
    wa2 = jnp.pad(gla_wa2, ((0, 0), (0, LANES - GLA_GATE_RANK), (0, KP - KW)))
    ba = jnp.pad(gla_ba, ((0, 0), (0, KP - KW))).reshape(L, 1, KP)
    return (norm1_g.reshape(L, 1, D), w_in, poolw.astype(_BF), pool_scale.reshape(L, 1, POOL_WIDTH),
            wa2.astype(_BF), ba, gla_norm_g.reshape(L, 1, VW), w_out)


def _prep_ffn_params(norm2_g, w_up, conv_w, conv_b, w_down):
    L, D, _ = w_up.shape
    cw = jnp.pad(jnp.concatenate([conv_w, conv_b[:, None, :]], axis=1),
                 ((0, 0), (0, SUBLANES - CONV_WIDTH - 1), (0, 0)))
    return norm2_g.reshape(L, 1, D), w_up.astype(_BF), cw, w_down


def kernel(x, c, positions, ada_w, ada_b, norm1_g, w_in, pool_w, pool_scale, gla_wa2, gla_ba,
           gla_norm_g, w_out, norm2_g, w_up, conv_w, conv_b, w_down, final_g):
    L = ada_w.shape[0]
    B, T, D = x.shape
    tb = {k: jnp.asarray(v) for k, v in _tables().items()}
    for name in ("head_ind", "gla_tri", "ret_hmk", "ret_hmv", "gla_hmk", "gla_hmv", "state_mask",
                 "rot_cos_expand", "rot_sin_expand"):
        tb[name] = tb[name].astype(_BF)
    mod = _modulation(c, ada_w, ada_b).reshape(L, B, 6, D)
    cos, sin = _rotary_tables(positions, tb)
    mixer_params = _prep_mixer_params(norm1_g, w_in, pool_w, pool_scale, gla_wa2, gla_ba, gla_norm_g, w_out)
    ffn_params = _prep_ffn_params(norm2_g, w_up, conv_w, conv_b, w_down)
    fg = final_g.reshape(1, D)
    for l in range(L):
        x = _mixer(x, mod, l, mixer_params, cos, sin, tb)
        x = _ffn(x, mod, l, ffn_params, fg, final_norm=(l == L - 1))
    return x
```

```python
import functools

import numpy as np
import jax
import jax.numpy as jnp
from jax import lax
from jax.experimental import pallas as pl
from jax.experimental.pallas import tpu as pltpu

POOL_WIDTH = 256
POOL_WINDOWS = (2, 4, 8, 16)
POOL_GROUP_DIM = 64
N_HEADS = 4
DK = 48
DV = 96
KW = N_HEADS * DK
VW = N_HEADS * DV
RET_CHUNK = 128
GLA_CHUNK = 64
GLA_GATE_RANK = 16
GLA_GATE_TAU = 16.0
ROPE_BASE = 10000.0
CONV_WIDTH = 3
EPS = 1e-6

LANES = 128
SUBLANES = 8
KP = 256
VMEM_LIMIT_BYTES = 56 * 1024 * 1024

OFF_POOL = 0
OFF_RQ = OFF_POOL + POOL_WIDTH
OFF_RK = OFF_RQ + KW
OFF_RV = OFF_RK + KW
OFF_RG = OFF_RV + VW
OFF_GQ = OFF_RG + VW
OFF_GK = OFF_GQ + KW
OFF_GV = OFF_GK + KW
OFF_GA = OFF_GV + VW
OFF_GG = OFF_GA + GLA_GATE_RANK
D_IN = OFF_GG + VW
N_IN = -(-D_IN // LANES) * LANES

MIX_TILE = 512
MIX_SUB = 256
PROJ_PIECE = 256
PROJ_TICKS = 3
FFN_TILE = 512
FFN_BLOCK = 256
FFN_NORM_ROWS = 128
FFN_EARLY_BLOCKS = 2
ROT_TILE = 1024
ROT_ROWS = 32
POOL_HIST = 32
N_GLA_LEVELS = 6
CAST_ROWS = 128
LOG2_E = 1.4426950408889634

_BF = jnp.bfloat16
_F32 = jnp.float32


def _dot(a, b):
    return jnp.dot(a, b, preferred_element_type=_F32)


def _dot_nt(a, b):
    return lax.dot_general(a, b, (((1,), (1,)), ((), ())), preferred_element_type=_F32)


def _dot_tn(a, b):
    return lax.dot_general(a, b, (((0,), (0,)), ((), ())), preferred_element_type=_F32)


def _split3(x):
    p1 = x.astype(_BF)
    r1 = x - p1.astype(_F32)
    p2 = r1.astype(_BF)
    p3 = (r1 - p2.astype(_F32)).astype(_BF)
    return p1, p2, p3


def _cast_weight(src_ref, dst_ref):
    n_rows, n_cols = src_ref.shape
    whole = (n_cols // LANES) * LANES
    for r in range(0, n_rows, CAST_ROWS):
        rows = slice(r, r + CAST_ROWS)
        dst_ref[rows, 0:whole] = src_ref[rows, 0:whole].astype(_BF)
        if dst_ref.shape[1] > whole:
            dst_ref[rows, whole:] = jnp.zeros((CAST_ROWS, dst_ref.shape[1] - whole), _BF)
        if n_cols > whole:
            dst_ref[rows, whole:n_cols] = src_ref[rows, whole:n_cols].astype(_BF)


def _sigmoid(x):
    return 1.0 / (1.0 + jnp.exp(-x))


def _silu(x):
    return x * _sigmoid(x)


def _key_head(d):
    return np.where(d < KW, d // DK, -1)


@functools.lru_cache(maxsize=None)
def _tables():
    t = {}
    kd = _key_head(np.arange(KP))
    vd = np.arange(VW) // DV

    def head_masks(chunk):
        rows = np.repeat(np.arange(N_HEADS), chunk)
        return ((rows[:, None] == kd[None, :]).astype(np.float32),
                (rows[:, None] == vd[None, :]).astype(np.float32))

    C = RET_CHUNK
    lg = np.log(1.0 - 2.0 ** (-5.0 - np.arange(N_HEADS, dtype=np.float64)))
    i = np.arange(C)
    rel = i[:, None] - i[None, :]
    dec = np.where(rel[None] >= 0, np.exp(np.maximum(rel, 0)[None] * lg[:, None, None]), 0.0)
    t["ret_decay"] = np.transpose(dec, (1, 0, 2)).reshape(C, N_HEADS * C).astype(np.float32)
    t["ret_xi"] = np.exp((i[:, None] + 1.0) * lg[vd][None, :]).astype(np.float32)
    zeta = np.exp((C - 1.0 - i)[:, None] * lg[np.maximum(kd, 0)][None, :]) * (kd >= 0)[None, :]
    t["ret_zeta"] = zeta.astype(np.float32)
    t["ret_gc"] = (np.exp(C * lg[np.maximum(kd, 0)]) * (kd >= 0))[None, :].astype(np.float32)
    t["ret_hmk"], t["ret_hmv"] = head_masks(C)
    t["state_mask"] = (vd[:, None] == kd[None, :]).astype(np.float32)
    t["head_ind"] = (vd[:, None] == vd[None, :]).astype(np.float32)

    C = GLA_CHUNK
    i = np.arange(C)
    masks = []
    s = C // 2
    while s >= 1:
        parent = (i // (2 * s)) * (2 * s)
        upper = (i - parent) >= s
        masks.append(upper[:, None] & (~upper)[None, :] & (parent[:, None] == parent[None, :]))
        s //= 2
    masks.append(i[:, None] == i[None, :])
    t["gla_mask"] = np.stack([np.tile(m, (1, N_HEADS)) for m in masks]).astype(np.float32)
    t["gla_tri"] = (i[None, :] <= i[:, None]).astype(np.float32)
    t["gla_hmk"], t["gla_hmv"] = head_masks(C)

    t["pool_win"] = np.repeat(np.asarray(POOL_WINDOWS, np.float32), POOL_GROUP_DIM)[None, :]

    half = DK // 2
    l = np.arange(KP)
    hit = (np.arange(ROT_ROWS)[:, None] == (l % half)[None, :]) & (l < KW)[None, :]
    t["rot_cos_expand"] = hit.astype(np.float32)
    t["rot_sin_expand"] = hit * np.where((l % DK) < half, -1.0, 1.0)[None, :].astype(np.float32)
    t["rot_first_half"] = ((l % DK) < half).astype(np.float32)[None, :]
    return t


def _mod_kernel(c_ref, w_ref, b_ref, o_ref):
    ca = _silu(c_ref[...]).astype(_BF)
    o_ref[...] = _dot(ca, w_ref[...].astype(_BF)) + b_ref[...]


def _modulation(c, ada_w, ada_b):
    L, D, N = ada_w.shape
    B = c.shape[0]
    tn = D
    return pl.pallas_call(
        _mod_kernel,
        grid=(L, N // tn),
        in_specs=[
            pl.BlockSpec((B, D), lambda l, n: (0, 0)),
            pl.BlockSpec((None, D, tn), lambda l, n: (l, 0, n)),
            pl.BlockSpec((None, 1, tn), lambda l, n: (l, 0, n)),
        ],
        out_specs=pl.BlockSpec((None, B, tn), lambda l, n: (l, 0, n)),
        out_shape=jax.ShapeDtypeStruct((L, B, N), _F32),
        compiler_params=pltpu.CompilerParams(
            dimension_semantics=("arbitrary", "arbitrary"), vmem_limit_bytes=VMEM_LIMIT_BYTES),
        name="adaln_modulation",
    )(c, ada_w, ada_b.reshape(L, 1, N))


def _rot_kernel(pos_ref, freq_ref, ec_ref, es_ref, cos_ref, sin_ref):
    ang = freq_ref[...] * pos_ref[0]
    ec = ec_ref[...]
    es = es_ref[...]

    def widen(v, e):
        p1, p2, p3 = _split3(v)
        return _dot_tn(p1, e) + _dot_tn(p2, e) + _dot_tn(p3, e)

    cos_ref[0] = widen(jnp.cos(ang), ec)
    sin_ref[0] = widen(jnp.sin(ang), es)


def _rotary_tables(positions, tb):
    B, T = positions.shape
    tile = ROT_TILE if T % ROT_TILE == 0 else T
    inv_freq = ROPE_BASE ** (-jnp.arange(0, DK, 2, dtype=_F32) / DK)
    freq = jnp.pad(inv_freq, (0, ROT_ROWS - DK // 2)).reshape(ROT_ROWS, 1)
    pos = positions.astype(_F32).reshape(B, 1, T)
    const = lambda b, t: (0, 0)
    return pl.pallas_call(
        _rot_kernel,
        grid=(B, T // tile),
        in_specs=[
            pl.BlockSpec((1, 1, tile), lambda b, t: (b, 0, t)),
            pl.BlockSpec((ROT_ROWS, 1), const),
            pl.BlockSpec((ROT_ROWS, KP), const),
            pl.BlockSpec((ROT_ROWS, KP), const),
        ],
        out_specs=[pl.BlockSpec((1, tile, KP), lambda b, t: (b, t, 0))] * 2,
        out_shape=[jax.ShapeDtypeStruct((B, T, KP), _F32)] * 2,
        compiler_params=pltpu.CompilerParams(dimension_semantics=("arbitrary", "arbitrary")),
        name="rotary_table",
    )(pos, freq, tb["rot_cos_expand"], tb["rot_sin_expand"])


def _mixer_kernel(x_ref, mod_ref, ng_ref, win32_ref, cos_ref, sin_ref,
                  poolw_ref, pools_ref, wa2_ref, ba_ref, gng_ref, wout32_ref,
                  poolwin_ref, rothalf_ref,
                  rdecay_ref, rxi_ref, rzeta_ref, rgc_ref, rhmk_ref, rhmv_ref,
                  smask_ref, hind_ref,
                  gtri_ref, gmask_ref, ghmk_ref, ghmv_ref,
                  o_ref,
                  win_ref, wout_ref,
                  hb_ref, proj_ref, ycat_ref, b_ref, dec_ref, oscr_ref, ext_ref, s2_ref, s4_ref, s8_ref,
                  hist_ref, sret_ref, sgla_ref, *, tile, sub):
    t_idx = pl.program_id(1)
    n_sub = tile // sub

    @pl.when((pl.program_id(0) == 0) & (t_idx == 0))
    def _():
        _cast_weight(win32_ref, win_ref)
        _cast_weight(wout32_ref, wout_ref)

    @pl.when(t_idx == 0)
    def _():
        hist_ref[...] = jnp.zeros_like(hist_ref)
        sret_ref[...] = jnp.zeros_like(sret_ref)
        sgla_ref[...] = jnp.zeros_like(sgla_ref)

    gate1 = mod_ref[0, 2:3, :]

    norm_scale = ng_ref[...] * (1.0 + mod_ref[0, 1:2, :])
    shift1 = mod_ref[0, 0:1, :]

    def projection_tasks(i):
        slot = i % 2

        def norm():
            x = x_ref[0, i * sub:(i + 1) * sub, :]
            ms = jnp.mean(x * x, axis=-1, keepdims=True)
            hb_ref[slot] = ((x * lax.rsqrt(ms + EPS)) * norm_scale + shift1).astype(_BF)

        def segment(off, width):
            def run():
                proj_ref[slot, :, off:off + width] = _dot(hb_ref[slot], win_ref[:, off:off + width])
            return run

        tasks = [(i, 0, norm)]
        for off in range(0, N_IN, PROJ_PIECE):
            width = min(PROJ_PIECE, N_IN - off)
            tasks.append((i, off + width, segment(off, width)))
        return tasks

    def key_block(pv, lrows, off):
        shift = off % LANES
        if shift == 0:
            return pv[lrows, off:off + KP]
        return pltpu.roll(pv[lrows, off - shift:off - shift + KP], KP - shift, axis=1)

    pending = projection_tasks(0)
    ticks = [0]

    def tick(weight=1):
        ticks[0] += weight
        if pending and ticks[0] >= PROJ_TICKS:
            ticks[0] = 0
            pending.pop(0)[2]()

    def require(i, column):
        while pending and (pending[0][0], pending[0][1]) <= (i, column):
            pending.pop(0)[2]()

    require(0, N_IN)

    def pooling(i):
        n = sub + POOL_HIST
        srows = slice(i * sub, (i + 1) * sub)
        u = proj_ref[i % 2, :, OFF_POOL:OFF_POOL + POOL_WIDTH]
        ext_ref[0:POOL_HIST, :] = hist_ref[...]
        ext_ref[POOL_HIST:n, :] = u
        hist_ref[...] = u[sub - POOL_HIST:sub, :]
        s2_ref[8:n, :] = ext_ref[8:n, :] + ext_ref[7:n - 1, :]
        s4_ref[16:n, :] = s2_ref[16:n, :] + s2_ref[14:n - 2, :]
        s8_ref[24:n, :] = s4_ref[24:n, :] + s4_ref[20:n - 4, :]
        s16 = s8_ref[32:n, :] + s8_ref[24:n - 8, :]
        lane = lax.broadcasted_iota(jnp.int32, (sub, POOL_WIDTH), 1)
        wsum = jnp.where(lane < POOL_GROUP_DIM, s2_ref[32:n, :],
                         jnp.where(lane < 2 * POOL_GROUP_DIM, s4_ref[32:n, :],
                                   jnp.where(lane < 3 * POOL_GROUP_DIM, s8_ref[32:n, :], s16)))
        t_abs = (t_idx * tile + i * sub
                 + lax.broadcasted_iota(jnp.int32, (sub, POOL_WIDTH), 0)).astype(_F32)
        cnt = jnp.minimum(t_abs + 1.0, poolwin_ref[...])
        pooled = wsum / cnt - u
        ycat_ref[srows, 0:POOL_WIDTH] = (_dot(pooled.astype(_BF), poolw_ref[...])
                                         * pools_ref[...]).astype(_BF)

    inv_dv = 1.0 / DV
    k_scale = DK ** -0.5
    q_scale = DK ** -0.5
    smask = smask_ref[...]

    def own_head(state):
        return state.astype(_BF) * smask
    hind = hind_ref[...]
    first_half = rothalf_ref[...] > 0.5

    def head_rms(o):
        ms_h = _dot((o * o).astype(_BF), hind) * inv_dv
        return o * lax.rsqrt(ms_h + EPS)

    def stack_heads(a):
        return jnp.concatenate([a] * N_HEADS, axis=0)

    def rotary(t, cosv, sinv):
        partner = jnp.where(first_half, pltpu.roll(t, KP - DK // 2, axis=1), pltpu.roll(t, DK // 2, axis=1))
        return t * cosv + partner * sinv

    def retention(i, r0):
        C = RET_CHUNK
        pv = proj_ref.at[i % 2]
        parts = []
        for c in range(sub // C):
            rows = slice(r0 + c * C, r0 + (c + 1) * C)
            lrows = slice(c * C, (c + 1) * C)
            cosv = cos_ref[0, rows, :]
            sinv = sin_ref[0, rows, :]
            q = rotary(key_block(pv, lrows, OFF_RQ), cosv, sinv)
            k = rotary(key_block(pv, lrows, OFF_RK), cosv, sinv) * k_scale
            vb = pv[lrows, OFF_RV:OFF_RV + VW].astype(_BF)
            qb = q.astype(_BF)
            kbd = stack_heads(k.astype(_BF)) * rhmk_ref[...]
            scores = _dot_nt(qb, kbd) * rdecay_ref[...]
            vbd = stack_heads(vb) * rhmv_ref[...]
            o_intra = _dot(scores.astype(_BF), vbd)
            upd = _dot_tn(vb, (k * rzeta_ref[...]).astype(_BF))
            parts.append((rows, qb, o_intra, upd))
            tick()
            yield
        state = states["ret"]
        for rows, qb, o_intra, upd in parts:
            oscr_ref[rows, 0:VW] = o_intra + _dot_nt(qb, own_head(state)) * rxi_ref[...]
            state = state * rgc_ref[...] + upd
            tick()
            yield
        states["ret"] = state

    def gla_decay(i):
        C = GLA_CHUNK
        n_chunks = sub // C
        W = n_chunks * KP
        pv = proj_ref.at[i % 2]
        logits = _dot(pv[:, OFF_GA:OFF_GA + LANES].astype(_BF), wa2_ref[...]) + ba_ref[...]
        log_sig = jnp.minimum(logits, 0.0) - jnp.log(1.0 + jnp.exp(-jnp.abs(logits)))
        la = log_sig * (LOG2_E / GLA_GATE_TAU)

        la_l = jnp.concatenate([la[c * C:(c + 1) * C, :] for c in range(n_chunks)], axis=1)
        tri = gtri_ref[...]
        p1, p2, p3 = _split3(la_l)
        b = _dot(tri, p1) + _dot(tri, p2) + _dot(tri, p3)
        b_ref[i % 2] = b
        yield

        row = lax.broadcasted_iota(jnp.int32, (C, W), 0)
        s = C // 2
        lvl = 0
        while s >= SUBLANES // 2:
            refs = [jnp.broadcast_to(b_ref[i % 2, p + s - 1:p + s, :], (2 * s, W)) for p in range(0, C, 2 * s)]
            b_at_ref = refs[0] if len(refs) == 1 else jnp.concatenate(refs, axis=0)
            dec_ref[i % 2, lvl * C:(lvl + 1) * C, :] = jnp.exp2(-jnp.abs(b - b_at_ref))
            s //= 2
            lvl += 1
            tick()
            yield
        up1 = pltpu.roll(b, 1, axis=0)
        up2 = pltpu.roll(b, 2, axis=0)
        dn1 = pltpu.roll(b, C - 1, axis=0)
        m4 = row % 4
        b_at_ref = jnp.where(m4 == 0, dn1, jnp.where(m4 == 1, b, jnp.where(m4 == 2, up1, up2)))
        dec_ref[i % 2, lvl * C:(lvl + 1) * C, :] = jnp.exp2(-jnp.abs(b - b_at_ref))
        lvl += 1
        tick()
        yield
        b_at_ref = jnp.where(row % 2 == 0, b, up1)
        dec_ref[i % 2, lvl * C:(lvl + 1) * C, :] = jnp.exp2(-jnp.abs(b - b_at_ref))
        lvl += 1
        tick()
        yield
        dec_ref[i % 2, lvl * C:(lvl + 1) * C, :] = jnp.exp2(b)
        tick()
        b_last = jnp.broadcast_to(b_ref[i % 2, C - 1:C, :], (C, W))
        dec_ref[i % 2, (lvl + 1) * C:(lvl + 2) * C, :] = jnp.exp2(b_last - b)
        tick()

    def gla_chunks(i, r0, state):
        C = GLA_CHUNK
        n_chunks = sub // C
        pv = proj_ref.at[i % 2]
        hmk = ghmk_ref[...]
        parts = []
        for c in range(n_chunks):
            rows = slice(r0 + c * C, r0 + (c + 1) * C)
            lrows = slice(c * C, (c + 1) * C)
            lanes = slice(c * KP, (c + 1) * KP)
            q = key_block(pv, lrows, OFF_GQ) * q_scale
            k = key_block(pv, lrows, OFF_GK)
            vb = pv[lrows, OFF_GV:OFF_GV + VW].astype(_BF)
            qb = q.astype(_BF)
            kbd0 = stack_heads(k.astype(_BF)) * hmk
            scores = jnp.zeros((C, N_HEADS * C), _F32)
            for lvl in range(N_GLA_LEVELS + 1):
                if lvl < N_GLA_LEVELS:
                    eb = dec_ref[i % 2, lvl * C:(lvl + 1) * C, lanes].astype(_BF)
                    ql, kbd = qb * eb, kbd0 * stack_heads(eb)
                else:
                    ql, kbd = qb, kbd0
                scores = scores + _dot_nt(ql, kbd) * gmask_ref[lvl]
                tick()
            vbd = stack_heads(vb) * ghmv_ref[...]
            e_cum = dec_ref[i % 2, N_GLA_LEVELS * C:(N_GLA_LEVELS + 1) * C, lanes]
            e_rev = dec_ref[i % 2, (N_GLA_LEVELS + 1) * C:(N_GLA_LEVELS + 2) * C, lanes]
            o_intra = _dot(scores.astype(_BF), vbd)
            upd = _dot_tn(vb, (k * e_rev).astype(_BF))
            parts.append((rows, (q * e_cum).astype(_BF), o_intra, upd, e_cum[C - 1:C, :]))
        for rows, qe, o_intra, upd, e_last in parts:
            oscr_ref[rows, VW:2 * VW] = o_intra + _dot_nt(qe, own_head(state))
            state = state * e_last + upd
            tick()
        return state

    states = {"ret": sret_ref[...]}
    gla_state = sgla_ref[...]
    for i in range(n_sub):
        r0 = i * sub
        srows = slice(r0, r0 + sub)
        if i + 1 < n_sub:
            pending.extend(projection_tasks(i + 1))
        tick()
        pooling(i)
        tick()
        stages = [retention(i, r0), gla_decay(i)]
        while stages:
            for stage in list(stages):
                if next(stage, stages) is stages:
                    stages.remove(stage)
        gla_state = gla_chunks(i, r0, gla_state)
        y_ret = _silu(proj_ref[i % 2, :, OFF_RG:OFF_RG + VW]) * head_rms(oscr_ref[srows, 0:VW])
        ycat_ref[srows, POOL_WIDTH:POOL_WIDTH + VW] = y_ret.astype(_BF)
        gg_off = OFF_GG % LANES
        gg = pltpu.roll(proj_ref[i % 2, :, OFF_GG - gg_off:N_IN], N_IN - OFF_GG, axis=1)[:, 0:VW]
        y_gla = (_silu(gg)
                 * (head_rms(oscr_ref[srows, VW:2 * VW]) * gng_ref[...]))
        require(i + 1, N_IN)
        ycat_ref[srows, POOL_WIDTH + VW:POOL_WIDTH + 2 * VW] = y_gla.astype(_BF)
        o_ref[0, srows, :] = x_ref[0, srows, :] + gate1 * _dot(ycat_ref[srows, :], wout_ref[...])
    sret_ref[...] = states["ret"]
    sgla_ref[...] = gla_state


def _whole(shape):
    nd = len(shape)
    return pl.BlockSpec(shape, lambda b, t: (0,) * nd, pipeline_mode=pl.Buffered(1))


def _layer_slab(shape, layer):
    nd = len(shape) - 1
    return pl.BlockSpec((None,) + tuple(shape[1:]), lambda b, t: (layer,) + (0,) * nd,
                        pipeline_mode=pl.Buffered(1))


def _mixer(x, mod, layer, params, cos, sin, tb):
    B, T, D = x.shape
    tile = MIX_TILE if T % MIX_TILE == 0 else T
    sub = MIX_SUB if tile % MIX_SUB == 0 else tile
    consts = [tb[k] for k in ("pool_win", "rot_first_half", "ret_decay", "ret_xi", "ret_zeta", "ret_gc",
                              "ret_hmk", "ret_hmv", "state_mask", "head_ind", "gla_tri", "gla_mask",
                              "gla_hmk", "gla_hmv")]
    norm_g, win = params[0], params[1]
    rest = list(params[2:])
    n_sub = tile // sub
    args = [x, mod, norm_g, win, cos, sin] + rest + consts
    in_specs = [
        pl.BlockSpec((1, tile, D), lambda b, t: (b, t, 0)),
        pl.BlockSpec((None, 1, 6, D), lambda b, t: (layer, b, 0, 0)),
        _layer_slab(norm_g.shape, layer),
        _layer_slab(win.shape, layer),
        pl.BlockSpec((1, tile, KP), lambda b, t: (b, t, 0)),
        pl.BlockSpec((1, tile, KP), lambda b, t: (b, t, 0)),
    ] + [_layer_slab(a.shape, layer) for a in rest] + [_whole(a.shape) for a in consts]
    n = sub + POOL_HIST
    gla_lanes = (sub // GLA_CHUNK) * KP
    return pl.pallas_call(
        functools.partial(_mixer_kernel, tile=tile, sub=sub),
        grid=(B, T // tile),
        in_specs=in_specs,
        out_specs=pl.BlockSpec((1, tile, D), lambda b, t: (b, t, 0)),
        out_shape=jax.ShapeDtypeStruct((B, T, D), _F32),
        scratch_shapes=[
            pltpu.VMEM((D, N_IN), _BF),
            pltpu.VMEM((POOL_WIDTH + 2 * VW, D), _BF),
            pltpu.VMEM((2, sub, D), _BF),
            pltpu.VMEM((2, sub, N_IN), _F32),
            pltpu.VMEM((tile, POOL_WIDTH + 2 * VW), _BF),
            pltpu.VMEM((2, GLA_CHUNK, gla_lanes), _F32),
            pltpu.VMEM((2, (N_GLA_LEVELS + 2) * GLA_CHUNK, gla_lanes), _F32),
            pltpu.VMEM((tile, 2 * VW), _F32),
            pltpu.VMEM((n, POOL_WIDTH), _F32),
            pltpu.VMEM((n, POOL_WIDTH), _F32),
            pltpu.VMEM((n, POOL_WIDTH), _F32),
            pltpu.VMEM((n, POOL_WIDTH), _F32),
            pltpu.VMEM((POOL_HIST, POOL_WIDTH), _F32),
            pltpu.VMEM((VW, KP), _F32),
            pltpu.VMEM((VW, KP), _F32),
        ],
        compiler_params=pltpu.CompilerParams(
            dimension_semantics=("arbitrary", "arbitrary"), vmem_limit_bytes=VMEM_LIMIT_BYTES),
        name="token_mixer",
    )(*args)


def _ffn_kernel(x_ref, mod_ref, ng_ref, wup_ref, cw_ref, wd32_ref, fg_ref,
                o_ref, wd_ref, h_ref, carry_ref, act_ref, *, tile, d_ff, fb, final_norm):
    t_idx = pl.program_id(1)
    S = SUBLANES
    R = tile // S
    D = x_ref.shape[-1]

    @pl.when(t_idx == 0)
    def _():
        carry_ref[...] = jnp.zeros_like(carry_ref)

    @pl.when((pl.program_id(0) == 0) & (t_idx == 0))
    def _():
        _cast_weight(wd32_ref, wd_ref)

    shift2 = mod_ref[0, 3:4, :]
    scale2 = mod_ref[0, 4:5, :]
    gate2 = mod_ref[0, 5:6, :]
    norm_scale = ng_ref[...] * (1.0 + scale2)
    groups = FFN_NORM_ROWS // S
    early_cols = ([j * fb for j in range(FFN_EARLY_BLOCKS)]
                  + [d_ff + j * fb for j in range(FFN_EARLY_BLOCKS)])
    early = {col: [] for col in early_cols}
    x_parts = []
    for p in range(tile // FFN_NORM_ROWS):
        prow = slice(p * FFN_NORM_ROWS, (p + 1) * FFN_NORM_ROWS)
        xs = jnp.concatenate([x_ref[0, s * R + p * groups:s * R + (p + 1) * groups, :]
                              for s in range(S)], axis=0)
        xs = xs.reshape(S, groups, D).swapaxes(0, 1).reshape(FFN_NORM_ROWS, D)
        ms = jnp.mean(xs * xs, axis=-1, keepdims=True)
        h_ref[prow, :] = ((xs * lax.rsqrt(ms + EPS)) * norm_scale + shift2).astype(_BF)
        x_parts.append(xs)
        for col in early_cols:
            early[col].append(_dot(h_ref[prow, :], wup_ref[:, col:col + fb]))
    x = jnp.concatenate(x_parts, axis=0)
    first_sublane = lax.broadcasted_iota(jnp.int32, (S, fb), 0) == 0

    def up_conv(col):
        cols = slice(col, col + fb)
        if col in early:
            u = jnp.concatenate(early[col], axis=0)
        else:
            u = _dot(h_ref[...], wup_ref[:, cols])
        back1 = jnp.where(first_sublane, pltpu.roll(carry_ref[S:2 * S, cols], 1, axis=0),
                          pltpu.roll(u[tile - S:tile, :], 1, axis=0))
        back2 = jnp.where(first_sublane, pltpu.roll(carry_ref[0:S, cols], 1, axis=0),
                          pltpu.roll(u[tile - 2 * S:tile - S, :], 1, axis=0))
        carry_ref[:, cols] = u[tile - 2 * S:tile, :]
        prev1 = jnp.concatenate([back1, u[0:tile - S, :]], axis=0)
        prev2 = jnp.concatenate([back2, back1, u[0:tile - 2 * S, :]], axis=0)
        cw = cw_ref[:, cols]
        return prev2 * cw[0:1, :] + prev1 * cw[1:2, :] + u * cw[2:3, :] + cw[3:4, :]

    for j in range(d_ff // fb):
        ya = up_conv(j * fb)
        yg = up_conv(d_ff + j * fb)
        act_ref[:, j * fb:(j + 1) * fb] = (_silu(yg) * ya).astype(_BF)

    out = x + gate2 * _dot(act_ref[...], wd_ref[...])
    if final_norm:
        ms_o = jnp.mean(out * out, axis=-1, keepdims=True)
        out = out * lax.rsqrt(ms_o + EPS) * fg_ref[...]
    o_ref[0] = out.reshape(R, S, D).swapaxes(0, 1).reshape(tile, D)


def _ffn(x, mod, layer, params, final_g, final_norm):
    B, T, D = x.shape
    tile = FFN_TILE if T % FFN_TILE == 0 else T
    norm_g, wup, cw, wd = params
    d_ff = wd.shape[1]
    in_specs = [
        pl.BlockSpec((1, tile, D), lambda b, t: (b, t, 0)),
        pl.BlockSpec((None, 1, 6, D), lambda b, t: (layer, b, 0, 0)),
    ] + [_layer_slab(a.shape, layer) for a in params] + [_whole(final_g.shape)]
    return pl.pallas_call(
        functools.partial(_ffn_kernel, tile=tile, d_ff=d_ff, fb=FFN_BLOCK, final_norm=final_norm),
        grid=(B, T // tile),
        in_specs=in_specs,
        out_specs=pl.BlockSpec((1, tile, D), lambda b, t: (b, t, 0)),
        out_shape=jax.ShapeDtypeStruct((B, T, D), _F32),
        scratch_shapes=[
            pltpu.VMEM((d_ff, D), _BF),
            pltpu.VMEM((tile, D), _BF),
            pltpu.VMEM((2 * SUBLANES, 2 * d_ff), _F32),
            pltpu.VMEM((tile, d_ff), _BF),
        ],
        compiler_params=pltpu.CompilerParams(
            dimension_semantics=("arbitrary", "arbitrary"), vmem_limit_bytes=VMEM_LIMIT_BYTES),
        name="conv_ffn",
    )(x, mod, *params, final_g)


def _prep_mixer_params(norm1_g, w_in, pool_w, pool_scale, gla_wa2, gla_ba, gla_norm_g, w_out):
    L, D, _ = w_in.shape
    G = pool_w.shape[1]
    eye = jnp.eye(G, dtype=pool_w.dtype)
    poolw = (pool_w[:, :, :, None, :] * eye[None, :, None, :, None]).reshape(L, POOL_WIDTH, POOL_WIDTH)
    wa2 = jnp.pad(gla_wa2, ((0, 0), (0, LANES - GLA_GATE_RANK), (0, KP - KW)))
    ba = jnp.pad(gla_ba, ((0, 0), (0, KP - KW))).reshape(L, 1, KP)
    return (norm1_g.reshape(L, 1, D), w_in, poolw.astype(_BF), pool_scale.reshape(L, 1, POOL_WIDTH),
            wa2.astype(_BF), ba, gla_norm_g.reshape(L, 1, VW), w_out)


def _prep_ffn_params(norm2_g, w_up, conv_w, conv_b, w_down):
    L, D, _ = w_up.shape
    cw = jnp.pad(jnp.concatenate([conv_w, conv_b[:, None, :]], axis=1),
                 ((0, 0), (0, SUBLANES - CONV_WIDTH - 1), (0, 0)))
    return norm2_g.reshape(L, 1, D), w_up.astype(_BF), cw, w_down


def kernel(x, c, positions, ada_w, ada_b, norm1_g, w_in, pool_w, pool_scale, gla_wa2, gla_ba,
           gla_norm_g, w_out, norm2_g, w_up, conv_w, conv_b, w_down, final_g):
    L = ada_w.shape[0]
    B, T, D = x.shape
    tb = {k: jnp.asarray(v) for k, v in _tables().items()}
    for name in ("head_ind", "gla_tri", "ret_hmk", "ret_hmv", "gla_hmk", "gla_hmv", "state_mask",
                 "rot_cos_expand", "rot_sin_expand"):
        tb[name] = tb[name].astype(_BF)
    mod = _modulation(c, ada_w, ada_b).reshape(L, B, 6, D)
    cos, sin = _rotary_tables(positions, tb)
    mixer_params = _prep_mixer_params(norm1_g, w_in, pool_w, pool_scale, gla_wa2, gla_ba, gla_norm_g, w_out)
    ffn_params = _prep_ffn_params(norm2_g, w_up, conv_w, conv_b, w_down)
    fg = final_g.reshape(1, D)
    for l in range(L):
        x = _mixer(x, mod, l, mixer_params, cos, sin, tb)
        x = _ffn(x, mod, l, ffn_params, fg, final_norm=(l == L - 1))
    return x
```

```python
import functools

import numpy as np
import jax
import jax.numpy as jnp
from jax import lax
from jax.experimental import pallas as pl
from jax.experimental.pallas import tpu as pltpu

POOL_WIDTH = 256
POOL_WINDOWS = (2, 4, 8, 16)
POOL_GROUP_DIM = 64
N_HEADS = 4
DK = 48
DV = 96
KW = N_HEADS * DK
VW = N_HEADS * DV
RET_CHUNK = 128
GLA_CHUNK = 64
GLA_GATE_RANK = 16
GLA_GATE_TAU = 16.0
ROPE_BASE = 10000.0
CONV_WIDTH = 3
EPS = 1e-6

LANES = 128
SUBLANES = 8
KP = 256
VMEM_LIMIT_BYTES = 56 * 1024 * 1024

OFF_POOL = 0
OFF_RQ = OFF_POOL + POOL_WIDTH
OFF_RK = OFF_RQ + KW
OFF_RV = OFF_RK + KW
OFF_RG = OFF_RV + VW
OFF_GQ = OFF_RG + VW
OFF_GK = OFF_GQ + KW
OFF_GV = OFF_GK + KW
OFF_GA = OFF_GV + VW
OFF_GG = OFF_GA + GLA_GATE_RANK
D_IN = OFF_GG + VW
N_IN = -(-D_IN // LANES) * LANES

MIX_TILE = 512
MIX_SUB = 256
PROJ_PIECE = 256
PROJ_TICKS = 3
FFN_TILE = 512
FFN_SUB = 512
FFN_CAST_STEPS = 8
FFN_BLOCK = 256
FFN_NORM_ROWS = 128
FFN_EARLY_BLOCKS = 2
ROT_TILE = 1024
ROT_ROWS = 32
POOL_HIST = 32
N_GLA_LEVELS = 6
CAST_ROWS = 128
LOG2_E = 1.4426950408889634

_BF = jnp.bfloat16
_F32 = jnp.float32


def _dot(a, b):
    return jnp.dot(a, b, preferred_element_type=_F32)


def _dot_nt(a, b):
    return lax.dot_general(a, b, (((1,), (1,)), ((), ())), preferred_element_type=_F32)


def _dot_tn(a, b):
    return lax.dot_general(a, b, (((0,), (0,)), ((), ())), preferred_element_type=_F32)


def _split3(x):
    p1 = x.astype(_BF)
    r1 = x - p1.astype(_F32)
    p2 = r1.astype(_BF)
    p3 = (r1 - p2.astype(_F32)).astype(_BF)
    return p1, p2, p3


def _cast_weight(src_ref, dst_ref):
    n_rows, n_cols = src_ref.shape
    whole = (n_cols // LANES) * LANES
    for r in range(0, n_rows, CAST_ROWS):
        rows = slice(r, r + CAST_ROWS)
        dst_ref[rows, 0:whole] = src_ref[rows, 0:whole].astype(_BF)
        if dst_ref.shape[1] > whole:
            dst_ref[rows, whole:] = jnp.zeros((CAST_ROWS, dst_ref.shape[1] - whole), _BF)
        if n_cols > whole:
            dst_ref[rows, whole:n_cols] = src_ref[rows, whole:n_cols].astype(_BF)


def _sigmoid(x):
    return 1.0 / (1.0 + jnp.exp(-x))


def _silu(x):
    return x * _sigmoid(x)


def _key_head(d):
    return np.where(d < KW, d // DK, -1)


@functools.lru_cache(maxsize=None)
def _tables():
    t = {}
    kd = _key_head(np.arange(KP))
    vd = np.arange(VW) // DV

    def head_masks(chunk):
        rows = np.repeat(np.arange(N_HEADS), chunk)
        return ((rows[:, None] == kd[None, :]).astype(np.float32),
                (rows[:, None] == vd[None, :]).astype(np.float32))

    C = RET_CHUNK
    lg = np.log(1.0 - 2.0 ** (-5.0 - np.arange(N_HEADS, dtype=np.float64)))
    i = np.arange(C)
    rel = i[:, None] - i[None, :]
    dec = np.where(rel[None] >= 0, np.exp(np.maximum(rel, 0)[None] * lg[:, None, None]), 0.0)
    t["ret_decay"] = np.transpose(dec, (1, 0, 2)).reshape(C, N_HEADS * C).astype(np.float32)
    t["ret_xi"] = np.exp((i[:, None] + 1.0) * lg[vd][None, :]).astype(np.float32)
    zeta = np.exp((C - 1.0 - i)[:, None] * lg[np.maximum(kd, 0)][None, :]) * (kd >= 0)[None, :]
    t["ret_zeta"] = zeta.astype(np.float32)
    t["ret_gc"] = (np.exp(C * lg[np.maximum(kd, 0)]) * (kd >= 0))[None, :].astype(np.float32)
    t["ret_hmk"], t["ret_hmv"] = head_masks(C)
    t["state_mask"] = (vd[:, None] == kd[None, :]).astype(np.float32)
    t["head_ind"] = (vd[:, None] == vd[None, :]).astype(np.float32)

    C = GLA_CHUNK
    i = np.arange(C)
    masks = []
    s = C // 2
    while s >= 1:
        parent = (i // (2 * s)) * (2 * s)
        upper = (i - parent) >= s
        masks.append(upper[:, None] & (~upper)[None, :] & (parent[:, None] == parent[None, :]))
        s //= 2
    masks.append(i[:, None] == i[None, :])
    t["gla_mask"] = np.stack([np.tile(m, (1, N_HEADS)) for m in masks]).astype(np.float32)
    t["gla_tri"] = (i[None, :] <= i[:, None]).astype(np.float32)
    t["gla_hmk"], t["gla_hmv"] = head_masks(C)

    t["pool_win"] = np.repeat(np.asarray(POOL_WINDOWS, np.float32), POOL_GROUP_DIM)[None, :]

    half = DK // 2
    l = np.arange(KP)
    hit = (np.arange(ROT_ROWS)[:, None] == (l % half)[None, :]) & (l < KW)[None, :]
    t["rot_cos_expand"] = hit.astype(np.float32)
    t["rot_sin_expand"] = hit * np.where((l % DK) < half, -1.0, 1.0)[None, :].astype(np.float32)
    t["rot_first_half"] = ((l % DK) < half).astype(np.float32)[None, :]
    return t


def _mod_kernel(c_ref, w_ref, b_ref, o_ref):
    ca = _silu(c_ref[...]).astype(_BF)
    o_ref[...] = _dot(ca, w_ref[...].astype(_BF)) + b_ref[...]


def _modulation(c, ada_w, ada_b):
    L, D, N = ada_w.shape
    B = c.shape[0]
    tn = D
    return pl.pallas_call(
        _mod_kernel,
        grid=(L, N // tn),
        in_specs=[
            pl.BlockSpec((B, D), lambda l, n: (0, 0)),
            pl.BlockSpec((None, D, tn), lambda l, n: (l, 0, n)),
            pl.BlockSpec((None, 1, tn), lambda l, n: (l, 0, n)),
        ],
        out_specs=pl.BlockSpec((None, B, tn), lambda l, n: (l, 0, n)),
        out_shape=jax.ShapeDtypeStruct((L, B, N), _F32),
        compiler_params=pltpu.CompilerParams(
            dimension_semantics=("arbitrary", "arbitrary"), vmem_limit_bytes=VMEM_LIMIT_BYTES),
        name="adaln_modulation",
    )(c, ada_w, ada_b.reshape(L, 1, N))


def _rot_kernel(pos_ref, freq_ref, ec_ref, es_ref, cos_ref, sin_ref):
    ang = freq_ref[...] * pos_ref[0]
    ec = ec_ref[...]
    es = es_ref[...]

    def widen(v, e):
        p1, p2, p3 = _split3(v)
        return _dot_tn(p1, e) + _dot_tn(p2, e) + _dot_tn(p3, e)

    cos_ref[0] = widen(jnp.cos(ang), ec)
    sin_ref[0] = widen(jnp.sin(ang), es)


def _rotary_tables(positions, tb):
    B, T = positions.shape
    tile = ROT_TILE if T % ROT_TILE == 0 else T
    inv_freq = ROPE_BASE ** (-jnp.arange(0, DK, 2, dtype=_F32) / DK)
    freq = jnp.pad(inv_freq, (0, ROT_ROWS - DK // 2)).reshape(ROT_ROWS, 1)
    pos = positions.astype(_F32).reshape(B, 1, T)
    const = lambda b, t: (0, 0)
    return pl.pallas_call(
        _rot_kernel,
        grid=(B, T // tile),
        in_specs=[
            pl.BlockSpec((1, 1, tile), lambda b, t: (b, 0, t)),
            pl.BlockSpec((ROT_ROWS, 1), const),
            pl.BlockSpec((ROT_ROWS, KP), const),
            pl.BlockSpec((ROT_ROWS, KP), const),
        ],
        out_specs=[pl.BlockSpec((1, tile, KP), lambda b, t: (b, t, 0))] * 2,
        out_shape=[jax.ShapeDtypeStruct((B, T, KP), _F32)] * 2,
        compiler_params=pltpu.CompilerParams(dimension_semantics=("arbitrary", "arbitrary")),
        name="rotary_table",
    )(pos, freq, tb["rot_cos_expand"], tb["rot_sin_expand"])


def _mixer_kernel(x_ref, mod_ref, ng_ref, win32_ref, cos_ref, sin_ref,
                  poolw_ref, pools_ref, wa2_ref, ba_ref, gng_ref, wout32_ref,
                  poolwin_ref, rothalf_ref,
                  rdecay_ref, rxi_ref, rzeta_ref, rgc_ref, rhmk_ref, rhmv_ref,
                  smask_ref, hind_ref,
                  gtri_ref, gmask_ref, ghmk_ref, ghmv_ref,
                  o_ref,
                  win_ref, wout_ref,
                  hb_ref, proj_ref, ycat_ref, b_ref, dec_ref, oscr_ref, ext_ref, s2_ref, s4_ref, s8_ref,
                  hist_ref, sret_ref, sgla_ref, *, tile, sub):
    t_idx = pl.program_id(1)
    n_sub = tile // sub

    @pl.when((pl.program_id(0) == 0) & (t_idx == 0))
    def _():
        _cast_weight(win32_ref, win_ref)
        _cast_weight(wout32_ref, wout_ref)

    @pl.when(t_idx == 0)
    def _():
        hist_ref[...] = jnp.zeros_like(hist_ref)
        sret_ref[...] = jnp.zeros_like(sret_ref)
        sgla_ref[...] = jnp.zeros_like(sgla_ref)

    gate1 = mod_ref[0, 2:3, :]

    norm_scale = ng_ref[...] * (1.0 + mod_ref[0, 1:2, :])
    shift1 = mod_ref[0, 0:1, :]

    def projection_tasks(i):
        slot = i % 2

        def norm():
            x = x_ref[0, i * sub:(i + 1) * sub, :]
            ms = jnp.mean(x * x, axis=-1, keepdims=True)
            hb_ref[slot] = ((x * lax.rsqrt(ms + EPS)) * norm_scale + shift1).astype(_BF)

        def segment(off, width):
            def run():
                proj_ref[slot, :, off:off + width] = _dot(hb_ref[slot], win_ref[:, off:off + width])
            return run

        tasks = [(i, 0, norm)]
        for off in range(0, N_IN, PROJ_PIECE):
            width = min(PROJ_PIECE, N_IN - off)
            tasks.append((i, off + width, segment(off, width)))
        return tasks

    def key_block(pv, lrows, off):
        shift = off % LANES
        if shift == 0:
            return pv[lrows, off:off + KP]
        return pltpu.roll(pv[lrows, off - shift:off - shift + KP], KP - shift, axis=1)

    pending = projection_tasks(0)
    ticks = [0]

    def tick(weight=1):
        ticks[0] += weight
        if pending and ticks[0] >= PROJ_TICKS:
            ticks[0] = 0
            pending.pop(0)[2]()

    def require(i, column):
        while pending and (pending[0][0], pending[0][1]) <= (i, column):
            pending.pop(0)[2]()

    require(0, N_IN)

    def pooling(i):
        n = sub + POOL_HIST
        srows = slice(i * sub, (i + 1) * sub)
        u = proj_ref[i % 2, :, OFF_POOL:OFF_POOL + POOL_WIDTH]
        ext_ref[0:POOL_HIST, :] = hist_ref[...]
        ext_ref[POOL_HIST:n, :] = u
        hist_ref[...] = u[sub - POOL_HIST:sub, :]
        s2_ref[8:n, :] = ext_ref[8:n, :] + ext_ref[7:n - 1, :]
        s4_ref[16:n, :] = s2_ref[16:n, :] + s2_ref[14:n - 2, :]
        s8_ref[24:n, :] = s4_ref[24:n, :] + s4_ref[20:n - 4, :]
        s16 = s8_ref[32:n, :] + s8_ref[24:n - 8, :]
        lane = lax.broadcasted_iota(jnp.int32, (sub, POOL_WIDTH), 1)
        wsum = jnp.where(lane < POOL_GROUP_DIM, s2_ref[32:n, :],
                         jnp.where(lane < 2 * POOL_GROUP_DIM, s4_ref[32:n, :],
                                   jnp.where(lane < 3 * POOL_GROUP_DIM, s8_ref[32:n, :], s16)))
        t_abs = (t_idx * tile + i * sub
                 + lax.broadcasted_iota(jnp.int32, (sub, POOL_WIDTH), 0)).astype(_F32)
        cnt = jnp.minimum(t_abs + 1.0, poolwin_ref[...])
        pooled = wsum / cnt - u
        ycat_ref[srows, 0:POOL_WIDTH] = (_dot(pooled.astype(_BF), poolw_ref[...])
                                         * pools_ref[...]).astype(_BF)

    inv_dv = 1.0 / DV
    k_scale = DK ** -0.5
    q_scale = DK ** -0.5
    smask = smask_ref[...]

    def own_head(state):
        return state.astype(_BF) * smask
    hind = hind_ref[...]
    first_half = rothalf_ref[...] > 0.5

    def head_rms(o):
        ms_h = _dot((o * o).astype(_BF), hind) * inv_dv
        return o * lax.rsqrt(ms_h + EPS)

    def stack_heads(a):
        return jnp.concatenate([a] * N_HEADS, axis=0)

    def rotary(t, cosv, sinv):
        partner = jnp.where(first_half, pltpu.roll(t, KP - DK // 2, axis=1), pltpu.roll(t, DK // 2, axis=1))
        return t * cosv + partner * sinv

    def retention(i, r0):
        C = RET_CHUNK
        pv = proj_ref.at[i % 2]
        parts = []
        for c in range(sub // C):
            rows = slice(r0 + c * C, r0 + (c + 1) * C)
            lrows = slice(c * C, (c + 1) * C)
            cosv = cos_ref[0, rows, :]
            sinv = sin_ref[0, rows, :]
            q = rotary(key_block(pv, lrows, OFF_RQ), cosv, sinv)
            k = rotary(key_block(pv, lrows, OFF_RK), cosv, sinv) * k_scale
            vb = pv[lrows, OFF_RV:OFF_RV + VW].astype(_BF)
            qb = q.astype(_BF)
            kbd = stack_heads(k.astype(_BF)) * rhmk_ref[...]
            scores = _dot_nt(qb, kbd) * rdecay_ref[...]
            vbd = stack_heads(vb) * rhmv_ref[...]
            o_intra = _dot(scores.astype(_BF), vbd)
            upd = _dot_tn(vb, (k * rzeta_ref[...]).astype(_BF))
            parts.append((rows, qb, o_intra, upd))
            tick()
            yield
        state = states["ret"]
        for rows, qb, o_intra, upd in parts:
            oscr_ref[rows, 0:VW] = o_intra + _dot_nt(qb, own_head(state)) * rxi_ref[...]
            state = state * rgc_ref[...] + upd
            tick()
            yield
        states["ret"] = state

    def gla_decay(i):
        C = GLA_CHUNK
        n_chunks = sub // C
        W = n_chunks * KP
        pv = proj_ref.at[i % 2]
        logits = _dot(pv[:, OFF_GA:OFF_GA + LANES].astype(_BF), wa2_ref[...]) + ba_ref[...]
        log_sig = jnp.minimum(logits, 0.0) - jnp.log(1.0 + jnp.exp(-jnp.abs(logits)))
        la = log_sig * (LOG2_E / GLA_GATE_TAU)

        la_l = jnp.concatenate([la[c * C:(c + 1) * C, :] for c in range(n_chunks)], axis=1)
        tri = gtri_ref[...]
        p1, p2, p3 = _split3(la_l)
        b = _dot(tri, p1) + _dot(tri, p2) + _dot(tri, p3)
        b_ref[i % 2] = b
        yield

        row = lax.broadcasted_iota(jnp.int32, (C, W), 0)
        s = C // 2
        lvl = 0
        while s >= SUBLANES // 2:
            refs = [jnp.broadcast_to(b_ref[i % 2, p + s - 1:p + s, :], (2 * s, W)) for p in range(0, C, 2 * s)]
            b_at_ref = refs[0] if len(refs) == 1 else jnp.concatenate(refs, axis=0)
            dec_ref[i % 2, lvl * C:(lvl + 1) * C, :] = jnp.exp2(-jnp.abs(b - b_at_ref))
            s //= 2
            lvl += 1
            tick()
            yield
        up1 = pltpu.roll(b, 1, axis=0)
        up2 = pltpu.roll(b, 2, axis=0)
        dn1 = pltpu.roll(b, C - 1, axis=0)
        m4 = row % 4
        b_at_ref = jnp.where(m4 == 0, dn1, jnp.where(m4 == 1, b, jnp.where(m4 == 2, up1, up2)))
        dec_ref[i % 2, lvl * C:(lvl + 1) * C, :] = jnp.exp2(-jnp.abs(b - b_at_ref))
        lvl += 1
        tick()
        yield
        b_at_ref = jnp.where(row % 2 == 0, b, up1)
        dec_ref[i % 2, lvl * C:(lvl + 1) * C, :] = jnp.exp2(-jnp.abs(b - b_at_ref))
        lvl += 1
        tick()
        yield
        dec_ref[i % 2, lvl * C:(lvl + 1) * C, :] = jnp.exp2(b)
        tick()
        b_last = jnp.broadcast_to(b_ref[i % 2, C - 1:C, :], (C, W))
        dec_ref[i % 2, (lvl + 1) * C:(lvl + 2) * C, :] = jnp.exp2(b_last - b)
        tick()

    def gla_chunks(i, r0, state):
        C = GLA_CHUNK
        n_chunks = sub // C
        pv = proj_ref.at[i % 2]
        hmk = ghmk_ref[...]
        parts = []
        for c in range(n_chunks):
            rows = slice(r0 + c * C, r0 + (c + 1) * C)
            lrows = slice(c * C, (c + 1) * C)
            lanes = slice(c * KP, (c + 1) * KP)
            q = key_block(pv, lrows, OFF_GQ) * q_scale
            k = key_block(pv, lrows, OFF_GK)
            vb = pv[lrows, OFF_GV:OFF_GV + VW].astype(_BF)
            qb = q.astype(_BF)
            kbd0 = stack_heads(k.astype(_BF)) * hmk
            scores = jnp.zeros((C, N_HEADS * C), _F32)
            for lvl in range(N_GLA_LEVELS + 1):
                if lvl < N_GLA_LEVELS:
                    eb = dec_ref[i % 2, lvl * C:(lvl + 1) * C, lanes].astype(_BF)
                    ql, kbd = qb * eb, kbd0 * stack_heads(eb)
                else:
                    ql, kbd = qb, kbd0
                scores = scores + _dot_nt(ql, kbd) * gmask_ref[lvl]
                tick()
            vbd = stack_heads(vb) * ghmv_ref[...]
            e_cum = dec_ref[i % 2, N_GLA_LEVELS * C:(N_GLA_LEVELS + 1) * C, lanes]
            e_rev = dec_ref[i % 2, (N_GLA_LEVELS + 1) * C:(N_GLA_LEVELS + 2) * C, lanes]
            o_intra = _dot(scores.astype(_BF), vbd)
            upd = _dot_tn(vb, (k * e_rev).astype(_BF))
            parts.append((rows, (q * e_cum).astype(_BF), o_intra, upd, e_cum[C - 1:C, :]))
        for rows, qe, o_intra, upd, e_last in parts:
            oscr_ref[rows, VW:2 * VW] = o_intra + _dot_nt(qe, own_head(state))
            state = state * e_last + upd
            tick()
        return state

    states = {"ret": sret_ref[...]}
    gla_state = sgla_ref[...]
    for i in range(n_sub):
        r0 = i * sub
        srows = slice(r0, r0 + sub)
        if i + 1 < n_sub:
            pending.extend(projection_tasks(i + 1))
        tick()
        pooling(i)
        tick()
        stages = [retention(i, r0), gla_decay(i)]
        while stages:
            for stage in list(stages):
                if next(stage, stages) is stages:
                    stages.remove(stage)
        gla_state = gla_chunks(i, r0, gla_state)
        y_ret = _silu(proj_ref[i % 2, :, OFF_RG:OFF_RG + VW]) * head_rms(oscr_ref[srows, 0:VW])
        ycat_ref[srows, POOL_WIDTH:POOL_WIDTH + VW] = y_ret.astype(_BF)
        gg_off = OFF_GG % LANES
        gg = pltpu.roll(proj_ref[i % 2, :, OFF_GG - gg_off:N_IN], N_IN - OFF_GG, axis=1)[:, 0:VW]
        y_gla = (_silu(gg)
                 * (head_rms(oscr_ref[srows, VW:2 * VW]) * gng_ref[...]))
        require(i + 1, N_IN)
        ycat_ref[srows, POOL_WIDTH + VW:POOL_WIDTH + 2 * VW] = y_gla.astype(_BF)
        o_ref[0, srows, :] = x_ref[0, srows, :] + gate1 * _dot(ycat_ref[srows, :], wout_ref[...])
    sret_ref[...] = states["ret"]
    sgla_ref[...] = gla_state


def _whole(shape):
    nd = len(shape)
    return pl.BlockSpec(shape, lambda b, t: (0,) * nd, pipeline_mode=pl.Buffered(1))


def _layer_slab(shape, layer):
    nd = len(shape) - 1
    return pl.BlockSpec((None,) + tuple(shape[1:]), lambda b, t: (layer,) + (0,) * nd,
                        pipeline_mode=pl.Buffered(1))


def _mixer(x, mod, layer, params, cos, sin, tb):
    B, T, D = x.shape
    tile = MIX_TILE if T % MIX_TILE == 0 else T
    sub = MIX_SUB if tile % MIX_SUB == 0 else tile
    consts = [tb[k] for k in ("pool_win", "rot_first_half", "ret_decay", "ret_xi", "ret_zeta", "ret_gc",
                              "ret_hmk", "ret_hmv", "state_mask", "head_ind", "gla_tri", "gla_mask",
                              "gla_hmk", "gla_hmv")]
    norm_g, win = params[0], params[1]
    rest = list(params[2:])
    n_sub = tile // sub
    args = [x, mod, norm_g, win, cos, sin] + rest + consts
    in_specs = [
        pl.BlockSpec((1, tile, D), lambda b, t: (b, t, 0)),
        pl.BlockSpec((None, 1, 6, D), lambda b, t: (layer, b, 0, 0)),
        _layer_slab(norm_g.shape, layer),
        _layer_slab(win.shape, layer),
        pl.BlockSpec((1, tile, KP), lambda b, t: (b, t, 0)),
        pl.BlockSpec((1, tile, KP), lambda b, t: (b, t, 0)),
    ] + [_layer_slab(a.shape, layer) for a in rest] + [_whole(a.shape) for a in consts]
    n = sub + POOL_HIST
    gla_lanes = (sub // GLA_CHUNK) * KP
    return pl.pallas_call(
        functools.partial(_mixer_kernel, tile=tile, sub=sub),
        grid=(B, T // tile),
        in_specs=in_specs,
        out_specs=pl.BlockSpec((1, tile, D), lambda b, t: (b, t, 0)),
        out_shape=jax.ShapeDtypeStruct((B, T, D), _F32),
        scratch_shapes=[
            pltpu.VMEM((D, N_IN), _BF),
            pltpu.VMEM((POOL_WIDTH + 2 * VW, D), _BF),
            pltpu.VMEM((2, sub, D), _BF),
            pltpu.VMEM((2, sub, N_IN), _F32),
            pltpu.VMEM((tile, POOL_WIDTH + 2 * VW), _BF),
            pltpu.VMEM((2, GLA_CHUNK, gla_lanes), _F32),
            pltpu.VMEM((2, (N_GLA_LEVELS + 2) * GLA_CHUNK, gla_lanes), _F32),
            pltpu.VMEM((tile, 2 * VW), _F32),
            pltpu.VMEM((n, POOL_WIDTH), _F32),
            pltpu.VMEM((n, POOL_WIDTH), _F32),
            pltpu.VMEM((n, POOL_WIDTH), _F32),
            pltpu.VMEM((n, POOL_WIDTH), _F32),
            pltpu.VMEM((POOL_HIST, POOL_WIDTH), _F32),
            pltpu.VMEM((VW, KP), _F32),
            pltpu.VMEM((VW, KP), _F32),
        ],
        compiler_params=pltpu.CompilerParams(
            dimension_semantics=("arbitrary", "arbitrary"), vmem_limit_bytes=VMEM_LIMIT_BYTES),
        name="token_mixer",
    )(*args)


def _ffn_kernel(x_ref, mod_ref, ng_ref, wup32_ref, cw_ref, wd32_ref, fg_ref,
                o_ref, wup_ref, wd_ref, h_ref, carry_ref, act_ref,
                *, tile, sub, n_tiles, n_cast, d_ff, fb, final_norm):
    g = pl.program_id(0)
    rows_per_step = wup32_ref.shape[0]

    @pl.when(g < n_cast)
    def _():
        r0 = pl.multiple_of(g * rows_per_step, rows_per_step)
        wup_ref[pl.ds(r0, rows_per_step), :] = wup32_ref[...].astype(_BF)

    @pl.when(g == 0)
    def _():
        _cast_weight(wd32_ref, wd_ref)

    @pl.when(g >= n_cast)
    def _():
        @pl.when((g - n_cast) % n_tiles == 0)
        def _():
            carry_ref[...] = jnp.zeros_like(carry_ref)

        for base in range(0, tile, sub):
            _ffn_sub_tile(x_ref, mod_ref, ng_ref, wup_ref, cw_ref, wd_ref, fg_ref, o_ref, h_ref,
                          carry_ref, act_ref, base=base, tile=sub, d_ff=d_ff, fb=fb,
                          final_norm=final_norm)


def _ffn_sub_tile(x_ref, mod_ref, ng_ref, wup_ref, cw_ref, wd_ref, fg_ref, o_ref, h_ref, carry_ref,
                  act_ref, *, base, tile, d_ff, fb, final_norm):
    S = SUBLANES
    R = tile // S
    D = x_ref.shape[-1]
    shift2 = mod_ref[0, 3:4, :]
    scale2 = mod_ref[0, 4:5, :]
    gate2 = mod_ref[0, 5:6, :]
    norm_scale = ng_ref[...] * (1.0 + scale2)
    groups = FFN_NORM_ROWS // S
    early_cols = ([j * fb for j in range(FFN_EARLY_BLOCKS)]
                  + [d_ff + j * fb for j in range(FFN_EARLY_BLOCKS)])
    early = {col: [] for col in early_cols}
    x_parts = []
    for p in range(tile // FFN_NORM_ROWS):
        prow = slice(p * FFN_NORM_ROWS, (p + 1) * FFN_NORM_ROWS)
        xs = jnp.concatenate([x_ref[0, base + s * R + p * groups:base + s * R + (p + 1) * groups, :]
                              for s in range(S)], axis=0)
        xs = xs.reshape(S, groups, D).swapaxes(0, 1).reshape(FFN_NORM_ROWS, D)
        ms = jnp.mean(xs * xs, axis=-1, keepdims=True)
        h_ref[prow, :] = ((xs * lax.rsqrt(ms + EPS)) * norm_scale + shift2).astype(_BF)
        x_parts.append(xs)
        for col in early_cols:
            early[col].append(_dot(h_ref[prow, :], wup_ref[:, col:col + fb]))
    x = jnp.concatenate(x_parts, axis=0)
    first_sublane = lax.broadcasted_iota(jnp.int32, (S, fb), 0) == 0

    def up_conv(col):
        cols = slice(col, col + fb)
        if col in early:
            u = jnp.concatenate(early[col], axis=0)
        else:
            u = _dot(h_ref[...], wup_ref[:, cols])
        back1 = jnp.where(first_sublane, pltpu.roll(carry_ref[S:2 * S, cols], 1, axis=0),
                          pltpu.roll(u[tile - S:tile, :], 1, axis=0))
        back2 = jnp.where(first_sublane, pltpu.roll(carry_ref[0:S, cols], 1, axis=0),
                          pltpu.roll(u[tile - 2 * S:tile - S, :], 1, axis=0))
        carry_ref[:, cols] = u[tile - 2 * S:tile, :]
        prev1 = jnp.concatenate([back1, u[0:tile - S, :]], axis=0)
        prev2 = jnp.concatenate([back2, back1, u[0:tile - 2 * S, :]], axis=0)
        cw = cw_ref[:, cols]
        return prev2 * cw[0:1, :] + prev1 * cw[1:2, :] + u * cw[2:3, :] + cw[3:4, :]

    for j in range(d_ff // fb):
        ya = up_conv(j * fb)
        yg = up_conv(d_ff + j * fb)
        act_ref[:, j * fb:(j + 1) * fb] = (_silu(yg) * ya).astype(_BF)

    out = x + gate2 * _dot(act_ref[...], wd_ref[...])
    if final_norm:
        ms_o = jnp.mean(out * out, axis=-1, keepdims=True)
        out = out * lax.rsqrt(ms_o + EPS) * fg_ref[...]
    o_ref[0, base:base + tile, :] = out.reshape(R, S, D).swapaxes(0, 1).reshape(tile, D)


def _ffn(x, mod, layer, params, final_g, final_norm):
    B, T, D = x.shape
    tile = FFN_TILE if T % FFN_TILE == 0 else T
    sub = FFN_SUB if tile % FFN_SUB == 0 else tile
    norm_g, wup, cw, wd = params
    d_ff = wd.shape[1]
    n_tiles = T // tile
    n_cast = FFN_CAST_STEPS
    cast_rows = D // n_cast

    def token_tile(g):
        k = jnp.maximum(g - n_cast, 0)
        return k // n_tiles, k % n_tiles

    def slab(a):
        nd = a.ndim - 1
        return pl.BlockSpec((None,) + tuple(a.shape[1:]), lambda g: (layer,) + (0,) * nd,
                            pipeline_mode=pl.Buffered(1))

    in_specs = [
        pl.BlockSpec((1, tile, D), lambda g: token_tile(g) + (0,)),
        pl.BlockSpec((None, 1, 6, D), lambda g: (layer, token_tile(g)[0], 0, 0)),
        slab(norm_g),
        pl.BlockSpec((None, cast_rows, 2 * d_ff), lambda g: (layer, jnp.minimum(g, n_cast - 1), 0)),
        slab(cw),
        slab(wd),
        pl.BlockSpec(final_g.shape, lambda g: (0, 0), pipeline_mode=pl.Buffered(1)),
    ]
    return pl.pallas_call(
        functools.partial(_ffn_kernel, tile=tile, sub=sub, n_tiles=n_tiles, n_cast=n_cast, d_ff=d_ff,
                          fb=FFN_BLOCK, final_norm=final_norm),
        grid=(B * n_tiles + n_cast,),
        in_specs=in_specs,
        out_specs=pl.BlockSpec((1, tile, D), lambda g: token_tile(g) + (0,)),
        out_shape=jax.ShapeDtypeStruct((B, T, D), _F32),
        scratch_shapes=[
            pltpu.VMEM((D, 2 * d_ff), _BF),
            pltpu.VMEM((d_ff, D), _BF),
            pltpu.VMEM((sub, D), _BF),
            pltpu.VMEM((2 * SUBLANES, 2 * d_ff), _F32),
            pltpu.VMEM((sub, d_ff), _BF),
        ],
        compiler_params=pltpu.CompilerParams(
            dimension_semantics=("arbitrary",), vmem_limit_bytes=VMEM_LIMIT_BYTES),
        name="conv_ffn",
    )(x, mod, *params, final_g)


def _prep_mixer_params(norm1_g, w_in, pool_w, pool_scale, gla_wa2, gla_ba, gla_norm_g, w_out):
    L, D, _ = w_in.shape
    G = pool_w.shape[1]
    eye = jnp.eye(G, dtype=pool_w.dtype)
    poolw = (pool_w[:, :, :, None, :] * eye[None, :, None, :, None]).reshape(L, POOL_WIDTH, POOL_WIDTH)
    wa2 = jnp.pad(gla_wa2, ((0, 0), (0, LANES - GLA_GATE_RANK), (0, KP - KW)))
    ba = jnp.pad(gla_ba, ((0, 0), (0, KP - KW))).reshape(L, 1, KP)
    w_in = jnp.pad(w_in, ((0, 0), (0, 0), (0, N_IN - D_IN)))
    return (norm1_g.reshape(L, 1, D), w_in, poolw.astype(_BF), pool_scale.reshape(L, 1, POOL_WIDTH),
            wa2.astype(_BF), ba, gla_norm_g.reshape(L, 1, VW), w_out)


def _prep_ffn_params(norm2_g, w_up, conv_w, conv_b, w_down):
    L, D, _ = w_up.shape
    cw = jnp.pad(jnp.concatenate([conv_w, conv_b[:, None, :]], axis=1),
                 ((0, 0), (0, SUBLANES - CONV_WIDTH - 1), (0, 0)))
    return norm2_g.reshape(L, 1, D), w_up, cw, w_down


def kernel(x, c, positions, ada_w, ada_b, norm1_g, w_in, pool_w, pool_scale, gla_wa2, gla_ba,
           gla_norm_g, w_out, norm2_g, w_up, conv_w, conv_b, w_down, final_g):
    L = ada_w.shape[0]
    B, T, D = x.shape
    tb = {k: jnp.asarray(v) for k, v in _tables().items()}
    for name in ("head_ind", "gla_tri", "ret_hmk", "ret_hmv", "gla_hmk", "gla_hmv", "state_mask",
                 "rot_cos_expand", "rot_sin_expand"):
        tb[name] = tb[name].astype(_BF)
    mod = _modulation(c, ada_w, ada_b).reshape(L, B, 6, D)
    cos, sin = _rotary_tables(positions, tb)
    mixer_params = _prep_mixer_params(norm1_g, w_in, pool_w, pool_scale, gla_wa2, gla_ba, gla_norm_g, w_out)
    ffn_params = _prep_ffn_params(norm2_g, w_up, conv_w, conv_b, w_down)
    fg = final_g.reshape(1, D)
    for l in range(L):
        x = _mixer(x, mod, l, mixer_params, cos, sin, tb)
        x = _ffn(x, mod, l, ffn_params, fg, final_norm=(l == L - 1))
    return x
```

```python
import functools

import numpy as np
import jax
import jax.numpy as jnp
from jax import lax
from jax.experimental import pallas as pl
from jax.experimental.pallas import tpu as pltpu

POOL_WIDTH = 256
POOL_WINDOWS = (2, 4, 8, 16)
POOL_GROUP_DIM = 64
N_HEADS = 4
DK = 48
DV = 96
KW = N_HEADS * DK
VW = N_HEADS * DV
RET_CHUNK = 128
GLA_CHUNK = 64
GLA_GATE_RANK = 16
GLA_GATE_TAU = 16.0
ROPE_BASE = 10000.0
CONV_WIDTH = 3
EPS = 1e-6

LANES = 128
SUBLANES = 8
KP = 256
VMEM_LIMIT_BYTES = 56 * 1024 * 1024

OFF_POOL = 0
OFF_RQ = OFF_POOL + POOL_WIDTH
OFF_RK = OFF_RQ + KW
OFF_RV = OFF_RK + KW
OFF_RG = OFF_RV + VW
OFF_GQ = OFF_RG + VW
OFF_GK = OFF_GQ + KW
OFF_GV = OFF_GK + KW
OFF_GA = OFF_GV + VW
OFF_GG = OFF_GA + GLA_GATE_RANK
D_IN = OFF_GG + VW
N_IN = -(-D_IN // LANES) * LANES

MIX_TILE = 512
MIX_SUB = 256
PROJ_PIECE = 256
PROJ_TICKS = 3
FFN_TILE = 512
FFN_SUB = 512
FFN_CAST_STEPS = 8
FFN_BLOCK = 256
FFN_NORM_ROWS = 128
FFN_EARLY_BLOCKS = 2
ROT_TILE = 1024
ROT_ROWS = 32
POOL_HIST = 32
N_GLA_LEVELS = 6
CAST_ROWS = 128
LOG2_E = 1.4426950408889634

_BF = jnp.bfloat16
_F32 = jnp.float32


def _dot(a, b):
    return jnp.dot(a, b, preferred_element_type=_F32)


def _dot_nt(a, b):
    return lax.dot_general(a, b, (((1,), (1,)), ((), ())), preferred_element_type=_F32)


def _dot_tn(a, b):
    return lax.dot_general(a, b, (((0,), (0,)), ((), ())), preferred_element_type=_F32)


def _split3(x):
    p1 = x.astype(_BF)
    r1 = x - p1.astype(_F32)
    p2 = r1.astype(_BF)
    p3 = (r1 - p2.astype(_F32)).astype(_BF)
    return p1, p2, p3


def _cast_weight(src_ref, dst_ref):
    for r in range(0, src_ref.shape[0], CAST_ROWS):
        rows = slice(r, r + CAST_ROWS)
        dst_ref[rows, :] = src_ref[rows, :].astype(_BF)


def _cast_weight_transposed(src_ref, dst_ref):
    n, k = src_ref.shape
    for c in range(0, dst_ref.shape[1], CAST_ROWS):
        rows = min(CAST_ROWS, max(n - c, 0))
        block = src_ref[c:c + rows, :] if rows else None
        if rows < CAST_ROWS:
            zeros = jnp.zeros((CAST_ROWS - rows, k), _F32)
            block = zeros if block is None else jnp.concatenate([block, zeros], axis=0)
        dst_ref[:, c:c + CAST_ROWS] = block.T.astype(_BF)


def _sigmoid(x):
    return 1.0 / (1.0 + jnp.exp(-x))


def _silu(x):
    return x * _sigmoid(x)


def _key_head(d):
    return np.where(d < KW, d // DK, -1)


@functools.lru_cache(maxsize=None)
def _tables():
    t = {}
    kd = _key_head(np.arange(KP))
    vd = np.arange(VW) // DV

    def head_masks(chunk):
        rows = np.repeat(np.arange(N_HEADS), chunk)
        return ((rows[:, None] == kd[None, :]).astype(np.float32),
                (rows[:, None] == vd[None, :]).astype(np.float32))

    C = RET_CHUNK
    lg = np.log(1.0 - 2.0 ** (-5.0 - np.arange(N_HEADS, dtype=np.float64)))
    i = np.arange(C)
    rel = i[:, None] - i[None, :]
    dec = np.where(rel[None] >= 0, np.exp(np.maximum(rel, 0)[None] * lg[:, None, None]), 0.0)
    t["ret_decay"] = np.transpose(dec, (1, 0, 2)).reshape(C, N_HEADS * C).astype(np.float32)
    t["ret_xi"] = np.exp((i[:, None] + 1.0) * lg[vd][None, :]).astype(np.float32)
    zeta = np.exp((C - 1.0 - i)[:, None] * lg[np.maximum(kd, 0)][None, :]) * (kd >= 0)[None, :]
    t["ret_zeta"] = zeta.astype(np.float32)
    t["ret_gc"] = (np.exp(C * lg[np.maximum(kd, 0)]) * (kd >= 0))[None, :].astype(np.float32)
    t["ret_hmk"], t["ret_hmv"] = head_masks(C)
    t["state_mask"] = (vd[:, None] == kd[None, :]).astype(np.float32)
    t["head_ind"] = (vd[:, None] == vd[None, :]).astype(np.float32)

    C = GLA_CHUNK
    i = np.arange(C)
    masks = []
    s = C // 2
    while s >= 1:
        parent = (i // (2 * s)) * (2 * s)
        upper = (i - parent) >= s
        masks.append(upper[:, None] & (~upper)[None, :] & (parent[:, None] == parent[None, :]))
        s //= 2
    masks.append(i[:, None] == i[None, :])
    t["gla_mask"] = np.stack([np.tile(m, (1, N_HEADS)) for m in masks]).astype(np.float32)
    t["gla_tri"] = (i[None, :] <= i[:, None]).astype(np.float32)
    t["gla_hmk"], t["gla_hmv"] = head_masks(C)

    t["pool_win"] = np.repeat(np.asarray(POOL_WINDOWS, np.float32), POOL_GROUP_DIM)[None, :]

    half = DK // 2
    l = np.arange(KP)
    hit = (np.arange(ROT_ROWS)[:, None] == (l % half)[None, :]) & (l < KW)[None, :]
    t["rot_cos_expand"] = hit.astype(np.float32)
    t["rot_sin_expand"] = hit * np.where((l % DK) < half, -1.0, 1.0)[None, :].astype(np.float32)
    t["rot_first_half"] = ((l % DK) < half).astype(np.float32)[None, :]
    return t


def _mod_kernel(c_ref, w_ref, b_ref, o_ref):
    ca = _silu(c_ref[...]).astype(_BF)
    o_ref[...] = _dot(ca, w_ref[...].astype(_BF)) + b_ref[...]


def _modulation(c, ada_w, ada_b):
    L, D, N = ada_w.shape
    B = c.shape[0]
    tn = D
    return pl.pallas_call(
        _mod_kernel,
        grid=(L, N // tn),
        in_specs=[
            pl.BlockSpec((B, D), lambda l, n: (0, 0)),
            pl.BlockSpec((None, D, tn), lambda l, n: (l, 0, n)),
            pl.BlockSpec((None, 1, tn), lambda l, n: (l, 0, n)),
        ],
        out_specs=pl.BlockSpec((None, B, tn), lambda l, n: (l, 0, n)),
        out_shape=jax.ShapeDtypeStruct((L, B, N), _F32),
        compiler_params=pltpu.CompilerParams(
            dimension_semantics=("arbitrary", "arbitrary"), vmem_limit_bytes=VMEM_LIMIT_BYTES),
        name="adaln_modulation",
    )(c, ada_w, ada_b.reshape(L, 1, N))


def _rot_kernel(pos_ref, freq_ref, ec_ref, es_ref, cos_ref, sin_ref):
    ang = freq_ref[...] * pos_ref[0]
    ec = ec_ref[...]
    es = es_ref[...]

    def widen(v, e):
        p1, p2, p3 = _split3(v)
        return _dot_tn(p1, e) + _dot_tn(p2, e) + _dot_tn(p3, e)

    cos_ref[0] = widen(jnp.cos(ang), ec)
    sin_ref[0] = widen(jnp.sin(ang), es)


def _rotary_tables(positions, tb):
    B, T = positions.shape
    tile = ROT_TILE if T % ROT_TILE == 0 else T
    inv_freq = ROPE_BASE ** (-jnp.arange(0, DK, 2, dtype=_F32) / DK)
    freq = jnp.pad(inv_freq, (0, ROT_ROWS - DK // 2)).reshape(ROT_ROWS, 1)
    pos = positions.astype(_F32).reshape(B, 1, T)
    const = lambda b, t: (0, 0)
    return pl.pallas_call(
        _rot_kernel,
        grid=(B, T // tile),
        in_specs=[
            pl.BlockSpec((1, 1, tile), lambda b, t: (b, 0, t)),
            pl.BlockSpec((ROT_ROWS, 1), const),
            pl.BlockSpec((ROT_ROWS, KP), const),
            pl.BlockSpec((ROT_ROWS, KP), const),
        ],
        out_specs=[pl.BlockSpec((1, tile, KP), lambda b, t: (b, t, 0))] * 2,
        out_shape=[jax.ShapeDtypeStruct((B, T, KP), _F32)] * 2,
        compiler_params=pltpu.CompilerParams(dimension_semantics=("arbitrary", "arbitrary")),
        name="rotary_table",
    )(pos, freq, tb["rot_cos_expand"], tb["rot_sin_expand"])


def _mixer_kernel(x_ref, mod_ref, ng_ref, win32_ref, cos_ref, sin_ref,
                  poolw_ref, pools_ref, wa2_ref, ba_ref, gng_ref, wout32_ref,
                  poolwin_ref, rothalf_ref,
                  rdecay_ref, rxi_ref, rzeta_ref, rgc_ref, rhmk_ref, rhmv_ref,
                  smask_ref, hind_ref,
                  gtri_ref, gmask_ref, ghmk_ref, ghmv_ref,
                  o_ref,
                  win_ref, wout_ref,
                  hb_ref, proj_ref, ycat_ref, b_ref, dec_ref, oscr_ref, ext_ref, s2_ref, s4_ref, s8_ref,
                  hist_ref, sret_ref, sgla_ref, *, tile, sub):
    t_idx = pl.program_id(1)
    n_sub = tile // sub

    @pl.when((pl.program_id(0) == 0) & (t_idx == 0))
    def _():
        _cast_weight_transposed(win32_ref, win_ref)
        _cast_weight(wout32_ref, wout_ref)

    @pl.when(t_idx == 0)
    def _():
        hist_ref[...] = jnp.zeros_like(hist_ref)
        sret_ref[...] = jnp.zeros_like(sret_ref)
        sgla_ref[...] = jnp.zeros_like(sgla_ref)

    gate1 = mod_ref[0, 2:3, :]

    norm_scale = ng_ref[...] * (1.0 + mod_ref[0, 1:2, :])
    shift1 = mod_ref[0, 0:1, :]

    def projection_tasks(i):
        slot = i % 2

        def norm():
            x = x_ref[0, i * sub:(i + 1) * sub, :]
            ms = jnp.mean(x * x, axis=-1, keepdims=True)
            hb_ref[slot] = ((x * lax.rsqrt(ms + EPS)) * norm_scale + shift1).astype(_BF)

        def segment(off, width):
            def run():
                proj_ref[slot, :, off:off + width] = _dot(hb_ref[slot], win_ref[:, off:off + width])
            return run

        tasks = [(i, 0, norm)]
        for off in range(0, N_IN, PROJ_PIECE):
            width = min(PROJ_PIECE, N_IN - off)
            tasks.append((i, off + width, segment(off, width)))
        return tasks

    def key_block(pv, lrows, off):
        shift = off % LANES
        if shift == 0:
            return pv[lrows, off:off + KP]
        return pltpu.roll(pv[lrows, off - shift:off - shift + KP], KP - shift, axis=1)

    pending = projection_tasks(0)
    ticks = [0]

    def tick(weight=1):
        ticks[0] += weight
        if pending and ticks[0] >= PROJ_TICKS:
            ticks[0] = 0
            pending.pop(0)[2]()

    def require(i, column):
        while pending and (pending[0][0], pending[0][1]) <= (i, column):
            pending.pop(0)[2]()

    require(0, N_IN)

    def pooling(i):
        n = sub + POOL_HIST
        srows = slice(i * sub, (i + 1) * sub)
        u = proj_ref[i % 2, :, OFF_POOL:OFF_POOL + POOL_WIDTH]
        ext_ref[0:POOL_HIST, :] = hist_ref[...]
        ext_ref[POOL_HIST:n, :] = u
        hist_ref[...] = u[sub - POOL_HIST:sub, :]
        s2_ref[8:n, :] = ext_ref[8:n, :] + ext_ref[7:n - 1, :]
        s4_ref[16:n, :] = s2_ref[16:n, :] + s2_ref[14:n - 2, :]
        s8_ref[24:n, :] = s4_ref[24:n, :] + s4_ref[20:n - 4, :]
        s16 = s8_ref[32:n, :] + s8_ref[24:n - 8, :]
        lane = lax.broadcasted_iota(jnp.int32, (sub, POOL_WIDTH), 1)
        wsum = jnp.where(lane < POOL_GROUP_DIM, s2_ref[32:n, :],
                         jnp.where(lane < 2 * POOL_GROUP_DIM, s4_ref[32:n, :],
                                   jnp.where(lane < 3 * POOL_GROUP_DIM, s8_ref[32:n, :], s16)))
        t_abs = (t_idx * tile + i * sub
                 + lax.broadcasted_iota(jnp.int32, (sub, POOL_WIDTH), 0)).astype(_F32)
        cnt = jnp.minimum(t_abs + 1.0, poolwin_ref[...])
        pooled = wsum / cnt - u
        ycat_ref[srows, 0:POOL_WIDTH] = (_dot(pooled.astype(_BF), poolw_ref[...])
                                         * pools_ref[...]).astype(_BF)

    inv_dv = 1.0 / DV
    k_scale = DK ** -0.5
    q_scale = DK ** -0.5
    smask = smask_ref[...]

    def own_head(state):
        return state.astype(_BF) * smask
    hind = hind_ref[...]
    first_half = rothalf_ref[...] > 0.5

    def head_rms(o):
        ms_h = _dot((o * o).astype(_BF), hind) * inv_dv
        return o * lax.rsqrt(ms_h + EPS)

    def stack_heads(a):
        return jnp.concatenate([a] * N_HEADS, axis=0)

    def rotary(t, cosv, sinv):
        partner = jnp.where(first_half, pltpu.roll(t, KP - DK // 2, axis=1), pltpu.roll(t, DK // 2, axis=1))
        return t * cosv + partner * sinv

    def retention(i, r0):
        C = RET_CHUNK
        pv = proj_ref.at[i % 2]
        parts = []
        for c in range(sub // C):
            rows = slice(r0 + c * C, r0 + (c + 1) * C)
            lrows = slice(c * C, (c + 1) * C)
            cosv = cos_ref[0, rows, :]
            sinv = sin_ref[0, rows, :]
            q = rotary(key_block(pv, lrows, OFF_RQ), cosv, sinv)
            k = rotary(key_block(pv, lrows, OFF_RK), cosv, sinv) * k_scale
            vb = pv[lrows, OFF_RV:OFF_RV + VW].astype(_BF)
            qb = q.astype(_BF)
            kbd = stack_heads(k.astype(_BF)) * rhmk_ref[...]
            scores = _dot_nt(qb, kbd) * rdecay_ref[...]
            vbd = stack_heads(vb) * rhmv_ref[...]
            o_intra = _dot(scores.astype(_BF), vbd)
            upd = _dot_tn(vb, (k * rzeta_ref[...]).astype(_BF))
            parts.append((rows, qb, o_intra, upd))
            tick()
            yield
        state = states["ret"]
        for rows, qb, o_intra, upd in parts:
            oscr_ref[rows, 0:VW] = o_intra + _dot_nt(qb, own_head(state)) * rxi_ref[...]
            state = state * rgc_ref[...] + upd
            tick()
            yield
        states["ret"] = state

    def gla_decay(i):
        C = GLA_CHUNK
        n_chunks = sub // C
        W = n_chunks * KP
        pv = proj_ref.at[i % 2]
        logits = _dot(pv[:, OFF_GA:OFF_GA + LANES].astype(_BF), wa2_ref[...]) + ba_ref[...]
        log_sig = jnp.minimum(logits, 0.0) - jnp.log(1.0 + jnp.exp(-jnp.abs(logits)))
        la = log_sig * (LOG2_E / GLA_GATE_TAU)

        la_l = jnp.concatenate([la[c * C:(c + 1) * C, :] for c in range(n_chunks)], axis=1)
        tri = gtri_ref[...]
        p1, p2, p3 = _split3(la_l)
        b = _dot(tri, p1) + _dot(tri, p2) + _dot(tri, p3)
        b_ref[i % 2] = b
        yield

        row = lax.broadcasted_iota(jnp.int32, (C, W), 0)
        s = C // 2
        lvl = 0
        while s >= SUBLANES // 2:
            refs = [jnp.broadcast_to(b_ref[i % 2, p + s - 1:p + s, :], (2 * s, W)) for p in range(0, C, 2 * s)]
            b_at_ref = refs[0] if len(refs) == 1 else jnp.concatenate(refs, axis=0)
            dec_ref[i % 2, lvl * C:(lvl + 1) * C, :] = jnp.exp2(-jnp.abs(b - b_at_ref))
            s //= 2
            lvl += 1
            tick()
            yield
        up1 = pltpu.roll(b, 1, axis=0)
        up2 = pltpu.roll(b, 2, axis=0)
        dn1 = pltpu.roll(b, C - 1, axis=0)
        m4 = row % 4
        b_at_ref = jnp.where(m4 == 0, dn1, jnp.where(m4 == 1, b, jnp.where(m4 == 2, up1, up2)))
        dec_ref[i % 2, lvl * C:(lvl + 1) * C, :] = jnp.exp2(-jnp.abs(b - b_at_ref))
        lvl += 1
        tick()
        yield
        b_at_ref = jnp.where(row % 2 == 0, b, up1)
        dec_ref[i % 2, lvl * C:(lvl + 1) * C, :] = jnp.exp2(-jnp.abs(b - b_at_ref))
        lvl += 1
        tick()
        yield
        dec_ref[i % 2, lvl * C:(lvl + 1) * C, :] = jnp.exp2(b)
        tick()
        b_last = jnp.broadcast_to(b_ref[i % 2, C - 1:C, :], (C, W))
        dec_ref[i % 2, (lvl + 1) * C:(lvl + 2) * C, :] = jnp.exp2(b_last - b)
        tick()

    def gla_chunks(i, r0, state):
        C = GLA_CHUNK
        n_chunks = sub // C
        pv = proj_ref.at[i % 2]
        hmk = ghmk_ref[...]
        parts = []
        for c in range(n_chunks):
            rows = slice(r0 + c * C, r0 + (c + 1) * C)
            lrows = slice(c * C, (c + 1) * C)
            lanes = slice(c * KP, (c + 1) * KP)
            q = key_block(pv, lrows, OFF_GQ) * q_scale
            k = key_block(pv, lrows, OFF_GK)
            vb = pv[lrows, OFF_GV:OFF_GV + VW].astype(_BF)
            qb = q.astype(_BF)
            kbd0 = stack_heads(k.astype(_BF)) * hmk
            scores = jnp.zeros((C, N_HEADS * C), _F32)
            for lvl in range(N_GLA_LEVELS + 1):
                if lvl < N_GLA_LEVELS:
                    eb = dec_ref[i % 2, lvl * C:(lvl + 1) * C, lanes].astype(_BF)
                    ql, kbd = qb * eb, kbd0 * stack_heads(eb)
                else:
                    ql, kbd = qb, kbd0
                scores = scores + _dot_nt(ql, kbd) * gmask_ref[lvl]
                tick()
            vbd = stack_heads(vb) * ghmv_ref[...]
            e_cum = dec_ref[i % 2, N_GLA_LEVELS * C:(N_GLA_LEVELS + 1) * C, lanes]
            e_rev = dec_ref[i % 2, (N_GLA_LEVELS + 1) * C:(N_GLA_LEVELS + 2) * C, lanes]
            o_intra = _dot(scores.astype(_BF), vbd)
            upd = _dot_tn(vb, (k * e_rev).astype(_BF))
            parts.append((rows, (q * e_cum).astype(_BF), o_intra, upd, e_cum[C - 1:C, :]))
        for rows, qe, o_intra, upd, e_last in parts:
            oscr_ref[rows, VW:2 * VW] = o_intra + _dot_nt(qe, own_head(state))
            state = state * e_last + upd
            tick()
        return state

    states = {"ret": sret_ref[...]}
    gla_state = sgla_ref[...]
    for i in range(n_sub):
        r0 = i * sub
        srows = slice(r0, r0 + sub)
        if i + 1 < n_sub:
            pending.extend(projection_tasks(i + 1))
        tick()
        pooling(i)
        tick()
        stages = [retention(i, r0), gla_decay(i)]
        while stages:
            for stage in list(stages):
                if next(stage, stages) is stages:
                    stages.remove(stage)
        gla_state = gla_chunks(i, r0, gla_state)
        y_ret = _silu(proj_ref[i % 2, :, OFF_RG:OFF_RG + VW]) * head_rms(oscr_ref[srows, 0:VW])
        ycat_ref[srows, POOL_WIDTH:POOL_WIDTH + VW] = y_ret.astype(_BF)
        gg_off = OFF_GG % LANES
        gg = pltpu.roll(proj_ref[i % 2, :, OFF_GG - gg_off:N_IN], N_IN - OFF_GG, axis=1)[:, 0:VW]
        y_gla = (_silu(gg)
                 * (head_rms(oscr_ref[srows, VW:2 * VW]) * gng_ref[...]))
        require(i + 1, N_IN)
        ycat_ref[srows, POOL_WIDTH + VW:POOL_WIDTH + 2 * VW] = y_gla.astype(_BF)
        o_ref[0, srows, :] = x_ref[0, srows, :] + gate1 * _dot(ycat_ref[srows, :], wout_ref[...])
    sret_ref[...] = states["ret"]
    sgla_ref[...] = gla_state


def _whole(shape):
    nd = len(shape)
    return pl.BlockSpec(shape, lambda b, t: (0,) * nd, pipeline_mode=pl.Buffered(1))


def _layer_slab(shape, layer):
    nd = len(shape) - 1
    return pl.BlockSpec((None,) + tuple(shape[1:]), lambda b, t: (layer,) + (0,) * nd,
                        pipeline_mode=pl.Buffered(1))


def _mixer(x, mod, layer, params, cos, sin, tb):
    B, T, D = x.shape
    tile = MIX_TILE if T % MIX_TILE == 0 else T
    sub = MIX_SUB if tile % MIX_SUB == 0 else tile
    consts = [tb[k] for k in ("pool_win", "rot_first_half", "ret_decay", "ret_xi", "ret_zeta", "ret_gc",
                              "ret_hmk", "ret_hmv", "state_mask", "head_ind", "gla_tri", "gla_mask",
                              "gla_hmk", "gla_hmv")]
    norm_g, win = params[0], params[1]
    rest = list(params[2:])
    n_sub = tile // sub
    args = [x, mod, norm_g, win, cos, sin] + rest + consts
    in_specs = [
        pl.BlockSpec((1, tile, D), lambda b, t: (b, t, 0)),
        pl.BlockSpec((None, 1, 6, D), lambda b, t: (layer, b, 0, 0)),
        _layer_slab(norm_g.shape, layer),
        _layer_slab(win.shape, layer),
        pl.BlockSpec((1, tile, KP), lambda b, t: (b, t, 0)),
        pl.BlockSpec((1, tile, KP), lambda b, t: (b, t, 0)),
    ] + [_layer_slab(a.shape, layer) for a in rest] + [_whole(a.shape) for a in consts]
    n = sub + POOL_HIST
    gla_lanes = (sub // GLA_CHUNK) * KP
    return pl.pallas_call(
        functools.partial(_mixer_kernel, tile=tile, sub=sub),
        grid=(B, T // tile),
        in_specs=in_specs,
        out_specs=pl.BlockSpec((1, tile, D), lambda b, t: (b, t, 0)),
        out_shape=jax.ShapeDtypeStruct((B, T, D), _F32),
        scratch_shapes=[
            pltpu.VMEM((D, N_IN), _BF),
            pltpu.VMEM((POOL_WIDTH + 2 * VW, D), _BF),
            pltpu.VMEM((2, sub, D), _BF),
            pltpu.VMEM((2, sub, N_IN), _F32),
            pltpu.VMEM((tile, POOL_WIDTH + 2 * VW), _BF),
            pltpu.VMEM((2, GLA_CHUNK, gla_lanes), _F32),
            pltpu.VMEM((2, (N_GLA_LEVELS + 2) * GLA_CHUNK, gla_lanes), _F32),
            pltpu.VMEM((tile, 2 * VW), _F32),
            pltpu.VMEM((n, POOL_WIDTH), _F32),
            pltpu.VMEM((n, POOL_WIDTH), _F32),
            pltpu.VMEM((n, POOL_WIDTH), _F32),
            pltpu.VMEM((n, POOL_WIDTH), _F32),
            pltpu.VMEM((POOL_HIST, POOL_WIDTH), _F32),
            pltpu.VMEM((VW, KP), _F32),
            pltpu.VMEM((VW, KP), _F32),
        ],
        compiler_params=pltpu.CompilerParams(
            dimension_semantics=("arbitrary", "arbitrary"), vmem_limit_bytes=VMEM_LIMIT_BYTES),
        name="token_mixer",
    )(*args)


def _ffn_kernel(x_ref, mod_ref, ng_ref, wup32_ref, cw_ref, wd32_ref, fg_ref,
                o_ref, wup_ref, wd_ref, h_ref, carry_ref, act_ref,
                *, tile, sub, n_tiles, n_cast, d_ff, fb, final_norm):
    g = pl.program_id(0)
    rows_per_step = wup32_ref.shape[0]

    @pl.when(g < n_cast)
    def _():
        r0 = pl.multiple_of(g * rows_per_step, rows_per_step)
        wup_ref[pl.ds(r0, rows_per_step), :] = wup32_ref[...].astype(_BF)

    @pl.when(g == 0)
    def _():
        _cast_weight(wd32_ref, wd_ref)

    @pl.when(g >= n_cast)
    def _():
        @pl.when((g - n_cast) % n_tiles == 0)
        def _():
            carry_ref[...] = jnp.zeros_like(carry_ref)

        for base in range(0, tile, sub):
            _ffn_sub_tile(x_ref, mod_ref, ng_ref, wup_ref, cw_ref, wd_ref, fg_ref, o_ref, h_ref,
                          carry_ref, act_ref, base=base, tile=sub, d_ff=d_ff, fb=fb,
                          final_norm=final_norm)


def _ffn_sub_tile(x_ref, mod_ref, ng_ref, wup_ref, cw_ref, wd_ref, fg_ref, o_ref, h_ref, carry_ref,
                  act_ref, *, base, tile, d_ff, fb, final_norm):
    S = SUBLANES
    R = tile // S
    D = x_ref.shape[-1]
    shift2 = mod_ref[0, 3:4, :]
    scale2 = mod_ref[0, 4:5, :]
    gate2 = mod_ref[0, 5:6, :]
    norm_scale = ng_ref[...] * (1.0 + scale2)
    groups = FFN_NORM_ROWS // S
    early_cols = ([j * fb for j in range(FFN_EARLY_BLOCKS)]
                  + [d_ff + j * fb for j in range(FFN_EARLY_BLOCKS)])
    early = {col: [] for col in early_cols}
    x_parts = []
    for p in range(tile // FFN_NORM_ROWS):
        prow = slice(p * FFN_NORM_ROWS, (p + 1) * FFN_NORM_ROWS)
        xs = jnp.concatenate([x_ref[0, base + s * R + p * groups:base + s * R + (p + 1) * groups, :]
                              for s in range(S)], axis=0)
        xs = xs.reshape(S, groups, D).swapaxes(0, 1).reshape(FFN_NORM_ROWS, D)
        ms = jnp.mean(xs * xs, axis=-1, keepdims=True)
        h_ref[prow, :] = ((xs * lax.rsqrt(ms + EPS)) * norm_scale + shift2).astype(_BF)
        x_parts.append(xs)
        for col in early_cols:
            early[col].append(_dot(h_ref[prow, :], wup_ref[:, col:col + fb]))
    x = jnp.concatenate(x_parts, axis=0)
    first_sublane = lax.broadcasted_iota(jnp.int32, (S, fb), 0) == 0

    def up_conv(col):
        cols = slice(col, col + fb)
        if col in early:
            u = jnp.concatenate(early[col], axis=0)
        else:
            u = _dot(h_ref[...], wup_ref[:, cols])
        back1 = jnp.where(first_sublane, pltpu.roll(carry_ref[S:2 * S, cols], 1, axis=0),
                          pltpu.roll(u[tile - S:tile, :], 1, axis=0))
        back2 = jnp.where(first_sublane, pltpu.roll(carry_ref[0:S, cols], 1, axis=0),
                          pltpu.roll(u[tile - 2 * S:tile - S, :], 1, axis=0))
        carry_ref[:, cols] = u[tile - 2 * S:tile, :]
        prev1 = jnp.concatenate([back1, u[0:tile - S, :]], axis=0)
        prev2 = jnp.concatenate([back2, back1, u[0:tile - 2 * S, :]], axis=0)
        cw = cw_ref[:, cols]
        return prev2 * cw[0:1, :] + prev1 * cw[1:2, :] + u * cw[2:3, :] + cw[3:4, :]

    for j in range(d_ff // fb):
        ya = up_conv(j * fb)
        yg = up_conv(d_ff + j * fb)
        act_ref[:, j * fb:(j + 1) * fb] = (_silu(yg) * ya).astype(_BF)

    out = x + gate2 * _dot(act_ref[...], wd_ref[...])
    if final_norm:
        ms_o = jnp.mean(out * out, axis=-1, keepdims=True)
        out = out * lax.rsqrt(ms_o + EPS) * fg_ref[...]
    o_ref[0, base:base + tile, :] = out.reshape(R, S, D).swapaxes(0, 1).reshape(tile, D)


def _ffn(x, mod, layer, params, final_g, final_norm):
    B, T, D = x.shape
    tile = FFN_TILE if T % FFN_TILE == 0 else T
    sub = FFN_SUB if tile % FFN_SUB == 0 else tile
    norm_g, wup, cw, wd = params
    d_ff = wd.shape[1]
    n_tiles = T // tile
    n_cast = FFN_CAST_STEPS
    cast_rows = D // n_cast

    def token_tile(g):
        k = jnp.maximum(g - n_cast, 0)
        return k // n_tiles, k % n_tiles

    def slab(a):
        nd = a.ndim - 1
        return pl.BlockSpec((None,) + tuple(a.shape[1:]), lambda g: (layer,) + (0,) * nd,
                            pipeline_mode=pl.Buffered(1))

    in_specs = [
        pl.BlockSpec((1, tile, D), lambda g: token_tile(g) + (0,)),
        pl.BlockSpec((None, 1, 6, D), lambda g: (layer, token_tile(g)[0], 0, 0)),
        slab(norm_g),
        pl.BlockSpec((None, cast_rows, 2 * d_ff), lambda g: (layer, jnp.minimum(g, n_cast - 1), 0)),
        slab(cw),
        slab(wd),
        pl.BlockSpec(final_g.shape, lambda g: (0, 0), pipeline_mode=pl.Buffered(1)),
    ]
    return pl.pallas_call(
        functools.partial(_ffn_kernel, tile=tile, sub=sub, n_tiles=n_tiles, n_cast=n_cast, d_ff=d_ff,
                          fb=FFN_BLOCK, final_norm=final_norm),
        grid=(B * n_tiles + n_cast,),
        in_specs=in_specs,
        out_specs=pl.BlockSpec((1, tile, D), lambda g: token_tile(g) + (0,)),
        out_shape=jax.ShapeDtypeStruct((B, T, D), _F32),
        scratch_shapes=[
            pltpu.VMEM((D, 2 * d_ff), _BF),
            pltpu.VMEM((d_ff, D), _BF),
            pltpu.VMEM((sub, D), _BF),
            pltpu.VMEM((2 * SUBLANES, 2 * d_ff), _F32),
            pltpu.VMEM((sub, d_ff), _BF),
        ],
        compiler_params=pltpu.CompilerParams(
            dimension_semantics=("arbitrary",), vmem_limit_bytes=VMEM_LIMIT_BYTES),
        name="conv_ffn",
    )(x, mod, *params, final_g)


def _prep_mixer_params(norm1_g, w_in, pool_w, pool_scale, gla_wa2, gla_ba, gla_norm_g, w_out):
    L, D, _ = w_in.shape
    G = pool_w.shape[1]
    eye = jnp.eye(G, dtype=pool_w.dtype)
    poolw = (pool_w[:, :, :, None, :] * eye[None, :, None, :, None]).reshape(L, POOL_WIDTH, POOL_WIDTH)
    wa2 = jnp.pad(gla_wa2, ((0, 0), (0, LANES - GLA_GATE_RANK), (0, KP - KW)))
    ba = jnp.pad(gla_ba, ((0, 0), (0, KP - KW))).reshape(L, 1, KP)
    w_in = jnp.swapaxes(w_in, 1, 2)
    return (norm1_g.reshape(L, 1, D), w_in, poolw.astype(_BF), pool_scale.reshape(L, 1, POOL_WIDTH),
            wa2.astype(_BF), ba, gla_norm_g.reshape(L, 1, VW), w_out)


def _prep_ffn_params(norm2_g, w_up, conv_w, conv_b, w_down):
    L, D, _ = w_up.shape
    cw = jnp.pad(jnp.concatenate([conv_w, conv_b[:, None, :]], axis=1),
                 ((0, 0), (0, SUBLANES - CONV_WIDTH - 1), (0, 0)))
    return norm2_g.reshape(L, 1, D), w_up, cw, w_down


def kernel(x, c, positions, ada_w, ada_b, norm1_g, w_in, pool_w, pool_scale, gla_wa2, gla_ba,
           gla_norm_g, w_out, norm2_g, w_up, conv_w, conv_b, w_down, final_g):
    L = ada_w.shape[0]
    B, T, D = x.shape
    tb = {k: jnp.asarray(v) for k, v in _tables().items()}
    for name in ("head_ind", "gla_tri", "ret_hmk", "ret_hmv", "gla_hmk", "gla_hmv", "state_mask",
                 "rot_cos_expand", "rot_sin_expand"):
        tb[name] = tb[name].astype(_BF)
    mod = _modulation(c, ada_w, ada_b).reshape(L, B, 6, D)
    cos, sin = _rotary_tables(positions, tb)
    mixer_params = _prep_mixer_params(norm1_g, w_in, pool_w, pool_scale, gla_wa2, gla_ba, gla_norm_g, w_out)
    ffn_params = _prep_ffn_params(norm2_g, w_up, conv_w, conv_b, w_down)
    fg = final_g.reshape(1, D)
    for l in range(L):
        x = _mixer(x, mod, l, mixer_params, cos, sin, tb)
        x = _ffn(x, mod, l, ffn_params, fg, final_norm=(l == L - 1))
    return x
```

```python
import functools

import numpy as np
import jax
import jax.numpy as jnp
from jax import lax
from jax.experimental import pallas as pl
from jax.experimental.pallas import tpu as pltpu

POOL_WIDTH = 256
POOL_WINDOWS = (2, 4, 8, 16)
POOL_GROUP_DIM = 64
N_HEADS = 4
DK = 48
DV = 96
KW = N_HEADS * DK
VW = N_HEADS * DV
RET_CHUNK = 128
GLA_CHUNK = 64
GLA_GATE_RANK = 16
GLA_GATE_TAU = 16.0
ROPE_BASE = 10000.0
CONV_WIDTH = 3
EPS = 1e-6

LANES = 128
SUBLANES = 8
KP = 256
VMEM_LIMIT_BYTES = 56 * 1024 * 1024

OFF_POOL = 0
OFF_RQ = OFF_POOL + POOL_WIDTH
OFF_RK = OFF_RQ + KW
OFF_RV = OFF_RK + KW
OFF_RG = OFF_RV + VW
OFF_GQ = OFF_RG + VW
OFF_GK = OFF_GQ + KW
OFF_GV = OFF_GK + KW
OFF_GA = OFF_GV + VW
OFF_GG = OFF_GA + GLA_GATE_RANK
D_IN = OFF_GG + VW
N_IN = -(-D_IN // LANES) * LANES

MIX_TILE = 512
MIX_SUB = 256
PROJ_PIECE = 256
PROJ_TICKS = 3
FFN_TILE = 512
FFN_SUB = 512
FFN_CAST_STEPS = 8
FFN_BLOCK = 256
FFN_NORM_ROWS = 128
FFN_EARLY_BLOCKS = 2
ROT_TILE = 1024
ROT_ROWS = 32
POOL_HIST = 32
N_GLA_LEVELS = 6
CAST_ROWS = 128
LOG2_E = 1.4426950408889634

_BF = jnp.bfloat16
_F32 = jnp.float32


def _dot(a, b):
    return jnp.dot(a, b, preferred_element_type=_F32)


def _dot_nt(a, b):
    return lax.dot_general(a, b, (((1,), (1,)), ((), ())), preferred_element_type=_F32)


def _dot_tn(a, b):
    return lax.dot_general(a, b, (((0,), (0,)), ((), ())), preferred_element_type=_F32)


def _split3(x):
    p1 = x.astype(_BF)
    r1 = x - p1.astype(_F32)
    p2 = r1.astype(_BF)
    p3 = (r1 - p2.astype(_F32)).astype(_BF)
    return p1, p2, p3


def _cast_weight(src_ref, dst_ref):
    for r in range(0, src_ref.shape[0], CAST_ROWS):
        rows = slice(r, r + CAST_ROWS)
        dst_ref[rows, :] = src_ref[rows, :].astype(_BF)


def _cast_weight_transposed(src_ref, dst_ref):
    n, k = src_ref.shape
    for c in range(0, dst_ref.shape[1], CAST_ROWS):
        rows = min(CAST_ROWS, max(n - c, 0))
        block = src_ref[c:c + rows, :] if rows else None
        if rows < CAST_ROWS:
            zeros = jnp.zeros((CAST_ROWS - rows, k), _F32)
            block = zeros if block is None else jnp.concatenate([block, zeros], axis=0)
        dst_ref[:, c:c + CAST_ROWS] = block.T.astype(_BF)


def _sigmoid(x):
    return 1.0 / (1.0 + jnp.exp(-x))


def _silu(x):
    return x * _sigmoid(x)


def _key_head(d):
    return np.where(d < KW, d // DK, -1)


@functools.lru_cache(maxsize=None)
def _tables():
    t = {}
    kd = _key_head(np.arange(KP))
    vd = np.arange(VW) // DV

    def head_masks(chunk):
        rows = np.repeat(np.arange(N_HEADS), chunk)
        return ((rows[:, None] == kd[None, :]).astype(np.float32),
                (rows[:, None] == vd[None, :]).astype(np.float32))

    C = RET_CHUNK
    lg = np.log(1.0 - 2.0 ** (-5.0 - np.arange(N_HEADS, dtype=np.float64)))
    i = np.arange(C)
    rel = i[:, None] - i[None, :]
    dec = np.where(rel[None] >= 0, np.exp(np.maximum(rel, 0)[None] * lg[:, None, None]), 0.0)
    t["ret_decay"] = np.transpose(dec, (1, 0, 2)).reshape(C, N_HEADS * C).astype(np.float32)
    t["ret_xi"] = np.exp((i[:, None] + 1.0) * lg[vd][None, :]).astype(np.float32)
    zeta = np.exp((C - 1.0 - i)[:, None] * lg[np.maximum(kd, 0)][None, :]) * (kd >= 0)[None, :]
    t["ret_zeta"] = zeta.astype(np.float32)
    t["ret_gc"] = (np.exp(C * lg[np.maximum(kd, 0)]) * (kd >= 0))[None, :].astype(np.float32)
    t["ret_hmk"], t["ret_hmv"] = head_masks(C)
    t["state_mask"] = (vd[:, None] == kd[None, :]).astype(np.float32)
    t["head_ind"] = (vd[:, None] == vd[None, :]).astype(np.float32)

    C = GLA_CHUNK
    i = np.arange(C)
    masks = []
    s = C // 2
    while s >= 1:
        parent = (i // (2 * s)) * (2 * s)
        upper = (i - parent) >= s
        masks.append(upper[:, None] & (~upper)[None, :] & (parent[:, None] == parent[None, :]))
        s //= 2
    masks.append(i[:, None] == i[None, :])
    t["gla_mask"] = np.stack([np.tile(m, (1, N_HEADS)) for m in masks]).astype(np.float32)
    t["gla_tri"] = (i[None, :] <= i[:, None]).astype(np.float32)
    t["gla_hmk"], t["gla_hmv"] = head_masks(C)

    t["pool_win"] = np.repeat(np.asarray(POOL_WINDOWS, np.float32), POOL_GROUP_DIM)[None, :]

    half = DK // 2
    l = np.arange(KP)
    hit = (np.arange(ROT_ROWS)[:, None] == (l % half)[None, :]) & (l < KW)[None, :]
    t["rot_cos_expand"] = hit.astype(np.float32)
    t["rot_sin_expand"] = hit * np.where((l % DK) < half, -1.0, 1.0)[None, :].astype(np.float32)
    t["rot_first_half"] = ((l % DK) < half).astype(np.float32)[None, :]
    return t


def _mod_kernel(c_ref, w_ref, b_ref, o_ref):
    ca = _silu(c_ref[...]).astype(_BF)
    o_ref[...] = _dot(ca, w_ref[...].astype(_BF)) + b_ref[...]


def _modulation(c, ada_w, ada_b):
    L, D, N = ada_w.shape
    B = c.shape[0]
    tn = D
    return pl.pallas_call(
        _mod_kernel,
        grid=(L, N // tn),
        in_specs=[
            pl.BlockSpec((B, D), lambda l, n: (0, 0)),
            pl.BlockSpec((None, D, tn), lambda l, n: (l, 0, n)),
            pl.BlockSpec((None, 1, tn), lambda l, n: (l, 0, n)),
        ],
        out_specs=pl.BlockSpec((None, B, tn), lambda l, n: (l, 0, n)),
        out_shape=jax.ShapeDtypeStruct((L, B, N), _F32),
        compiler_params=pltpu.CompilerParams(
            dimension_semantics=("arbitrary", "arbitrary"), vmem_limit_bytes=VMEM_LIMIT_BYTES),
        name="adaln_modulation",
    )(c, ada_w, ada_b.reshape(L, 1, N))


def _rot_kernel(pos_ref, freq_ref, ec_ref, es_ref, cos_ref, sin_ref):
    ang = freq_ref[...] * pos_ref[0]
    ec = ec_ref[...]
    es = es_ref[...]

    def widen(v, e):
        pieces = jnp.concatenate(_split3(v), axis=0)
        return _dot_tn(pieces, jnp.concatenate([e] * 3, axis=0))

    cos_ref[0] = widen(jnp.cos(ang), ec)
    sin_ref[0] = widen(jnp.sin(ang), es)


def _rotary_tables(positions, tb):
    B, T = positions.shape
    tile = ROT_TILE if T % ROT_TILE == 0 else T
    inv_freq = ROPE_BASE ** (-jnp.arange(0, DK, 2, dtype=_F32) / DK)
    freq = jnp.pad(inv_freq, (0, ROT_ROWS - DK // 2)).reshape(ROT_ROWS, 1)
    pos = positions.astype(_F32).reshape(B, 1, T)
    const = lambda b, t: (0, 0)
    return pl.pallas_call(
        _rot_kernel,
        grid=(B, T // tile),
        in_specs=[
            pl.BlockSpec((1, 1, tile), lambda b, t: (b, 0, t)),
            pl.BlockSpec((ROT_ROWS, 1), const),
            pl.BlockSpec((ROT_ROWS, KP), const),
            pl.BlockSpec((ROT_ROWS, KP), const),
        ],
        out_specs=[pl.BlockSpec((1, tile, KP), lambda b, t: (b, t, 0))] * 2,
        out_shape=[jax.ShapeDtypeStruct((B, T, KP), _F32)] * 2,
        compiler_params=pltpu.CompilerParams(dimension_semantics=("arbitrary", "arbitrary")),
        name="rotary_table",
    )(pos, freq, tb["rot_cos_expand"], tb["rot_sin_expand"])


def _mixer_kernel(x_ref, mod_ref, ng_ref, win32_ref, cos_ref, sin_ref,
                  poolw_ref, pools_ref, wa2_ref, ba_ref, gng_ref, wout32_ref,
                  poolwin_ref, rothalf_ref,
                  rdecay_ref, rxi_ref, rzeta_ref, rgc_ref, rhmk_ref, rhmv_ref,
                  smask_ref, hind_ref,
                  gtri_ref, gmask_ref, ghmk_ref, ghmv_ref,
                  o_ref,
                  win_ref, wout_ref,
                  hb_ref, proj_ref, ycat_ref, b_ref, dec_ref, oscr_ref, ext_ref, s2_ref, s4_ref, s8_ref,
                  hist_ref, sret_ref, sgla_ref, *, tile, sub):
    t_idx = pl.program_id(1)
    n_sub = tile // sub

    @pl.when((pl.program_id(0) == 0) & (t_idx == 0))
    def _():
        _cast_weight_transposed(win32_ref, win_ref)
        _cast_weight(wout32_ref, wout_ref)

    @pl.when(t_idx == 0)
    def _():
        hist_ref[...] = jnp.zeros_like(hist_ref)
        sret_ref[...] = jnp.zeros_like(sret_ref)
        sgla_ref[...] = jnp.zeros_like(sgla_ref)

    gate1 = mod_ref[0, 2:3, :]

    norm_scale = ng_ref[...] * (1.0 + mod_ref[0, 1:2, :])
    shift1 = mod_ref[0, 0:1, :]

    def projection_tasks(i):
        slot = i % 2

        def norm():
            x = x_ref[0, i * sub:(i + 1) * sub, :]
            ms = jnp.mean(x * x, axis=-1, keepdims=True)
            hb_ref[slot] = ((x * lax.rsqrt(ms + EPS)) * norm_scale + shift1).astype(_BF)

        def segment(off, width):
            def run():
                proj_ref[slot, :, off:off + width] = _dot(hb_ref[slot], win_ref[:, off:off + width])
            return run

        tasks = [(i, 0, norm)]
        for off in range(0, N_IN, PROJ_PIECE):
            width = min(PROJ_PIECE, N_IN - off)
            tasks.append((i, off + width, segment(off, width)))
        return tasks

    def key_block(pv, lrows, off):
        shift = off % LANES
        if shift == 0:
            return pv[lrows, off:off + KP]
        return pltpu.roll(pv[lrows, off - shift:off - shift + KP], KP - shift, axis=1)

    pending = projection_tasks(0)
    ticks = [0]

    def tick(weight=1):
        ticks[0] += weight
        if pending and ticks[0] >= PROJ_TICKS:
            ticks[0] = 0
            pending.pop(0)[2]()

    def require(i, column):
        while pending and (pending[0][0], pending[0][1]) <= (i, column):
            pending.pop(0)[2]()

    require(0, N_IN)

    def pooling(i):
        n = sub + POOL_HIST
        srows = slice(i * sub, (i + 1) * sub)
        u = proj_ref[i % 2, :, OFF_POOL:OFF_POOL + POOL_WIDTH]
        ext_ref[0:POOL_HIST, :] = hist_ref[...]
        ext_ref[POOL_HIST:n, :] = u
        hist_ref[...] = u[sub - POOL_HIST:sub, :]
        s2_ref[8:n, :] = ext_ref[8:n, :] + ext_ref[7:n - 1, :]
        s4_ref[16:n, :] = s2_ref[16:n, :] + s2_ref[14:n - 2, :]
        s8_ref[24:n, :] = s4_ref[24:n, :] + s4_ref[20:n - 4, :]
        s16 = s8_ref[32:n, :] + s8_ref[24:n - 8, :]
        lane = lax.broadcasted_iota(jnp.int32, (sub, POOL_WIDTH), 1)
        wsum = jnp.where(lane < POOL_GROUP_DIM, s2_ref[32:n, :],
                         jnp.where(lane < 2 * POOL_GROUP_DIM, s4_ref[32:n, :],
                                   jnp.where(lane < 3 * POOL_GROUP_DIM, s8_ref[32:n, :], s16)))
        t_abs = (t_idx * tile + i * sub
                 + lax.broadcasted_iota(jnp.int32, (sub, POOL_WIDTH), 0)).astype(_F32)
        cnt = jnp.minimum(t_abs + 1.0, poolwin_ref[...])
        pooled = wsum / cnt - u
        ycat_ref[srows, 0:POOL_WIDTH] = (_dot(pooled.astype(_BF), poolw_ref[...])
                                         * pools_ref[...]).astype(_BF)

    inv_dv = 1.0 / DV
    k_scale = DK ** -0.5
    q_scale = DK ** -0.5
    smask = smask_ref[...]

    def own_head(state):
        return state.astype(_BF) * smask
    hind = hind_ref[...]
    first_half = rothalf_ref[...] > 0.5

    def head_rms(o):
        ms_h = _dot((o * o).astype(_BF), hind) * inv_dv
        return o * lax.rsqrt(ms_h + EPS)

    def stack_heads(a):
        return jnp.concatenate([a] * N_HEADS, axis=0)

    def rotary(t, cosv, sinv):
        partner = jnp.where(first_half, pltpu.roll(t, KP - DK // 2, axis=1), pltpu.roll(t, DK // 2, axis=1))
        return t * cosv + partner * sinv

    def retention(i, r0):
        C = RET_CHUNK
        pv = proj_ref.at[i % 2]
        parts = []
        for c in range(sub // C):
            rows = slice(r0 + c * C, r0 + (c + 1) * C)
            lrows = slice(c * C, (c + 1) * C)
            cosv = cos_ref[0, rows, :]
            sinv = sin_ref[0, rows, :]
            q = rotary(key_block(pv, lrows, OFF_RQ), cosv, sinv)
            k = rotary(key_block(pv, lrows, OFF_RK), cosv, sinv) * k_scale
            vb = pv[lrows, OFF_RV:OFF_RV + VW].astype(_BF)
            qb = q.astype(_BF)
            kbd = stack_heads(k.astype(_BF)) * rhmk_ref[...]
            scores = _dot_nt(qb, kbd) * rdecay_ref[...]
            vbd = stack_heads(vb) * rhmv_ref[...]
            o_intra = _dot(scores.astype(_BF), vbd)
            upd = _dot_tn(vb, (k * rzeta_ref[...]).astype(_BF))
            parts.append((rows, qb, o_intra, upd))
            tick()
            yield
        state = states["ret"]
        for rows, qb, o_intra, upd in parts:
            oscr_ref[rows, 0:VW] = o_intra + _dot_nt(qb, own_head(state)) * rxi_ref[...]
            state = state * rgc_ref[...] + upd
            tick()
            yield
        states["ret"] = state

    def gla_decay(i):
        C = GLA_CHUNK
        n_chunks = sub // C
        W = n_chunks * KP
        pv = proj_ref.at[i % 2]
        logits = _dot(pv[:, OFF_GA:OFF_GA + LANES].astype(_BF), wa2_ref[...]) + ba_ref[...]
        log_sig = jnp.minimum(logits, 0.0) - jnp.log(1.0 + jnp.exp(-jnp.abs(logits)))
        la = log_sig * (LOG2_E / GLA_GATE_TAU)

        la_l = jnp.concatenate([la[c * C:(c + 1) * C, :] for c in range(n_chunks)], axis=1)
        tri = gtri_ref[...]
        p1, p2, p3 = _split3(la_l)
        b = _dot(tri, p1) + _dot(tri, p2) + _dot(tri, p3)
        b_ref[i % 2] = b
        yield

        row = lax.broadcasted_iota(jnp.int32, (C, W), 0)
        s = C // 2
        lvl = 0
        while s >= SUBLANES // 2:
            refs = [jnp.broadcast_to(b_ref[i % 2, p + s - 1:p + s, :], (2 * s, W)) for p in range(0, C, 2 * s)]
            b_at_ref = refs[0] if len(refs) == 1 else jnp.concatenate(refs, axis=0)
            dec_ref[i % 2, lvl * C:(lvl + 1) * C, :] = jnp.exp2(-jnp.abs(b - b_at_ref))
            s //= 2
            lvl += 1
            tick()
            yield
        up1 = pltpu.roll(b, 1, axis=0)
        up2 = pltpu.roll(b, 2, axis=0)
        dn1 = pltpu.roll(b, C - 1, axis=0)
        m4 = row % 4
        b_at_ref = jnp.where(m4 == 0, dn1, jnp.where(m4 == 1, b, jnp.where(m4 == 2, up1, up2)))
        dec_ref[i % 2, lvl * C:(lvl + 1) * C, :] = jnp.exp2(-jnp.abs(b - b_at_ref))
        lvl += 1
        tick()
        yield
        b_at_ref = jnp.where(row % 2 == 0, b, up1)
        dec_ref[i % 2, lvl * C:(lvl + 1) * C, :] = jnp.exp2(-jnp.abs(b - b_at_ref))
        lvl += 1
        tick()
        yield
        dec_ref[i % 2, lvl * C:(lvl + 1) * C, :] = jnp.exp2(b)
        tick()
        b_last = jnp.broadcast_to(b_ref[i % 2, C - 1:C, :], (C, W))
        dec_ref[i % 2, (lvl + 1) * C:(lvl + 2) * C, :] = jnp.exp2(b_last - b)
        tick()

    def gla_chunks(i, r0, state):
        C = GLA_CHUNK
        n_chunks = sub // C
        pv = proj_ref.at[i % 2]
        hmk = ghmk_ref[...]
        parts = []
        for c in range(n_chunks):
            rows = slice(r0 + c * C, r0 + (c + 1) * C)
            lrows = slice(c * C, (c + 1) * C)
            lanes = slice(c * KP, (c + 1) * KP)
            q = key_block(pv, lrows, OFF_GQ) * q_scale
            k = key_block(pv, lrows, OFF_GK)
            vb = pv[lrows, OFF_GV:OFF_GV + VW].astype(_BF)
            qb = q.astype(_BF)
            kbd0 = stack_heads(k.astype(_BF)) * hmk
            scores = jnp.zeros((C, N_HEADS * C), _F32)
            for lvl in range(N_GLA_LEVELS + 1):
                if lvl < N_GLA_LEVELS:
                    eb = dec_ref[i % 2, lvl * C:(lvl + 1) * C, lanes].astype(_BF)
                    ql, kbd = qb * eb, kbd0 * stack_heads(eb)
                else:
                    ql, kbd = qb, kbd0
                scores = scores + _dot_nt(ql, kbd) * gmask_ref[lvl]
                tick()
            vbd = stack_heads(vb) * ghmv_ref[...]
            e_cum = dec_ref[i % 2, N_GLA_LEVELS * C:(N_GLA_LEVELS + 1) * C, lanes]
            e_rev = dec_ref[i % 2, (N_GLA_LEVELS + 1) * C:(N_GLA_LEVELS + 2) * C, lanes]
            o_intra = _dot(scores.astype(_BF), vbd)
            upd = _dot_tn(vb, (k * e_rev).astype(_BF))
            parts.append((rows, (q * e_cum).astype(_BF), o_intra, upd, e_cum[C - 1:C, :]))
        for rows, qe, o_intra, upd, e_last in parts:
            oscr_ref[rows, VW:2 * VW] = o_intra + _dot_nt(qe, own_head(state))
            state = state * e_last + upd
            tick()
        return state

    states = {"ret": sret_ref[...]}
    gla_state = sgla_ref[...]
    for i in range(n_sub):
        r0 = i * sub
        srows = slice(r0, r0 + sub)
        if i + 1 < n_sub:
            pending.extend(projection_tasks(i + 1))
        tick()
        pooling(i)
        tick()
        stages = [retention(i, r0), gla_decay(i)]
        while stages:
            for stage in list(stages):
                if next(stage, stages) is stages:
                    stages.remove(stage)
        gla_state = gla_chunks(i, r0, gla_state)
        y_ret = _silu(proj_ref[i % 2, :, OFF_RG:OFF_RG + VW]) * head_rms(oscr_ref[srows, 0:VW])
        ycat_ref[srows, POOL_WIDTH:POOL_WIDTH + VW] = y_ret.astype(_BF)
        gg_off = OFF_GG % LANES
        gg = pltpu.roll(proj_ref[i % 2, :, OFF_GG - gg_off:N_IN], N_IN - OFF_GG, axis=1)[:, 0:VW]
        y_gla = (_silu(gg)
                 * (head_rms(oscr_ref[srows, VW:2 * VW]) * gng_ref[...]))
        require(i + 1, N_IN)
        ycat_ref[srows, POOL_WIDTH + VW:POOL_WIDTH + 2 * VW] = y_gla.astype(_BF)
        o_ref[0, srows, :] = x_ref[0, srows, :] + gate1 * _dot(ycat_ref[srows, :], wout_ref[...])
    sret_ref[...] = states["ret"]
    sgla_ref[...] = gla_state


def _whole(shape):
    nd = len(shape)
    return pl.BlockSpec(shape, lambda b, t: (0,) * nd, pipeline_mode=pl.Buffered(1))


def _layer_slab(shape, layer):
    nd = len(shape) - 1
    return pl.BlockSpec((None,) + tuple(shape[1:]), lambda b, t: (layer,) + (0,) * nd,
                        pipeline_mode=pl.Buffered(1))


def _mixer(x, mod, layer, params, cos, sin, tb):
    B, T, D = x.shape
    tile = MIX_TILE if T % MIX_TILE == 0 else T
    sub = MIX_SUB if tile % MIX_SUB == 0 else tile
    consts = [tb[k] for k in ("pool_win", "rot_first_half", "ret_decay", "ret_xi", "ret_zeta", "ret_gc",
                              "ret_hmk", "ret_hmv", "state_mask", "head_ind", "gla_tri", "gla_mask",
                              "gla_hmk", "gla_hmv")]
    norm_g, win = params[0], params[1]
    rest = list(params[2:])
    n_sub = tile // sub
    args = [x, mod, norm_g, win, cos, sin] + rest + consts
    in_specs = [
        pl.BlockSpec((1, tile, D), lambda b, t: (b, t, 0)),
        pl.BlockSpec((None, 1, 6, D), lambda b, t: (layer, b, 0, 0)),
        _layer_slab(norm_g.shape, layer),
        _layer_slab(win.shape, layer),
        pl.BlockSpec((1, tile, KP), lambda b, t: (b, t, 0)),
        pl.BlockSpec((1, tile, KP), lambda b, t: (b, t, 0)),
    ] + [_layer_slab(a.shape, layer) for a in rest] + [_whole(a.shape) for a in consts]
    n = sub + POOL_HIST
    gla_lanes = (sub // GLA_CHUNK) * KP
    return pl.pallas_call(
        functools.partial(_mixer_kernel, tile=tile, sub=sub),
        grid=(B, T // tile),
        in_specs=in_specs,
        out_specs=pl.BlockSpec((1, tile, D), lambda b, t: (b, t, 0)),
        out_shape=jax.ShapeDtypeStruct((B, T, D), _F32),
        scratch_shapes=[
            pltpu.VMEM((D, N_IN), _BF),
            pltpu.VMEM((POOL_WIDTH + 2 * VW, D), _BF),
            pltpu.VMEM((2, sub, D), _BF),
            pltpu.VMEM((2, sub, N_IN), _F32),
            pltpu.VMEM((tile, POOL_WIDTH + 2 * VW), _BF),
            pltpu.VMEM((2, GLA_CHUNK, gla_lanes), _F32),
            pltpu.VMEM((2, (N_GLA_LEVELS + 2) * GLA_CHUNK, gla_lanes), _F32),
            pltpu.VMEM((tile, 2 * VW), _F32),
            pltpu.VMEM((n, POOL_WIDTH), _F32),
            pltpu.VMEM((n, POOL_WIDTH), _F32),
            pltpu.VMEM((n, POOL_WIDTH), _F32),
            pltpu.VMEM((n, POOL_WIDTH), _F32),
            pltpu.VMEM((POOL_HIST, POOL_WIDTH), _F32),
            pltpu.VMEM((VW, KP), _F32),
            pltpu.VMEM((VW, KP), _F32),
        ],
        compiler_params=pltpu.CompilerParams(
            dimension_semantics=("arbitrary", "arbitrary"), vmem_limit_bytes=VMEM_LIMIT_BYTES),
        name="token_mixer",
    )(*args)


def _ffn_kernel(x_ref, mod_ref, ng_ref, wup32_ref, cw_ref, wd32_ref, fg_ref,
                o_ref, wup_ref, wd_ref, h_ref, carry_ref, act_ref,
                *, tile, sub, n_tiles, n_cast, d_ff, fb, final_norm):
    g = pl.program_id(0)
    rows_per_step = wup32_ref.shape[0]

    @pl.when(g < n_cast)
    def _():
        r0 = pl.multiple_of(g * rows_per_step, rows_per_step)
        wup_ref[pl.ds(r0, rows_per_step), :] = wup32_ref[...].astype(_BF)

    @pl.when(g == 0)
    def _():
        _cast_weight(wd32_ref, wd_ref)

    @pl.when(g >= n_cast)
    def _():
        @pl.when((g - n_cast) % n_tiles == 0)
        def _():
            carry_ref[...] = jnp.zeros_like(carry_ref)

        for base in range(0, tile, sub):
            _ffn_sub_tile(x_ref, mod_ref, ng_ref, wup_ref, cw_ref, wd_ref, fg_ref, o_ref, h_ref,
                          carry_ref, act_ref, base=base, tile=sub, d_ff=d_ff, fb=fb,
                          final_norm=final_norm)


def _ffn_sub_tile(x_ref, mod_ref, ng_ref, wup_ref, cw_ref, wd_ref, fg_ref, o_ref, h_ref, carry_ref,
                  act_ref, *, base, tile, d_ff, fb, final_norm):
    S = SUBLANES
    R = tile // S
    D = x_ref.shape[-1]
    shift2 = mod_ref[0, 3:4, :]
    scale2 = mod_ref[0, 4:5, :]
    gate2 = mod_ref[0, 5:6, :]
    norm_scale = ng_ref[...] * (1.0 + scale2)
    groups = FFN_NORM_ROWS // S
    early_cols = ([j * fb for j in range(FFN_EARLY_BLOCKS)]
                  + [d_ff + j * fb for j in range(FFN_EARLY_BLOCKS)])
    early = {col: [] for col in early_cols}
    x_parts = []
    for p in range(tile // FFN_NORM_ROWS):
        prow = slice(p * FFN_NORM_ROWS, (p + 1) * FFN_NORM_ROWS)
        xs = jnp.concatenate([x_ref[0, base + s * R + p * groups:base + s * R + (p + 1) * groups, :]
                              for s in range(S)], axis=0)
        xs = xs.reshape(S, groups, D).swapaxes(0, 1).reshape(FFN_NORM_ROWS, D)
        ms = jnp.mean(xs * xs, axis=-1, keepdims=True)
        h_ref[prow, :] = ((xs * lax.rsqrt(ms + EPS)) * norm_scale + shift2).astype(_BF)
        x_parts.append(xs)
        for col in early_cols:
            early[col].append(_dot(h_ref[prow, :], wup_ref[:, col:col + fb]))
    x = jnp.concatenate(x_parts, axis=0)
    first_sublane = lax.broadcasted_iota(jnp.int32, (S, fb), 0) == 0

    def up_conv(col):
        cols = slice(col, col + fb)
        if col in early:
            u = jnp.concatenate(early[col], axis=0)
        else:
            u = _dot(h_ref[...], wup_ref[:, cols])
        back1 = jnp.where(first_sublane, pltpu.roll(carry_ref[S:2 * S, cols], 1, axis=0),
                          pltpu.roll(u[tile - S:tile, :], 1, axis=0))
        back2 = jnp.where(first_sublane, pltpu.roll(carry_ref[0:S, cols], 1, axis=0),
                          pltpu.roll(u[tile - 2 * S:tile - S, :], 1, axis=0))
        carry_ref[:, cols] = u[tile - 2 * S:tile, :]
        prev1 = jnp.concatenate([back1, u[0:tile - S, :]], axis=0)
        prev2 = jnp.concatenate([back2, back1, u[0:tile - 2 * S, :]], axis=0)
        cw = cw_ref[:, cols]
        return prev2 * cw[0:1, :] + prev1 * cw[1:2, :] + u * cw[2:3, :] + cw[3:4, :]

    for j in range(d_ff // fb):
        ya = up_conv(j * fb)
        yg = up_conv(d_ff + j * fb)
        act_ref[:, j * fb:(j + 1) * fb] = (_silu(yg) * ya).astype(_BF)

    out = x + gate2 * _dot(act_ref[...], wd_ref[...])
    if final_norm:
        ms_o = jnp.mean(out * out, axis=-1, keepdims=True)
        out = out * lax.rsqrt(ms_o + EPS) * fg_ref[...]
    o_ref[0, base:base + tile, :] = out.reshape(R, S, D).swapaxes(0, 1).reshape(tile, D)


def _ffn(x, mod, layer, params, final_g, final_norm):
    B, T, D = x.shape
    tile = FFN_TILE if T % FFN_TILE == 0 else T
    sub = FFN_SUB if tile % FFN_SUB == 0 else tile
    norm_g, wup, cw, wd = params
    d_ff = wd.shape[1]
    n_tiles = T // tile
    n_cast = FFN_CAST_STEPS
    cast_rows = D // n_cast

    def token_tile(g):
        k = jnp.maximum(g - n_cast, 0)
        return k // n_tiles, k % n_tiles

    def slab(a):
        nd = a.ndim - 1
        return pl.BlockSpec((None,) + tuple(a.shape[1:]), lambda g: (layer,) + (0,) * nd,
                            pipeline_mode=pl.Buffered(1))

    in_specs = [
        pl.BlockSpec((1, tile, D), lambda g: token_tile(g) + (0,)),
        pl.BlockSpec((None, 1, 6, D), lambda g: (layer, token_tile(g)[0], 0, 0)),
        slab(norm_g),
        pl.BlockSpec((None, cast_rows, 2 * d_ff), lambda g: (layer, jnp.minimum(g, n_cast - 1), 0)),
        slab(cw),
        slab(wd),
        pl.BlockSpec(final_g.shape, lambda g: (0, 0), pipeline_mode=pl.Buffered(1)),
    ]
    return pl.pallas_call(
        functools.partial(_ffn_kernel, tile=tile, sub=sub, n_tiles=n_tiles, n_cast=n_cast, d_ff=d_ff,
                          fb=FFN_BLOCK, final_norm=final_norm),
        grid=(B * n_tiles + n_cast,),
        in_specs=in_specs,
        out_specs=pl.BlockSpec((1, tile, D), lambda g: token_tile(g) + (0,)),
        out_shape=jax.ShapeDtypeStruct((B, T, D), _F32),
        scratch_shapes=[
            pltpu.VMEM((D, 2 * d_ff), _BF),
            pltpu.VMEM((d_ff, D), _BF),
            pltpu.VMEM((sub, D), _BF),
            pltpu.VMEM((2 * SUBLANES, 2 * d_ff), _F32),
            pltpu.VMEM((sub, d_ff), _BF),
        ],
        compiler_params=pltpu.CompilerParams(
            dimension_semantics=("arbitrary",), vmem_limit_bytes=VMEM_LIMIT_BYTES),
        name="conv_ffn",
    )(x, mod, *params, final_g)


def _prep_mixer_params(norm1_g, w_in, pool_w, pool_scale, gla_wa2, gla_ba, gla_norm_g, w_out):
    L, D, _ = w_in.shape
    G = pool_w.shape[1]
    eye = jnp.eye(G, dtype=pool_w.dtype)
    poolw = (pool_w[:, :, :, None, :] * eye[None, :, None, :, None]).reshape(L, POOL_WIDTH, POOL_WIDTH)
    wa2 = jnp.pad(gla_wa2, ((0, 0), (0, LANES - GLA_GATE_RANK), (0, KP - KW)))
    ba = jnp.pad(gla_ba, ((0, 0), (0, KP - KW))).reshape(L, 1, KP)
    w_in = jnp.swapaxes(w_in, 1, 2)
    return (norm1_g.reshape(L, 1, D), w_in, poolw.astype(_BF), pool_scale.reshape(L, 1, POOL_WIDTH),
            wa2.astype(_BF), ba, gla_norm_g.reshape(L, 1, VW), w_out)


def _prep_ffn_params(norm2_g, w_up, conv_w, conv_b, w_down):
    L, D, _ = w_up.shape
    cw = jnp.pad(jnp.concatenate([conv_w, conv_b[:, None, :]], axis=1),
                 ((0, 0), (0, SUBLANES - CONV_WIDTH - 1), (0, 0)))
    return norm2_g.reshape(L, 1, D), w_up, cw, w_down


def kernel(x, c, positions, ada_w, ada_b, norm1_g, w_in, pool_w, pool_scale, gla_wa2, gla_ba,
           gla_norm_g, w_out, norm2_g, w_up, conv_w, conv_b, w_down, final_g):
    L = ada_w.shape[0]
    B, T, D = x.shape
    tb = {k: jnp.asarray(v) for k, v in _tables().items()}
    for name in ("head_ind", "gla_tri", "ret_hmk", "ret_hmv", "gla_hmk", "gla_hmv", "state_mask",
                 "rot_cos_expand", "rot_sin_expand"):
        tb[name] = tb[name].astype(_BF)
    mod = _modulation(c, ada_w, ada_b).reshape(L, B, 6, D)
    cos, sin = _rotary_tables(positions, tb)
    mixer_params = _prep_mixer_params(norm1_g, w_in, pool_w, pool_scale, gla_wa2, gla_ba, gla_norm_g, w_out)
    ffn_params = _prep_ffn_params(norm2_g, w_up, conv_w, conv_b, w_down)
    fg = final_g.reshape(1, D)
    for l in range(L):
        x = _mixer(x, mod, l, mixer_params, cos, sin, tb)
        x = _ffn(x, mod, l, ffn_params, fg, final_norm=(l == L - 1))
    return x
```

```python
import functools

import numpy as np
import jax
import jax.numpy as jnp
from jax import lax
from jax.experimental import pallas as pl
from jax.experimental.pallas import tpu as pltpu

POOL_WIDTH = 256
POOL_WINDOWS = (2, 4, 8, 16)
POOL_GROUP_DIM = 64
N_HEADS = 4
DK = 48
DV = 96
KW = N_HEADS * DK
VW = N_HEADS * DV
RET_CHUNK = 128
GLA_CHUNK = 64
GLA_GATE_RANK = 16
GLA_GATE_TAU = 16.0
ROPE_BASE = 10000.0
CONV_WIDTH = 3
EPS = 1e-6

LANES = 128
SUBLANES = 8
KP = 256
VMEM_LIMIT_BYTES = 56 * 1024 * 1024

OFF_POOL = 0
OFF_RQ = OFF_POOL + POOL_WIDTH
OFF_RK = OFF_RQ + KW
OFF_RV = OFF_RK + KW
OFF_RG = OFF_RV + VW
OFF_GQ = OFF_RG + VW
OFF_GK = OFF_GQ + KW
OFF_GV = OFF_GK + KW
OFF_GA = OFF_GV + VW
OFF_GG = OFF_GA + GLA_GATE_RANK
D_IN = OFF_GG + VW
N_IN = -(-D_IN // LANES) * LANES

MIX_TILE = 512
MIX_SUB = 256
PROJ_PIECE = 256
PROJ_TICKS = 3
FFN_TILE = 512
FFN_SUB = 512
FFN_CAST_STEPS = 8
FFN_BLOCK = 256
FFN_NORM_ROWS = 128
FFN_EARLY_BLOCKS = 2
ROT_TILE = 1024
ROT_ROWS = 32
POOL_HIST = 32
N_GLA_LEVELS = 6
CAST_ROWS = 128
LOG2_E = 1.4426950408889634

_BF = jnp.bfloat16
_F32 = jnp.float32


def _dot(a, b):
    return jnp.dot(a, b, preferred_element_type=_F32)


def _dot_nt(a, b):
    return lax.dot_general(a, b, (((1,), (1,)), ((), ())), preferred_element_type=_F32)


def _dot_tn(a, b):
    return lax.dot_general(a, b, (((0,), (0,)), ((), ())), preferred_element_type=_F32)


def _split3(x):
    p1 = x.astype(_BF)
    r1 = x - p1.astype(_F32)
    p2 = r1.astype(_BF)
    p3 = (r1 - p2.astype(_F32)).astype(_BF)
    return p1, p2, p3


def _cast_weight(src_ref, dst_ref):
    for r in range(0, src_ref.shape[0], CAST_ROWS):
        rows = slice(r, r + CAST_ROWS)
        dst_ref[rows, :] = src_ref[rows, :].astype(_BF)


def _cast_weight_transposed(src_ref, dst_ref):
    n, k = src_ref.shape
    for c in range(0, dst_ref.shape[1], CAST_ROWS):
        rows = min(CAST_ROWS, max(n - c, 0))
        block = src_ref[c:c + rows, :] if rows else None
        if rows < CAST_ROWS:
            zeros = jnp.zeros((CAST_ROWS - rows, k), _F32)
            block = zeros if block is None else jnp.concatenate([block, zeros], axis=0)
        dst_ref[:, c:c + CAST_ROWS] = block.T.astype(_BF)


def _sigmoid(x):
    return 1.0 / (1.0 + jnp.exp(-x))


def _silu(x):
    return x * _sigmoid(x)


def _key_head(d):
    return np.where(d < KW, d // DK, -1)


@functools.lru_cache(maxsize=None)
def _tables():
    t = {}
    kd = _key_head(np.arange(KP))
    vd = np.arange(VW) // DV

    def head_masks(chunk):
        rows = np.repeat(np.arange(N_HEADS), chunk)
        return ((rows[:, None] == kd[None, :]).astype(np.float32),
                (rows[:, None] == vd[None, :]).astype(np.float32))

    C = RET_CHUNK
    lg = np.log(1.0 - 2.0 ** (-5.0 - np.arange(N_HEADS, dtype=np.float64)))
    i = np.arange(C)
    rel = i[:, None] - i[None, :]
    dec = np.where(rel[None] >= 0, np.exp(np.maximum(rel, 0)[None] * lg[:, None, None]), 0.0)
    t["ret_decay"] = np.transpose(dec, (1, 0, 2)).reshape(C, N_HEADS * C).astype(np.float32)
    t["ret_xi"] = np.exp((i[:, None] + 1.0) * lg[vd][None, :]).astype(np.float32)
    zeta = np.exp((C - 1.0 - i)[:, None] * lg[np.maximum(kd, 0)][None, :]) * (kd >= 0)[None, :]
    t["ret_zeta"] = zeta.astype(np.float32)
    t["ret_gc"] = (np.exp(C * lg[np.maximum(kd, 0)]) * (kd >= 0))[None, :].astype(np.float32)
    t["ret_hmk"], t["ret_hmv"] = head_masks(C)
    t["state_mask"] = (vd[:, None] == kd[None, :]).astype(np.float32)
    t["head_ind"] = (vd[:, None] == vd[None, :]).astype(np.float32)

    C = GLA_CHUNK
    i = np.arange(C)
    masks = []
    s = C // 2
    while s >= 1:
        parent = (i // (2 * s)) * (2 * s)
        upper = (i - parent) >= s
        masks.append(upper[:, None] & (~upper)[None, :] & (parent[:, None] == parent[None, :]))
        s //= 2
    masks.append(i[:, None] == i[None, :])
    t["gla_mask"] = np.stack([np.tile(m, (1, N_HEADS)) for m in masks]).astype(np.float32)
    t["gla_tri"] = (i[None, :] <= i[:, None]).astype(np.float32)
    t["gla_hmk"], t["gla_hmv"] = head_masks(C)

    t["pool_win"] = np.repeat(np.asarray(POOL_WINDOWS, np.float32), POOL_GROUP_DIM)[None, :]

    half = DK // 2
    l = np.arange(KP)
    hit = (np.arange(ROT_ROWS)[:, None] == (l % half)[None, :]) & (l < KW)[None, :]
    t["rot_cos_expand"] = hit.astype(np.float32)
    t["rot_sin_expand"] = hit * np.where((l % DK) < half, -1.0, 1.0)[None, :].astype(np.float32)
    t["rot_first_half"] = ((l % DK) < half).astype(np.float32)[None, :]
    return t


def _mod_kernel(c_ref, w_ref, b_ref, o_ref):
    ca = _silu(c_ref[...]).astype(_BF)
    o_ref[...] = _dot(ca, w_ref[...].astype(_BF)) + b_ref[...]


def _modulation(c, ada_w, ada_b):
    L, D, N = ada_w.shape
    B = c.shape[0]
    tn = D
    return pl.pallas_call(
        _mod_kernel,
        grid=(L, N // tn),
        in_specs=[
            pl.BlockSpec((B, D), lambda l, n: (0, 0)),
            pl.BlockSpec((None, D, tn), lambda l, n: (l, 0, n)),
            pl.BlockSpec((None, 1, tn), lambda l, n: (l, 0, n)),
        ],
        out_specs=pl.BlockSpec((None, B, tn), lambda l, n: (l, 0, n)),
        out_shape=jax.ShapeDtypeStruct((L, B, N), _F32),
        compiler_params=pltpu.CompilerParams(
            dimension_semantics=("arbitrary", "arbitrary"), vmem_limit_bytes=VMEM_LIMIT_BYTES),
        name="adaln_modulation",
    )(c, ada_w, ada_b.reshape(L, 1, N))


def _rot_kernel(pos_ref, freq_ref, ec_ref, es_ref, cos_ref, sin_ref):
    ang = freq_ref[...] * pos_ref[0]
    ec = ec_ref[...]
    es = es_ref[...]

    def widen(v, e):
        pieces = jnp.concatenate(_split3(v), axis=0)
        return _dot_tn(pieces, jnp.concatenate([e] * 3, axis=0))

    cos_ref[0] = widen(jnp.cos(ang), ec)
    sin_ref[0] = widen(jnp.sin(ang), es)


def _rotary_tables(positions, tb):
    B, T = positions.shape
    tile = ROT_TILE if T % ROT_TILE == 0 else T
    inv_freq = ROPE_BASE ** (-jnp.arange(0, DK, 2, dtype=_F32) / DK)
    freq = jnp.pad(inv_freq, (0, ROT_ROWS - DK // 2)).reshape(ROT_ROWS, 1)
    pos = positions.astype(_F32).reshape(B, 1, T)
    const = lambda b, t: (0, 0)
    return pl.pallas_call(
        _rot_kernel,
        grid=(B, T // tile),
        in_specs=[
            pl.BlockSpec((1, 1, tile), lambda b, t: (b, 0, t)),
            pl.BlockSpec((ROT_ROWS, 1), const),
            pl.BlockSpec((ROT_ROWS, KP), const),
            pl.BlockSpec((ROT_ROWS, KP), const),
        ],
        out_specs=[pl.BlockSpec((1, tile, KP), lambda b, t: (b, t, 0))] * 2,
        out_shape=[jax.ShapeDtypeStruct((B, T, KP), _F32)] * 2,
        compiler_params=pltpu.CompilerParams(dimension_semantics=("arbitrary", "arbitrary")),
        name="rotary_table",
    )(pos, freq, tb["rot_cos_expand"], tb["rot_sin_expand"])


def _mixer_kernel(x_ref, mod_ref, ng_ref, win32_ref, cos_ref, sin_ref,
                  poolw_ref, pools_ref, wa2_ref, ba_ref, gng_ref, wout32_ref,
                  poolwin_ref, rothalf_ref,
                  rdecay_ref, rxi_ref, rzeta_ref, rgc_ref, rhmk_ref, rhmv_ref,
                  smask_ref, hind_ref,
                  gtri_ref, gmask_ref, ghmk_ref, ghmv_ref,
                  o_ref,
                  win_ref, wout_ref,
                  hb_ref, proj_ref, ycat_ref, b_ref, dec_ref, oscr_ref, ext_ref, s2_ref, s4_ref, s8_ref,
                  hist_ref, sret_ref, sgla_ref, *, tile, sub):
    t_idx = pl.program_id(1)
    n_sub = tile // sub

    @pl.when((pl.program_id(0) == 0) & (t_idx == 0))
    def _():
        _cast_weight_transposed(win32_ref, win_ref)
        _cast_weight(wout32_ref, wout_ref)

    @pl.when(t_idx == 0)
    def _():
        hist_ref[...] = jnp.zeros_like(hist_ref)
        sret_ref[...] = jnp.zeros_like(sret_ref)
        sgla_ref[...] = jnp.zeros_like(sgla_ref)

    gate1 = mod_ref[0, 2:3, :]

    norm_scale = ng_ref[...] * (1.0 + mod_ref[0, 1:2, :])
    shift1 = mod_ref[0, 0:1, :]

    def projection_tasks(i):
        slot = i % 2

        def norm():
            x = x_ref[0, i * sub:(i + 1) * sub, :]
            ms = jnp.mean(x * x, axis=-1, keepdims=True)
            hb_ref[slot] = ((x * lax.rsqrt(ms + EPS)) * norm_scale + shift1).astype(_BF)

        def segment(off, width):
            def run():
                proj_ref[slot, :, off:off + width] = _dot(hb_ref[slot], win_ref[:, off:off + width])
            return run

        tasks = [(i, 0, norm)]
        for off in range(0, N_IN, PROJ_PIECE):
            width = min(PROJ_PIECE, N_IN - off)
            tasks.append((i, off + width, segment(off, width)))
        return tasks

    def key_block(pv, lrows, off):
        shift = off % LANES
        if shift == 0:
            return pv[lrows, off:off + KP]
        return pltpu.roll(pv[lrows, off - shift:off - shift + KP], KP - shift, axis=1)

    pending = projection_tasks(0)
    ticks = [0]

    def tick(weight=1):
        ticks[0] += weight
        if pending and ticks[0] >= PROJ_TICKS:
            ticks[0] = 0
            pending.pop(0)[2]()

    def require(i, column):
        while pending and (pending[0][0], pending[0][1]) <= (i, column):
            pending.pop(0)[2]()

    require(0, N_IN)

    def pooling(i):
        n = sub + POOL_HIST
        srows = slice(i * sub, (i + 1) * sub)
        u = proj_ref[i % 2, :, OFF_POOL:OFF_POOL + POOL_WIDTH]
        ext_ref[0:POOL_HIST, :] = hist_ref[...]
        ext_ref[POOL_HIST:n, :] = u
        hist_ref[...] = u[sub - POOL_HIST:sub, :]
        s2_ref[8:n, :] = ext_ref[8:n, :] + ext_ref[7:n - 1, :]
        s4_ref[16:n, :] = s2_ref[16:n, :] + s2_ref[14:n - 2, :]
        s8_ref[24:n, :] = s4_ref[24:n, :] + s4_ref[20:n - 4, :]
        s16 = s8_ref[32:n, :] + s8_ref[24:n - 8, :]
        lane = lax.broadcasted_iota(jnp.int32, (sub, POOL_WIDTH), 1)
        wsum = jnp.where(lane < POOL_GROUP_DIM, s2_ref[32:n, :],
                         jnp.where(lane < 2 * POOL_GROUP_DIM, s4_ref[32:n, :],
                                   jnp.where(lane < 3 * POOL_GROUP_DIM, s8_ref[32:n, :], s16)))
        t_abs = (t_idx * tile + i * sub
                 + lax.broadcasted_iota(jnp.int32, (sub, POOL_WIDTH), 0)).astype(_F32)
        cnt = jnp.minimum(t_abs + 1.0, poolwin_ref[...])
        pooled = wsum / cnt - u
        ycat_ref[srows, 0:POOL_WIDTH] = (_dot(pooled.astype(_BF), poolw_ref[...])
                                         * pools_ref[...]).astype(_BF)

    inv_dv = 1.0 / DV
    k_scale = DK ** -0.5
    q_scale = DK ** -0.5
    smask = smask_ref[...]

    state_blocks = []
    for h in range(N_HEADS):
        lane0 = (h * DK) // LANES * LANES
        lane1 = -(-((h + 1) * DK) // LANES) * LANES
        state_blocks.append((slice(h * DV, (h + 1) * DV), slice(lane0, lane1)))

    def load_state(ref):
        return [ref[rows, lanes] for rows, lanes in state_blocks]

    def store_state(ref, blocks):
        for (rows, lanes), blk in zip(state_blocks, blocks):
            ref[rows, lanes] = blk

    def advance_state(blocks, decay, upd):
        return [blk * decay[:, lanes] + upd[rows, lanes]
                for (rows, lanes), blk in zip(state_blocks, blocks)]

    def own_head(blocks):
        out = []
        for (rows, lanes), blk in zip(state_blocks, blocks):
            part = blk.astype(_BF) * smask[rows, lanes]
            left = lanes.start
            right = KP - lanes.stop
            pieces = ([jnp.zeros((DV, left), _BF)] if left else []) + [part] \
                + ([jnp.zeros((DV, right), _BF)] if right else [])
            out.append(pieces[0] if len(pieces) == 1 else jnp.concatenate(pieces, axis=1))
        return jnp.concatenate(out, axis=0)
    hind = hind_ref[...]
    first_half = rothalf_ref[...] > 0.5

    def head_rms(o):
        ms_h = _dot((o * o).astype(_BF), hind) * inv_dv
        return o * lax.rsqrt(ms_h + EPS)

    def stack_heads(a):
        return jnp.concatenate([a] * N_HEADS, axis=0)

    def rotary(t, cosv, sinv):
        partner = jnp.where(first_half, pltpu.roll(t, KP - DK // 2, axis=1), pltpu.roll(t, DK // 2, axis=1))
        return t * cosv + partner * sinv

    def retention(i, r0):
        C = RET_CHUNK
        pv = proj_ref.at[i % 2]
        parts = []
        for c in range(sub // C):
            rows = slice(r0 + c * C, r0 + (c + 1) * C)
            lrows = slice(c * C, (c + 1) * C)
            cosv = cos_ref[0, rows, :]
            sinv = sin_ref[0, rows, :]
            q = rotary(key_block(pv, lrows, OFF_RQ), cosv, sinv)
            k = rotary(key_block(pv, lrows, OFF_RK), cosv, sinv) * k_scale
            vb = pv[lrows, OFF_RV:OFF_RV + VW].astype(_BF)
            qb = q.astype(_BF)
            kbd = stack_heads(k.astype(_BF)) * rhmk_ref[...]
            scores = _dot_nt(qb, kbd) * rdecay_ref[...]
            vbd = stack_heads(vb) * rhmv_ref[...]
            o_intra = _dot(scores.astype(_BF), vbd)
            upd = _dot_tn(vb, (k * rzeta_ref[...]).astype(_BF))
            parts.append((rows, qb, o_intra, upd))
            tick()
            yield
        state = states["ret"]
        for rows, qb, o_intra, upd in parts:
            oscr_ref[rows, 0:VW] = o_intra + _dot_nt(qb, own_head(state)) * rxi_ref[...]
            state = advance_state(state, rgc_ref[...], upd)
            tick()
            yield
        states["ret"] = state

    def gla_decay(i):
        C = GLA_CHUNK
        n_chunks = sub // C
        W = n_chunks * KP
        pv = proj_ref.at[i % 2]
        logits = _dot(pv[:, OFF_GA:OFF_GA + LANES].astype(_BF), wa2_ref[...]) + ba_ref[...]
        log_sig = jnp.minimum(logits, 0.0) - jnp.log(1.0 + jnp.exp(-jnp.abs(logits)))
        la = log_sig * (LOG2_E / GLA_GATE_TAU)

        la_l = jnp.concatenate([la[c * C:(c + 1) * C, :] for c in range(n_chunks)], axis=1)
        tri = gtri_ref[...]
        p1, p2, p3 = _split3(la_l)
        b = _dot(tri, p1) + _dot(tri, p2) + _dot(tri, p3)
        b_ref[i % 2] = b
        yield

        row = lax.broadcasted_iota(jnp.int32, (C, W), 0)
        s = C // 2
        lvl = 0
        while s >= SUBLANES // 2:
            refs = [jnp.broadcast_to(b_ref[i % 2, p + s - 1:p + s, :], (2 * s, W)) for p in range(0, C, 2 * s)]
            b_at_ref = refs[0] if len(refs) == 1 else jnp.concatenate(refs, axis=0)
            dec_ref[i % 2, lvl * C:(lvl + 1) * C, :] = jnp.exp2(-jnp.abs(b - b_at_ref))
            s //= 2
            lvl += 1
            tick()
            yield
        up1 = pltpu.roll(b, 1, axis=0)
        up2 = pltpu.roll(b, 2, axis=0)
        dn1 = pltpu.roll(b, C - 1, axis=0)
        m4 = row % 4
        b_at_ref = jnp.where(m4 == 0, dn1, jnp.where(m4 == 1, b, jnp.where(m4 == 2, up1, up2)))
        dec_ref[i % 2, lvl * C:(lvl + 1) * C, :] = jnp.exp2(-jnp.abs(b - b_at_ref))
        lvl += 1
        tick()
        yield
        b_at_ref = jnp.where(row % 2 == 0, b, up1)
        dec_ref[i % 2, lvl * C:(lvl + 1) * C, :] = jnp.exp2(-jnp.abs(b - b_at_ref))
        lvl += 1
        tick()
        yield
        dec_ref[i % 2, lvl * C:(lvl + 1) * C, :] = jnp.exp2(b)
        tick()
        b_last = jnp.broadcast_to(b_ref[i % 2, C - 1:C, :], (C, W))
        dec_ref[i % 2, (lvl + 1) * C:(lvl + 2) * C, :] = jnp.exp2(b_last - b)
        tick()

    def gla_chunks(i, r0, state):
        C = GLA_CHUNK
        n_chunks = sub // C
        pv = proj_ref.at[i % 2]
        hmk = ghmk_ref[...]
        parts = []
        for c in range(n_chunks):
            rows = slice(r0 + c * C, r0 + (c + 1) * C)
            lrows = slice(c * C, (c + 1) * C)
            lanes = slice(c * KP, (c + 1) * KP)
            q = key_block(pv, lrows, OFF_GQ) * q_scale
            k = key_block(pv, lrows, OFF_GK)
            vb = pv[lrows, OFF_GV:OFF_GV + VW].astype(_BF)
            qb = q.astype(_BF)
            kbd0 = stack_heads(k.astype(_BF)) * hmk
            scores = jnp.zeros((C, N_HEADS * C), _F32)
            for lvl in range(N_GLA_LEVELS + 1):
                if lvl < N_GLA_LEVELS:
                    eb = dec_ref[i % 2, lvl * C:(lvl + 1) * C, lanes].astype(_BF)
                    ql, kbd = qb * eb, kbd0 * stack_heads(eb)
                else:
                    ql, kbd = qb, kbd0
                scores = scores + _dot_nt(ql, kbd) * gmask_ref[lvl]
                tick()
            vbd = stack_heads(vb) * ghmv_ref[...]
            e_cum = dec_ref[i % 2, N_GLA_LEVELS * C:(N_GLA_LEVELS + 1) * C, lanes]
            e_rev = dec_ref[i % 2, (N_GLA_LEVELS + 1) * C:(N_GLA_LEVELS + 2) * C, lanes]
            o_intra = _dot(scores.astype(_BF), vbd)
            upd = _dot_tn(vb, (k * e_rev).astype(_BF))
            parts.append((rows, (q * e_cum).astype(_BF), o_intra, upd, e_cum[C - 1:C, :]))
        for rows, qe, o_intra, upd, e_last in parts:
            oscr_ref[rows, VW:2 * VW] = o_intra + _dot_nt(qe, own_head(state))
            state = advance_state(state, e_last, upd)
            tick()
        return state

    states = {"ret": load_state(sret_ref)}
    gla_state = load_state(sgla_ref)
    for i in range(n_sub):
        r0 = i * sub
        srows = slice(r0, r0 + sub)
        if i + 1 < n_sub:
            pending.extend(projection_tasks(i + 1))
        tick()
        pooling(i)
        tick()
        stages = [retention(i, r0), gla_decay(i)]
        while stages:
            for stage in list(stages):
                if next(stage, stages) is stages:
                    stages.remove(stage)
        gla_state = gla_chunks(i, r0, gla_state)
        y_ret = _silu(proj_ref[i % 2, :, OFF_RG:OFF_RG + VW]) * head_rms(oscr_ref[srows, 0:VW])
        ycat_ref[srows, POOL_WIDTH:POOL_WIDTH + VW] = y_ret.astype(_BF)
        gg_off = OFF_GG % LANES
        gg = pltpu.roll(proj_ref[i % 2, :, OFF_GG - gg_off:N_IN], N_IN - OFF_GG, axis=1)[:, 0:VW]
        y_gla = (_silu(gg)
                 * (head_rms(oscr_ref[srows, VW:2 * VW]) * gng_ref[...]))
        require(i + 1, N_IN)
        ycat_ref[srows, POOL_WIDTH + VW:POOL_WIDTH + 2 * VW] = y_gla.astype(_BF)
        o_ref[0, srows, :] = x_ref[0, srows, :] + gate1 * _dot(ycat_ref[srows, :], wout_ref[...])
    store_state(sret_ref, states["ret"])
    store_state(sgla_ref, gla_state)


def _whole(shape):
    nd = len(shape)
    return pl.BlockSpec(shape, lambda b, t: (0,) * nd, pipeline_mode=pl.Buffered(1))


def _layer_slab(shape, layer):
    nd = len(shape) - 1
    return pl.BlockSpec((None,) + tuple(shape[1:]), lambda b, t: (layer,) + (0,) * nd,
                        pipeline_mode=pl.Buffered(1))


def _mixer(x, mod, layer, params, cos, sin, tb):
    B, T, D = x.shape
    tile = MIX_TILE if T % MIX_TILE == 0 else T
    sub = MIX_SUB if tile % MIX_SUB == 0 else tile
    consts = [tb[k] for k in ("pool_win", "rot_first_half", "ret_decay", "ret_xi", "ret_zeta", "ret_gc",
                              "ret_hmk", "ret_hmv", "state_mask", "head_ind", "gla_tri", "gla_mask",
                              "gla_hmk", "gla_hmv")]
    norm_g, win = params[0], params[1]
    rest = list(params[2:])
    n_sub = tile // sub
    args = [x, mod, norm_g, win, cos, sin] + rest + consts
    in_specs = [
        pl.BlockSpec((1, tile, D), lambda b, t: (b, t, 0)),
        pl.BlockSpec((None, 1, 6, D), lambda b, t: (layer, b, 0, 0)),
        _layer_slab(norm_g.shape, layer),
        _layer_slab(win.shape, layer),
        pl.BlockSpec((1, tile, KP), lambda b, t: (b, t, 0)),
        pl.BlockSpec((1, tile, KP), lambda b, t: (b, t, 0)),
    ] + [_layer_slab(a.shape, layer) for a in rest] + [_whole(a.shape) for a in consts]
    n = sub + POOL_HIST
    gla_lanes = (sub // GLA_CHUNK) * KP
    return pl.pallas_call(
        functools.partial(_mixer_kernel, tile=tile, sub=sub),
        grid=(B, T // tile),
        in_specs=in_specs,
        out_specs=pl.BlockSpec((1, tile, D), lambda b, t: (b, t, 0)),
        out_shape=jax.ShapeDtypeStruct((B, T, D), _F32),
        scratch_shapes=[
            pltpu.VMEM((D, N_IN), _BF),
            pltpu.VMEM((POOL_WIDTH + 2 * VW, D), _BF),
            pltpu.VMEM((2, sub, D), _BF),
            pltpu.VMEM((2, sub, N_IN), _F32),
            pltpu.VMEM((tile, POOL_WIDTH + 2 * VW), _BF),
            pltpu.VMEM((2, GLA_CHUNK, gla_lanes), _F32),
            pltpu.VMEM((2, (N_GLA_LEVELS + 2) * GLA_CHUNK, gla_lanes), _F32),
            pltpu.VMEM((tile, 2 * VW), _F32),
            pltpu.VMEM((n, POOL_WIDTH), _F32),
            pltpu.VMEM((n, POOL_WIDTH), _F32),
            pltpu.VMEM((n, POOL_WIDTH), _F32),
            pltpu.VMEM((n, POOL_WIDTH), _F32),
            pltpu.VMEM((POOL_HIST, POOL_WIDTH), _F32),
            pltpu.VMEM((VW, KP), _F32),
            pltpu.VMEM((VW, KP), _F32),
        ],
        compiler_params=pltpu.CompilerParams(
            dimension_semantics=("arbitrary", "arbitrary"), vmem_limit_bytes=VMEM_LIMIT_BYTES),
        name="token_mixer",
    )(*args)


def _ffn_kernel(x_ref, mod_ref, ng_ref, wup32_ref, cw_ref, wd32_ref, fg_ref,
                o_ref, wup_ref, wd_ref, h_ref, carry_ref, act_ref,
                *, tile, sub, n_tiles, n_cast, d_ff, fb, final_norm):
    g = pl.program_id(0)
    rows_per_step = wup32_ref.shape[0]

    @pl.when(g < n_cast)
    def _():
        r0 = pl.multiple_of(g * rows_per_step, rows_per_step)
        wup_ref[pl.ds(r0, rows_per_step), :] = wup32_ref[...].astype(_BF)

    @pl.when(g == 0)
    def _():
        _cast_weight(wd32_ref, wd_ref)

    @pl.when(g >= n_cast)
    def _():
        @pl.when((g - n_cast) % n_tiles == 0)
        def _():
            carry_ref[...] = jnp.zeros_like(carry_ref)

        for base in range(0, tile, sub):
            _ffn_sub_tile(x_ref, mod_ref, ng_ref, wup_ref, cw_ref, wd_ref, fg_ref, o_ref, h_ref,
                          carry_ref, act_ref, base=base, tile=sub, d_ff=d_ff, fb=fb,
                          final_norm=final_norm)


def _ffn_sub_tile(x_ref, mod_ref, ng_ref, wup_ref, cw_ref, wd_ref, fg_ref, o_ref, h_ref, carry_ref,
                  act_ref, *, base, tile, d_ff, fb, final_norm):
    S = SUBLANES
    R = tile // S
    D = x_ref.shape[-1]
    shift2 = mod_ref[0, 3:4, :]
    scale2 = mod_ref[0, 4:5, :]
    gate2 = mod_ref[0, 5:6, :]
    norm_scale = ng_ref[...] * (1.0 + scale2)
    groups = FFN_NORM_ROWS // S
    early_cols = ([j * fb for j in range(FFN_EARLY_BLOCKS)]
                  + [d_ff + j * fb for j in range(FFN_EARLY_BLOCKS)])
    early = {col: [] for col in early_cols}
    x_parts = []
    for p in range(tile // FFN_NORM_ROWS):
        prow = slice(p * FFN_NORM_ROWS, (p + 1) * FFN_NORM_ROWS)
        xs = jnp.concatenate([x_ref[0, base + s * R + p * groups:base + s * R + (p + 1) * groups, :]
                              for s in range(S)], axis=0)
        xs = xs.reshape(S, groups, D).swapaxes(0, 1).reshape(FFN_NORM_ROWS, D)
        ms = jnp.mean(xs * xs, axis=-1, keepdims=True)
        h_ref[prow, :] = ((xs * lax.rsqrt(ms + EPS)) * norm_scale + shift2).astype(_BF)
        x_parts.append(xs)
        for col in early_cols:
            early[col].append(_dot(h_ref[prow, :], wup_ref[:, col:col + fb]))
    x = jnp.concatenate(x_parts, axis=0)
    first_sublane = lax.broadcasted_iota(jnp.int32, (S, fb), 0) == 0

    def up_conv(col):
        cols = slice(col, col + fb)
        if col in early:
            u = jnp.concatenate(early[col], axis=0)
        else:
            u = _dot(h_ref[...], wup_ref[:, cols])
        back1 = jnp.where(first_sublane, pltpu.roll(carry_ref[S:2 * S, cols], 1, axis=0),
                          pltpu.roll(u[tile - S:tile, :], 1, axis=0))
        back2 = jnp.where(first_sublane, pltpu.roll(carry_ref[0:S, cols], 1, axis=0),
                          pltpu.roll(u[tile - 2 * S:tile - S, :], 1, axis=0))
        carry_ref[:, cols] = u[tile - 2 * S:tile, :]
        prev1 = jnp.concatenate([back1, u[0:tile - S, :]], axis=0)
        prev2 = jnp.concatenate([back2, back1, u[0:tile - 2 * S, :]], axis=0)
        cw = cw_ref[:, cols]
        return prev2 * cw[0:1, :] + prev1 * cw[1:2, :] + u * cw[2:3, :] + cw[3:4, :]

    for j in range(d_ff // fb):
        ya = up_conv(j * fb)
        yg = up_conv(d_ff + j * fb)
        act_ref[:, j * fb:(j + 1) * fb] = (_silu(yg) * ya).astype(_BF)

    out = x + gate2 * _dot(act_ref[...], wd_ref[...])
    if final_norm:
        ms_o = jnp.mean(out * out, axis=-1, keepdims=True)
        out = out * lax.rsqrt(ms_o + EPS) * fg_ref[...]
    o_ref[0, base:base + tile, :] = out.reshape(R, S, D).swapaxes(0, 1).reshape(tile, D)


def _ffn(x, mod, layer, params, final_g, final_norm):
    B, T, D = x.shape
    tile = FFN_TILE if T % FFN_TILE == 0 else T
    sub = FFN_SUB if tile % FFN_SUB == 0 else tile
    norm_g, wup, cw, wd = params
    d_ff = wd.shape[1]
    n_tiles = T // tile
    n_cast = FFN_CAST_STEPS
    cast_rows = D // n_cast

    def token_tile(g):
        k = jnp.maximum(g - n_cast, 0)
        return k // n_tiles, k % n_tiles

    def slab(a):
        nd = a.ndim - 1
        return pl.BlockSpec((None,) + tuple(a.shape[1:]), lambda g: (layer,) + (0,) * nd,
                            pipeline_mode=pl.Buffered(1))

    in_specs = [
        pl.BlockSpec((1, tile, D), lambda g: token_tile(g) + (0,)),
        pl.BlockSpec((None, 1, 6, D), lambda g: (layer, token_tile(g)[0], 0, 0)),
        slab(norm_g),
        pl.BlockSpec((None, cast_rows, 2 * d_ff), lambda g: (layer, jnp.minimum(g, n_cast - 1), 0)),
        slab(cw),
        slab(wd),
        pl.BlockSpec(final_g.shape, lambda g: (0, 0), pipeline_mode=pl.Buffered(1)),
    ]
    return pl.pallas_call(
        functools.partial(_ffn_kernel, tile=tile, sub=sub, n_tiles=n_tiles, n_cast=n_cast, d_ff=d_ff,
                          fb=FFN_BLOCK, final_norm=final_norm),
        grid=(B * n_tiles + n_cast,),
        in_specs=in_specs,
        out_specs=pl.BlockSpec((1, tile, D), lambda g: token_tile(g) + (0,)),
        out_shape=jax.ShapeDtypeStruct((B, T, D), _F32),
        scratch_shapes=[
            pltpu.VMEM((D, 2 * d_ff), _BF),
            pltpu.VMEM((d_ff, D), _BF),
            pltpu.VMEM((sub, D), _BF),
            pltpu.VMEM((2 * SUBLANES, 2 * d_ff), _F32),
            pltpu.VMEM((sub, d_ff), _BF),
        ],
        compiler_params=pltpu.CompilerParams(
            dimension_semantics=("arbitrary",), vmem_limit_bytes=VMEM_LIMIT_BYTES),
        name="conv_ffn",
    )(x, mod, *params, final_g)


def _prep_mixer_params(norm1_g, w_in, pool_w, pool_scale, gla_wa2, gla_ba, gla_norm_g, w_out):
    L, D, _ = w_in.shape
    G = pool_w.shape[1]
    eye = jnp.eye(G, dtype=pool_w.dtype)
    poolw = (pool_w[:, :, :, None, :] * eye[None, :, None, :, None]).reshape(L, POOL_WIDTH, POOL_WIDTH)
    wa2 = jnp.pad(gla_wa2, ((0, 0), (0, LANES - GLA_GATE_RANK), (0, KP - KW)))
    ba = jnp.pad(gla_ba, ((0, 0), (0, KP - KW))).reshape(L, 1, KP)
    w_in = jnp.swapaxes(w_in, 1, 2)
    return (norm1_g.reshape(L, 1, D), w_in, poolw.astype(_BF), pool_scale.reshape(L, 1, POOL_WIDTH),
            wa2.astype(_BF), ba, gla_norm_g.reshape(L, 1, VW), w_out)


def _prep_ffn_params(norm2_g, w_up, conv_w, conv_b, w_down):
    L, D, _ = w_up.shape
    cw = jnp.pad(jnp.concatenate([conv_w, conv_b[:, None, :]], axis=1),
                 ((0, 0), (0, SUBLANES - CONV_WIDTH - 1), (0, 0)))
    return norm2_g.reshape(L, 1, D), w_up, cw, w_down


def kernel(x, c, positions, ada_w, ada_b, norm1_g, w_in, pool_w, pool_scale, gla_wa2, gla_ba,
           gla_norm_g, w_out, norm2_g, w_up, conv_w, conv_b, w_down, final_g):
    L = ada_w.shape[0]
    B, T, D = x.shape
    tb = {k: jnp.asarray(v) for k, v in _tables().items()}
    for name in ("head_ind", "gla_tri", "ret_hmk", "ret_hmv", "gla_hmk", "gla_hmv", "state_mask",
                 "rot_cos_expand", "rot_sin_expand"):
        tb[name] = tb[name].astype(_BF)
    mod = _modulation(c, ada_w, ada_b).reshape(L, B, 6, D)
    cos, sin = _rotary_tables(positions, tb)
    mixer_params = _prep_mixer_params(norm1_g, w_in, pool_w, pool_scale, gla_wa2, gla_ba, gla_norm_g, w_out)
    ffn_params = _prep_ffn_params(norm2_g, w_up, conv_w, conv_b, w_down)
    fg = final_g.reshape(1, D)
    for l in range(L):
        x = _mixer(x, mod, l, mixer_params, cos, sin, tb)
        x = _ffn(x, mod, l, ffn_params, fg, final_norm=(l == L - 1))
    return x
```

```python
import functools

import numpy as np
import jax
import jax.numpy as jnp
from jax import lax
from jax.experimental import pallas as pl
from jax.experimental.pallas import tpu as pltpu

POOL_WIDTH = 256
POOL_WINDOWS = (2, 4, 8, 16)
POOL_GROUP_DIM = 64
N_HEADS = 4
DK = 48
DV = 96
KW = N_HEADS * DK
VW = N_HEADS * DV
RET_CHUNK = 128
GLA_CHUNK = 64
GLA_GATE_RANK = 16
GLA_GATE_TAU = 16.0
ROPE_BASE = 10000.0
CONV_WIDTH = 3
EPS = 1e-6

LANES = 128
SUBLANES = 8
KP = 256
VMEM_LIMIT_BYTES = 56 * 1024 * 1024

OFF_POOL = 0
OFF_RQ = OFF_POOL + POOL_WIDTH
OFF_RK = OFF_RQ + KW
OFF_RV = OFF_RK + KW
OFF_RG = OFF_RV + VW
OFF_GQ = OFF_RG + VW
OFF_GK = OFF_GQ + KW
OFF_GV = OFF_GK + KW
OFF_GA = OFF_GV + VW
OFF_GG = OFF_GA + GLA_GATE_RANK
D_IN = OFF_GG + VW
N_IN = -(-D_IN // LANES) * LANES

MIX_TILE = 512
MIX_SUB = 256
PROJ_PIECE = 256
PROJ_TICKS = 3
FFN_TILE = 512
FFN_SUB = 512
FFN_CAST_STEPS = 8
FFN_BLOCK = 256
FFN_NORM_ROWS = 128
FFN_EARLY_BLOCKS = 2
ROT_TILE = 1024
ROT_ROWS = 32
POOL_HIST = 32
N_GLA_LEVELS = 6
CAST_ROWS = 128
LOG2_E = 1.4426950408889634

_BF = jnp.bfloat16
_F32 = jnp.float32


def _dot(a, b):
    return jnp.dot(a, b, preferred_element_type=_F32)


def _dot_nt(a, b):
    return lax.dot_general(a, b, (((1,), (1,)), ((), ())), preferred_element_type=_F32)


def _dot_tn(a, b):
    return lax.dot_general(a, b, (((0,), (0,)), ((), ())), preferred_element_type=_F32)


def _split3(x):
    p1 = x.astype(_BF)
    r1 = x - p1.astype(_F32)
    p2 = r1.astype(_BF)
    p3 = (r1 - p2.astype(_F32)).astype(_BF)
    return p1, p2, p3


def _cast_weight(src_ref, dst_ref):
    for r in range(0, src_ref.shape[0], CAST_ROWS):
        rows = slice(r, r + CAST_ROWS)
        dst_ref[rows, :] = src_ref[rows, :].astype(_BF)


def _cast_weight_transposed(src_ref, dst_ref):
    n, k = src_ref.shape
    for c in range(0, dst_ref.shape[1], CAST_ROWS):
        rows = min(CAST_ROWS, max(n - c, 0))
        block = src_ref[c:c + rows, :] if rows else None
        if rows < CAST_ROWS:
            zeros = jnp.zeros((CAST_ROWS - rows, k), _F32)
            block = zeros if block is None else jnp.concatenate([block, zeros], axis=0)
        dst_ref[:, c:c + CAST_ROWS] = block.T.astype(_BF)


def _sigmoid(x):
    return 1.0 / (1.0 + jnp.exp(-x))


def _silu(x):
    return x * _sigmoid(x)


def _key_head(d):
    return np.where(d < KW, d // DK, -1)


@functools.lru_cache(maxsize=None)
def _tables():
    t = {}
    kd = _key_head(np.arange(KP))
    vd = np.arange(VW) // DV

    def head_masks(chunk):
        rows = np.repeat(np.arange(N_HEADS), chunk)
        return ((rows[:, None] == kd[None, :]).astype(np.float32),
                (rows[:, None] == vd[None, :]).astype(np.float32))

    C = RET_CHUNK
    lg = np.log(1.0 - 2.0 ** (-5.0 - np.arange(N_HEADS, dtype=np.float64)))
    i = np.arange(C)
    rel = i[:, None] - i[None, :]
    dec = np.where(rel[None] >= 0, np.exp(np.maximum(rel, 0)[None] * lg[:, None, None]), 0.0)
    t["ret_decay"] = np.transpose(dec, (1, 0, 2)).reshape(C, N_HEADS * C).astype(np.float32)
    t["ret_xi"] = np.exp((i[:, None] + 1.0) * lg[vd][None, :]).astype(np.float32)
    zeta = np.exp((C - 1.0 - i)[:, None] * lg[np.maximum(kd, 0)][None, :]) * (kd >= 0)[None, :]
    t["ret_zeta"] = zeta.astype(np.float32)
    t["ret_gc"] = (np.exp(C * lg[np.maximum(kd, 0)]) * (kd >= 0))[None, :].astype(np.float32)
    t["ret_hmk"], t["ret_hmv"] = head_masks(C)
    t["state_mask"] = (vd[:, None] == kd[None, :]).astype(np.float32)
    t["head_ind"] = (vd[:, None] == vd[None, :]).astype(np.float32)

    C = GLA_CHUNK
    i = np.arange(C)
    masks = []
    s = C // 2
    while s >= 1:
        parent = (i // (2 * s)) * (2 * s)
        upper = (i - parent) >= s
        masks.append(upper[:, None] & (~upper)[None, :] & (parent[:, None] == parent[None, :]))
        s //= 2
    masks.append(i[:, None] == i[None, :])
    t["gla_mask"] = np.stack([np.tile(m, (1, N_HEADS)) for m in masks]).astype(np.float32)
    t["gla_tri"] = np.tile((i[None, :] <= i[:, None]).astype(np.float32), (1, 3))
    t["gla_hmk"], t["gla_hmv"] = head_masks(C)

    t["pool_win"] = np.repeat(np.asarray(POOL_WINDOWS, np.float32), POOL_GROUP_DIM)[None, :]

    half = DK // 2
    l = np.arange(KP)
    hit = (np.arange(ROT_ROWS)[:, None] == (l % half)[None, :]) & (l < KW)[None, :]
    t["rot_cos_expand"] = hit.astype(np.float32)
    t["rot_sin_expand"] = hit * np.where((l % DK) < half, -1.0, 1.0)[None, :].astype(np.float32)
    t["rot_first_half"] = ((l % DK) < half).astype(np.float32)[None, :]
    return t


def _mod_kernel(c_ref, w_ref, b_ref, o_ref):
    ca = _silu(c_ref[...]).astype(_BF)
    o_ref[...] = _dot(ca, w_ref[...].astype(_BF)) + b_ref[...]


def _modulation(c, ada_w, ada_b):
    L, D, N = ada_w.shape
    B = c.shape[0]
    tn = D
    return pl.pallas_call(
        _mod_kernel,
        grid=(L, N // tn),
        in_specs=[
            pl.BlockSpec((B, D), lambda l, n: (0, 0)),
            pl.BlockSpec((None, D, tn), lambda l, n: (l, 0, n)),
            pl.BlockSpec((None, 1, tn), lambda l, n: (l, 0, n)),
        ],
        out_specs=pl.BlockSpec((None, B, tn), lambda l, n: (l, 0, n)),
        out_shape=jax.ShapeDtypeStruct((L, B, N), _F32),
        compiler_params=pltpu.CompilerParams(
            dimension_semantics=("arbitrary", "arbitrary"), vmem_limit_bytes=VMEM_LIMIT_BYTES),
        name="adaln_modulation",
    )(c, ada_w, ada_b.reshape(L, 1, N))


def _rot_kernel(pos_ref, freq_ref, ec_ref, es_ref, cos_ref, sin_ref):
    ang = freq_ref[...] * pos_ref[0]
    ec = ec_ref[...]
    es = es_ref[...]

    def widen(v, e):
        pieces = jnp.concatenate(_split3(v), axis=0)
        return _dot_tn(pieces, jnp.concatenate([e] * 3, axis=0))

    cos_ref[0] = widen(jnp.cos(ang), ec)
    sin_ref[0] = widen(jnp.sin(ang), es)


def _rotary_tables(positions, tb):
    B, T = positions.shape
    tile = ROT_TILE if T % ROT_TILE == 0 else T
    inv_freq = ROPE_BASE ** (-jnp.arange(0, DK, 2, dtype=_F32) / DK)
    freq = jnp.pad(inv_freq, (0, ROT_ROWS - DK // 2)).reshape(ROT_ROWS, 1)
    pos = positions.astype(_F32).reshape(B, 1, T)
    const = lambda b, t: (0, 0)
    return pl.pallas_call(
        _rot_kernel,
        grid=(B, T // tile),
        in_specs=[
            pl.BlockSpec((1, 1, tile), lambda b, t: (b, 0, t)),
            pl.BlockSpec((ROT_ROWS, 1), const),
            pl.BlockSpec((ROT_ROWS, KP), const),
            pl.BlockSpec((ROT_ROWS, KP), const),
        ],
        out_specs=[pl.BlockSpec((1, tile, KP), lambda b, t: (b, t, 0))] * 2,
        out_shape=[jax.ShapeDtypeStruct((B, T, KP), _F32)] * 2,
        compiler_params=pltpu.CompilerParams(dimension_semantics=("arbitrary", "arbitrary")),
        name="rotary_table",
    )(pos, freq, tb["rot_cos_expand"], tb["rot_sin_expand"])


def _mixer_kernel(x_ref, mod_ref, ng_ref, win32_ref, cos_ref, sin_ref,
                  poolw_ref, pools_ref, wa2_ref, ba_ref, gng_ref, wout32_ref,
                  poolwin_ref, rothalf_ref,
                  rdecay_ref, rxi_ref, rzeta_ref, rgc_ref, rhmk_ref, rhmv_ref,
                  smask_ref, hind_ref,
                  gtri_ref, gmask_ref, ghmk_ref, ghmv_ref,
                  o_ref,
                  win_ref, wout_ref,
                  hb_ref, proj_ref, ycat_ref, b_ref, dec_ref, oscr_ref, ext_ref, s2_ref, s4_ref, s8_ref,
                  hist_ref, sret_ref, sgla_ref, *, tile, sub):
    t_idx = pl.program_id(1)
    n_sub = tile // sub

    @pl.when((pl.program_id(0) == 0) & (t_idx == 0))
    def _():
        _cast_weight_transposed(win32_ref, win_ref)
        _cast_weight(wout32_ref, wout_ref)

    @pl.when(t_idx == 0)
    def _():
        hist_ref[...] = jnp.zeros_like(hist_ref)
        sret_ref[...] = jnp.zeros_like(sret_ref)
        sgla_ref[...] = jnp.zeros_like(sgla_ref)

    gate1 = mod_ref[0, 2:3, :]

    norm_scale = ng_ref[...] * (1.0 + mod_ref[0, 1:2, :])
    shift1 = mod_ref[0, 0:1, :]

    def projection_tasks(i):
        slot = i % 2

        def norm():
            x = x_ref[0, i * sub:(i + 1) * sub, :]
            ms = jnp.mean(x * x, axis=-1, keepdims=True)
            hb_ref[slot] = ((x * lax.rsqrt(ms + EPS)) * norm_scale + shift1).astype(_BF)

        def segment(off, width):
            def run():
                proj_ref[slot, :, off:off + width] = _dot(hb_ref[slot], win_ref[:, off:off + width])
            return run

        tasks = [(i, 0, norm)]
        for off in range(0, N_IN, PROJ_PIECE):
            width = min(PROJ_PIECE, N_IN - off)
            tasks.append((i, off + width, segment(off, width)))
        return tasks

    def key_block(pv, lrows, off):
        shift = off % LANES
        if shift == 0:
            return pv[lrows, off:off + KP]
        return pltpu.roll(pv[lrows, off - shift:off - shift + KP], KP - shift, axis=1)

    pending = projection_tasks(0)
    ticks = [0]

    def tick(weight=1):
        ticks[0] += weight
        if pending and ticks[0] >= PROJ_TICKS:
            ticks[0] = 0
            pending.pop(0)[2]()

    def require(i, column):
        while pending and (pending[0][0], pending[0][1]) <= (i, column):
            pending.pop(0)[2]()

    require(0, N_IN)

    def pooling(i):
        n = sub + POOL_HIST
        srows = slice(i * sub, (i + 1) * sub)
        u = proj_ref[i % 2, :, OFF_POOL:OFF_POOL + POOL_WIDTH]
        ext_ref[0:POOL_HIST, :] = hist_ref[...]
        ext_ref[POOL_HIST:n, :] = u
        hist_ref[...] = u[sub - POOL_HIST:sub, :]
        s2_ref[8:n, :] = ext_ref[8:n, :] + ext_ref[7:n - 1, :]
        s4_ref[16:n, :] = s2_ref[16:n, :] + s2_ref[14:n - 2, :]
        s8_ref[24:n, :] = s4_ref[24:n, :] + s4_ref[20:n - 4, :]
        s16 = s8_ref[32:n, :] + s8_ref[24:n - 8, :]
        lane = lax.broadcasted_iota(jnp.int32, (sub, POOL_WIDTH), 1)
        wsum = jnp.where(lane < POOL_GROUP_DIM, s2_ref[32:n, :],
                         jnp.where(lane < 2 * POOL_GROUP_DIM, s4_ref[32:n, :],
                                   jnp.where(lane < 3 * POOL_GROUP_DIM, s8_ref[32:n, :], s16)))
        t_abs = (t_idx * tile + i * sub
                 + lax.broadcasted_iota(jnp.int32, (sub, POOL_WIDTH), 0)).astype(_F32)
        cnt = jnp.minimum(t_abs + 1.0, poolwin_ref[...])
        pooled = wsum / cnt - u
        ycat_ref[srows, 0:POOL_WIDTH] = (_dot(pooled.astype(_BF), poolw_ref[...])
                                         * pools_ref[...]).astype(_BF)

    inv_dv = 1.0 / DV
    k_scale = DK ** -0.5
    q_scale = DK ** -0.5
    smask = smask_ref[...]

    state_blocks = []
    for h in range(N_HEADS):
        lane0 = (h * DK) // LANES * LANES
        lane1 = -(-((h + 1) * DK) // LANES) * LANES
        state_blocks.append((slice(h * DV, (h + 1) * DV), slice(lane0, lane1)))

    def load_state(ref):
        return [ref[rows, lanes] for rows, lanes in state_blocks]

    def store_state(ref, blocks):
        for (rows, lanes), blk in zip(state_blocks, blocks):
            ref[rows, lanes] = blk

    def advance_state(blocks, decay, upd):
        return [blk * decay[:, lanes] + upd[rows, lanes]
                for (rows, lanes), blk in zip(state_blocks, blocks)]

    def own_head(blocks):
        out = []
        for (rows, lanes), blk in zip(state_blocks, blocks):
            part = blk.astype(_BF) * smask[rows, lanes]
            left = lanes.start
            right = KP - lanes.stop
            pieces = ([jnp.zeros((DV, left), _BF)] if left else []) + [part] \
                + ([jnp.zeros((DV, right), _BF)] if right else [])
            out.append(pieces[0] if len(pieces) == 1 else jnp.concatenate(pieces, axis=1))
        return jnp.concatenate(out, axis=0)
    hind = hind_ref[...]
    first_half = rothalf_ref[...] > 0.5

    def head_rms(o):
        ms_h = _dot((o * o).astype(_BF), hind) * inv_dv
        return o * lax.rsqrt(ms_h + EPS)

    def stack_heads(a):
        return jnp.concatenate([a] * N_HEADS, axis=0)

    def rotary(t, cosv, sinv):
        partner = jnp.where(first_half, pltpu.roll(t, KP - DK // 2, axis=1), pltpu.roll(t, DK // 2, axis=1))
        return t * cosv + partner * sinv

    def retention(i, r0):
        C = RET_CHUNK
        pv = proj_ref.at[i % 2]
        parts = []
        for c in range(sub // C):
            rows = slice(r0 + c * C, r0 + (c + 1) * C)
            lrows = slice(c * C, (c + 1) * C)
            cosv = cos_ref[0, rows, :]
            sinv = sin_ref[0, rows, :]
            q = rotary(key_block(pv, lrows, OFF_RQ), cosv, sinv)
            k = rotary(key_block(pv, lrows, OFF_RK), cosv, sinv) * k_scale
            vb = pv[lrows, OFF_RV:OFF_RV + VW].astype(_BF)
            qb = q.astype(_BF)
            kbd = stack_heads(k.astype(_BF)) * rhmk_ref[...]
            scores = _dot_nt(qb, kbd) * rdecay_ref[...]
            vbd = stack_heads(vb) * rhmv_ref[...]
            o_intra = _dot(scores.astype(_BF), vbd)
            upd = _dot_tn(vb, (k * rzeta_ref[...]).astype(_BF))
            parts.append((rows, qb, o_intra, upd))
            tick()
            yield
        state = states["ret"]
        for rows, qb, o_intra, upd in parts:
            oscr_ref[rows, 0:VW] = o_intra + _dot_nt(qb, own_head(state)) * rxi_ref[...]
            state = advance_state(state, rgc_ref[...], upd)
            tick()
            yield
        states["ret"] = state

    def gla_decay(i):
        C = GLA_CHUNK
        n_chunks = sub // C
        W = n_chunks * KP
        pv = proj_ref.at[i % 2]
        logits = _dot(pv[:, OFF_GA:OFF_GA + LANES].astype(_BF), wa2_ref[...]) + ba_ref[...]
        log_sig = jnp.minimum(logits, 0.0) - jnp.log(1.0 + jnp.exp(-jnp.abs(logits)))
        la = log_sig * (LOG2_E / GLA_GATE_TAU)

        la_l = jnp.concatenate([la[c * C:(c + 1) * C, :] for c in range(n_chunks)], axis=1)
        b = _dot(gtri_ref[...], jnp.concatenate(_split3(la_l), axis=0))
        b_ref[i % 2] = b
        yield

        row = lax.broadcasted_iota(jnp.int32, (C, W), 0)
        s = C // 2
        lvl = 0
        while s >= SUBLANES // 2:
            refs = [jnp.broadcast_to(b_ref[i % 2, p + s - 1:p + s, :], (2 * s, W)) for p in range(0, C, 2 * s)]
            b_at_ref = refs[0] if len(refs) == 1 else jnp.concatenate(refs, axis=0)
            dec_ref[i % 2, lvl * C:(lvl + 1) * C, :] = jnp.exp2(-jnp.abs(b - b_at_ref))
            s //= 2
            lvl += 1
            tick()
            yield
        up1 = pltpu.roll(b, 1, axis=0)
        up2 = pltpu.roll(b, 2, axis=0)
        dn1 = pltpu.roll(b, C - 1, axis=0)
        m4 = row % 4
        b_at_ref = jnp.where(m4 == 0, dn1, jnp.where(m4 == 1, b, jnp.where(m4 == 2, up1, up2)))
        dec_ref[i % 2, lvl * C:(lvl + 1) * C, :] = jnp.exp2(-jnp.abs(b - b_at_ref))
        lvl += 1
        tick()
        yield
        b_at_ref = jnp.where(row % 2 == 0, b, up1)
        dec_ref[i % 2, lvl * C:(lvl + 1) * C, :] = jnp.exp2(-jnp.abs(b - b_at_ref))
        lvl += 1
        tick()
        yield
        dec_ref[i % 2, lvl * C:(lvl + 1) * C, :] = jnp.exp2(b)
        tick()
        b_last = jnp.broadcast_to(b_ref[i % 2, C - 1:C, :], (C, W))
        dec_ref[i % 2, (lvl + 1) * C:(lvl + 2) * C, :] = jnp.exp2(b_last - b)
        tick()

    def gla_chunks(i, r0, state):
        C = GLA_CHUNK
        n_chunks = sub // C
        pv = proj_ref.at[i % 2]
        hmk = ghmk_ref[...]
        parts = []
        for c in range(n_chunks):
            rows = slice(r0 + c * C, r0 + (c + 1) * C)
            lrows = slice(c * C, (c + 1) * C)
            lanes = slice(c * KP, (c + 1) * KP)
            q = key_block(pv, lrows, OFF_GQ) * q_scale
            k = key_block(pv, lrows, OFF_GK)
            vb = pv[lrows, OFF_GV:OFF_GV + VW].astype(_BF)
            qb = q.astype(_BF)
            kbd0 = stack_heads(k.astype(_BF)) * hmk
            scores = jnp.zeros((C, N_HEADS * C), _F32)
            for lvl in range(N_GLA_LEVELS + 1):
                if lvl < N_GLA_LEVELS:
                    eb = dec_ref[i % 2, lvl * C:(lvl + 1) * C, lanes].astype(_BF)
                    ql, kbd = qb * eb, kbd0 * stack_heads(eb)
                else:
                    ql, kbd = qb, kbd0
                scores = scores + _dot_nt(ql, kbd) * gmask_ref[lvl]
                tick()
            vbd = stack_heads(vb) * ghmv_ref[...]
            e_cum = dec_ref[i % 2, N_GLA_LEVELS * C:(N_GLA_LEVELS + 1) * C, lanes]
            e_rev = dec_ref[i % 2, (N_GLA_LEVELS + 1) * C:(N_GLA_LEVELS + 2) * C, lanes]
            o_intra = _dot(scores.astype(_BF), vbd)
            upd = _dot_tn(vb, (k * e_rev).astype(_BF))
            parts.append((rows, (q * e_cum).astype(_BF), o_intra, upd, e_cum[C - 1:C, :]))
        for rows, qe, o_intra, upd, e_last in parts:
            oscr_ref[rows, VW:2 * VW] = o_intra + _dot_nt(qe, own_head(state))
            state = advance_state(state, e_last, upd)
            tick()
        return state

    states = {"ret": load_state(sret_ref)}
    gla_state = load_state(sgla_ref)
    for i in range(n_sub):
        r0 = i * sub
        srows = slice(r0, r0 + sub)
        if i + 1 < n_sub:
            pending.extend(projection_tasks(i + 1))
        tick()
        pooling(i)
        tick()
        stages = [retention(i, r0), gla_decay(i)]
        while stages:
            for stage in list(stages):
                if next(stage, stages) is stages:
                    stages.remove(stage)
        gla_state = gla_chunks(i, r0, gla_state)
        y_ret = _silu(proj_ref[i % 2, :, OFF_RG:OFF_RG + VW]) * head_rms(oscr_ref[srows, 0:VW])
        ycat_ref[srows, POOL_WIDTH:POOL_WIDTH + VW] = y_ret.astype(_BF)
        gg_off = OFF_GG % LANES
        gg = pltpu.roll(proj_ref[i % 2, :, OFF_GG - gg_off:N_IN], N_IN - OFF_GG, axis=1)[:, 0:VW]
        y_gla = (_silu(gg)
                 * (head_rms(oscr_ref[srows, VW:2 * VW]) * gng_ref[...]))
        require(i + 1, N_IN)
        ycat_ref[srows, POOL_WIDTH + VW:POOL_WIDTH + 2 * VW] = y_gla.astype(_BF)
        o_ref[0, srows, :] = x_ref[0, srows, :] + gate1 * _dot(ycat_ref[srows, :], wout_ref[...])
    store_state(sret_ref, states["ret"])
    store_state(sgla_ref, gla_state)


def _whole(shape):
    nd = len(shape)
    return pl.BlockSpec(shape, lambda b, t: (0,) * nd, pipeline_mode=pl.Buffered(1))


def _layer_slab(shape, layer):
    nd = len(shape) - 1
    return pl.BlockSpec((None,) + tuple(shape[1:]), lambda b, t: (layer,) + (0,) * nd,
                        pipeline_mode=pl.Buffered(1))


def _mixer(x, mod, layer, params, cos, sin, tb):
    B, T, D = x.shape
    tile = MIX_TILE if T % MIX_TILE == 0 else T
    sub = MIX_SUB if tile % MIX_SUB == 0 else tile
    consts = [tb[k] for k in ("pool_win", "rot_first_half", "ret_decay", "ret_xi", "ret_zeta", "ret_gc",
                              "ret_hmk", "ret_hmv", "state_mask", "head_ind", "gla_tri", "gla_mask",
                              "gla_hmk", "gla_hmv")]
    norm_g, win = params[0], params[1]
    rest = list(params[2:])
    n_sub = tile // sub
    args = [x, mod, norm_g, win, cos, sin] + rest + consts
    in_specs = [
        pl.BlockSpec((1, tile, D), lambda b, t: (b, t, 0)),
        pl.BlockSpec((None, 1, 6, D), lambda b, t: (layer, b, 0, 0)),
        _layer_slab(norm_g.shape, layer),
        _layer_slab(win.shape, layer),
        pl.BlockSpec((1, tile, KP), lambda b, t: (b, t, 0)),
        pl.BlockSpec((1, tile, KP), lambda b, t: (b, t, 0)),
    ] + [_layer_slab(a.shape, layer) for a in rest] + [_whole(a.shape) for a in consts]
    n = sub + POOL_HIST
    gla_lanes = (sub // GLA_CHUNK) * KP
    return pl.pallas_call(
        functools.partial(_mixer_kernel, tile=tile, sub=sub),
        grid=(B, T // tile),
        in_specs=in_specs,
        out_specs=pl.BlockSpec((1, tile, D), lambda b, t: (b, t, 0)),
        out_shape=jax.ShapeDtypeStruct((B, T, D), _F32),
        scratch_shapes=[
            pltpu.VMEM((D, N_IN), _BF),
            pltpu.VMEM((POOL_WIDTH + 2 * VW, D), _BF),
            pltpu.VMEM((2, sub, D), _BF),
            pltpu.VMEM((2, sub, N_IN), _F32),
            pltpu.VMEM((tile, POOL_WIDTH + 2 * VW), _BF),
            pltpu.VMEM((2, GLA_CHUNK, gla_lanes), _F32),
            pltpu.VMEM((2, (N_GLA_LEVELS + 2) * GLA_CHUNK, gla_lanes), _F32),
            pltpu.VMEM((tile, 2 * VW), _F32),
            pltpu.VMEM((n, POOL_WIDTH), _F32),
            pltpu.VMEM((n, POOL_WIDTH), _F32),
            pltpu.VMEM((n, POOL_WIDTH), _F32),
            pltpu.VMEM((n, POOL_WIDTH), _F32),
            pltpu.VMEM((POOL_HIST, POOL_WIDTH), _F32),
            pltpu.VMEM((VW, KP), _F32),
            pltpu.VMEM((VW, KP), _F32),
        ],
        compiler_params=pltpu.CompilerParams(
            dimension_semantics=("arbitrary", "arbitrary"), vmem_limit_bytes=VMEM_LIMIT_BYTES),
        name="token_mixer",
    )(*args)


def _ffn_kernel(x_ref, mod_ref, ng_ref, wup32_ref, cw_ref, wd32_ref, fg_ref,
                o_ref, wup_ref, wd_ref, h_ref, carry_ref, act_ref,
                *, tile, sub, n_tiles, n_cast, d_ff, fb, final_norm):
    g = pl.program_id(0)
    rows_per_step = wup32_ref.shape[0]

    @pl.when(g < n_cast)
    def _():
        r0 = pl.multiple_of(g * rows_per_step, rows_per_step)
        wup_ref[pl.ds(r0, rows_per_step), :] = wup32_ref[...].astype(_BF)

    @pl.when(g == 0)
    def _():
        _cast_weight(wd32_ref, wd_ref)

    @pl.when(g >= n_cast)
    def _():
        @pl.when((g - n_cast) % n_tiles == 0)
        def _():
            carry_ref[...] = jnp.zeros_like(carry_ref)

        for base in range(0, tile, sub):
            _ffn_sub_tile(x_ref, mod_ref, ng_ref, wup_ref, cw_ref, wd_ref, fg_ref, o_ref, h_ref,
                          carry_ref, act_ref, base=base, tile=sub, d_ff=d_ff, fb=fb,
                          final_norm=final_norm)


def _ffn_sub_tile(x_ref, mod_ref, ng_ref, wup_ref, cw_ref, wd_ref, fg_ref, o_ref, h_ref, carry_ref,
                  act_ref, *, base, tile, d_ff, fb, final_norm):
    S = SUBLANES
    R = tile // S
    D = x_ref.shape[-1]
    shift2 = mod_ref[0, 3:4, :]
    scale2 = mod_ref[0, 4:5, :]
    gate2 = mod_ref[0, 5:6, :]
    norm_scale = ng_ref[...] * (1.0 + scale2)
    groups = FFN_NORM_ROWS // S
    early_cols = ([j * fb for j in range(FFN_EARLY_BLOCKS)]
                  + [d_ff + j * fb for j in range(FFN_EARLY_BLOCKS)])
    early = {col: [] for col in early_cols}
    x_parts = []
    for p in range(tile // FFN_NORM_ROWS):
        prow = slice(p * FFN_NORM_ROWS, (p + 1) * FFN_NORM_ROWS)
        xs = jnp.concatenate([x_ref[0, base + s * R + p * groups:base + s * R + (p + 1) * groups, :]
                              for s in range(S)], axis=0)
        xs = xs.reshape(S, groups, D).swapaxes(0, 1).reshape(FFN_NORM_ROWS, D)
        ms = jnp.mean(xs * xs, axis=-1, keepdims=True)
        h_ref[prow, :] = ((xs * lax.rsqrt(ms + EPS)) * norm_scale + shift2).astype(_BF)
        x_parts.append(xs)
        for col in early_cols:
            early[col].append(_dot(h_ref[prow, :], wup_ref[:, col:col + fb]))
    x = jnp.concatenate(x_parts, axis=0)
    first_sublane = lax.broadcasted_iota(jnp.int32, (S, fb), 0) == 0

    def up_conv(col):
        cols = slice(col, col + fb)
        if col in early:
            u = jnp.concatenate(early[col], axis=0)
        else:
            u = _dot(h_ref[...], wup_ref[:, cols])
        back1 = jnp.where(first_sublane, pltpu.roll(carry_ref[S:2 * S, cols], 1, axis=0),
                          pltpu.roll(u[tile - S:tile, :], 1, axis=0))
        back2 = jnp.where(first_sublane, pltpu.roll(carry_ref[0:S, cols], 1, axis=0),
                          pltpu.roll(u[tile - 2 * S:tile - S, :], 1, axis=0))
        carry_ref[:, cols] = u[tile - 2 * S:tile, :]
        prev1 = jnp.concatenate([back1, u[0:tile - S, :]], axis=0)
        prev2 = jnp.concatenate([back2, back1, u[0:tile - 2 * S, :]], axis=0)
        cw = cw_ref[:, cols]
        return prev2 * cw[0:1, :] + prev1 * cw[1:2, :] + u * cw[2:3, :] + cw[3:4, :]

    for j in range(d_ff // fb):
        ya = up_conv(j * fb)
        yg = up_conv(d_ff + j * fb)
        act_ref[:, j * fb:(j + 1) * fb] = (_silu(yg) * ya).astype(_BF)

    out = x + gate2 * _dot(act_ref[...], wd_ref[...])
    if final_norm:
        ms_o = jnp.mean(out * out, axis=-1, keepdims=True)
        out = out * lax.rsqrt(ms_o + EPS) * fg_ref[...]
    o_ref[0, base:base + tile, :] = out.reshape(R, S, D).swapaxes(0, 1).reshape(tile, D)


def _ffn(x, mod, layer, params, final_g, final_norm):
    B, T, D = x.shape
    tile = FFN_TILE if T % FFN_TILE == 0 else T
    sub = FFN_SUB if tile % FFN_SUB == 0 else tile
    norm_g, wup, cw, wd = params
    d_ff = wd.shape[1]
    n_tiles = T // tile
    n_cast = FFN_CAST_STEPS
    cast_rows = D // n_cast

    def token_tile(g):
        k = jnp.maximum(g - n_cast, 0)
        return k // n_tiles, k % n_tiles

    def slab(a):
        nd = a.ndim - 1
        return pl.BlockSpec((None,) + tuple(a.shape[1:]), lambda g: (layer,) + (0,) * nd,
                            pipeline_mode=pl.Buffered(1))

    in_specs = [
        pl.BlockSpec((1, tile, D), lambda g: token_tile(g) + (0,)),
        pl.BlockSpec((None, 1, 6, D), lambda g: (layer, token_tile(g)[0], 0, 0)),
        slab(norm_g),
        pl.BlockSpec((None, cast_rows, 2 * d_ff), lambda g: (layer, jnp.minimum(g, n_cast - 1), 0)),
        slab(cw),
        slab(wd),
        pl.BlockSpec(final_g.shape, lambda g: (0, 0), pipeline_mode=pl.Buffered(1)),
    ]
    return pl.pallas_call(
        functools.partial(_ffn_kernel, tile=tile, sub=sub, n_tiles=n_tiles, n_cast=n_cast, d_ff=d_ff,
                          fb=FFN_BLOCK, final_norm=final_norm),
        grid=(B * n_tiles + n_cast,),
        in_specs=in_specs,
        out_specs=pl.BlockSpec((1, tile, D), lambda g: token_tile(g) + (0,)),
        out_shape=jax.ShapeDtypeStruct((B, T, D), _F32),
        scratch_shapes=[
            pltpu.VMEM((D, 2 * d_ff), _BF),
            pltpu.VMEM((d_ff, D), _BF),
            pltpu.VMEM((sub, D), _BF),
            pltpu.VMEM((2 * SUBLANES, 2 * d_ff), _F32),
            pltpu.VMEM((sub, d_ff), _BF),
        ],
        compiler_params=pltpu.CompilerParams(
            dimension_semantics=("arbitrary",), vmem_limit_bytes=VMEM_LIMIT_BYTES),
        name="conv_ffn",
    )(x, mod, *params, final_g)


def _prep_mixer_params(norm1_g, w_in, pool_w, pool_scale, gla_wa2, gla_ba, gla_norm_g, w_out):
    L, D, _ = w_in.shape
    G = pool_w.shape[1]
    eye = jnp.eye(G, dtype=pool_w.dtype)
    poolw = (pool_w[:, :, :, None, :] * eye[None, :, None, :, None]).reshape(L, POOL_WIDTH, POOL_WIDTH)
    wa2 = jnp.pad(gla_wa2, ((0, 0), (0, LANES - GLA_GATE_RANK), (0, KP - KW)))
    ba = jnp.pad(gla_ba, ((0, 0), (0, KP - KW))).reshape(L, 1, KP)
    w_in = jnp.swapaxes(w_in, 1, 2)
    return (norm1_g.reshape(L, 1, D), w_in, poolw.astype(_BF), pool_scale.reshape(L, 1, POOL_WIDTH),
            wa2.astype(_BF), ba, gla_norm_g.reshape(L, 1, VW), w_out)


def _prep_ffn_params(norm2_g, w_up, conv_w, conv_b, w_down):
    L, D, _ = w_up.shape
    cw = jnp.pad(jnp.concatenate([conv_w, conv_b[:, None, :]], axis=1),
                 ((0, 0), (0, SUBLANES - CONV_WIDTH - 1), (0, 0)))
    return norm2_g.reshape(L, 1, D), w_up, cw, w_down


def kernel(x, c, positions, ada_w, ada_b, norm1_g, w_in, pool_w, pool_scale, gla_wa2, gla_ba,
           gla_norm_g, w_out, norm2_g, w_up, conv_w, conv_b, w_down, final_g):
    L = ada_w.shape[0]
    B, T, D = x.shape
    tb = {k: jnp.asarray(v) for k, v in _tables().items()}
    for name in ("head_ind", "gla_tri", "ret_hmk", "ret_hmv", "gla_hmk", "gla_hmv", "state_mask",
                 "rot_cos_expand", "rot_sin_expand"):
        tb[name] = tb[name].astype(_BF)
    mod = _modulation(c, ada_w, ada_b).reshape(L, B, 6, D)
    cos, sin = _rotary_tables(positions, tb)
    mixer_params = _prep_mixer_params(norm1_g, w_in, pool_w, pool_scale, gla_wa2, gla_ba, gla_norm_g, w_out)
    ffn_params = _prep_ffn_params(norm2_g, w_up, conv_w, conv_b, w_down)
    fg = final_g.reshape(1, D)
    for l in range(L):
        x = _mixer(x, mod, l, mixer_params, cos, sin, tb)
        x = _ffn(x, mod, l, ffn_params, fg, final_norm=(l == L - 1))
    return x
```

```python
import functools

import numpy as np
import jax
import jax.numpy as jnp
from jax import lax
from jax.experimental import pallas as pl
from jax.experimental.pallas import tpu as pltpu

POOL_WIDTH = 256
POOL_WINDOWS = (2, 4, 8, 16)
POOL_GROUP_DIM = 64
N_HEADS = 4
DK = 48
DV = 96
KW = N_HEADS * DK
VW = N_HEADS * DV
RET_CHUNK = 128
GLA_CHUNK = 64
GLA_GATE_RANK = 16
GLA_GATE_TAU = 16.0
ROPE_BASE = 10000.0
CONV_WIDTH = 3
EPS = 1e-6

LANES = 128
SUBLANES = 8
KP = 256
VMEM_LIMIT_BYTES = 56 * 1024 * 1024

OFF_POOL = 0
OFF_RQ = OFF_POOL + POOL_WIDTH
OFF_RK = OFF_RQ + KW
OFF_RV = OFF_RK + KW
OFF_RG = OFF_RV + VW
OFF_GQ = OFF_RG + VW
OFF_GK = OFF_GQ + KW
OFF_GV = OFF_GK + KW
OFF_GA = OFF_GV + VW
OFF_GG = OFF_GA + GLA_GATE_RANK
D_IN = OFF_GG + VW
N_IN = -(-D_IN // LANES) * LANES

MIX_TILE = 512
MIX_SUB = 256
PROJ_PIECE = 256
PROJ_TICKS = 3
FFN_TILE = 512
FFN_SUB = 512
FFN_BLOCK = 256
FFN_NORM_ROWS = 128
FFN_EARLY_BLOCKS = 2
ROT_TILE = 1024
ROT_ROWS = 32
POOL_HIST = 32
N_GLA_LEVELS = 6
CAST_ROWS = 128
LOG2_E = 1.4426950408889634

_BF = jnp.bfloat16
_F32 = jnp.float32


def _dot(a, b):
    return jnp.dot(a, b, preferred_element_type=_F32)


def _dot_nt(a, b):
    return lax.dot_general(a, b, (((1,), (1,)), ((), ())), preferred_element_type=_F32)


def _dot_tn(a, b):
    return lax.dot_general(a, b, (((0,), (0,)), ((), ())), preferred_element_type=_F32)


def _split3(x):
    p1 = x.astype(_BF)
    r1 = x - p1.astype(_F32)
    p2 = r1.astype(_BF)
    p3 = (r1 - p2.astype(_F32)).astype(_BF)
    return p1, p2, p3


def _cast_weight(src_ref, dst_ref):
    for r in range(0, src_ref.shape[0], CAST_ROWS):
        rows = slice(r, r + CAST_ROWS)
        dst_ref[rows, :] = src_ref[rows, :].astype(_BF)


def _cast_weight_transposed(src_ref, dst_ref):
    n, k = src_ref.shape
    for c in range(0, dst_ref.shape[1], CAST_ROWS):
        rows = min(CAST_ROWS, max(n - c, 0))
        block = src_ref[c:c + rows, :] if rows else None
        if rows < CAST_ROWS:
            zeros = jnp.zeros((CAST_ROWS - rows, k), _F32)
            block = zeros if block is None else jnp.concatenate([block, zeros], axis=0)
        dst_ref[:, c:c + CAST_ROWS] = block.T.astype(_BF)


def _sigmoid(x):
    return 1.0 / (1.0 + jnp.exp(-x))


def _silu(x):
    return x * _sigmoid(x)


def _key_head(d):
    return np.where(d < KW, d // DK, -1)


@functools.lru_cache(maxsize=None)
def _tables():
    t = {}
    kd = _key_head(np.arange(KP))
    vd = np.arange(VW) // DV

    def head_masks(chunk):
        rows = np.repeat(np.arange(N_HEADS), chunk)
        return ((rows[:, None] == kd[None, :]).astype(np.float32),
                (rows[:, None] == vd[None, :]).astype(np.float32))

    C = RET_CHUNK
    lg = np.log(1.0 - 2.0 ** (-5.0 - np.arange(N_HEADS, dtype=np.float64)))
    i = np.arange(C)
    rel = i[:, None] - i[None, :]
    dec = np.where(rel[None] >= 0, np.exp(np.maximum(rel, 0)[None] * lg[:, None, None]), 0.0)
    t["ret_decay"] = np.transpose(dec, (1, 0, 2)).reshape(C, N_HEADS * C).astype(np.float32)
    t["ret_xi"] = np.exp((i[:, None] + 1.0) * lg[vd][None, :]).astype(np.float32)
    zeta = np.exp((C - 1.0 - i)[:, None] * lg[np.maximum(kd, 0)][None, :]) * (kd >= 0)[None, :]
    t["ret_zeta"] = zeta.astype(np.float32)
    t["ret_gc"] = (np.exp(C * lg[np.maximum(kd, 0)]) * (kd >= 0))[None, :].astype(np.float32)
    t["ret_hmk"], t["ret_hmv"] = head_masks(C)
    t["state_mask"] = (vd[:, None] == kd[None, :]).astype(np.float32)
    t["head_ind"] = (vd[:, None] == vd[None, :]).astype(np.float32)

    C = GLA_CHUNK
    i = np.arange(C)
    masks = []
    s = C // 2
    while s >= 1:
        parent = (i // (2 * s)) * (2 * s)
        upper = (i - parent) >= s
        masks.append(upper[:, None] & (~upper)[None, :] & (parent[:, None] == parent[None, :]))
        s //= 2
    masks.append(i[:, None] == i[None, :])
    t["gla_mask"] = np.stack([np.tile(m, (1, N_HEADS)) for m in masks]).astype(np.float32)
    t["gla_tri"] = (i[None, :] <= i[:, None]).astype(np.float32)
    t["gla_hmk"], t["gla_hmv"] = head_masks(C)

    t["pool_win"] = np.repeat(np.asarray(POOL_WINDOWS, np.float32), POOL_GROUP_DIM)[None, :]

    half = DK // 2
    l = np.arange(KP)
    hit = (np.arange(ROT_ROWS)[:, None] == (l % half)[None, :]) & (l < KW)[None, :]
    t["rot_cos_expand"] = hit.astype(np.float32)
    t["rot_sin_expand"] = hit * np.where((l % DK) < half, -1.0, 1.0)[None, :].astype(np.float32)
    t["rot_first_half"] = ((l % DK) < half).astype(np.float32)[None, :]
    return t


def _mod_kernel(c_ref, w_ref, b_ref, o_ref):
    ca = _silu(c_ref[...]).astype(_BF)
    o_ref[...] = _dot(ca, w_ref[...].astype(_BF)) + b_ref[...]


def _modulation(c, ada_w, ada_b):
    L, D, N = ada_w.shape
    B = c.shape[0]
    tn = D
    return pl.pallas_call(
        _mod_kernel,
        grid=(L, N // tn),
        in_specs=[
            pl.BlockSpec((B, D), lambda l, n: (0, 0)),
            pl.BlockSpec((None, D, tn), lambda l, n: (l, 0, n)),
            pl.BlockSpec((None, 1, tn), lambda l, n: (l, 0, n)),
        ],
        out_specs=pl.BlockSpec((None, B, tn), lambda l, n: (l, 0, n)),
        out_shape=jax.ShapeDtypeStruct((L, B, N), _F32),
        compiler_params=pltpu.CompilerParams(
            dimension_semantics=("arbitrary", "arbitrary"), vmem_limit_bytes=VMEM_LIMIT_BYTES),
        name="adaln_modulation",
    )(c, ada_w, ada_b.reshape(L, 1, N))


def _rot_kernel(pos_ref, freq_ref, ec_ref, es_ref, cos_ref, sin_ref):
    ang = freq_ref[...] * pos_ref[0]
    ec = ec_ref[...]
    es = es_ref[...]

    def widen(v, e):
        pieces = jnp.concatenate(_split3(v), axis=0)
        return _dot_tn(pieces, jnp.concatenate([e] * 3, axis=0))

    cos_ref[0] = widen(jnp.cos(ang), ec)
    sin_ref[0] = widen(jnp.sin(ang), es)


def _rotary_tables(positions, tb):
    B, T = positions.shape
    tile = ROT_TILE if T % ROT_TILE == 0 else T
    inv_freq = ROPE_BASE ** (-jnp.arange(0, DK, 2, dtype=_F32) / DK)
    freq = jnp.pad(inv_freq, (0, ROT_ROWS - DK // 2)).reshape(ROT_ROWS, 1)
    pos = positions.astype(_F32).reshape(B, 1, T)
    const = lambda b, t: (0, 0)
    return pl.pallas_call(
        _rot_kernel,
        grid=(B, T // tile),
        in_specs=[
            pl.BlockSpec((1, 1, tile), lambda b, t: (b, 0, t)),
            pl.BlockSpec((ROT_ROWS, 1), const),
            pl.BlockSpec((ROT_ROWS, KP), const),
            pl.BlockSpec((ROT_ROWS, KP), const),
        ],
        out_specs=[pl.BlockSpec((1, tile, KP), lambda b, t: (b, t, 0))] * 2,
        out_shape=[jax.ShapeDtypeStruct((B, T, KP), _F32)] * 2,
        compiler_params=pltpu.CompilerParams(dimension_semantics=("arbitrary", "arbitrary")),
        name="rotary_table",
    )(pos, freq, tb["rot_cos_expand"], tb["rot_sin_expand"])


def _mixer_kernel(x_ref, mod_ref, ng_ref, win32_ref, cos_ref, sin_ref,
                  poolw_ref, pools_ref, wa2_ref, ba_ref, gng_ref, wout32_ref,
                  poolwin_ref, rothalf_ref,
                  rdecay_ref, rxi_ref, rzeta_ref, rgc_ref, rhmk_ref, rhmv_ref,
                  smask_ref, hind_ref,
                  gtri_ref, gmask_ref, ghmk_ref, ghmv_ref, wup32_ref,
                  o_ref, wup_bf_ref,
                  win_ref, wout_ref,
                  hb_ref, proj_ref, ycat_ref, b_ref, dec_ref, oscr_ref, ext_ref, s2_ref, s4_ref, s8_ref,
                  hist_ref, sret_ref, sgla_ref, *, tile, sub):
    t_idx = pl.program_id(1)
    n_sub = tile // sub

    @pl.when((pl.program_id(0) == 0) & (t_idx == 0))
    def _():
        _cast_weight_transposed(win32_ref, win_ref)
        _cast_weight(wout32_ref, wout_ref)

    @pl.when(t_idx == 0)
    def _():
        hist_ref[...] = jnp.zeros_like(hist_ref)
        sret_ref[...] = jnp.zeros_like(sret_ref)
        sgla_ref[...] = jnp.zeros_like(sgla_ref)

    gate1 = mod_ref[0, 2:3, :]

    norm_scale = ng_ref[...] * (1.0 + mod_ref[0, 1:2, :])
    shift1 = mod_ref[0, 0:1, :]

    def projection_tasks(i):
        slot = i % 2

        def norm():
            x = x_ref[0, i * sub:(i + 1) * sub, :]
            ms = jnp.mean(x * x, axis=-1, keepdims=True)
            hb_ref[slot] = ((x * lax.rsqrt(ms + EPS)) * norm_scale + shift1).astype(_BF)

        def segment(off, width):
            def run():
                proj_ref[slot, :, off:off + width] = _dot(hb_ref[slot], win_ref[:, off:off + width])
            return run

        tasks = [(i, 0, norm)]
        for off in range(0, N_IN, PROJ_PIECE):
            width = min(PROJ_PIECE, N_IN - off)
            tasks.append((i, off + width, segment(off, width)))
        return tasks

    def key_block(pv, lrows, off):
        shift = off % LANES
        if shift == 0:
            return pv[lrows, off:off + KP]
        return pltpu.roll(pv[lrows, off - shift:off - shift + KP], KP - shift, axis=1)

    pending = projection_tasks(0)
    ticks = [0]

    def tick(weight=1):
        ticks[0] += weight
        if pending and ticks[0] >= PROJ_TICKS:
            ticks[0] = 0
            pending.pop(0)[2]()

    def require(i, column):
        while pending and (pending[0][0], pending[0][1]) <= (i, column):
            pending.pop(0)[2]()

    require(0, N_IN)

    def pooling(i):
        n = sub + POOL_HIST
        srows = slice(i * sub, (i + 1) * sub)
        u = proj_ref[i % 2, :, OFF_POOL:OFF_POOL + POOL_WIDTH]
        ext_ref[0:POOL_HIST, :] = hist_ref[...]
        ext_ref[POOL_HIST:n, :] = u
        hist_ref[...] = u[sub - POOL_HIST:sub, :]
        s2_ref[8:n, :] = ext_ref[8:n, :] + ext_ref[7:n - 1, :]
        s4_ref[16:n, :] = s2_ref[16:n, :] + s2_ref[14:n - 2, :]
        s8_ref[24:n, :] = s4_ref[24:n, :] + s4_ref[20:n - 4, :]
        s16 = s8_ref[32:n, :] + s8_ref[24:n - 8, :]
        lane = lax.broadcasted_iota(jnp.int32, (sub, POOL_WIDTH), 1)
        wsum = jnp.where(lane < POOL_GROUP_DIM, s2_ref[32:n, :],
                         jnp.where(lane < 2 * POOL_GROUP_DIM, s4_ref[32:n, :],
                                   jnp.where(lane < 3 * POOL_GROUP_DIM, s8_ref[32:n, :], s16)))
        t_abs = (t_idx * tile + i * sub
                 + lax.broadcasted_iota(jnp.int32, (sub, POOL_WIDTH), 0)).astype(_F32)
        cnt = jnp.minimum(t_abs + 1.0, poolwin_ref[...])
        pooled = wsum / cnt - u
        ycat_ref[srows, 0:POOL_WIDTH] = (_dot(pooled.astype(_BF), poolw_ref[...])
                                         * pools_ref[...]).astype(_BF)

    inv_dv = 1.0 / DV
    k_scale = DK ** -0.5
    q_scale = DK ** -0.5
    smask = smask_ref[...]

    state_blocks = []
    for h in range(N_HEADS):
        lane0 = (h * DK) // LANES * LANES
        lane1 = -(-((h + 1) * DK) // LANES) * LANES
        state_blocks.append((slice(h * DV, (h + 1) * DV), slice(lane0, lane1)))

    def load_state(ref):
        return [ref[rows, lanes] for rows, lanes in state_blocks]

    def store_state(ref, blocks):
        for (rows, lanes), blk in zip(state_blocks, blocks):
            ref[rows, lanes] = blk

    def advance_state(blocks, decay, upd):
        return [blk * decay[:, lanes] + upd[rows, lanes]
                for (rows, lanes), blk in zip(state_blocks, blocks)]

    def own_head(blocks):
        out = []
        for (rows, lanes), blk in zip(state_blocks, blocks):
            part = blk.astype(_BF) * smask[rows, lanes]
            left = lanes.start
            right = KP - lanes.stop
            pieces = ([jnp.zeros((DV, left), _BF)] if left else []) + [part] \
                + ([jnp.zeros((DV, right), _BF)] if right else [])
            out.append(pieces[0] if len(pieces) == 1 else jnp.concatenate(pieces, axis=1))
        return jnp.concatenate(out, axis=0)
    hind = hind_ref[...]
    first_half = rothalf_ref[...] > 0.5

    def head_rms(o):
        ms_h = _dot((o * o).astype(_BF), hind) * inv_dv
        return o * lax.rsqrt(ms_h + EPS)

    def stack_heads(a):
        return jnp.concatenate([a] * N_HEADS, axis=0)

    def rotary(t, cosv, sinv):
        partner = jnp.where(first_half, pltpu.roll(t, KP - DK // 2, axis=1), pltpu.roll(t, DK // 2, axis=1))
        return t * cosv + partner * sinv

    def retention(i, r0):
        C = RET_CHUNK
        pv = proj_ref.at[i % 2]
        parts = []
        for c in range(sub // C):
            rows = slice(r0 + c * C, r0 + (c + 1) * C)
            lrows = slice(c * C, (c + 1) * C)
            cosv = cos_ref[0, rows, :]
            sinv = sin_ref[0, rows, :]
            q = rotary(key_block(pv, lrows, OFF_RQ), cosv, sinv)
            k = rotary(key_block(pv, lrows, OFF_RK), cosv, sinv) * k_scale
            vb = pv[lrows, OFF_RV:OFF_RV + VW].astype(_BF)
            qb = q.astype(_BF)
            kbd = stack_heads(k.astype(_BF)) * rhmk_ref[...]
            scores = _dot_nt(qb, kbd) * rdecay_ref[...]
            vbd = stack_heads(vb) * rhmv_ref[...]
            o_intra = _dot(scores.astype(_BF), vbd)
            upd = _dot_tn(vb, (k * rzeta_ref[...]).astype(_BF))
            parts.append((rows, qb, o_intra, upd))
            tick()
            yield
        state = states["ret"]
        for rows, qb, o_intra, upd in parts:
            oscr_ref[rows, 0:VW] = o_intra + _dot_nt(qb, own_head(state)) * rxi_ref[...]
            state = advance_state(state, rgc_ref[...], upd)
            tick()
            yield
        states["ret"] = state

    def gla_decay(i):
        C = GLA_CHUNK
        n_chunks = sub // C
        W = n_chunks * KP
        pv = proj_ref.at[i % 2]
        logits = _dot(pv[:, OFF_GA:OFF_GA + LANES].astype(_BF), wa2_ref[...]) + ba_ref[...]
        log_sig = jnp.minimum(logits, 0.0) - jnp.log(1.0 + jnp.exp(-jnp.abs(logits)))
        la = log_sig * (LOG2_E / GLA_GATE_TAU)

        la_l = jnp.concatenate([la[c * C:(c + 1) * C, :] for c in range(n_chunks)], axis=1)
        tri = gtri_ref[...]
        p1, p2, p3 = _split3(la_l)
        b = _dot(tri, p1) + _dot(tri, p2) + _dot(tri, p3)
        b_ref[i % 2] = b
        yield

        row = lax.broadcasted_iota(jnp.int32, (C, W), 0)
        s = C // 2
        lvl = 0
        while s >= SUBLANES // 2:
            refs = [jnp.broadcast_to(b_ref[i % 2, p + s - 1:p + s, :], (2 * s, W)) for p in range(0, C, 2 * s)]
            b_at_ref = refs[0] if len(refs) == 1 else jnp.concatenate(refs, axis=0)
            dec_ref[i % 2, lvl * C:(lvl + 1) * C, :] = jnp.exp2(-jnp.abs(b - b_at_ref))
            s //= 2
            lvl += 1
            tick()
            yield
        up1 = pltpu.roll(b, 1, axis=0)
        up2 = pltpu.roll(b, 2, axis=0)
        dn1 = pltpu.roll(b, C - 1, axis=0)
        m4 = row % 4
        b_at_ref = jnp.where(m4 == 0, dn1, jnp.where(m4 == 1, b, jnp.where(m4 == 2, up1, up2)))
        dec_ref[i % 2, lvl * C:(lvl + 1) * C, :] = jnp.exp2(-jnp.abs(b - b_at_ref))
        lvl += 1
        tick()
        yield
        b_at_ref = jnp.where(row % 2 == 0, b, up1)
        dec_ref[i % 2, lvl * C:(lvl + 1) * C, :] = jnp.exp2(-jnp.abs(b - b_at_ref))
        lvl += 1
        tick()
        yield
        dec_ref[i % 2, lvl * C:(lvl + 1) * C, :] = jnp.exp2(b)
        tick()
        b_last = jnp.broadcast_to(b_ref[i % 2, C - 1:C, :], (C, W))
        dec_ref[i % 2, (lvl + 1) * C:(lvl + 2) * C, :] = jnp.exp2(b_last - b)
        tick()

    def gla_chunks(i, r0, state):
        C = GLA_CHUNK
        n_chunks = sub // C
        pv = proj_ref.at[i % 2]
        hmk = ghmk_ref[...]
        parts = []
        for c in range(n_chunks):
            rows = slice(r0 + c * C, r0 + (c + 1) * C)
            lrows = slice(c * C, (c + 1) * C)
            lanes = slice(c * KP, (c + 1) * KP)
            q = key_block(pv, lrows, OFF_GQ) * q_scale
            k = key_block(pv, lrows, OFF_GK)
            vb = pv[lrows, OFF_GV:OFF_GV + VW].astype(_BF)
            qb = q.astype(_BF)
            kbd0 = stack_heads(k.astype(_BF)) * hmk
            scores = jnp.zeros((C, N_HEADS * C), _F32)
            for lvl in range(N_GLA_LEVELS + 1):
                if lvl < N_GLA_LEVELS:
                    eb = dec_ref[i % 2, lvl * C:(lvl + 1) * C, lanes].astype(_BF)
                    ql, kbd = qb * eb, kbd0 * stack_heads(eb)
                else:
                    ql, kbd = qb, kbd0
                scores = scores + _dot_nt(ql, kbd) * gmask_ref[lvl]
                tick()
            vbd = stack_heads(vb) * ghmv_ref[...]
            e_cum = dec_ref[i % 2, N_GLA_LEVELS * C:(N_GLA_LEVELS + 1) * C, lanes]
            e_rev = dec_ref[i % 2, (N_GLA_LEVELS + 1) * C:(N_GLA_LEVELS + 2) * C, lanes]
            o_intra = _dot(scores.astype(_BF), vbd)
            upd = _dot_tn(vb, (k * e_rev).astype(_BF))
            parts.append((rows, (q * e_cum).astype(_BF), o_intra, upd, e_cum[C - 1:C, :]))
        for rows, qe, o_intra, upd, e_last in parts:
            oscr_ref[rows, VW:2 * VW] = o_intra + _dot_nt(qe, own_head(state))
            state = advance_state(state, e_last, upd)
            tick()
        return state

    states = {"ret": load_state(sret_ref)}
    gla_state = load_state(sgla_ref)
    for i in range(n_sub):
        r0 = i * sub
        srows = slice(r0, r0 + sub)
        if i + 1 < n_sub:
            pending.extend(projection_tasks(i + 1))
        tick()
        pooling(i)
        tick()
        stages = [retention(i, r0), gla_decay(i)]
        while stages:
            for stage in list(stages):
                if next(stage, stages) is stages:
                    stages.remove(stage)
        gla_state = gla_chunks(i, r0, gla_state)
        y_ret = _silu(proj_ref[i % 2, :, OFF_RG:OFF_RG + VW]) * head_rms(oscr_ref[srows, 0:VW])
        ycat_ref[srows, POOL_WIDTH:POOL_WIDTH + VW] = y_ret.astype(_BF)
        gg_off = OFF_GG % LANES
        gg = pltpu.roll(proj_ref[i % 2, :, OFF_GG - gg_off:N_IN], N_IN - OFF_GG, axis=1)[:, 0:VW]
        y_gla = (_silu(gg)
                 * (head_rms(oscr_ref[srows, VW:2 * VW]) * gng_ref[...]))
        require(i + 1, N_IN)
        ycat_ref[srows, POOL_WIDTH + VW:POOL_WIDTH + 2 * VW] = y_gla.astype(_BF)
        o_ref[0, srows, :] = x_ref[0, srows, :] + gate1 * _dot(ycat_ref[srows, :], wout_ref[...])
    store_state(sret_ref, states["ret"])
    store_state(sgla_ref, gla_state)

    wup_bf_ref[...] = wup32_ref[...].astype(_BF)


def _whole(shape):
    nd = len(shape)
    return pl.BlockSpec(shape, lambda b, t: (0,) * nd, pipeline_mode=pl.Buffered(1))


def _layer_slab(shape, layer):
    nd = len(shape) - 1
    return pl.BlockSpec((None,) + tuple(shape[1:]), lambda b, t: (layer,) + (0,) * nd,
                        pipeline_mode=pl.Buffered(1))


def _mixer(x, mod, layer, params, w_up, cos, sin, tb):
    B, T, D = x.shape
    tile = MIX_TILE if T % MIX_TILE == 0 else T
    sub = MIX_SUB if tile % MIX_SUB == 0 else tile
    consts = [tb[k] for k in ("pool_win", "rot_first_half", "ret_decay", "ret_xi", "ret_zeta", "ret_gc",
                              "ret_hmk", "ret_hmv", "state_mask", "head_ind", "gla_tri", "gla_mask",
                              "gla_hmk", "gla_hmv")]
    norm_g, win = params[0], params[1]
    rest = list(params[2:])
    n_sub = tile // sub
    n_tiles = T // tile
    up_rows, up_cols = w_up.shape[1], w_up.shape[2]
    chunk = up_rows // (B * n_tiles)
    assert chunk * B * n_tiles == up_rows and chunk % (2 * SUBLANES) == 0
    args = [x, mod, norm_g, win, cos, sin] + rest + consts + [w_up]
    in_specs = [
        pl.BlockSpec((1, tile, D), lambda b, t: (b, t, 0)),
        pl.BlockSpec((None, 1, 6, D), lambda b, t: (layer, b, 0, 0)),
        _layer_slab(norm_g.shape, layer),
        _layer_slab(win.shape, layer),
        pl.BlockSpec((1, tile, KP), lambda b, t: (b, t, 0)),
        pl.BlockSpec((1, tile, KP), lambda b, t: (b, t, 0)),
    ] + [_layer_slab(a.shape, layer) for a in rest] + [_whole(a.shape) for a in consts] + [
        pl.BlockSpec((None, chunk, up_cols), lambda b, t: (layer, b * n_tiles + t, 0))]
    n = sub + POOL_HIST
    gla_lanes = (sub // GLA_CHUNK) * KP
    return pl.pallas_call(
        functools.partial(_mixer_kernel, tile=tile, sub=sub),
        grid=(B, T // tile),
        in_specs=in_specs,
        out_specs=[pl.BlockSpec((1, tile, D), lambda b, t: (b, t, 0)),
                   pl.BlockSpec((chunk, up_cols), lambda b, t: (b * n_tiles + t, 0))],
        out_shape=[jax.ShapeDtypeStruct((B, T, D), _F32),
                   jax.ShapeDtypeStruct((up_rows, up_cols), _BF)],
        scratch_shapes=[
            pltpu.VMEM((D, N_IN), _BF),
            pltpu.VMEM((POOL_WIDTH + 2 * VW, D), _BF),
            pltpu.VMEM((2, sub, D), _BF),
            pltpu.VMEM((2, sub, N_IN), _F32),
            pltpu.VMEM((tile, POOL_WIDTH + 2 * VW), _BF),
            pltpu.VMEM((2, GLA_CHUNK, gla_lanes), _F32),
            pltpu.VMEM((2, (N_GLA_LEVELS + 2) * GLA_CHUNK, gla_lanes), _F32),
            pltpu.VMEM((tile, 2 * VW), _F32),
            pltpu.VMEM((n, POOL_WIDTH), _F32),
            pltpu.VMEM((n, POOL_WIDTH), _F32),
            pltpu.VMEM((n, POOL_WIDTH), _F32),
            pltpu.VMEM((n, POOL_WIDTH), _F32),
            pltpu.VMEM((POOL_HIST, POOL_WIDTH), _F32),
            pltpu.VMEM((VW, KP), _F32),
            pltpu.VMEM((VW, KP), _F32),
        ],
        compiler_params=pltpu.CompilerParams(
            dimension_semantics=("arbitrary", "arbitrary"), vmem_limit_bytes=VMEM_LIMIT_BYTES),
        name="token_mixer",
    )(*args)


def _ffn_kernel(x_ref, mod_ref, ng_ref, wup_ref, cw_ref, wd32_ref, fg_ref,
                o_ref, wd_ref, h_ref, carry_ref, act_ref,
                *, tile, sub, d_ff, fb, final_norm):
    t_idx = pl.program_id(1)

    @pl.when((pl.program_id(0) == 0) & (t_idx == 0))
    def _():
        _cast_weight(wd32_ref, wd_ref)

    @pl.when(t_idx == 0)
    def _():
        carry_ref[...] = jnp.zeros_like(carry_ref)

    for base in range(0, tile, sub):
        _ffn_sub_tile(x_ref, mod_ref, ng_ref, wup_ref, cw_ref, wd_ref, fg_ref, o_ref, h_ref,
                      carry_ref, act_ref, base=base, tile=sub, d_ff=d_ff, fb=fb,
                      final_norm=final_norm)


def _ffn_sub_tile(x_ref, mod_ref, ng_ref, wup_ref, cw_ref, wd_ref, fg_ref, o_ref, h_ref, carry_ref,
                  act_ref, *, base, tile, d_ff, fb, final_norm):
    S = SUBLANES
    R = tile // S
    D = x_ref.shape[-1]
    shift2 = mod_ref[0, 3:4, :]
    scale2 = mod_ref[0, 4:5, :]
    gate2 = mod_ref[0, 5:6, :]
    norm_scale = ng_ref[...] * (1.0 + scale2)
    groups = FFN_NORM_ROWS // S
    early_cols = ([j * fb for j in range(FFN_EARLY_BLOCKS)]
                  + [d_ff + j * fb for j in range(FFN_EARLY_BLOCKS)])
    early = {col: [] for col in early_cols}
    x_parts = []
    for p in range(tile // FFN_NORM_ROWS):
        prow = slice(p * FFN_NORM_ROWS, (p + 1) * FFN_NORM_ROWS)
        xs = jnp.concatenate([x_ref[0, base + s * R + p * groups:base + s * R + (p + 1) * groups, :]
                              for s in range(S)], axis=0)
        xs = xs.reshape(S, groups, D).swapaxes(0, 1).reshape(FFN_NORM_ROWS, D)
        ms = jnp.mean(xs * xs, axis=-1, keepdims=True)
        h_ref[prow, :] = ((xs * lax.rsqrt(ms + EPS)) * norm_scale + shift2).astype(_BF)
        x_parts.append(xs)
        for col in early_cols:
            early[col].append(_dot(h_ref[prow, :], wup_ref[:, col:col + fb]))
    x = jnp.concatenate(x_parts, axis=0)
    first_sublane = lax.broadcasted_iota(jnp.int32, (S, fb), 0) == 0

    def up_conv(col):
        cols = slice(col, col + fb)
        if col in early:
            u = jnp.concatenate(early[col], axis=0)
        else:
            u = _dot(h_ref[...], wup_ref[:, cols])
        back1 = jnp.where(first_sublane, pltpu.roll(carry_ref[S:2 * S, cols], 1, axis=0),
                          pltpu.roll(u[tile - S:tile, :], 1, axis=0))
        back2 = jnp.where(first_sublane, pltpu.roll(carry_ref[0:S, cols], 1, axis=0),
                          pltpu.roll(u[tile - 2 * S:tile - S, :], 1, axis=0))
        carry_ref[:, cols] = u[tile - 2 * S:tile, :]
        prev1 = jnp.concatenate([back1, u[0:tile - S, :]], axis=0)
        prev2 = jnp.concatenate([back2, back1, u[0:tile - 2 * S, :]], axis=0)
        cw = cw_ref[:, cols]
        return prev2 * cw[0:1, :] + prev1 * cw[1:2, :] + u * cw[2:3, :] + cw[3:4, :]

    for j in range(d_ff // fb):
        ya = up_conv(j * fb)
        yg = up_conv(d_ff + j * fb)
        act_ref[:, j * fb:(j + 1) * fb] = (_silu(yg) * ya).astype(_BF)

    out = x + gate2 * _dot(act_ref[...], wd_ref[...])
    if final_norm:
        ms_o = jnp.mean(out * out, axis=-1, keepdims=True)
        out = out * lax.rsqrt(ms_o + EPS) * fg_ref[...]
    o_ref[0, base:base + tile, :] = out.reshape(R, S, D).swapaxes(0, 1).reshape(tile, D)


def _ffn(x, mod, layer, params, final_g, final_norm):
    B, T, D = x.shape
    tile = FFN_TILE if T % FFN_TILE == 0 else T
    sub = FFN_SUB if tile % FFN_SUB == 0 else tile
    norm_g, wup, cw, wd = params
    d_ff = wd.shape[1]
    in_specs = [
        pl.BlockSpec((1, tile, D), lambda b, t: (b, t, 0)),
        pl.BlockSpec((None, 1, 6, D), lambda b, t: (layer, b, 0, 0)),
        _layer_slab(norm_g.shape, layer),
        _whole(wup.shape),
        _layer_slab(cw.shape, layer),
        _layer_slab(wd.shape, layer),
        _whole(final_g.shape),
    ]
    return pl.pallas_call(
        functools.partial(_ffn_kernel, tile=tile, sub=sub, d_ff=d_ff, fb=FFN_BLOCK, final_norm=final_norm),
        grid=(B, T // tile),
        in_specs=in_specs,
        out_specs=pl.BlockSpec((1, tile, D), lambda b, t: (b, t, 0)),
        out_shape=jax.ShapeDtypeStruct((B, T, D), _F32),
        scratch_shapes=[
            pltpu.VMEM((d_ff, D), _BF),
            pltpu.VMEM((sub, D), _BF),
            pltpu.VMEM((2 * SUBLANES, 2 * d_ff), _F32),
            pltpu.VMEM((sub, d_ff), _BF),
        ],
        compiler_params=pltpu.CompilerParams(
            dimension_semantics=("arbitrary", "arbitrary"), vmem_limit_bytes=VMEM_LIMIT_BYTES),
        name="conv_ffn",
    )(x, mod, *params, final_g)


def _prep_mixer_params(norm1_g, w_in, pool_w, pool_scale, gla_wa2, gla_ba, gla_norm_g, w_out):
    L, D, _ = w_in.shape
    G = pool_w.shape[1]
    eye = jnp.eye(G, dtype=pool_w.dtype)
    poolw = (pool_w[:, :, :, None, :] * eye[None, :, None, :, None]).reshape(L, POOL_WIDTH, POOL_WIDTH)
    wa2 = jnp.pad(gla_wa2, ((0, 0), (0, LANES - GLA_GATE_RANK), (0, KP - KW)))
    ba = jnp.pad(gla_ba, ((0, 0), (0, KP - KW))).reshape(L, 1, KP)
    w_in = jnp.swapaxes(w_in, 1, 2)
    return (norm1_g.reshape(L, 1, D), w_in, poolw.astype(_BF), pool_scale.reshape(L, 1, POOL_WIDTH),
            wa2.astype(_BF), ba, gla_norm_g.reshape(L, 1, VW), w_out)


def _prep_ffn_params(norm2_g, w_up, conv_w, conv_b, w_down):
    L, D, _ = w_up.shape
    cw = jnp.pad(jnp.concatenate([conv_w, conv_b[:, None, :]], axis=1),
                 ((0, 0), (0, SUBLANES - CONV_WIDTH - 1), (0, 0)))
    return norm2_g.reshape(L, 1, D), cw, w_down


def kernel(x, c, positions, ada_w, ada_b, norm1_g, w_in, pool_w, pool_scale, gla_wa2, gla_ba,
           gla_norm_g, w_out, norm2_g, w_up, conv_w, conv_b, w_down, final_g):
    L = ada_w.shape[0]
    B, T, D = x.shape
    tb = {k: jnp.asarray(v) for k, v in _tables().items()}
    for name in ("head_ind", "gla_tri", "ret_hmk", "ret_hmv", "gla_hmk", "gla_hmv", "state_mask",
                 "rot_cos_expand", "rot_sin_expand"):
        tb[name] = tb[name].astype(_BF)
    mod = _modulation(c, ada_w, ada_b).reshape(L, B, 6, D)
    cos, sin = _rotary_tables(positions, tb)
    mixer_params = _prep_mixer_params(norm1_g, w_in, pool_w, pool_scale, gla_wa2, gla_ba, gla_norm_g, w_out)
    norm2, conv_rows, w_down_f32 = _prep_ffn_params(norm2_g, w_up, conv_w, conv_b, w_down)
    fg = final_g.reshape(1, D)
    for l in range(L):
        x, w_up_bf16 = _mixer(x, mod, l, mixer_params, w_up, cos, sin, tb)
        x = _ffn(x, mod, l, (norm2, w_up_bf16, conv_rows, w_down_f32), fg, final_norm=(l == L - 1))
    return x
```

```python
import functools

import numpy as np
import jax
import jax.numpy as jnp
from jax import lax
from jax.experimental import pallas as pl
from jax.experimental.pallas import tpu as pltpu

POOL_WIDTH = 256
POOL_WINDOWS = (2, 4, 8, 16)
POOL_GROUP_DIM = 64
N_HEADS = 4
DK = 48
DV = 96
KW = N_HEADS * DK
VW = N_HEADS * DV
RET_CHUNK = 128
GLA_CHUNK = 64
GLA_GATE_RANK = 16
GLA_GATE_TAU = 16.0
ROPE_BASE = 10000.0
CONV_WIDTH = 3
EPS = 1e-6

LANES = 128
SUBLANES = 8
KP = 256
VMEM_LIMIT_BYTES = 56 * 1024 * 1024

OFF_POOL = 0
OFF_RQ = OFF_POOL + POOL_WIDTH
OFF_RK = OFF_RQ + KW
OFF_RV = OFF_RK + KW
OFF_RG = OFF_RV + VW
OFF_GQ = OFF_RG + VW
OFF_GK = OFF_GQ + KW
OFF_GV = OFF_GK + KW
OFF_GA = OFF_GV + VW
OFF_GG = OFF_GA + GLA_GATE_RANK
D_IN = OFF_GG + VW
N_IN = -(-D_IN // LANES) * LANES

MIX_TILE = 512
MIX_SUB = 256
PROJ_PIECE = 256
PROJ_TICKS = 3
FFN_TILE = 512
FFN_SUB = 512
FFN_BLOCK = 256
FFN_NORM_ROWS = 128
FFN_EARLY_BLOCKS = 2
ROT_TILE = 1024
ROT_ROWS = 32
POOL_HIST = 32
N_GLA_LEVELS = 6
CAST_ROWS = 128
LOG2_E = 1.4426950408889634

_BF = jnp.bfloat16
_F32 = jnp.float32


def _dot(a, b):
    return jnp.dot(a, b, preferred_element_type=_F32)


def _dot_nt(a, b):
    return lax.dot_general(a, b, (((1,), (1,)), ((), ())), preferred_element_type=_F32)


def _dot_tn(a, b):
    return lax.dot_general(a, b, (((0,), (0,)), ((), ())), preferred_element_type=_F32)


def _split3(x):
    p1 = x.astype(_BF)
    r1 = x - p1.astype(_F32)
    p2 = r1.astype(_BF)
    p3 = (r1 - p2.astype(_F32)).astype(_BF)
    return p1, p2, p3


def _cast_weight(src_ref, dst_ref):
    for r in range(0, src_ref.shape[0], CAST_ROWS):
        rows = slice(r, r + CAST_ROWS)
        dst_ref[rows, :] = src_ref[rows, :].astype(_BF)


def _cast_weight_transposed(src_ref, dst_ref):
    n, k = src_ref.shape
    for c in range(0, dst_ref.shape[1], CAST_ROWS):
        rows = min(CAST_ROWS, max(n - c, 0))
        block = src_ref[c:c + rows, :] if rows else None
        if rows < CAST_ROWS:
            zeros = jnp.zeros((CAST_ROWS - rows, k), _F32)
            block = zeros if block is None else jnp.concatenate([block, zeros], axis=0)
        dst_ref[:, c:c + CAST_ROWS] = block.T.astype(_BF)


def _sigmoid(x):
    return 1.0 / (1.0 + jnp.exp(-x))


def _silu(x):
    return x * _sigmoid(x)


def _key_head(d):
    return np.where(d < KW, d // DK, -1)


@functools.lru_cache(maxsize=None)
def _tables():
    t = {}
    kd = _key_head(np.arange(KP))
    vd = np.arange(VW) // DV

    def head_masks(chunk):
        rows = np.repeat(np.arange(N_HEADS), chunk)
        return ((rows[:, None] == kd[None, :]).astype(np.float32),
                (rows[:, None] == vd[None, :]).astype(np.float32))

    C = RET_CHUNK
    lg = np.log(1.0 - 2.0 ** (-5.0 - np.arange(N_HEADS, dtype=np.float64)))
    i = np.arange(C)
    rel = i[:, None] - i[None, :]
    dec = np.where(rel[None] >= 0, np.exp(np.maximum(rel, 0)[None] * lg[:, None, None]), 0.0)
    t["ret_decay"] = np.transpose(dec, (1, 0, 2)).reshape(C, N_HEADS * C).astype(np.float32)
    t["ret_xi"] = np.exp((i[:, None] + 1.0) * lg[vd][None, :]).astype(np.float32)
    zeta = np.exp((C - 1.0 - i)[:, None] * lg[np.maximum(kd, 0)][None, :]) * (kd >= 0)[None, :]
    t["ret_zeta"] = zeta.astype(np.float32)
    t["ret_gc"] = (np.exp(C * lg[np.maximum(kd, 0)]) * (kd >= 0))[None, :].astype(np.float32)
    t["ret_hmk"], t["ret_hmv"] = head_masks(C)
    t["state_mask"] = (vd[:, None] == kd[None, :]).astype(np.float32)
    t["head_ind"] = (vd[:, None] == vd[None, :]).astype(np.float32)

    C = GLA_CHUNK
    i = np.arange(C)
    masks = []
    s = C // 2
    while s >= 1:
        parent = (i // (2 * s)) * (2 * s)
        upper = (i - parent) >= s
        masks.append(upper[:, None] & (~upper)[None, :] & (parent[:, None] == parent[None, :]))
        s //= 2
    masks.append(i[:, None] == i[None, :])
    t["gla_mask"] = np.stack([np.tile(m, (1, N_HEADS)) for m in masks]).astype(np.float32)
    t["gla_tri"] = (i[None, :] <= i[:, None]).astype(np.float32)
    t["gla_hmk"], t["gla_hmv"] = head_masks(C)

    t["pool_win"] = np.repeat(np.asarray(POOL_WINDOWS, np.float32), POOL_GROUP_DIM)[None, :]

    half = DK // 2
    l = np.arange(KP)
    hit = (np.arange(ROT_ROWS)[:, None] == (l % half)[None, :]) & (l < KW)[None, :]
    t["rot_cos_expand"] = hit.astype(np.float32)
    t["rot_sin_expand"] = hit * np.where((l % DK) < half, -1.0, 1.0)[None, :].astype(np.float32)
    t["rot_first_half"] = ((l % DK) < half).astype(np.float32)[None, :]
    return t


def _mod_kernel(c_ref, w_ref, b_ref, o_ref):
    ca = _silu(c_ref[...]).astype(_BF)
    o_ref[...] = _dot(ca, w_ref[...].astype(_BF)) + b_ref[...]


def _modulation(c, ada_w, ada_b):
    L, D, N = ada_w.shape
    B = c.shape[0]
    tn = D
    return pl.pallas_call(
        _mod_kernel,
        grid=(L, N // tn),
        in_specs=[
            pl.BlockSpec((B, D), lambda l, n: (0, 0)),
            pl.BlockSpec((None, D, tn), lambda l, n: (l, 0, n)),
            pl.BlockSpec((None, 1, tn), lambda l, n: (l, 0, n)),
        ],
        out_specs=pl.BlockSpec((None, B, tn), lambda l, n: (l, 0, n)),
        out_shape=jax.ShapeDtypeStruct((L, B, N), _F32),
        compiler_params=pltpu.CompilerParams(
            dimension_semantics=("arbitrary", "arbitrary"), vmem_limit_bytes=VMEM_LIMIT_BYTES),
        name="adaln_modulation",
    )(c, ada_w, ada_b.reshape(L, 1, N))


def _rot_kernel(pos_ref, freq_ref, ec_ref, es_ref, cos_ref, sin_ref):
    ang = freq_ref[...] * pos_ref[0]
    ec = ec_ref[...]
    es = es_ref[...]

    def widen(v, e):
        pieces = jnp.concatenate(_split3(v), axis=0)
        return _dot_tn(pieces, jnp.concatenate([e] * 3, axis=0))

    cos_ref[0] = widen(jnp.cos(ang), ec)
    sin_ref[0] = widen(jnp.sin(ang), es)


def _rotary_tables(positions, tb):
    B, T = positions.shape
    tile = ROT_TILE if T % ROT_TILE == 0 else T
    inv_freq = ROPE_BASE ** (-jnp.arange(0, DK, 2, dtype=_F32) / DK)
    freq = jnp.pad(inv_freq, (0, ROT_ROWS - DK // 2)).reshape(ROT_ROWS, 1)
    pos = positions.astype(_F32).reshape(B, 1, T)
    const = lambda b, t: (0, 0)
    return pl.pallas_call(
        _rot_kernel,
        grid=(B, T // tile),
        in_specs=[
            pl.BlockSpec((1, 1, tile), lambda b, t: (b, 0, t)),
            pl.BlockSpec((ROT_ROWS, 1), const),
            pl.BlockSpec((ROT_ROWS, KP), const),
            pl.BlockSpec((ROT_ROWS, KP), const),
        ],
        out_specs=[pl.BlockSpec((1, tile, KP), lambda b, t: (b, t, 0))] * 2,
        out_shape=[jax.ShapeDtypeStruct((B, T, KP), _F32)] * 2,
        compiler_params=pltpu.CompilerParams(dimension_semantics=("arbitrary", "arbitrary")),
        name="rotary_table",
    )(pos, freq, tb["rot_cos_expand"], tb["rot_sin_expand"])


def _mixer_kernel(x_ref, mod_ref, ng_ref, win32_ref, cos_ref, sin_ref,
                  poolw_ref, pools_ref, wa2_ref, ba_ref, gng_ref, wout32_ref,
                  poolwin_ref, rothalf_ref,
                  rdecay_ref, rxi_ref, rzeta_ref, rgc_ref, rhmk_ref, rhmv_ref,
                  smask_ref, hind_ref,
                  gtri_ref, gmask_ref, ghmk_ref, ghmv_ref, wup32_ref,
                  o_ref, wup_bf_ref,
                  win_ref, wout_ref,
                  hb_ref, proj_ref, ycat_ref, b_ref, dec_ref, oscr_ref, ext_ref, s2_ref, s4_ref, s8_ref,
                  hist_ref, sret_ref, sgla_ref, *, tile, sub, layer):
    lrow = slice(layer, layer + 1)
    t_idx = pl.program_id(1)
    n_sub = tile // sub

    @pl.when((pl.program_id(0) == 0) & (t_idx == 0))
    def _():
        _cast_weight_transposed(win32_ref, win_ref)
        _cast_weight(wout32_ref, wout_ref)

    @pl.when(t_idx == 0)
    def _():
        hist_ref[...] = jnp.zeros_like(hist_ref)
        sret_ref[...] = jnp.zeros_like(sret_ref)
        sgla_ref[...] = jnp.zeros_like(sgla_ref)

    gate1 = mod_ref[0, 2:3, :]

    norm_scale = ng_ref[lrow, :] * (1.0 + mod_ref[0, 1:2, :])
    shift1 = mod_ref[0, 0:1, :]

    def projection_tasks(i):
        slot = i % 2

        def norm():
            x = x_ref[0, i * sub:(i + 1) * sub, :]
            ms = jnp.mean(x * x, axis=-1, keepdims=True)
            hb_ref[slot] = ((x * lax.rsqrt(ms + EPS)) * norm_scale + shift1).astype(_BF)

        def segment(off, width):
            def run():
                proj_ref[slot, :, off:off + width] = _dot(hb_ref[slot], win_ref[:, off:off + width])
            return run

        tasks = [(i, 0, norm)]
        for off in range(0, N_IN, PROJ_PIECE):
            width = min(PROJ_PIECE, N_IN - off)
            tasks.append((i, off + width, segment(off, width)))
        return tasks

    def key_block(pv, lrows, off):
        shift = off % LANES
        if shift == 0:
            return pv[lrows, off:off + KP]
        return pltpu.roll(pv[lrows, off - shift:off - shift + KP], KP - shift, axis=1)

    pending = projection_tasks(0)
    ticks = [0]

    def tick(weight=1):
        ticks[0] += weight
        if pending and ticks[0] >= PROJ_TICKS:
            ticks[0] = 0
            pending.pop(0)[2]()

    def require(i, column):
        while pending and (pending[0][0], pending[0][1]) <= (i, column):
            pending.pop(0)[2]()

    require(0, N_IN)

    def pooling(i):
        n = sub + POOL_HIST
        srows = slice(i * sub, (i + 1) * sub)
        u = proj_ref[i % 2, :, OFF_POOL:OFF_POOL + POOL_WIDTH]
        ext_ref[0:POOL_HIST, :] = hist_ref[...]
        ext_ref[POOL_HIST:n, :] = u
        hist_ref[...] = u[sub - POOL_HIST:sub, :]
        s2_ref[8:n, :] = ext_ref[8:n, :] + ext_ref[7:n - 1, :]
        s4_ref[16:n, :] = s2_ref[16:n, :] + s2_ref[14:n - 2, :]
        s8_ref[24:n, :] = s4_ref[24:n, :] + s4_ref[20:n - 4, :]
        s16 = s8_ref[32:n, :] + s8_ref[24:n - 8, :]
        lane = lax.broadcasted_iota(jnp.int32, (sub, POOL_WIDTH), 1)
        wsum = jnp.where(lane < POOL_GROUP_DIM, s2_ref[32:n, :],
                         jnp.where(lane < 2 * POOL_GROUP_DIM, s4_ref[32:n, :],
                                   jnp.where(lane < 3 * POOL_GROUP_DIM, s8_ref[32:n, :], s16)))
        t_abs = (t_idx * tile + i * sub
                 + lax.broadcasted_iota(jnp.int32, (sub, POOL_WIDTH), 0)).astype(_F32)
        cnt = jnp.minimum(t_abs + 1.0, poolwin_ref[...])
        pooled = wsum / cnt - u
        ycat_ref[srows, 0:POOL_WIDTH] = (_dot(pooled.astype(_BF), poolw_ref[...])
                                         * pools_ref[lrow, :]).astype(_BF)

    inv_dv = 1.0 / DV
    k_scale = DK ** -0.5
    q_scale = DK ** -0.5
    smask = smask_ref[...]

    state_blocks = []
    for h in range(N_HEADS):
        lane0 = (h * DK) // LANES * LANES
        lane1 = -(-((h + 1) * DK) // LANES) * LANES
        state_blocks.append((slice(h * DV, (h + 1) * DV), slice(lane0, lane1)))

    def load_state(ref):
        return [ref[rows, lanes] for rows, lanes in state_blocks]

    def store_state(ref, blocks):
        for (rows, lanes), blk in zip(state_blocks, blocks):
            ref[rows, lanes] = blk

    def advance_state(blocks, decay, upd):
        return [blk * decay[:, lanes] + upd[rows, lanes]
                for (rows, lanes), blk in zip(state_blocks, blocks)]

    def own_head(blocks):
        out = []
        for (rows, lanes), blk in zip(state_blocks, blocks):
            part = blk.astype(_BF) * smask[rows, lanes]
            left = lanes.start
            right = KP - lanes.stop
            pieces = ([jnp.zeros((DV, left), _BF)] if left else []) + [part] \
                + ([jnp.zeros((DV, right), _BF)] if right else [])
            out.append(pieces[0] if len(pieces) == 1 else jnp.concatenate(pieces, axis=1))
        return jnp.concatenate(out, axis=0)
    hind = hind_ref[...]
    first_half = rothalf_ref[...] > 0.5

    def head_rms(o):
        ms_h = _dot((o * o).astype(_BF), hind) * inv_dv
        return o * lax.rsqrt(ms_h + EPS)

    def stack_heads(a):
        return jnp.concatenate([a] * N_HEADS, axis=0)

    def rotary(t, cosv, sinv):
        partner = jnp.where(first_half, pltpu.roll(t, KP - DK // 2, axis=1), pltpu.roll(t, DK // 2, axis=1))
        return t * cosv + partner * sinv

    def retention(i, r0):
        C = RET_CHUNK
        pv = proj_ref.at[i % 2]
        parts = []
        for c in range(sub // C):
            rows = slice(r0 + c * C, r0 + (c + 1) * C)
            lrows = slice(c * C, (c + 1) * C)
            cosv = cos_ref[0, rows, :]
            sinv = sin_ref[0, rows, :]
            q = rotary(key_block(pv, lrows, OFF_RQ), cosv, sinv)
            k = rotary(key_block(pv, lrows, OFF_RK), cosv, sinv) * k_scale
            vb = pv[lrows, OFF_RV:OFF_RV + VW].astype(_BF)
            qb = q.astype(_BF)
            kbd = stack_heads(k.astype(_BF)) * rhmk_ref[...]
            scores = _dot_nt(qb, kbd) * rdecay_ref[...]
            vbd = stack_heads(vb) * rhmv_ref[...]
            o_intra = _dot(scores.astype(_BF), vbd)
            upd = _dot_tn(vb, (k * rzeta_ref[...]).astype(_BF))
            parts.append((rows, qb, o_intra, upd))
            tick()
            yield
        state = states["ret"]
        for rows, qb, o_intra, upd in parts:
            oscr_ref[rows, 0:VW] = o_intra + _dot_nt(qb, own_head(state)) * rxi_ref[...]
            state = advance_state(state, rgc_ref[...], upd)
            tick()
            yield
        states["ret"] = state

    def gla_decay(i):
        C = GLA_CHUNK
        n_chunks = sub // C
        W = n_chunks * KP
        pv = proj_ref.at[i % 2]
        logits = _dot(pv[:, OFF_GA:OFF_GA + LANES].astype(_BF), wa2_ref[...]) + ba_ref[lrow, :]
        log_sig = jnp.minimum(logits, 0.0) - jnp.log(1.0 + jnp.exp(-jnp.abs(logits)))
        la = log_sig * (LOG2_E / GLA_GATE_TAU)

        la_l = jnp.concatenate([la[c * C:(c + 1) * C, :] for c in range(n_chunks)], axis=1)
        tri = gtri_ref[...]
        p1, p2, p3 = _split3(la_l)
        b = _dot(tri, p1) + _dot(tri, p2) + _dot(tri, p3)
        b_ref[i % 2] = b
        yield

        row = lax.broadcasted_iota(jnp.int32, (C, W), 0)
        s = C // 2
        lvl = 0
        while s >= SUBLANES // 2:
            refs = [jnp.broadcast_to(b_ref[i % 2, p + s - 1:p + s, :], (2 * s, W)) for p in range(0, C, 2 * s)]
            b_at_ref = refs[0] if len(refs) == 1 else jnp.concatenate(refs, axis=0)
            dec_ref[i % 2, lvl * C:(lvl + 1) * C, :] = jnp.exp2(-jnp.abs(b - b_at_ref))
            s //= 2
            lvl += 1
            tick()
            yield
        up1 = pltpu.roll(b, 1, axis=0)
        up2 = pltpu.roll(b, 2, axis=0)
        dn1 = pltpu.roll(b, C - 1, axis=0)
        m4 = row % 4
        b_at_ref = jnp.where(m4 == 0, dn1, jnp.where(m4 == 1, b, jnp.where(m4 == 2, up1, up2)))
        dec_ref[i % 2, lvl * C:(lvl + 1) * C, :] = jnp.exp2(-jnp.abs(b - b_at_ref))
        lvl += 1
        tick()
        yield
        b_at_ref = jnp.where(row % 2 == 0, b, up1)
        dec_ref[i % 2, lvl * C:(lvl + 1) * C, :] = jnp.exp2(-jnp.abs(b - b_at_ref))
        lvl += 1
        tick()
        yield
        dec_ref[i % 2, lvl * C:(lvl + 1) * C, :] = jnp.exp2(b)
        tick()
        b_last = jnp.broadcast_to(b_ref[i % 2, C - 1:C, :], (C, W))
        dec_ref[i % 2, (lvl + 1) * C:(lvl + 2) * C, :] = jnp.exp2(b_last - b)
        tick()

    def gla_chunks(i, r0, state):
        C = GLA_CHUNK
        n_chunks = sub // C
        pv = proj_ref.at[i % 2]
        hmk = ghmk_ref[...]
        parts = []
        for c in range(n_chunks):
            rows = slice(r0 + c * C, r0 + (c + 1) * C)
            lrows = slice(c * C, (c + 1) * C)
            lanes = slice(c * KP, (c + 1) * KP)
            q = key_block(pv, lrows, OFF_GQ) * q_scale
            k = key_block(pv, lrows, OFF_GK)
            vb = pv[lrows, OFF_GV:OFF_GV + VW].astype(_BF)
            qb = q.astype(_BF)
            kbd0 = stack_heads(k.astype(_BF)) * hmk
            scores = jnp.zeros((C, N_HEADS * C), _F32)
            for lvl in range(N_GLA_LEVELS + 1):
                if lvl < N_GLA_LEVELS:
                    eb = dec_ref[i % 2, lvl * C:(lvl + 1) * C, lanes].astype(_BF)
                    ql, kbd = qb * eb, kbd0 * stack_heads(eb)
                else:
                    ql, kbd = qb, kbd0
                scores = scores + _dot_nt(ql, kbd) * gmask_ref[lvl]
                tick()
            vbd = stack_heads(vb) * ghmv_ref[...]
            e_cum = dec_ref[i % 2, N_GLA_LEVELS * C:(N_GLA_LEVELS + 1) * C, lanes]
            e_rev = dec_ref[i % 2, (N_GLA_LEVELS + 1) * C:(N_GLA_LEVELS + 2) * C, lanes]
            o_intra = _dot(scores.astype(_BF), vbd)
            upd = _dot_tn(vb, (k * e_rev).astype(_BF))
            parts.append((rows, (q * e_cum).astype(_BF), o_intra, upd, e_cum[C - 1:C, :]))
        for rows, qe, o_intra, upd, e_last in parts:
            oscr_ref[rows, VW:2 * VW] = o_intra + _dot_nt(qe, own_head(state))
            state = advance_state(state, e_last, upd)
            tick()
        return state

    states = {"ret": load_state(sret_ref)}
    gla_state = load_state(sgla_ref)
    for i in range(n_sub):
        r0 = i * sub
        srows = slice(r0, r0 + sub)
        if i + 1 < n_sub:
            pending.extend(projection_tasks(i + 1))
        tick()
        pooling(i)
        tick()
        stages = [retention(i, r0), gla_decay(i)]
        while stages:
            for stage in list(stages):
                if next(stage, stages) is stages:
                    stages.remove(stage)
        gla_state = gla_chunks(i, r0, gla_state)
        y_ret = _silu(proj_ref[i % 2, :, OFF_RG:OFF_RG + VW]) * head_rms(oscr_ref[srows, 0:VW])
        ycat_ref[srows, POOL_WIDTH:POOL_WIDTH + VW] = y_ret.astype(_BF)
        gg_off = OFF_GG % LANES
        gg = pltpu.roll(proj_ref[i % 2, :, OFF_GG - gg_off:N_IN], N_IN - OFF_GG, axis=1)[:, 0:VW]
        y_gla = (_silu(gg)
                 * (head_rms(oscr_ref[srows, VW:2 * VW]) * gng_ref[lrow, :]))
        require(i + 1, N_IN)
        ycat_ref[srows, POOL_WIDTH + VW:POOL_WIDTH + 2 * VW] = y_gla.astype(_BF)
        o_ref[0, srows, :] = x_ref[0, srows, :] + gate1 * _dot(ycat_ref[srows, :], wout_ref[...])
    store_state(sret_ref, states["ret"])
    store_state(sgla_ref, gla_state)

    wup_bf_ref[...] = wup32_ref[...].astype(_BF)


def _whole(shape):
    nd = len(shape)
    return pl.BlockSpec(shape, lambda b, t: (0,) * nd, pipeline_mode=pl.Buffered(1))


def _layer_slab(shape, layer):
    nd = len(shape) - 1
    return pl.BlockSpec((None,) + tuple(shape[1:]), lambda b, t: (layer,) + (0,) * nd,
                        pipeline_mode=pl.Buffered(1))


def _mixer(x, mod, layer, params, w_up, cos, sin, tb):
    B, T, D = x.shape
    tile = MIX_TILE if T % MIX_TILE == 0 else T
    sub = MIX_SUB if tile % MIX_SUB == 0 else tile
    consts = [tb[k] for k in ("pool_win", "rot_first_half", "ret_decay", "ret_xi", "ret_zeta", "ret_gc",
                              "ret_hmk", "ret_hmv", "state_mask", "head_ind", "gla_tri", "gla_mask",
                              "gla_hmk", "gla_hmv")]
    norm_g, win = params[0], params[1]
    rest = list(params[2:])
    n_sub = tile // sub
    n_tiles = T // tile
    up_rows, up_cols = w_up.shape[1], w_up.shape[2]
    chunk = up_rows // (B * n_tiles)
    assert chunk * B * n_tiles == up_rows and chunk % (2 * SUBLANES) == 0
    args = [x, mod, norm_g, win, cos, sin] + rest + consts + [w_up]
    in_specs = [
        pl.BlockSpec((1, tile, D), lambda b, t: (b, t, 0)),
        pl.BlockSpec((None, 1, 6, D), lambda b, t: (layer, b, 0, 0)),
        _whole(norm_g.shape),
        _layer_slab(win.shape, layer),
        pl.BlockSpec((1, tile, KP), lambda b, t: (b, t, 0)),
        pl.BlockSpec((1, tile, KP), lambda b, t: (b, t, 0)),
    ] + [_whole(a.shape) if a.ndim == 2 else _layer_slab(a.shape, layer) for a in rest] + [
        _whole(a.shape) for a in consts] + [
        pl.BlockSpec((None, chunk, up_cols), lambda b, t: (layer, b * n_tiles + t, 0))]
    n = sub + POOL_HIST
    gla_lanes = (sub // GLA_CHUNK) * KP
    return pl.pallas_call(
        functools.partial(_mixer_kernel, tile=tile, sub=sub, layer=layer),
        grid=(B, T // tile),
        in_specs=in_specs,
        out_specs=[pl.BlockSpec((1, tile, D), lambda b, t: (b, t, 0)),
                   pl.BlockSpec((chunk, up_cols), lambda b, t: (b * n_tiles + t, 0))],
        out_shape=[jax.ShapeDtypeStruct((B, T, D), _F32),
                   jax.ShapeDtypeStruct((up_rows, up_cols), _BF)],
        scratch_shapes=[
            pltpu.VMEM((D, N_IN), _BF),
            pltpu.VMEM((POOL_WIDTH + 2 * VW, D), _BF),
            pltpu.VMEM((2, sub, D), _BF),
            pltpu.VMEM((2, sub, N_IN), _F32),
            pltpu.VMEM((tile, POOL_WIDTH + 2 * VW), _BF),
            pltpu.VMEM((2, GLA_CHUNK, gla_lanes), _F32),
            pltpu.VMEM((2, (N_GLA_LEVELS + 2) * GLA_CHUNK, gla_lanes), _F32),
            pltpu.VMEM((tile, 2 * VW), _F32),
            pltpu.VMEM((n, POOL_WIDTH), _F32),
            pltpu.VMEM((n, POOL_WIDTH), _F32),
            pltpu.VMEM((n, POOL_WIDTH), _F32),
            pltpu.VMEM((n, POOL_WIDTH), _F32),
            pltpu.VMEM((POOL_HIST, POOL_WIDTH), _F32),
            pltpu.VMEM((VW, KP), _F32),
            pltpu.VMEM((VW, KP), _F32),
        ],
        compiler_params=pltpu.CompilerParams(
            dimension_semantics=("arbitrary", "arbitrary"), vmem_limit_bytes=VMEM_LIMIT_BYTES),
        name="token_mixer",
    )(*args)


def _ffn_kernel(x_ref, mod_ref, ng_ref, wup_ref, cw_ref, wd32_ref, fg_ref,
                o_ref, wd_ref, h_ref, carry_ref, act_ref,
                *, tile, sub, d_ff, fb, final_norm, layer):
    t_idx = pl.program_id(1)

    @pl.when((pl.program_id(0) == 0) & (t_idx == 0))
    def _():
        _cast_weight(wd32_ref, wd_ref)

    @pl.when(t_idx == 0)
    def _():
        carry_ref[...] = jnp.zeros_like(carry_ref)

    for base in range(0, tile, sub):
        _ffn_sub_tile(x_ref, mod_ref, ng_ref, wup_ref, cw_ref, wd_ref, fg_ref, o_ref, h_ref,
                      carry_ref, act_ref, base=base, tile=sub, d_ff=d_ff, fb=fb,
                      final_norm=final_norm, layer=layer)


def _ffn_sub_tile(x_ref, mod_ref, ng_ref, wup_ref, cw_ref, wd_ref, fg_ref, o_ref, h_ref, carry_ref,
                  act_ref, *, base, tile, d_ff, fb, final_norm, layer):
    S = SUBLANES
    R = tile // S
    D = x_ref.shape[-1]
    shift2 = mod_ref[0, 3:4, :]
    scale2 = mod_ref[0, 4:5, :]
    gate2 = mod_ref[0, 5:6, :]
    norm_scale = ng_ref[layer:layer + 1, :] * (1.0 + scale2)
    groups = FFN_NORM_ROWS // S
    early_cols = ([j * fb for j in range(FFN_EARLY_BLOCKS)]
                  + [d_ff + j * fb for j in range(FFN_EARLY_BLOCKS)])
    early = {col: [] for col in early_cols}
    x_parts = []
    for p in range(tile // FFN_NORM_ROWS):
        prow = slice(p * FFN_NORM_ROWS, (p + 1) * FFN_NORM_ROWS)
        xs = jnp.concatenate([x_ref[0, base + s * R + p * groups:base + s * R + (p + 1) * groups, :]
                              for s in range(S)], axis=0)
        xs = xs.reshape(S, groups, D).swapaxes(0, 1).reshape(FFN_NORM_ROWS, D)
        ms = jnp.mean(xs * xs, axis=-1, keepdims=True)
        h_ref[prow, :] = ((xs * lax.rsqrt(ms + EPS)) * norm_scale + shift2).astype(_BF)
        x_parts.append(xs)
        for col in early_cols:
            early[col].append(_dot(h_ref[prow, :], wup_ref[:, col:col + fb]))
    x = jnp.concatenate(x_parts, axis=0)
    first_sublane = lax.broadcasted_iota(jnp.int32, (S, fb), 0) == 0

    def up_conv(col):
        cols = slice(col, col + fb)
        if col in early:
            u = jnp.concatenate(early[col], axis=0)
        else:
            u = _dot(h_ref[...], wup_ref[:, cols])
        back1 = jnp.where(first_sublane, pltpu.roll(carry_ref[S:2 * S, cols], 1, axis=0),
                          pltpu.roll(u[tile - S:tile, :], 1, axis=0))
        back2 = jnp.where(first_sublane, pltpu.roll(carry_ref[0:S, cols], 1, axis=0),
                          pltpu.roll(u[tile - 2 * S:tile - S, :], 1, axis=0))
        carry_ref[:, cols] = u[tile - 2 * S:tile, :]
        prev1 = jnp.concatenate([back1, u[0:tile - S, :]], axis=0)
        prev2 = jnp.concatenate([back2, back1, u[0:tile - 2 * S, :]], axis=0)
        cw = cw_ref[:, cols]
        return prev2 * cw[0:1, :] + prev1 * cw[1:2, :] + u * cw[2:3, :] + cw[3:4, :]

    for j in range(d_ff // fb):
        ya = up_conv(j * fb)
        yg = up_conv(d_ff + j * fb)
        act_ref[:, j * fb:(j + 1) * fb] = (_silu(yg) * ya).astype(_BF)

    out = x + gate2 * _dot(act_ref[...], wd_ref[...])
    if final_norm:
        ms_o = jnp.mean(out * out, axis=-1, keepdims=True)
        out = out * lax.rsqrt(ms_o + EPS) * fg_ref[...]
    o_ref[0, base:base + tile, :] = out.reshape(R, S, D).swapaxes(0, 1).reshape(tile, D)


def _ffn(x, mod, layer, params, final_g, final_norm):
    B, T, D = x.shape
    tile = FFN_TILE if T % FFN_TILE == 0 else T
    sub = FFN_SUB if tile % FFN_SUB == 0 else tile
    norm_g, wup, cw, wd = params
    d_ff = wd.shape[1]
    in_specs = [
        pl.BlockSpec((1, tile, D), lambda b, t: (b, t, 0)),
        pl.BlockSpec((None, 1, 6, D), lambda b, t: (layer, b, 0, 0)),
        _whole(norm_g.shape),
        _whole(wup.shape),
        _layer_slab(cw.shape, layer),
        _layer_slab(wd.shape, layer),
        _whole(final_g.shape),
    ]
    return pl.pallas_call(
        functools.partial(_ffn_kernel, tile=tile, sub=sub, d_ff=d_ff, fb=FFN_BLOCK, final_norm=final_norm,
                          layer=layer),
        grid=(B, T // tile),
        in_specs=in_specs,
        out_specs=pl.BlockSpec((1, tile, D), lambda b, t: (b, t, 0)),
        out_shape=jax.ShapeDtypeStruct((B, T, D), _F32),
        scratch_shapes=[
            pltpu.VMEM((d_ff, D), _BF),
            pltpu.VMEM((sub, D), _BF),
            pltpu.VMEM((2 * SUBLANES, 2 * d_ff), _F32),
            pltpu.VMEM((sub, d_ff), _BF),
        ],
        compiler_params=pltpu.CompilerParams(
            dimension_semantics=("arbitrary", "arbitrary"), vmem_limit_bytes=VMEM_LIMIT_BYTES),
        name="conv_ffn",
    )(x, mod, *params, final_g)


def _prep_mixer_params(norm1_g, w_in, pool_w, pool_scale, gla_wa2, gla_ba, gla_norm_g, w_out):
    L, D, _ = w_in.shape
    G = pool_w.shape[1]
    eye = jnp.eye(G, dtype=pool_w.dtype)
    poolw = (pool_w[:, :, :, None, :] * eye[None, :, None, :, None]).reshape(L, POOL_WIDTH, POOL_WIDTH)
    wa2 = jnp.pad(gla_wa2, ((0, 0), (0, LANES - GLA_GATE_RANK), (0, KP - KW)))
    ba = jnp.pad(gla_ba, ((0, 0), (0, KP - KW)))
    w_in = jnp.swapaxes(w_in, 1, 2)
    return (norm1_g, w_in, poolw.astype(_BF), pool_scale, wa2.astype(_BF), ba, gla_norm_g, w_out)


def _prep_ffn_params(norm2_g, w_up, conv_w, conv_b, w_down):
    L, D, _ = w_up.shape
    cw = jnp.pad(jnp.concatenate([conv_w, conv_b[:, None, :]], axis=1),
                 ((0, 0), (0, SUBLANES - CONV_WIDTH - 1), (0, 0)))
    return norm2_g, cw, w_down


def kernel(x, c, positions, ada_w, ada_b, norm1_g, w_in, pool_w, pool_scale, gla_wa2, gla_ba,
           gla_norm_g, w_out, norm2_g, w_up, conv_w, conv_b, w_down, final_g):
    L = ada_w.shape[0]
    B, T, D = x.shape
    tb = {k: jnp.asarray(v) for k, v in _tables().items()}
    for name in ("head_ind", "gla_tri", "ret_hmk", "ret_hmv", "gla_hmk", "gla_hmv", "state_mask",
                 "rot_cos_expand", "rot_sin_expand"):
        tb[name] = tb[name].astype(_BF)
    mod = _modulation(c, ada_w, ada_b).reshape(L, B, 6, D)
    cos, sin = _rotary_tables(positions, tb)
    mixer_params = _prep_mixer_params(norm1_g, w_in, pool_w, pool_scale, gla_wa2, gla_ba, gla_norm_g, w_out)
    norm2, conv_rows, w_down_f32 = _prep_ffn_params(norm2_g, w_up, conv_w, conv_b, w_down)
    fg = final_g.reshape(1, D)
    for l in range(L):
        x, w_up_bf16 = _mixer(x, mod, l, mixer_params, w_up, cos, sin, tb)
        x = _ffn(x, mod, l, (norm2, w_up_bf16, conv_rows, w_down_f32), fg, final_norm=(l == L - 1))
    return x
```

```python
import functools

import numpy as np
import jax
import jax.numpy as jnp
from jax import lax
from jax.experimental import pallas as pl
from jax.experimental.pallas import tpu as pltpu

POOL_WIDTH = 256
POOL_WINDOWS = (2, 4, 8, 16)
POOL_GROUP_DIM = 64
N_HEADS = 4
DK = 48
DV = 96
KW = N_HEADS * DK
VW = N_HEADS * DV
RET_CHUNK = 128
GLA_CHUNK = 64
GLA_GATE_RANK = 16
GLA_GATE_TAU = 16.0
ROPE_BASE = 10000.0
CONV_WIDTH = 3
EPS = 1e-6

LANES = 128
SUBLANES = 8
KP = 256
VMEM_LIMIT_BYTES = 56 * 1024 * 1024

OFF_POOL = 0
OFF_RQ = OFF_POOL + POOL_WIDTH
OFF_RK = OFF_RQ + KW
OFF_RV = OFF_RK + KW
OFF_RG = OFF_RV + VW
OFF_GQ = OFF_RG + VW
OFF_GK = OFF_GQ + KW
OFF_GV = OFF_GK + KW
OFF_GA = OFF_GV + VW
OFF_GG = OFF_GA + GLA_GATE_RANK
D_IN = OFF_GG + VW
N_IN = -(-D_IN // LANES) * LANES

MIX_TILE = 512
MIX_SUB = 256
PROJ_PIECE = 256
PROJ_TICKS = 3
FFN_TILE = 512
FFN_SUB = 512
FFN_BLOCK = 256
FFN_NORM_ROWS = 128
FFN_EARLY_BLOCKS = 2
ROT_TILE = 1024
ROT_ROWS = 32
POOL_HIST = 32
N_GLA_LEVELS = 6
CAST_ROWS = 128
LOG2_E = 1.4426950408889634

_BF = jnp.bfloat16
_F32 = jnp.float32


def _dot(a, b):
    return jnp.dot(a, b, preferred_element_type=_F32)


def _dot_nt(a, b):
    return lax.dot_general(a, b, (((1,), (1,)), ((), ())), preferred_element_type=_F32)


def _dot_tn(a, b):
    return lax.dot_general(a, b, (((0,), (0,)), ((), ())), preferred_element_type=_F32)


def _split3(x):
    p1 = x.astype(_BF)
    r1 = x - p1.astype(_F32)
    p2 = r1.astype(_BF)
    p3 = (r1 - p2.astype(_F32)).astype(_BF)
    return p1, p2, p3


def _cast_weight(src_ref, dst_ref):
    for r in range(0, src_ref.shape[0], CAST_ROWS):
        rows = slice(r, r + CAST_ROWS)
        dst_ref[rows, :] = src_ref[rows, :].astype(_BF)


def _cast_weight_transposed(src_ref, dst_ref):
    n, k = src_ref.shape
    for c in range(0, dst_ref.shape[1], CAST_ROWS):
        rows = min(CAST_ROWS, max(n - c, 0))
        block = src_ref[c:c + rows, :] if rows else None
        if rows < CAST_ROWS:
            zeros = jnp.zeros((CAST_ROWS - rows, k), _F32)
            block = zeros if block is None else jnp.concatenate([block, zeros], axis=0)
        dst_ref[:, c:c + CAST_ROWS] = block.T.astype(_BF)


def _sigmoid(x):
    return 1.0 / (1.0 + jnp.exp(-x))


def _silu(x):
    return x * _sigmoid(x)


def _key_head(d):
    return np.where(d < KW, d // DK, -1)


@functools.lru_cache(maxsize=None)
def _tables():
    t = {}
    kd = _key_head(np.arange(KP))
    vd = np.arange(VW) // DV

    def head_masks(chunk):
        rows = np.repeat(np.arange(N_HEADS), chunk)
        return ((rows[:, None] == kd[None, :]).astype(np.float32),
                (rows[:, None] == vd[None, :]).astype(np.float32))

    C = RET_CHUNK
    lg = np.log(1.0 - 2.0 ** (-5.0 - np.arange(N_HEADS, dtype=np.float64)))
    i = np.arange(C)
    rel = i[:, None] - i[None, :]
    dec = np.where(rel[None] >= 0, np.exp(np.maximum(rel, 0)[None] * lg[:, None, None]), 0.0)
    t["ret_decay"] = np.transpose(dec, (1, 0, 2)).reshape(C, N_HEADS * C).astype(np.float32)
    t["ret_xi"] = np.exp((i[:, None] + 1.0) * lg[vd][None, :]).astype(np.float32)
    zeta = np.exp((C - 1.0 - i)[:, None] * lg[np.maximum(kd, 0)][None, :]) * (kd >= 0)[None, :]
    t["ret_zeta"] = zeta.astype(np.float32)
    t["ret_gc"] = (np.exp(C * lg[np.maximum(kd, 0)]) * (kd >= 0))[None, :].astype(np.float32)
    t["ret_hmk"], t["ret_hmv"] = head_masks(C)
    t["state_mask"] = (vd[:, None] == kd[None, :]).astype(np.float32)
    t["head_ind"] = (vd[:, None] == vd[None, :]).astype(np.float32)

    C = GLA_CHUNK
    i = np.arange(C)
    masks = []
    s = C // 2
    while s >= 1:
        parent = (i // (2 * s)) * (2 * s)
        upper = (i - parent) >= s
        masks.append(upper[:, None] & (~upper)[None, :] & (parent[:, None] == parent[None, :]))
        s //= 2
    masks.append(i[:, None] == i[None, :])
    t["gla_mask"] = np.stack([np.tile(m, (1, N_HEADS)) for m in masks]).astype(np.float32)
    t["gla_tri"] = (i[None, :] <= i[:, None]).astype(np.float32)
    t["gla_hmk"], t["gla_hmv"] = head_masks(C)

    t["pool_win"] = np.repeat(np.asarray(POOL_WINDOWS, np.float32), POOL_GROUP_DIM)[None, :]

    half = DK // 2
    l = np.arange(KP)
    hit = (np.arange(ROT_ROWS)[:, None] == (l % half)[None, :]) & (l < KW)[None, :]
    t["rot_cos_expand"] = hit.astype(np.float32)
    t["rot_sin_expand"] = hit * np.where((l % DK) < half, -1.0, 1.0)[None, :].astype(np.float32)
    t["rot_first_half"] = ((l % DK) < half).astype(np.float32)[None, :]
    return t


def _mod_kernel(c_ref, w_ref, b_ref, o_ref):
    ca = _silu(c_ref[...]).astype(_BF)
    o_ref[...] = _dot(ca, w_ref[...].astype(_BF)) + b_ref[pl.ds(pl.program_id(0), 1), :]


def _modulation(c, ada_w, ada_b):
    L, D, N = ada_w.shape
    B = c.shape[0]
    tn = D
    return pl.pallas_call(
        _mod_kernel,
        grid=(L, N // tn),
        in_specs=[
            pl.BlockSpec((B, D), lambda l, n: (0, 0)),
            pl.BlockSpec((None, D, tn), lambda l, n: (l, 0, n)),
            pl.BlockSpec((L, tn), lambda l, n: (0, n)),
        ],
        out_specs=pl.BlockSpec((None, B, tn), lambda l, n: (l, 0, n)),
        out_shape=jax.ShapeDtypeStruct((L, B, N), _F32),
        compiler_params=pltpu.CompilerParams(
            dimension_semantics=("arbitrary", "arbitrary"), vmem_limit_bytes=VMEM_LIMIT_BYTES),
        name="adaln_modulation",
    )(c, ada_w, ada_b)


def _rot_kernel(pos_ref, freq_ref, ec_ref, es_ref, cos_ref, sin_ref):
    ang = freq_ref[...] * pos_ref[0]
    ec = ec_ref[...]
    es = es_ref[...]

    def widen(v, e):
        pieces = jnp.concatenate(_split3(v), axis=0)
        return _dot_tn(pieces, jnp.concatenate([e] * 3, axis=0))

    cos_ref[0] = widen(jnp.cos(ang), ec)
    sin_ref[0] = widen(jnp.sin(ang), es)


def _rotary_tables(positions, tb):
    B, T = positions.shape
    tile = ROT_TILE if T % ROT_TILE == 0 else T
    inv_freq = ROPE_BASE ** (-jnp.arange(0, DK, 2, dtype=_F32) / DK)
    freq = jnp.pad(inv_freq, (0, ROT_ROWS - DK // 2)).reshape(ROT_ROWS, 1)
    pos = positions.astype(_F32).reshape(B, 1, T)
    const = lambda b, t: (0, 0)
    return pl.pallas_call(
        _rot_kernel,
        grid=(B, T // tile),
        in_specs=[
            pl.BlockSpec((1, 1, tile), lambda b, t: (b, 0, t)),
            pl.BlockSpec((ROT_ROWS, 1), const),
            pl.BlockSpec((ROT_ROWS, KP), const),
            pl.BlockSpec((ROT_ROWS, KP), const),
        ],
        out_specs=[pl.BlockSpec((1, tile, KP), lambda b, t: (b, t, 0))] * 2,
        out_shape=[jax.ShapeDtypeStruct((B, T, KP), _F32)] * 2,
        compiler_params=pltpu.CompilerParams(dimension_semantics=("arbitrary", "arbitrary")),
        name="rotary_table",
    )(pos, freq, tb["rot_cos_expand"], tb["rot_sin_expand"])


def _mixer_kernel(x_ref, mod_ref, ng_ref, win32_ref, cos_ref, sin_ref,
                  poolw_ref, pools_ref, wa2_ref, ba_ref, gng_ref, wout32_ref,
                  poolwin_ref, rothalf_ref,
                  rdecay_ref, rxi_ref, rzeta_ref, rgc_ref, rhmk_ref, rhmv_ref,
                  smask_ref, hind_ref,
                  gtri_ref, gmask_ref, ghmk_ref, ghmv_ref, wup32_ref,
                  o_ref, wup_bf_ref,
                  win_ref, wout_ref,
                  hb_ref, proj_ref, ycat_ref, b_ref, dec_ref, oscr_ref, ext_ref, s2_ref, s4_ref, s8_ref,
                  hist_ref, sret_ref, sgla_ref, *, tile, sub, layer):
    lrow = slice(layer, layer + 1)
    t_idx = pl.program_id(1)
    n_sub = tile // sub

    @pl.when((pl.program_id(0) == 0) & (t_idx == 0))
    def _():
        _cast_weight_transposed(win32_ref, win_ref)
        _cast_weight(wout32_ref, wout_ref)

    @pl.when(t_idx == 0)
    def _():
        hist_ref[...] = jnp.zeros_like(hist_ref)
        sret_ref[...] = jnp.zeros_like(sret_ref)
        sgla_ref[...] = jnp.zeros_like(sgla_ref)

    D = x_ref.shape[-1]
    brow = pl.ds(pl.program_id(0), 1)
    gate1 = mod_ref[brow, 2 * D:3 * D]

    norm_scale = ng_ref[lrow, :] * (1.0 + mod_ref[brow, D:2 * D])
    shift1 = mod_ref[brow, 0:D]

    def projection_tasks(i):
        slot = i % 2

        def norm():
            x = x_ref[0, i * sub:(i + 1) * sub, :]
            ms = jnp.mean(x * x, axis=-1, keepdims=True)
            hb_ref[slot] = ((x * lax.rsqrt(ms + EPS)) * norm_scale + shift1).astype(_BF)

        def segment(off, width):
            def run():
                proj_ref[slot, :, off:off + width] = _dot(hb_ref[slot], win_ref[:, off:off + width])
            return run

        tasks = [(i, 0, norm)]
        for off in range(0, N_IN, PROJ_PIECE):
            width = min(PROJ_PIECE, N_IN - off)
            tasks.append((i, off + width, segment(off, width)))
        return tasks

    def key_block(pv, lrows, off):
        shift = off % LANES
        if shift == 0:
            return pv[lrows, off:off + KP]
        return pltpu.roll(pv[lrows, off - shift:off - shift + KP], KP - shift, axis=1)

    pending = projection_tasks(0)
    ticks = [0]

    def tick(weight=1):
        ticks[0] += weight
        if pending and ticks[0] >= PROJ_TICKS:
            ticks[0] = 0
            pending.pop(0)[2]()

    def require(i, column):
        while pending and (pending[0][0], pending[0][1]) <= (i, column):
            pending.pop(0)[2]()

    require(0, N_IN)

    def pooling(i):
        n = sub + POOL_HIST
        srows = slice(i * sub, (i + 1) * sub)
        u = proj_ref[i % 2, :, OFF_POOL:OFF_POOL + POOL_WIDTH]
        ext_ref[0:POOL_HIST, :] = hist_ref[...]
        ext_ref[POOL_HIST:n, :] = u
        hist_ref[...] = u[sub - POOL_HIST:sub, :]
        s2_ref[8:n, :] = ext_ref[8:n, :] + ext_ref[7:n - 1, :]
        s4_ref[16:n, :] = s2_ref[16:n, :] + s2_ref[14:n - 2, :]
        s8_ref[24:n, :] = s4_ref[24:n, :] + s4_ref[20:n - 4, :]
        s16 = s8_ref[32:n, :] + s8_ref[24:n - 8, :]
        lane = lax.broadcasted_iota(jnp.int32, (sub, POOL_WIDTH), 1)
        wsum = jnp.where(lane < POOL_GROUP_DIM, s2_ref[32:n, :],
                         jnp.where(lane < 2 * POOL_GROUP_DIM, s4_ref[32:n, :],
                                   jnp.where(lane < 3 * POOL_GROUP_DIM, s8_ref[32:n, :], s16)))
        t_abs = (t_idx * tile + i * sub
                 + lax.broadcasted_iota(jnp.int32, (sub, POOL_WIDTH), 0)).astype(_F32)
        cnt = jnp.minimum(t_abs + 1.0, poolwin_ref[...])
        pooled = wsum / cnt - u
        ycat_ref[srows, 0:POOL_WIDTH] = (_dot(pooled.astype(_BF), poolw_ref[...])
                                         * pools_ref[lrow, :]).astype(_BF)

    inv_dv = 1.0 / DV
    k_scale = DK ** -0.5
    q_scale = DK ** -0.5
    smask = smask_ref[...]

    state_blocks = []
    for h in range(N_HEADS):
        lane0 = (h * DK) // LANES * LANES
        lane1 = -(-((h + 1) * DK) // LANES) * LANES
        state_blocks.append((slice(h * DV, (h + 1) * DV), slice(lane0, lane1)))

    def load_state(ref):
        return [ref[rows, lanes] for rows, lanes in state_blocks]

    def store_state(ref, blocks):
        for (rows, lanes), blk in zip(state_blocks, blocks):
            ref[rows, lanes] = blk

    def advance_state(blocks, decay, upd):
        return [blk * decay[:, lanes] + upd[rows, lanes]
                for (rows, lanes), blk in zip(state_blocks, blocks)]

    def own_head(blocks):
        out = []
        for (rows, lanes), blk in zip(state_blocks, blocks):
            part = blk.astype(_BF) * smask[rows, lanes]
            left = lanes.start
            right = KP - lanes.stop
            pieces = ([jnp.zeros((DV, left), _BF)] if left else []) + [part] \
                + ([jnp.zeros((DV, right), _BF)] if right else [])
            out.append(pieces[0] if len(pieces) == 1 else jnp.concatenate(pieces, axis=1))
        return jnp.concatenate(out, axis=0)
    hind = hind_ref[...]
    first_half = rothalf_ref[...] > 0.5

    def head_rms(o):
        ms_h = _dot((o * o).astype(_BF), hind) * inv_dv
        return o * lax.rsqrt(ms_h + EPS)

    def stack_heads(a):
        return jnp.concatenate([a] * N_HEADS, axis=0)

    def rotary(t, cosv, sinv):
        partner = jnp.where(first_half, pltpu.roll(t, KP - DK // 2, axis=1), pltpu.roll(t, DK // 2, axis=1))
        return t * cosv + partner * sinv

    def retention(i, r0):
        C = RET_CHUNK
        pv = proj_ref.at[i % 2]
        parts = []
        for c in range(sub // C):
            rows = slice(r0 + c * C, r0 + (c + 1) * C)
            lrows = slice(c * C, (c + 1) * C)
            cosv = cos_ref[0, rows, :]
            sinv = sin_ref[0, rows, :]
            q = rotary(key_block(pv, lrows, OFF_RQ), cosv, sinv)
            k = rotary(key_block(pv, lrows, OFF_RK), cosv, sinv) * k_scale
            vb = pv[lrows, OFF_RV:OFF_RV + VW].astype(_BF)
            qb = q.astype(_BF)
            kbd = stack_heads(k.astype(_BF)) * rhmk_ref[...]
            scores = _dot_nt(qb, kbd) * rdecay_ref[...]
            vbd = stack_heads(vb) * rhmv_ref[...]
            o_intra = _dot(scores.astype(_BF), vbd)
            upd = _dot_tn(vb, (k * rzeta_ref[...]).astype(_BF))
            parts.append((rows, qb, o_intra, upd))
            tick()
            yield
        state = states["ret"]
        for rows, qb, o_intra, upd in parts:
            oscr_ref[rows, 0:VW] = o_intra + _dot_nt(qb, own_head(state)) * rxi_ref[...]
            state = advance_state(state, rgc_ref[...], upd)
            tick()
            yield
        states["ret"] = state

    def gla_decay(i):
        C = GLA_CHUNK
        n_chunks = sub // C
        W = n_chunks * KP
        pv = proj_ref.at[i % 2]
        logits = _dot(pv[:, OFF_GA:OFF_GA + LANES].astype(_BF), wa2_ref[...]) + ba_ref[lrow, :]
        log_sig = jnp.minimum(logits, 0.0) - jnp.log(1.0 + jnp.exp(-jnp.abs(logits)))
        la = log_sig * (LOG2_E / GLA_GATE_TAU)

        la_l = jnp.concatenate([la[c * C:(c + 1) * C, :] for c in range(n_chunks)], axis=1)
        tri = gtri_ref[...]
        p1, p2, p3 = _split3(la_l)
        b = _dot(tri, p1) + _dot(tri, p2) + _dot(tri, p3)
        b_ref[i % 2] = b
        yield

        row = lax.broadcasted_iota(jnp.int32, (C, W), 0)
        s = C // 2
        lvl = 0
        while s >= SUBLANES // 2:
            refs = [jnp.broadcast_to(b_ref[i % 2, p + s - 1:p + s, :], (2 * s, W)) for p in range(0, C, 2 * s)]
            b_at_ref = refs[0] if len(refs) == 1 else jnp.concatenate(refs, axis=0)
            dec_ref[i % 2, lvl * C:(lvl + 1) * C, :] = jnp.exp2(-jnp.abs(b - b_at_ref))
            s //= 2
            lvl += 1
            tick()
            yield
        up1 = pltpu.roll(b, 1, axis=0)
        up2 = pltpu.roll(b, 2, axis=0)
        dn1 = pltpu.roll(b, C - 1, axis=0)
        m4 = row % 4
        b_at_ref = jnp.where(m4 == 0, dn1, jnp.where(m4 == 1, b, jnp.where(m4 == 2, up1, up2)))
        dec_ref[i % 2, lvl * C:(lvl + 1) * C, :] = jnp.exp2(-jnp.abs(b - b_at_ref))
        lvl += 1
        tick()
        yield
        b_at_ref = jnp.where(row % 2 == 0, b, up1)
        dec_ref[i % 2, lvl * C:(lvl + 1) * C, :] = jnp.exp2(-jnp.abs(b - b_at_ref))
        lvl += 1
        tick()
        yield
        dec_ref[i % 2, lvl * C:(lvl + 1) * C, :] = jnp.exp2(b)
        tick()
        b_last = jnp.broadcast_to(b_ref[i % 2, C - 1:C, :], (C, W))
        dec_ref[i % 2, (lvl + 1) * C:(lvl + 2) * C, :] = jnp.exp2(b_last - b)
        tick()

    def gla_chunks(i, r0, state):
        C = GLA_CHUNK
        n_chunks = sub // C
        pv = proj_ref.at[i % 2]
        hmk = ghmk_ref[...]
        parts = []
        for c in range(n_chunks):
            rows = slice(r0 + c * C, r0 + (c + 1) * C)
            lrows = slice(c * C, (c + 1) * C)
            lanes = slice(c * KP, (c + 1) * KP)
            q = key_block(pv, lrows, OFF_GQ) * q_scale
            k = key_block(pv, lrows, OFF_GK)
            vb = pv[lrows, OFF_GV:OFF_GV + VW].astype(_BF)
            qb = q.astype(_BF)
            kbd0 = stack_heads(k.astype(_BF)) * hmk
            scores = jnp.zeros((C, N_HEADS * C), _F32)
            for lvl in range(N_GLA_LEVELS + 1):
                if lvl < N_GLA_LEVELS:
                    eb = dec_ref[i % 2, lvl * C:(lvl + 1) * C, lanes].astype(_BF)
                    ql, kbd = qb * eb, kbd0 * stack_heads(eb)
                else:
                    ql, kbd = qb, kbd0
                scores = scores + _dot_nt(ql, kbd) * gmask_ref[lvl]
                tick()
            vbd = stack_heads(vb) * ghmv_ref[...]
            e_cum = dec_ref[i % 2, N_GLA_LEVELS * C:(N_GLA_LEVELS + 1) * C, lanes]
            e_rev = dec_ref[i % 2, (N_GLA_LEVELS + 1) * C:(N_GLA_LEVELS + 2) * C, lanes]
            o_intra = _dot(scores.astype(_BF), vbd)
            upd = _dot_tn(vb, (k * e_rev).astype(_BF))
            parts.append((rows, (q * e_cum).astype(_BF), o_intra, upd, e_cum[C - 1:C, :]))
        for rows, qe, o_intra, upd, e_last in parts:
            oscr_ref[rows, VW:2 * VW] = o_intra + _dot_nt(qe, own_head(state))
            state = advance_state(state, e_last, upd)
            tick()
        return state

    states = {"ret": load_state(sret_ref)}
    gla_state = load_state(sgla_ref)
    for i in range(n_sub):
        r0 = i * sub
        srows = slice(r0, r0 + sub)
        if i + 1 < n_sub:
            pending.extend(projection_tasks(i + 1))
        tick()
        pooling(i)
        tick()
        stages = [retention(i, r0), gla_decay(i)]
        while stages:
            for stage in list(stages):
                if next(stage, stages) is stages:
                    stages.remove(stage)
        gla_state = gla_chunks(i, r0, gla_state)
        y_ret = _silu(proj_ref[i % 2, :, OFF_RG:OFF_RG + VW]) * head_rms(oscr_ref[srows, 0:VW])
        ycat_ref[srows, POOL_WIDTH:POOL_WIDTH + VW] = y_ret.astype(_BF)
        gg_off = OFF_GG % LANES
        gg = pltpu.roll(proj_ref[i % 2, :, OFF_GG - gg_off:N_IN], N_IN - OFF_GG, axis=1)[:, 0:VW]
        y_gla = (_silu(gg)
                 * (head_rms(oscr_ref[srows, VW:2 * VW]) * gng_ref[lrow, :]))
        require(i + 1, N_IN)
        ycat_ref[srows, POOL_WIDTH + VW:POOL_WIDTH + 2 * VW] = y_gla.astype(_BF)
        o_ref[0, srows, :] = x_ref[0, srows, :] + gate1 * _dot(ycat_ref[srows, :], wout_ref[...])
    store_state(sret_ref, states["ret"])
    store_state(sgla_ref, gla_state)

    wup_bf_ref[...] = wup32_ref[...].astype(_BF)


def _whole(shape):
    nd = len(shape)
    return pl.BlockSpec(shape, lambda b, t: (0,) * nd, pipeline_mode=pl.Buffered(1))


def _layer_slab(shape, layer):
    nd = len(shape) - 1
    return pl.BlockSpec((None,) + tuple(shape[1:]), lambda b, t: (layer,) + (0,) * nd,
                        pipeline_mode=pl.Buffered(1))


def _mixer(x, mod, layer, params, w_up, cos, sin, tb):
    B, T, D = x.shape
    tile = MIX_TILE if T % MIX_TILE == 0 else T
    sub = MIX_SUB if tile % MIX_SUB == 0 else tile
    consts = [tb[k] for k in ("pool_win", "rot_first_half", "ret_decay", "ret_xi", "ret_zeta", "ret_gc",
                              "ret_hmk", "ret_hmv", "state_mask", "head_ind", "gla_tri", "gla_mask",
                              "gla_hmk", "gla_hmv")]
    norm_g, win = params[0], params[1]
    rest = list(params[2:])
    n_sub = tile // sub
    n_tiles = T // tile
    up_rows, up_cols = w_up.shape[1], w_up.shape[2]
    chunk = up_rows // (B * n_tiles)
    assert chunk * B * n_tiles == up_rows and chunk % (2 * SUBLANES) == 0
    args = [x, mod, norm_g, win, cos, sin] + rest + consts + [w_up]
    in_specs = [
        pl.BlockSpec((1, tile, D), lambda b, t: (b, t, 0)),
        _layer_slab(mod.shape, layer),
        _whole(norm_g.shape),
        _layer_slab(win.shape, layer),
        pl.BlockSpec((1, tile, KP), lambda b, t: (b, t, 0)),
        pl.BlockSpec((1, tile, KP), lambda b, t: (b, t, 0)),
    ] + [_whole(a.shape) if a.ndim == 2 else _layer_slab(a.shape, layer) for a in rest] + [
        _whole(a.shape) for a in consts] + [
        pl.BlockSpec((None, chunk, up_cols), lambda b, t: (layer, b * n_tiles + t, 0))]
    n = sub + POOL_HIST
    gla_lanes = (sub // GLA_CHUNK) * KP
    return pl.pallas_call(
        functools.partial(_mixer_kernel, tile=tile, sub=sub, layer=layer),
        grid=(B, T // tile),
        in_specs=in_specs,
        out_specs=[pl.BlockSpec((1, tile, D), lambda b, t: (b, t, 0)),
                   pl.BlockSpec((chunk, up_cols), lambda b, t: (b * n_tiles + t, 0))],
        out_shape=[jax.ShapeDtypeStruct((B, T, D), _F32),
                   jax.ShapeDtypeStruct((up_rows, up_cols), _BF)],
        scratch_shapes=[
            pltpu.VMEM((D, N_IN), _BF),
            pltpu.VMEM((POOL_WIDTH + 2 * VW, D), _BF),
            pltpu.VMEM((2, sub, D), _BF),
            pltpu.VMEM((2, sub, N_IN), _F32),
            pltpu.VMEM((tile, POOL_WIDTH + 2 * VW), _BF),
            pltpu.VMEM((2, GLA_CHUNK, gla_lanes), _F32),
            pltpu.VMEM((2, (N_GLA_LEVELS + 2) * GLA_CHUNK, gla_lanes), _F32),
            pltpu.VMEM((tile, 2 * VW), _F32),
            pltpu.VMEM((n, POOL_WIDTH), _F32),
            pltpu.VMEM((n, POOL_WIDTH), _F32),
            pltpu.VMEM((n, POOL_WIDTH), _F32),
            pltpu.VMEM((n, POOL_WIDTH), _F32),
            pltpu.VMEM((POOL_HIST, POOL_WIDTH), _F32),
            pltpu.VMEM((VW, KP), _F32),
            pltpu.VMEM((VW, KP), _F32),
        ],
        compiler_params=pltpu.CompilerParams(
            dimension_semantics=("arbitrary", "arbitrary"), vmem_limit_bytes=VMEM_LIMIT_BYTES),
        name="token_mixer",
    )(*args)


def _ffn_kernel(x_ref, mod_ref, ng_ref, wup_ref, cw_ref, wd32_ref, fg_ref,
                o_ref, wd_ref, h_ref, carry_ref, act_ref,
                *, tile, sub, d_ff, fb, final_norm, layer):
    t_idx = pl.program_id(1)

    @pl.when((pl.program_id(0) == 0) & (t_idx == 0))
    def _():
        _cast_weight(wd32_ref, wd_ref)

    @pl.when(t_idx == 0)
    def _():
        carry_ref[...] = jnp.zeros_like(carry_ref)

    for base in range(0, tile, sub):
        _ffn_sub_tile(x_ref, mod_ref, ng_ref, wup_ref, cw_ref, wd_ref, fg_ref, o_ref, h_ref,
                      carry_ref, act_ref, base=base, tile=sub, d_ff=d_ff, fb=fb,
                      final_norm=final_norm, layer=layer)


def _ffn_sub_tile(x_ref, mod_ref, ng_ref, wup_ref, cw_ref, wd_ref, fg_ref, o_ref, h_ref, carry_ref,
                  act_ref, *, base, tile, d_ff, fb, final_norm, layer):
    S = SUBLANES
    R = tile // S
    D = x_ref.shape[-1]
    brow = pl.ds(pl.program_id(0), 1)
    shift2 = mod_ref[brow, 3 * D:4 * D]
    scale2 = mod_ref[brow, 4 * D:5 * D]
    gate2 = mod_ref[brow, 5 * D:6 * D]
    norm_scale = ng_ref[layer:layer + 1, :] * (1.0 + scale2)
    groups = FFN_NORM_ROWS // S
    early_cols = ([j * fb for j in range(FFN_EARLY_BLOCKS)]
                  + [d_ff + j * fb for j in range(FFN_EARLY_BLOCKS)])
    early = {col: [] for col in early_cols}
    x_parts = []
    for p in range(tile // FFN_NORM_ROWS):
        prow = slice(p * FFN_NORM_ROWS, (p + 1) * FFN_NORM_ROWS)
        xs = jnp.concatenate([x_ref[0, base + s * R + p * groups:base + s * R + (p + 1) * groups, :]
                              for s in range(S)], axis=0)
        xs = xs.reshape(S, groups, D).swapaxes(0, 1).reshape(FFN_NORM_ROWS, D)
        ms = jnp.mean(xs * xs, axis=-1, keepdims=True)
        h_ref[prow, :] = ((xs * lax.rsqrt(ms + EPS)) * norm_scale + shift2).astype(_BF)
        x_parts.append(xs)
        for col in early_cols:
            early[col].append(_dot(h_ref[prow, :], wup_ref[:, col:col + fb]))
    x = jnp.concatenate(x_parts, axis=0)
    first_sublane = lax.broadcasted_iota(jnp.int32, (S, fb), 0) == 0

    def up_conv(col):
        cols = slice(col, col + fb)
        if col in early:
            u = jnp.concatenate(early[col], axis=0)
        else:
            u = _dot(h_ref[...], wup_ref[:, cols])
        back1 = jnp.where(first_sublane, pltpu.roll(carry_ref[S:2 * S, cols], 1, axis=0),
                          pltpu.roll(u[tile - S:tile, :], 1, axis=0))
        back2 = jnp.where(first_sublane, pltpu.roll(carry_ref[0:S, cols], 1, axis=0),
                          pltpu.roll(u[tile - 2 * S:tile - S, :], 1, axis=0))
        carry_ref[:, cols] = u[tile - 2 * S:tile, :]
        prev1 = jnp.concatenate([back1, u[0:tile - S, :]], axis=0)
        prev2 = jnp.concatenate([back2, back1, u[0:tile - 2 * S, :]], axis=0)
        cw = cw_ref[:, cols]
        return prev2 * cw[0:1, :] + prev1 * cw[1:2, :] + u * cw[2:3, :] + cw[3:4, :]

    for j in range(d_ff // fb):
        ya = up_conv(j * fb)
        yg = up_conv(d_ff + j * fb)
        act_ref[:, j * fb:(j + 1) * fb] = (_silu(yg) * ya).astype(_BF)

    out = x + gate2 * _dot(act_ref[...], wd_ref[...])
    if final_norm:
        ms_o = jnp.mean(out * out, axis=-1, keepdims=True)
        out = out * lax.rsqrt(ms_o + EPS) * fg_ref[...]
    o_ref[0, base:base + tile, :] = out.reshape(R, S, D).swapaxes(0, 1).reshape(tile, D)


def _ffn(x, mod, layer, params, final_g, final_norm):
    B, T, D = x.shape
    tile = FFN_TILE if T % FFN_TILE == 0 else T
    sub = FFN_SUB if tile % FFN_SUB == 0 else tile
    norm_g, wup, cw, wd = params
    d_ff = wd.shape[1]
    in_specs = [
        pl.BlockSpec((1, tile, D), lambda b, t: (b, t, 0)),
        _layer_slab(mod.shape, layer),
        _whole(norm_g.shape),
        _whole(wup.shape),
        _layer_slab(cw.shape, layer),
        _layer_slab(wd.shape, layer),
        _whole(final_g.shape),
    ]
    return pl.pallas_call(
        functools.partial(_ffn_kernel, tile=tile, sub=sub, d_ff=d_ff, fb=FFN_BLOCK, final_norm=final_norm,
                          layer=layer),
        grid=(B, T // tile),
        in_specs=in_specs,
        out_specs=pl.BlockSpec((1, tile, D), lambda b, t: (b, t, 0)),
        out_shape=jax.ShapeDtypeStruct((B, T, D), _F32),
        scratch_shapes=[
            pltpu.VMEM((d_ff, D), _BF),
            pltpu.VMEM((sub, D), _BF),
            pltpu.VMEM((2 * SUBLANES, 2 * d_ff), _F32),
            pltpu.VMEM((sub, d_ff), _BF),
        ],
        compiler_params=pltpu.CompilerParams(
            dimension_semantics=("arbitrary", "arbitrary"), vmem_limit_bytes=VMEM_LIMIT_BYTES),
        name="conv_ffn",
    )(x, mod, *params, final_g)


def _prep_mixer_params(norm1_g, w_in, pool_w, pool_scale, gla_wa2, gla_ba, gla_norm_g, w_out):
    L, D, _ = w_in.shape
    G = pool_w.shape[1]
    eye = jnp.eye(G, dtype=pool_w.dtype)
    poolw = (pool_w[:, :, :, None, :] * eye[None, :, None, :, None]).reshape(L, POOL_WIDTH, POOL_WIDTH)
    wa2 = jnp.pad(gla_wa2, ((0, 0), (0, LANES - GLA_GATE_RANK), (0, KP - KW)))
    ba = jnp.pad(gla_ba, ((0, 0), (0, KP - KW)))
    w_in = jnp.swapaxes(w_in, 1, 2)
    return (norm1_g, w_in, poolw.astype(_BF), pool_scale, wa2.astype(_BF), ba, gla_norm_g, w_out)


def _prep_ffn_params(norm2_g, w_up, conv_w, conv_b, w_down):
    L, D, _ = w_up.shape
    cw = jnp.pad(jnp.concatenate([conv_w, conv_b[:, None, :]], axis=1),
                 ((0, 0), (0, SUBLANES - CONV_WIDTH - 1), (0, 0)))
    return norm2_g, cw, w_down


def kernel(x, c, positions, ada_w, ada_b, norm1_g, w_in, pool_w, pool_scale, gla_wa2, gla_ba,
           gla_norm_g, w_out, norm2_g, w_up, conv_w, conv_b, w_down, final_g):
    L = ada_w.shape[0]
    B, T, D = x.shape
    tb = {k: jnp.asarray(v) for k, v in _tables().items()}
    for name in ("head_ind", "gla_tri", "ret_hmk", "ret_hmv", "gla_hmk", "gla_hmv", "state_mask",
                 "rot_cos_expand", "rot_sin_expand"):
        tb[name] = tb[name].astype(_BF)
    mod = _modulation(c, ada_w, ada_b)
    cos, sin = _rotary_tables(positions, tb)
    mixer_params = _prep_mixer_params(norm1_g, w_in, pool_w, pool_scale, gla_wa2, gla_ba, gla_norm_g, w_out)
    norm2, conv_rows, w_down_f32 = _prep_ffn_params(norm2_g, w_up, conv_w, conv_b, w_down)
    fg = final_g.reshape(1, D)
    for l in range(L):
        x, w_up_bf16 = _mixer(x, mod, l, mixer_params, w_up, cos, sin, tb)
        x = _ffn(x, mod, l, (norm2, w_up_bf16, conv_rows, w_down_f32), fg, final_norm=(l == L - 1))
    return x
```

```python
import functools

import numpy as np
import jax
import jax.numpy as jnp
from jax import lax
from jax.experimental import pallas as pl
from jax.experimental.pallas import tpu as pltpu

POOL_WIDTH = 256
POOL_WINDOWS = (2, 4, 8, 16)
POOL_GROUP_DIM = 64
N_HEADS = 4
DK = 48
DV = 96
KW = N_HEADS * DK
VW = N_HEADS * DV
RET_CHUNK = 128
GLA_CHUNK = 64
GLA_GATE_RANK = 16
GLA_GATE_TAU = 16.0
ROPE_BASE = 10000.0
CONV_WIDTH = 3
EPS = 1e-6

LANES = 128
SUBLANES = 8
KP = 256
VMEM_LIMIT_BYTES = 56 * 1024 * 1024

OFF_POOL = 0
OFF_RQ = OFF_POOL + POOL_WIDTH
OFF_RK = OFF_RQ + KW
OFF_RV = OFF_RK + KW
OFF_RG = OFF_RV + VW
OFF_GQ = OFF_RG + VW
OFF_GK = OFF_GQ + KW
OFF_GV = OFF_GK + KW
OFF_GA = OFF_GV + VW
OFF_GG = OFF_GA + GLA_GATE_RANK
D_IN = OFF_GG + VW
N_IN = -(-D_IN // LANES) * LANES

MIX_TILE = 512
MIX_SUB = 256
PROJ_PIECE = 256
PROJ_TICKS = 3
FFN_TILE = 512
FFN_SUB = 512
FFN_BLOCK = 256
FFN_NORM_ROWS = 128
FFN_EARLY_BLOCKS = 2
ROT_TILE = 1024
ROT_ROWS = 32
POOL_HIST = 32
N_GLA_LEVELS = 6
CAST_ROWS = 128
LOG2_E = 1.4426950408889634

_BF = jnp.bfloat16
_F32 = jnp.float32


def _dot(a, b):
    return jnp.dot(a, b, preferred_element_type=_F32)


def _dot_nt(a, b):
    return lax.dot_general(a, b, (((1,), (1,)), ((), ())), preferred_element_type=_F32)


def _dot_tn(a, b):
    return lax.dot_general(a, b, (((0,), (0,)), ((), ())), preferred_element_type=_F32)


def _split3(x):
    p1 = x.astype(_BF)
    r1 = x - p1.astype(_F32)
    p2 = r1.astype(_BF)
    p3 = (r1 - p2.astype(_F32)).astype(_BF)
    return p1, p2, p3


def _cast_weight(src_ref, dst_ref):
    for r in range(0, src_ref.shape[0], CAST_ROWS):
        rows = slice(r, r + CAST_ROWS)
        dst_ref[rows, :] = src_ref[rows, :].astype(_BF)


def _cast_weight_transposed(src_ref, dst_ref):
    n, k = src_ref.shape
    for c in range(0, dst_ref.shape[1], CAST_ROWS):
        rows = min(CAST_ROWS, max(n - c, 0))
        block = src_ref[c:c + rows, :] if rows else None
        if rows < CAST_ROWS:
            zeros = jnp.zeros((CAST_ROWS - rows, k), _F32)
            block = zeros if block is None else jnp.concatenate([block, zeros], axis=0)
        dst_ref[:, c:c + CAST_ROWS] = block.T.astype(_BF)


def _sigmoid(x):
    return 1.0 / (1.0 + jnp.exp(-x))


def _silu(x):
    return x * _sigmoid(x)


def _key_head(d):
    return np.where(d < KW, d // DK, -1)


@functools.lru_cache(maxsize=None)
def _tables():
    t = {}
    kd = _key_head(np.arange(KP))
    vd = np.arange(VW) // DV

    def head_masks(chunk):
        rows = np.repeat(np.arange(N_HEADS), chunk)
        return ((rows[:, None] == kd[None, :]).astype(np.float32),
                (rows[:, None] == vd[None, :]).astype(np.float32))

    C = RET_CHUNK
    lg = np.log(1.0 - 2.0 ** (-5.0 - np.arange(N_HEADS, dtype=np.float64)))
    i = np.arange(C)
    rel = i[:, None] - i[None, :]
    dec = np.where(rel[None] >= 0, np.exp(np.maximum(rel, 0)[None] * lg[:, None, None]), 0.0)
    t["ret_decay"] = np.transpose(dec, (1, 0, 2)).reshape(C, N_HEADS * C).astype(np.float32)
    t["ret_xi"] = np.exp((i[:, None] + 1.0) * lg[vd][None, :]).astype(np.float32)
    zeta = np.exp((C - 1.0 - i)[:, None] * lg[np.maximum(kd, 0)][None, :]) * (kd >= 0)[None, :]
    t["ret_zeta"] = zeta.astype(np.float32)
    t["ret_gc"] = (np.exp(C * lg[np.maximum(kd, 0)]) * (kd >= 0))[None, :].astype(np.float32)
    t["ret_hmk"], t["ret_hmv"] = head_masks(C)
    t["state_mask"] = (vd[:, None] == kd[None, :]).astype(np.float32)
    t["head_ind"] = (vd[:, None] == vd[None, :]).astype(np.float32)

    C = GLA_CHUNK
    i = np.arange(C)
    masks = []
    s = C // 2
    while s >= 1:
        parent = (i // (2 * s)) * (2 * s)
        upper = (i - parent) >= s
        masks.append(upper[:, None] & (~upper)[None, :] & (parent[:, None] == parent[None, :]))
        s //= 2
    masks.append(i[:, None] == i[None, :])
    t["gla_mask"] = np.stack([np.tile(m, (1, N_HEADS)) for m in masks]).astype(np.float32)
    t["gla_tri"] = (i[None, :] <= i[:, None]).astype(np.float32)
    t["gla_hmk"], t["gla_hmv"] = head_masks(C)

    t["pool_win"] = np.repeat(np.asarray(POOL_WINDOWS, np.float32), POOL_GROUP_DIM)[None, :]

    half = DK // 2
    l = np.arange(KP)
    hit = (np.arange(ROT_ROWS)[:, None] == (l % half)[None, :]) & (l < KW)[None, :]
    t["rot_cos_expand"] = hit.astype(np.float32)
    t["rot_sin_expand"] = hit * np.where((l % DK) < half, -1.0, 1.0)[None, :].astype(np.float32)
    t["rot_first_half"] = ((l % DK) < half).astype(np.float32)[None, :]
    return t


def _mod_kernel(c_ref, w_ref, b_ref, o_ref):
    ca = _silu(c_ref[...]).astype(_BF)
    o_ref[...] = _dot(ca, w_ref[...].astype(_BF)) + b_ref[...]


def _modulation(c, ada_w, ada_b):
    L, D, N = ada_w.shape
    B = c.shape[0]
    tn = D
    return pl.pallas_call(
        _mod_kernel,
        grid=(L, N // tn),
        in_specs=[
            pl.BlockSpec((B, D), lambda l, n: (0, 0)),
            pl.BlockSpec((None, D, tn), lambda l, n: (l, 0, n)),
            pl.BlockSpec((None, 1, tn), lambda l, n: (l, 0, n)),
        ],
        out_specs=pl.BlockSpec((None, B, tn), lambda l, n: (l, 0, n)),
        out_shape=jax.ShapeDtypeStruct((L, B, N), _F32),
        compiler_params=pltpu.CompilerParams(
            dimension_semantics=("arbitrary", "arbitrary"), vmem_limit_bytes=VMEM_LIMIT_BYTES),
        name="adaln_modulation",
    )(c, ada_w, ada_b.reshape(L, 1, N))


def _rot_kernel(pos_ref, freq_ref, ec_ref, es_ref, cos_ref, sin_ref):
    ang = freq_ref[...] * pos_ref[0]
    ec = ec_ref[...]
    es = es_ref[...]

    def widen(v, e):
        pieces = jnp.concatenate(_split3(v), axis=0)
        return _dot_tn(pieces, jnp.concatenate([e] * 3, axis=0))

    cos_ref[0] = widen(jnp.cos(ang), ec)
    sin_ref[0] = widen(jnp.sin(ang), es)


def _rotary_tables(positions, tb):
    B, T = positions.shape
    tile = ROT_TILE if T % ROT_TILE == 0 else T
    inv_freq = ROPE_BASE ** (-jnp.arange(0, DK, 2, dtype=_F32) / DK)
    freq = jnp.pad(inv_freq, (0, ROT_ROWS - DK // 2)).reshape(ROT_ROWS, 1)
    pos = positions.astype(_F32).reshape(B, 1, T)
    const = lambda b, t: (0, 0)
    return pl.pallas_call(
        _rot_kernel,
        grid=(B, T // tile),
        in_specs=[
            pl.BlockSpec((1, 1, tile), lambda b, t: (b, 0, t)),
            pl.BlockSpec((ROT_ROWS, 1), const),
            pl.BlockSpec((ROT_ROWS, KP), const),
            pl.BlockSpec((ROT_ROWS, KP), const),
        ],
        out_specs=[pl.BlockSpec((1, tile, KP), lambda b, t: (b, t, 0))] * 2,
        out_shape=[jax.ShapeDtypeStruct((B, T, KP), _F32)] * 2,
        compiler_params=pltpu.CompilerParams(dimension_semantics=("arbitrary", "arbitrary")),
        name="rotary_table",
    )(pos, freq, tb["rot_cos_expand"], tb["rot_sin_expand"])


def _mixer_kernel(x_ref, mod_ref, ng_ref, win32_ref, cos_ref, sin_ref,
                  poolw_ref, pools_ref, wa2_ref, ba_ref, gng_ref, wout32_ref,
                  poolwin_ref, rothalf_ref,
                  rdecay_ref, rxi_ref, rzeta_ref, rgc_ref, rhmk_ref, rhmv_ref,
                  smask_ref, hind_ref,
                  gtri_ref, gmask_ref, ghmk_ref, ghmv_ref, wup32_ref,
                  o_ref, wup_bf_ref,
                  win_ref, wout_ref,
                  hb_ref, proj_ref, ycat_ref, b_ref, dec_ref, oscr_ref, ext_ref, s2_ref, s4_ref, s8_ref,
                  hist_ref, sret_ref, sgla_ref, *, tile, sub, layer):
    lrow = slice(layer, layer + 1)
    t_idx = pl.program_id(1)
    n_sub = tile // sub

    @pl.when((pl.program_id(0) == 0) & (t_idx == 0))
    def _():
        _cast_weight_transposed(win32_ref, win_ref)
        _cast_weight(wout32_ref, wout_ref)

    @pl.when(t_idx == 0)
    def _():
        hist_ref[...] = jnp.zeros_like(hist_ref)
        sret_ref[...] = jnp.zeros_like(sret_ref)
        sgla_ref[...] = jnp.zeros_like(sgla_ref)

    gate1 = mod_ref[0, 2:3, :]

    norm_scale = ng_ref[lrow, :] * (1.0 + mod_ref[0, 1:2, :])
    shift1 = mod_ref[0, 0:1, :]

    def projection_tasks(i):
        slot = i % 2

        def norm():
            x = x_ref[0, i * sub:(i + 1) * sub, :]
            ms = jnp.mean(x * x, axis=-1, keepdims=True)
            hb_ref[slot] = ((x * lax.rsqrt(ms + EPS)) * norm_scale + shift1).astype(_BF)

        def segment(off, width):
            def run():
                proj_ref[slot, :, off:off + width] = _dot(hb_ref[slot], win_ref[:, off:off + width])
            return run

        tasks = [(i, 0, norm)]
        for off in range(0, N_IN, PROJ_PIECE):
            width = min(PROJ_PIECE, N_IN - off)
            tasks.append((i, off + width, segment(off, width)))
        return tasks

    def key_block(pv, lrows, off):
        shift = off % LANES
        if shift == 0:
            return pv[lrows, off:off + KP]
        return pltpu.roll(pv[lrows, off - shift:off - shift + KP], KP - shift, axis=1)

    pending = projection_tasks(0)
    ticks = [0]

    def tick(weight=1):
        ticks[0] += weight
        if pending and ticks[0] >= PROJ_TICKS:
            ticks[0] = 0
            pending.pop(0)[2]()

    def require(i, column):
        while pending and (pending[0][0], pending[0][1]) <= (i, column):
            pending.pop(0)[2]()

    require(0, N_IN)

    def pooling(i):
        n = sub + POOL_HIST
        srows = slice(i * sub, (i + 1) * sub)
        u = proj_ref[i % 2, :, OFF_POOL:OFF_POOL + POOL_WIDTH]
        ext_ref[0:POOL_HIST, :] = hist_ref[...]
        ext_ref[POOL_HIST:n, :] = u
        hist_ref[...] = u[sub - POOL_HIST:sub, :]
        s2_ref[8:n, :] = ext_ref[8:n, :] + ext_ref[7:n - 1, :]
        s4_ref[16:n, :] = s2_ref[16:n, :] + s2_ref[14:n - 2, :]
        s8_ref[24:n, :] = s4_ref[24:n, :] + s4_ref[20:n - 4, :]
        s16 = s8_ref[32:n, :] + s8_ref[24:n - 8, :]
        lane = lax.broadcasted_iota(jnp.int32, (sub, POOL_WIDTH), 1)
        wsum = jnp.where(lane < POOL_GROUP_DIM, s2_ref[32:n, :],
                         jnp.where(lane < 2 * POOL_GROUP_DIM, s4_ref[32:n, :],
                                   jnp.where(lane < 3 * POOL_GROUP_DIM, s8_ref[32:n, :], s16)))
        t_abs = (t_idx * tile + i * sub
                 + lax.broadcasted_iota(jnp.int32, (sub, POOL_WIDTH), 0)).astype(_F32)
        cnt = jnp.minimum(t_abs + 1.0, poolwin_ref[...])
        pooled = wsum / cnt - u
        ycat_ref[srows, 0:POOL_WIDTH] = (_dot(pooled.astype(_BF), poolw_ref[...])
                                         * pools_ref[lrow, :]).astype(_BF)

    inv_dv = 1.0 / DV
    k_scale = DK ** -0.5
    q_scale = DK ** -0.5
    smask = smask_ref[...]

    state_blocks = []
    for h in range(N_HEADS):
        lane0 = (h * DK) // LANES * LANES
        lane1 = -(-((h + 1) * DK) // LANES) * LANES
        state_blocks.append((slice(h * DV, (h + 1) * DV), slice(lane0, lane1)))

    def load_state(ref):
        return [ref[rows, lanes] for rows, lanes in state_blocks]

    def store_state(ref, blocks):
        for (rows, lanes), blk in zip(state_blocks, blocks):
            ref[rows, lanes] = blk

    def advance_state(blocks, decay, upd):
        return [blk * decay[:, lanes] + upd[rows, lanes]
                for (rows, lanes), blk in zip(state_blocks, blocks)]

    def own_head(blocks):
        out = []
        for (rows, lanes), blk in zip(state_blocks, blocks):
            part = blk.astype(_BF) * smask[rows, lanes]
            left = lanes.start
            right = KP - lanes.stop
            pieces = ([jnp.zeros((DV, left), _BF)] if left else []) + [part] \
                + ([jnp.zeros((DV, right), _BF)] if right else [])
            out.append(pieces[0] if len(pieces) == 1 else jnp.concatenate(pieces, axis=1))
        return jnp.concatenate(out, axis=0)
    hind = hind_ref[...]
    first_half = rothalf_ref[...] > 0.5

    def head_rms(o):
        ms_h = _dot((o * o).astype(_BF), hind) * inv_dv
        return o * lax.rsqrt(ms_h + EPS)

    def stack_heads(a):
        return jnp.concatenate([a] * N_HEADS, axis=0)

    def rotary(t, cosv, sinv):
        partner = jnp.where(first_half, pltpu.roll(t, KP - DK // 2, axis=1), pltpu.roll(t, DK // 2, axis=1))
        return t * cosv + partner * sinv

    def retention(i, r0):
        C = RET_CHUNK
        pv = proj_ref.at[i % 2]
        parts = []
        for c in range(sub // C):
            rows = slice(r0 + c * C, r0 + (c + 1) * C)
            lrows = slice(c * C, (c + 1) * C)
            cosv = cos_ref[0, rows, :]
            sinv = sin_ref[0, rows, :]
            q = rotary(key_block(pv, lrows, OFF_RQ), cosv, sinv)
            k = rotary(key_block(pv, lrows, OFF_RK), cosv, sinv) * k_scale
            vb = pv[lrows, OFF_RV:OFF_RV + VW].astype(_BF)
            qb = q.astype(_BF)
            kbd = stack_heads(k.astype(_BF)) * rhmk_ref[...]
            scores = _dot_nt(qb, kbd) * rdecay_ref[...]
            vbd = stack_heads(vb) * rhmv_ref[...]
            o_intra = _dot(scores.astype(_BF), vbd)
            upd = _dot_tn(vb, (k * rzeta_ref[...]).astype(_BF))
            parts.append((rows, qb, o_intra, upd))
            tick()
            yield
        state = states["ret"]
        for rows, qb, o_intra, upd in parts:
            oscr_ref[rows, 0:VW] = o_intra + _dot_nt(qb, own_head(state)) * rxi_ref[...]
            state = advance_state(state, rgc_ref[...], upd)
            tick()
            yield
        states["ret"] = state

    def gla_decay(i):
        C = GLA_CHUNK
        n_chunks = sub // C
        W = n_chunks * KP
        pv = proj_ref.at[i % 2]
        logits = _dot(pv[:, OFF_GA:OFF_GA + LANES].astype(_BF), wa2_ref[...]) + ba_ref[lrow, :]
        log_sig = jnp.minimum(logits, 0.0) - jnp.log(1.0 + jnp.exp(-jnp.abs(logits)))
        la = log_sig * (LOG2_E / GLA_GATE_TAU)

        la_l = jnp.concatenate([la[c * C:(c + 1) * C, :] for c in range(n_chunks)], axis=1)
        tri = gtri_ref[...]
        p1, p2, p3 = _split3(la_l)
        b = _dot(tri, p1) + _dot(tri, p2) + _dot(tri, p3)
        b_ref[i % 2] = b
        yield

        row = lax.broadcasted_iota(jnp.int32, (C, W), 0)
        s = C // 2
        lvl = 0
        while s >= SUBLANES // 2:
            refs = [jnp.broadcast_to(b_ref[i % 2, p + s - 1:p + s, :], (2 * s, W)) for p in range(0, C, 2 * s)]
            b_at_ref = refs[0] if len(refs) == 1 else jnp.concatenate(refs, axis=0)
            dec_ref[i % 2, lvl * C:(lvl + 1) * C, :] = jnp.exp2(-jnp.abs(b - b_at_ref))
            s //= 2
            lvl += 1
            tick()
            yield
        up1 = pltpu.roll(b, 1, axis=0)
        up2 = pltpu.roll(b, 2, axis=0)
        dn1 = pltpu.roll(b, C - 1, axis=0)
        m4 = row % 4
        b_at_ref = jnp.where(m4 == 0, dn1, jnp.where(m4 == 1, b, jnp.where(m4 == 2, up1, up2)))
        dec_ref[i % 2, lvl * C:(lvl + 1) * C, :] = jnp.exp2(-jnp.abs(b - b_at_ref))
        lvl += 1
        tick()
        yield
        b_at_ref = jnp.where(row % 2 == 0, b, up1)
        dec_ref[i % 2, lvl * C:(lvl + 1) * C, :] = jnp.exp2(-jnp.abs(b - b_at_ref))
        lvl += 1
        tick()
        yield
        dec_ref[i % 2, lvl * C:(lvl + 1) * C, :] = jnp.exp2(b)
        tick()
        b_last = jnp.broadcast_to(b_ref[i % 2, C - 1:C, :], (C, W))
        dec_ref[i % 2, (lvl + 1) * C:(lvl + 2) * C, :] = jnp.exp2(b_last - b)
        tick()

    def gla_chunks(i, r0, state):
        C = GLA_CHUNK
        n_chunks = sub // C
        pv = proj_ref.at[i % 2]
        hmk = ghmk_ref[...]
        parts = []
        for c in range(n_chunks):
            rows = slice(r0 + c * C, r0 + (c + 1) * C)
            lrows = slice(c * C, (c + 1) * C)
            lanes = slice(c * KP, (c + 1) * KP)
            q = key_block(pv, lrows, OFF_GQ) * q_scale
            k = key_block(pv, lrows, OFF_GK)
            vb = pv[lrows, OFF_GV:OFF_GV + VW].astype(_BF)
            qb = q.astype(_BF)
            kbd0 = stack_heads(k.astype(_BF)) * hmk
            scores = jnp.zeros((C, N_HEADS * C), _F32)
            for lvl in range(N_GLA_LEVELS + 1):
                if lvl < N_GLA_LEVELS:
                    eb = dec_ref[i % 2, lvl * C:(lvl + 1) * C, lanes].astype(_BF)
                    ql, kbd = qb * eb, kbd0 * stack_heads(eb)
                else:
                    ql, kbd = qb, kbd0
                scores = scores + _dot_nt(ql, kbd) * gmask_ref[lvl]
                tick()
            vbd = stack_heads(vb) * ghmv_ref[...]
            e_cum = dec_ref[i % 2, N_GLA_LEVELS * C:(N_GLA_LEVELS + 1) * C, lanes]
            e_rev = dec_ref[i % 2, (N_GLA_LEVELS + 1) * C:(N_GLA_LEVELS + 2) * C, lanes]
            o_intra = _dot(scores.astype(_BF), vbd)
            upd = _dot_tn(vb, (k * e_rev).astype(_BF))
            parts.append((rows, (q * e_cum).astype(_BF), o_intra, upd, e_cum[C - 1:C, :]))
        for rows, qe, o_intra, upd, e_last in parts:
            oscr_ref[rows, VW:2 * VW] = o_intra + _dot_nt(qe, own_head(state))
            state = advance_state(state, e_last, upd)
            tick()
        return state

    states = {"ret": load_state(sret_ref)}
    gla_state = load_state(sgla_ref)
    for i in range(n_sub):
        r0 = i * sub
        srows = slice(r0, r0 + sub)
        if i + 1 < n_sub:
            pending.extend(projection_tasks(i + 1))
        tick()
        pooling(i)
        tick()
        stages = [retention(i, r0), gla_decay(i)]
        while stages:
            for stage in list(stages):
                if next(stage, stages) is stages:
                    stages.remove(stage)
        gla_state = gla_chunks(i, r0, gla_state)
        y_ret = _silu(proj_ref[i % 2, :, OFF_RG:OFF_RG + VW]) * head_rms(oscr_ref[srows, 0:VW])
        ycat_ref[srows, POOL_WIDTH:POOL_WIDTH + VW] = y_ret.astype(_BF)
        gg_off = OFF_GG % LANES
        gg = pltpu.roll(proj_ref[i % 2, :, OFF_GG - gg_off:N_IN], N_IN - OFF_GG, axis=1)[:, 0:VW]
        y_gla = (_silu(gg)
                 * (head_rms(oscr_ref[srows, VW:2 * VW]) * gng_ref[lrow, :]))
        require(i + 1, N_IN)
        ycat_ref[srows, POOL_WIDTH + VW:POOL_WIDTH + 2 * VW] = y_gla.astype(_BF)
        o_ref[0, srows, :] = x_ref[0, srows, :] + gate1 * _dot(ycat_ref[srows, :], wout_ref[...])
    store_state(sret_ref, states["ret"])
    store_state(sgla_ref, gla_state)

    wup_bf_ref[...] = wup32_ref[...].astype(_BF)


def _whole(shape):
    nd = len(shape)
    return pl.BlockSpec(shape, lambda b, t: (0,) * nd, pipeline_mode=pl.Buffered(1))


def _layer_slab(shape, layer):
    nd = len(shape) - 1
    return pl.BlockSpec((None,) + tuple(shape[1:]), lambda b, t: (layer,) + (0,) * nd,
                        pipeline_mode=pl.Buffered(1))


def _mixer(x, mod, layer, params, w_up, cos, sin, tb):
    B, T, D = x.shape
    tile = MIX_TILE if T % MIX_TILE == 0 else T
    sub = MIX_SUB if tile % MIX_SUB == 0 else tile
    consts = [tb[k] for k in ("pool_win", "rot_first_half", "ret_decay", "ret_xi", "ret_zeta", "ret_gc",
                              "ret_hmk", "ret_hmv", "state_mask", "head_ind", "gla_tri", "gla_mask",
                              "gla_hmk", "gla_hmv")]
    norm_g, win = params[0], params[1]
    rest = list(params[2:])
    n_sub = tile // sub
    n_tiles = T // tile
    up_rows, up_cols = w_up.shape[1], w_up.shape[2]
    chunk = up_rows // (B * n_tiles)
    assert chunk * B * n_tiles == up_rows and chunk % (2 * SUBLANES) == 0
    args = [x, mod, norm_g, win, cos, sin] + rest + consts + [w_up]
    in_specs = [
        pl.BlockSpec((1, tile, D), lambda b, t: (b, t, 0)),
        pl.BlockSpec((None, 1, 6, D), lambda b, t: (layer, b, 0, 0)),
        _whole(norm_g.shape),
        _layer_slab(win.shape, layer),
        pl.BlockSpec((1, tile, KP), lambda b, t: (b, t, 0)),
        pl.BlockSpec((1, tile, KP), lambda b, t: (b, t, 0)),
    ] + [_whole(a.shape) if a.ndim == 2 else _layer_slab(a.shape, layer) for a in rest] + [
        _whole(a.shape) for a in consts] + [
        pl.BlockSpec((None, chunk, up_cols), lambda b, t: (layer, b * n_tiles + t, 0))]
    n = sub + POOL_HIST
    gla_lanes = (sub // GLA_CHUNK) * KP
    return pl.pallas_call(
        functools.partial(_mixer_kernel, tile=tile, sub=sub, layer=layer),
        grid=(B, T // tile),
        in_specs=in_specs,
        out_specs=[pl.BlockSpec((1, tile, D), lambda b, t: (b, t, 0)),
                   pl.BlockSpec((chunk, up_cols), lambda b, t: (b * n_tiles + t, 0))],
        out_shape=[jax.ShapeDtypeStruct((B, T, D), _F32),
                   jax.ShapeDtypeStruct((up_rows, up_cols), _BF)],
        scratch_shapes=[
            pltpu.VMEM((D, N_IN), _BF),
            pltpu.VMEM((POOL_WIDTH + 2 * VW, D), _BF),
            pltpu.VMEM((2, sub, D), _BF),
            pltpu.VMEM((2, sub, N_IN), _F32),
            pltpu.VMEM((tile, POOL_WIDTH + 2 * VW), _BF),
            pltpu.VMEM((2, GLA_CHUNK, gla_lanes), _F32),
            pltpu.VMEM((2, (N_GLA_LEVELS + 2) * GLA_CHUNK, gla_lanes), _F32),
            pltpu.VMEM((tile, 2 * VW), _F32),
            pltpu.VMEM((n, POOL_WIDTH), _F32),
            pltpu.VMEM((n, POOL_WIDTH), _F32),
            pltpu.VMEM((n, POOL_WIDTH), _F32),
            pltpu.VMEM((n, POOL_WIDTH), _F32),
            pltpu.VMEM((POOL_HIST, POOL_WIDTH), _F32),
            pltpu.VMEM((VW, KP), _F32),
            pltpu.VMEM((VW, KP), _F32),
        ],
        compiler_params=pltpu.CompilerParams(
            dimension_semantics=("arbitrary", "arbitrary"), vmem_limit_bytes=VMEM_LIMIT_BYTES),
        name="token_mixer",
    )(*args)


def _ffn_kernel(x_ref, mod_ref, ng_ref, wup_ref, cw_ref, wd32_ref, fg_ref,
                o_ref, wd_ref, h_ref, carry_ref, act_ref,
                *, tile, sub, d_ff, fb, final_norm, layer):
    t_idx = pl.program_id(1)

    @pl.when((pl.program_id(0) == 0) & (t_idx == 0))
    def _():
        _cast_weight(wd32_ref, wd_ref)

    @pl.when(t_idx == 0)
    def _():
        carry_ref[...] = jnp.zeros_like(carry_ref)

    for base in range(0, tile, sub):
        _ffn_sub_tile(x_ref, mod_ref, ng_ref, wup_ref, cw_ref, wd_ref, fg_ref, o_ref, h_ref,
                      carry_ref, act_ref, base=base, tile=sub, d_ff=d_ff, fb=fb,
                      final_norm=final_norm, layer=layer)


def _ffn_sub_tile(x_ref, mod_ref, ng_ref, wup_ref, cw_ref, wd_ref, fg_ref, o_ref, h_ref, carry_ref,
                  act_ref, *, base, tile, d_ff, fb, final_norm, layer):
    S = SUBLANES
    R = tile // S
    D = x_ref.shape[-1]
    shift2 = mod_ref[0, 3:4, :]
    scale2 = mod_ref[0, 4:5, :]
    gate2 = mod_ref[0, 5:6, :]
    norm_scale = ng_ref[layer:layer + 1, :] * (1.0 + scale2)
    groups = FFN_NORM_ROWS // S
    early_cols = ([j * fb for j in range(FFN_EARLY_BLOCKS)]
                  + [d_ff + j * fb for j in range(FFN_EARLY_BLOCKS)])
    early = {col: [] for col in early_cols}
    for p in range(tile // FFN_NORM_ROWS):
        prow = slice(p * FFN_NORM_ROWS, (p + 1) * FFN_NORM_ROWS)
        xs = jnp.concatenate([x_ref[0, base + s * R + p * groups:base + s * R + (p + 1) * groups, :]
                              for s in range(S)], axis=0)
        xs = xs.reshape(S, groups, D).swapaxes(0, 1).reshape(FFN_NORM_ROWS, D)
        ms = jnp.mean(xs * xs, axis=-1, keepdims=True)
        h_ref[prow, :] = ((xs * lax.rsqrt(ms + EPS)) * norm_scale + shift2).astype(_BF)
        for col in early_cols:
            early[col].append(_dot(h_ref[prow, :], wup_ref[:, col:col + fb]))
    first_sublane = lax.broadcasted_iota(jnp.int32, (S, fb), 0) == 0

    def up_conv(col):
        cols = slice(col, col + fb)
        if col in early:
            u = jnp.concatenate(early[col], axis=0)
        else:
            u = _dot(h_ref[...], wup_ref[:, cols])
        back1 = jnp.where(first_sublane, pltpu.roll(carry_ref[S:2 * S, cols], 1, axis=0),
                          pltpu.roll(u[tile - S:tile, :], 1, axis=0))
        back2 = jnp.where(first_sublane, pltpu.roll(carry_ref[0:S, cols], 1, axis=0),
                          pltpu.roll(u[tile - 2 * S:tile - S, :], 1, axis=0))
        carry_ref[:, cols] = u[tile - 2 * S:tile, :]
        prev1 = jnp.concatenate([back1, u[0:tile - S, :]], axis=0)
        prev2 = jnp.concatenate([back2, back1, u[0:tile - 2 * S, :]], axis=0)
        cw = cw_ref[:, cols]
        return prev2 * cw[0:1, :] + prev1 * cw[1:2, :] + u * cw[2:3, :] + cw[3:4, :]

    for j in range(d_ff // fb):
        ya = up_conv(j * fb)
        yg = up_conv(d_ff + j * fb)
        act_ref[:, j * fb:(j + 1) * fb] = (_silu(yg) * ya).astype(_BF)

    y = _dot(act_ref[...], wd_ref[...]).reshape(R, S, D).swapaxes(0, 1).reshape(tile, D)
    out = x_ref[0, base:base + tile, :] + gate2 * y
    if final_norm:
        ms_o = jnp.mean(out * out, axis=-1, keepdims=True)
        out = out * lax.rsqrt(ms_o + EPS) * fg_ref[...]
    o_ref[0, base:base + tile, :] = out


def _ffn(x, mod, layer, params, final_g, final_norm):
    B, T, D = x.shape
    tile = FFN_TILE if T % FFN_TILE == 0 else T
    sub = FFN_SUB if tile % FFN_SUB == 0 else tile
    norm_g, wup, cw, wd = params
    d_ff = wd.shape[1]
    in_specs = [
        pl.BlockSpec((1, tile, D), lambda b, t: (b, t, 0)),
        pl.BlockSpec((None, 1, 6, D), lambda b, t: (layer, b, 0, 0)),
        _whole(norm_g.shape),
        _whole(wup.shape),
        _layer_slab(cw.shape, layer),
        _layer_slab(wd.shape, layer),
        _whole(final_g.shape),
    ]
    return pl.pallas_call(
        functools.partial(_ffn_kernel, tile=tile, sub=sub, d_ff=d_ff, fb=FFN_BLOCK, final_norm=final_norm,
                          layer=layer),
        grid=(B, T // tile),
        in_specs=in_specs,
        out_specs=pl.BlockSpec((1, tile, D), lambda b, t: (b, t, 0)),
        out_shape=jax.ShapeDtypeStruct((B, T, D), _F32),
        scratch_shapes=[
            pltpu.VMEM((d_ff, D), _BF),
            pltpu.VMEM((sub, D), _BF),
            pltpu.VMEM((2 * SUBLANES, 2 * d_ff), _F32),
            pltpu.VMEM((sub, d_ff), _BF),
        ],
        compiler_params=pltpu.CompilerParams(
            dimension_semantics=("arbitrary", "arbitrary"), vmem_limit_bytes=VMEM_LIMIT_BYTES),
        name="conv_ffn",
    )(x, mod, *params, final_g)


def _prep_mixer_params(norm1_g, w_in, pool_w, pool_scale, gla_wa2, gla_ba, gla_norm_g, w_out):
    L, D, _ = w_in.shape
    G = pool_w.shape[1]
    eye = jnp.eye(G, dtype=pool_w.dtype)
    poolw = (pool_w[:, :, :, None, :] * eye[None, :, None, :, None]).reshape(L, POOL_WIDTH, POOL_WIDTH)
    wa2 = jnp.pad(gla_wa2, ((0, 0), (0, LANES - GLA_GATE_RANK), (0, KP - KW)))
    ba = jnp.pad(gla_ba, ((0, 0), (0, KP - KW)))
    w_in = jnp.swapaxes(w_in, 1, 2)
    return (norm1_g, w_in, poolw.astype(_BF), pool_scale, wa2.astype(_BF), ba, gla_norm_g, w_out)


def _prep_ffn_params(norm2_g, w_up, conv_w, conv_b, w_down):
    L, D, _ = w_up.shape
    cw = jnp.pad(jnp.concatenate([conv_w, conv_b[:, None, :]], axis=1),
                 ((0, 0), (0, SUBLANES - CONV_WIDTH - 1), (0, 0)))
    return norm2_g, cw, w_down


def kernel(x, c, positions, ada_w, ada_b, norm1_g, w_in, pool_w, pool_scale, gla_wa2, gla_ba,
           gla_norm_g, w_out, norm2_g, w_up, conv_w, conv_b, w_down, final_g):
    L = ada_w.shape[0]
    B, T, D = x.shape
    tb = {k: jnp.asarray(v) for k, v in _tables().items()}
    for name in ("head_ind", "gla_tri", "ret_hmk", "ret_hmv", "gla_hmk", "gla_hmv", "state_mask",
                 "rot_cos_expand", "rot_sin_expand"):
        tb[name] = tb[name].astype(_BF)
    mod = _modulation(c, ada_w, ada_b).reshape(L, B, 6, D)
    cos, sin = _rotary_tables(positions, tb)
    mixer_params = _prep_mixer_params(norm1_g, w_in, pool_w, pool_scale, gla_wa2, gla_ba, gla_norm_g, w_out)
    norm2, conv_rows, w_down_f32 = _prep_ffn_params(norm2_g, w_up, conv_w, conv_b, w_down)
    fg = final_g.reshape(1, D)
    for l in range(L):
        x, w_up_bf16 = _mixer(x, mod, l, mixer_params, w_up, cos, sin, tb)
        x = _ffn(x, mod, l, (norm2, w_up_bf16, conv_rows, w_down_f32), fg, final_norm=(l == L - 1))
    return x
```
